```python
import jax, jax.numpy as jnp
from jax import lax
import numpy as np

D_MODEL = 1024
BATCH = 8
SEQ = 2048
DEPTH = 1

RWKV_HEAD_DIM = 64
RWKV_HEADS = 8
RWKV_WIDTH = RWKV_HEADS * RWKV_HEAD_DIM
DECAY_LORA = 32
ICLR_LORA = 32
GATE_LORA = 96
N_DIR = 2
RWKV_COLS = 3 * RWKV_WIDTH + N_DIR * DECAY_LORA + N_DIR * ICLR_LORA + GATE_LORA
LN_X_EPS = 64e-5

MLA_HEADS = 8
QK_NOPE_DIM = 64
QK_ROPE_DIM = 32
V_HEAD_DIM = 64
MLA_WIDTH = MLA_HEADS * V_HEAD_DIM
Q_LORA_RANK = 256
KV_LORA_RANK = 128
MLA_COLS = Q_LORA_RANK + KV_LORA_RANK + QK_ROPE_DIM
ROPE_THETA = 10000.0
Q_BLOCK = 128

D_IN = RWKV_COLS + MLA_COLS
D_MIX = RWKV_WIDTH + MLA_WIDTH

MEM_TOKENS = 256
MEM_HEADS = 4
MEM_HEAD_DIM = D_MODEL // MEM_HEADS

D_FF = 4 * D_MODEL
NORM_EPS = 1e-6

kernel_name = "hymba_rwkv7_mla_memxattn_encoder"


def rms_norm(x, g):
    xf = x.astype(jnp.float32)
    y = xf * lax.rsqrt(jnp.mean(xf * xf, axis=-1, keepdims=True) + NORM_EPS)
    return (y * g.astype(jnp.float32)).astype(x.dtype)


def short_conv(z, c):
    prev = jnp.pad(z[:, :-1], ((0, 0), (1, 0), (0, 0)))
    nxt = jnp.pad(z[:, 1:], ((0, 0), (0, 1), (0, 0)))
    return c[0] * prev + c[1] * z + c[2] * nxt


def rope_tables(positions):
    inv_freq = ROPE_THETA ** (-jnp.arange(0, QK_ROPE_DIM, 2, dtype=jnp.float32) / QK_ROPE_DIM)
    ang = positions.astype(jnp.float32)[..., None] * inv_freq
    return jnp.cos(ang), jnp.sin(ang)


def apply_rope(x, cos, sin):
    xf = x.astype(jnp.float32)
    x1, x2 = jnp.split(xf, 2, axis=-1)
    return jnp.concatenate([x1 * cos - x2 * sin, x2 * cos + x1 * sin], axis=-1).astype(x.dtype)


def _dirs_time_major(t):
    B, S = t.shape[0], t.shape[1]
    t = t.reshape(B, S, N_DIR, RWKV_HEADS, RWKV_HEAD_DIM).transpose(1, 2, 0, 3, 4)
    return jnp.stack([t[:, 0], t[::-1, 1]], axis=1)


def _wkv7_step(state, inp):
    r, w, k, v, a, b = inp
    sa = jnp.einsum('dbhvk,dbhk->dbhv', state, a)
    state = state * w[..., None, :] + sa[..., :, None] * b[..., None, :] + v[..., :, None] * k[..., None, :]
    out = jnp.einsum('dbhvk,dbhk->dbhv', state, r)
    return state, out


def rwkv7_bidirectional(z, w0, w2, a0, a2, g2, k_k, k_a, r_k, lnx_w, lnx_b):
    B, S, _ = z.shape
    C, H, N = RWKV_WIDTH, RWKV_HEADS, RWKV_HEAD_DIM
    f32 = jnp.float32
    cuts = [C, 2 * C, 3 * C, 3 * C + N_DIR * DECAY_LORA, 3 * C + N_DIR * (DECAY_LORA + ICLR_LORA)]
    r, k, v, xw, xa, xg = jnp.split(z.astype(f32), cuts, axis=-1)
    xw = xw.reshape(B, S, N_DIR, DECAY_LORA)
    xa = xa.reshape(B, S, N_DIR, ICLR_LORA)
    w_log = -jax.nn.softplus(-(w0.astype(f32) + jnp.einsum('bsdr,drc->bsdc', jnp.tanh(xw), w2.astype(f32)))) - 0.5
    decay = jnp.exp(-jnp.exp(w_log))
    a = jax.nn.sigmoid(a0.astype(f32) + jnp.einsum('bsdr,drc->bsdc', xa, a2.astype(f32)))
    g = jax.nn.sigmoid(xg) @ g2.astype(f32)
    kk = (k * k_k.astype(f32)).reshape(B, S, H, N)
    kk = (kk * lax.rsqrt(jnp.maximum(jnp.sum(kk * kk, axis=-1, keepdims=True), 1e-24))).reshape(B, S, C)
    k_dir = k[:, :, None, :] * (1.0 + (a - 1.0) * k_a.astype(f32))
    shared = lambda t: jnp.broadcast_to(t[:, :, None, :], (B, S, N_DIR, C))
    inputs = tuple(_dirs_time_major(t) for t in
                   (shared(r), decay, k_dir, shared(v), shared(-kk), kk[:, :, None, :] * a))
    state0 = jnp.zeros((N_DIR, B, H, N, N), f32)
    _, o = lax.scan(_wkv7_step, state0, inputs)
    o = (o[:, 0] + o[::-1, 1]).transpose(1, 0, 2, 3)
    mu = jnp.mean(o, axis=-1, keepdims=True)
    var = jnp.mean(jnp.square(o - mu), axis=-1, keepdims=True)
    o = ((o - mu) * lax.rsqrt(var + LN_X_EPS)).reshape(B, S, C) * lnx_w.astype(f32) + lnx_b.astype(f32)
    bonus = jnp.sum((r * jnp.sum(k_dir, axis=2)).reshape(B, S, H, N) * r_k.astype(f32), axis=-1, keepdims=True)
    bonus = (bonus * v.reshape(B, S, H, N)).reshape(B, S, C)
    return ((o + bonus) * g).astype(z.dtype)


def mla_bidirectional(z, positions, q_norm, w_uq, kv_norm, w_ukv):
    B, S, _ = z.shape
    c_q, c_kv, k_rope = jnp.split(z, [Q_LORA_RANK, Q_LORA_RANK + KV_LORA_RANK], axis=-1)
    q = (rms_norm(c_q, q_norm) @ w_uq).reshape(B, S, MLA_HEADS, QK_NOPE_DIM + QK_ROPE_DIM)
    q_nope, q_rope = jnp.split(q, [QK_NOPE_DIM], axis=-1)
    kv = (rms_norm(c_kv, kv_norm) @ w_ukv).reshape(B, S, MLA_HEADS, QK_NOPE_DIM + V_HEAD_DIM)
    k_nope, v = jnp.split(kv, [QK_NOPE_DIM], axis=-1)
    cos, sin = rope_tables(positions)
    q_rope = apply_rope(q_rope, cos[:, :, None, :], sin[:, :, None, :])
    k_rope = apply_rope(k_rope, cos, sin)
    n_blocks = S // Q_BLOCK
    scale = (QK_NOPE_DIM + QK_ROPE_DIM) ** -0.5

    def to_blocks(t):
        return t.reshape(B, n_blocks, Q_BLOCK, t.shape[2], t.shape[3]).swapaxes(0, 1)

    def attend_block(qb):
        qn, qr = qb
        s = jnp.einsum('bqhd,bkhd->bhqk', qn, k_nope) + jnp.einsum('bqhr,bkr->bhqk', qr, k_rope)
        p = jax.nn.softmax(s.astype(jnp.float32) * scale, axis=-1).astype(v.dtype)
        return jnp.einsum('bhqk,bkhd->bqhd', p, v)

    o = lax.map(attend_block, (to_blocks(q_nope), to_blocks(q_rope)))
    return o.swapaxes(0, 1).reshape(B, S, MLA_WIDTH)


def memory_cross_attention(h, mem, g_mem, wq, wkv, wo):
    B, S, _ = h.shape
    m = rms_norm(mem, g_mem)
    q = (h @ wq).reshape(B, S, MEM_HEADS, MEM_HEAD_DIM)
    k, v = jnp.split(m @ wkv, 2, axis=-1)
    k = k.reshape(B, MEM_TOKENS, MEM_HEADS, MEM_HEAD_DIM)
    v = v.reshape(B, MEM_TOKENS, MEM_HEADS, MEM_HEAD_DIM)
    s = jnp.einsum('bqhd,bkhd->bhqk', q, k).astype(jnp.float32) * (MEM_HEAD_DIM ** -0.5)
    p = jax.nn.softmax(s, axis=-1).astype(v.dtype)
    o = jnp.einsum('bhqk,bkhd->bqhd', p, v).reshape(B, S, MEM_HEADS * MEM_HEAD_DIM)
    return o @ wo


def setup_inputs(seed: int = 0) -> dict:
    key = jax.random.key(seed)
    keys = iter(jax.random.split(key, 40))
    L, D, C = DEPTH, D_MODEL, RWKV_WIDTH

    def nrm(shape, scale):
        return scale * jax.random.normal(next(keys), shape, jnp.float32)

    def gain(n):
        return 1.0 + nrm((L, n), 0.05)

    x = nrm((BATCH, SEQ, D), 1.0)
    mem = nrm((BATCH, MEM_TOKENS, D), 1.0)
    positions = (jnp.cumsum(jax.random.randint(next(keys), (BATCH, SEQ), 1, 3, dtype=jnp.int32), axis=1) - 1).astype(jnp.int32)
    return {
        'x': x,
        'mem': mem,
        'positions': positions,
        'norm_mix_pre': gain(D),
        'w_in': nrm((L, D, D_IN), D ** -0.5),
        'conv_rwkv': jnp.array([0.25, 0.5, 0.25], jnp.float32)[None, :, None] + nrm((L, 3, RWKV_COLS), 0.05),
        'rwkv_w0': jax.random.uniform(next(keys), (L, N_DIR, C), jnp.float32, -6.0, 2.0),
        'rwkv_w2': nrm((L, N_DIR, DECAY_LORA, C), 0.1 * DECAY_LORA ** -0.5),
        'rwkv_a0': nrm((L, N_DIR, C), 0.5),
        'rwkv_a2': nrm((L, N_DIR, ICLR_LORA, C), ICLR_LORA ** -0.5),
        'rwkv_g2': nrm((L, GATE_LORA, C), GATE_LORA ** -0.5),
        'rwkv_k_k': 0.85 + nrm((L, C), 0.05),
        'rwkv_k_a': 1.0 + nrm((L, C), 0.05),
        'rwkv_r_k': nrm((L, RWKV_HEADS, RWKV_HEAD_DIM), 0.1),
        'rwkv_lnx_w': gain(C),
        'rwkv_lnx_b': nrm((L, C), 0.02),
        'mla_q_norm': gain(Q_LORA_RANK),
        'mla_w_uq': nrm((L, Q_LORA_RANK, MLA_HEADS * (QK_NOPE_DIM + QK_ROPE_DIM)), Q_LORA_RANK ** -0.5),
        'mla_kv_norm': gain(KV_LORA_RANK),
        'mla_w_ukv': nrm((L, KV_LORA_RANK, MLA_HEADS * (QK_NOPE_DIM + V_HEAD_DIM)), KV_LORA_RANK ** -0.5),
        'w_out': nrm((L, D_MIX, D), D_MIX ** -0.5),
        'norm_mix_post': gain(D),
        'norm_mem_pre': gain(D),
        'norm_memtok': gain(D),
        'mem_wq': nrm((L, D, MEM_HEADS * MEM_HEAD_DIM), D ** -0.5),
        'mem_wkv': nrm((L, D, 2 * MEM_HEADS * MEM_HEAD_DIM), D ** -0.5),
        'mem_wo': nrm((L, MEM_HEADS * MEM_HEAD_DIM, D), (MEM_HEADS * MEM_HEAD_DIM) ** -0.5),
        'norm_mem_post': gain(D),
        'norm_mlp_pre': gain(D),
        'mlp_w1': nrm((L, D, D_FF), D ** -0.5),
        'mlp_w2': nrm((L, D_FF, D), D_FF ** -0.5),
        'norm_mlp_post': gain(D),
    }


def reference(x, mem, positions, norm_mix_pre, w_in, conv_rwkv, rwkv_w0, rwkv_w2, rwkv_a0, rwkv_a2,
              rwkv_g2, rwkv_k_k, rwkv_k_a, rwkv_r_k, rwkv_lnx_w, rwkv_lnx_b, mla_q_norm, mla_w_uq,
              mla_kv_norm, mla_w_ukv, w_out, norm_mix_post, norm_mem_pre, norm_memtok, mem_wq, mem_wkv,
              mem_wo, norm_mem_post, norm_mlp_pre, mlp_w1, mlp_w2, norm_mlp_post):
    for l in range(DEPTH):
        h = rms_norm(x, norm_mix_pre[l])
        z = h @ w_in[l]
        z_rwkv, z_mla = z[..., :RWKV_COLS], z[..., RWKV_COLS:]
        y_rwkv = rwkv7_bidirectional(short_conv(z_rwkv, conv_rwkv[l]), rwkv_w0[l], rwkv_w2[l], rwkv_a0[l],
                                     rwkv_a2[l], rwkv_g2[l], rwkv_k_k[l], rwkv_k_a[l], rwkv_r_k[l],
                                     rwkv_lnx_w[l], rwkv_lnx_b[l])
        y_mla = mla_bidirectional(z_mla, positions, mla_q_norm[l], mla_w_uq[l], mla_kv_norm[l], mla_w_ukv[l])
        y = jnp.concatenate([y_rwkv, y_mla], axis=-1) @ w_out[l]
        x = x + rms_norm(y, norm_mix_post[l])
        h = rms_norm(x, norm_mem_pre[l])
        y = memory_cross_attention(h, mem, norm_memtok[l], mem_wq[l], mem_wkv[l], mem_wo[l])
        x = x + rms_norm(y, norm_mem_post[l])
        h = rms_norm(x, norm_mlp_pre[l])
        y = jnp.square(jax.nn.relu(h @ mlp_w1[l])) @ mlp_w2[l]
        x = x + rms_norm(y, norm_mlp_post[l])
    return x
```

```python
import functools
import math

import jax
import jax.numpy as jnp
from jax import lax
from jax.experimental import pallas as pl
from jax.experimental.pallas import tpu as pltpu

F32 = jnp.float32
BF16 = jnp.bfloat16

D_MODEL = 1024
NORM_EPS = 1e-6

RWKV_HEADS = 8
HEAD_DIM = 64
RWKV_WIDTH = RWKV_HEADS * HEAD_DIM
DECAY_LORA = 32
ICLR_LORA = 32
GATE_LORA = 96
LORA_COLS = 2 * DECAY_LORA + 2 * ICLR_LORA + GATE_LORA
LORA_TILE = 256
RWKV_COLS = 3 * RWKV_WIDTH + LORA_COLS
RWKV_TILE_COLS = 3 * RWKV_WIDTH + LORA_TILE
LN_X_EPS = 64e-5
CHUNK = 64
DECAY_SCALE = math.exp(-0.5)

MLA_HEADS = 8
QK_NOPE_DIM = 64
QK_ROPE_DIM = 32
V_HEAD_DIM = 64
MLA_WIDTH = MLA_HEADS * V_HEAD_DIM
Q_LORA_RANK = 256
KV_LORA_RANK = 128
ROPE_THETA = 10000.0
LANES = 128
MLA_TILE_COLS = Q_LORA_RANK + KV_LORA_RANK + 2 * LANES

MEM_HEADS = 4
MEM_HEAD_DIM = D_MODEL // MEM_HEADS
D_FF = 4 * D_MODEL

VMEM_LIMIT = 56 * 1024 * 1024


def _mm(a, b):
    return jnp.dot(a.astype(BF16), b.astype(BF16), preferred_element_type=F32)


def _split2(t):
    hi = t.astype(BF16)
    lo = (t - hi.astype(F32)).astype(BF16)
    return hi, lo


def _split3(t):
    hi = t.astype(BF16)
    rest = t - hi.astype(F32)
    mid = rest.astype(BF16)
    lo = (rest - mid.astype(F32)).astype(BF16)
    return hi, mid, lo


def _dot(a, b):
    return jnp.dot(a, b, preferred_element_type=F32)


def _mm_x3(a, b):
    a_hi, a_lo = _split2(a)
    b_hi, b_lo = _split2(b)
    return _dot(a_hi, b_hi) + (_dot(a_hi, b_lo) + _dot(a_lo, b_hi))


def _mm_exact_lhs(a, b):
    a = a.astype(BF16)
    hi, mid, lo = _split3(b)
    return _dot(a, hi) + (_dot(a, mid) + _dot(a, lo))


def _mm_exact_rhs(a, b):
    b = b.astype(BF16)
    hi, mid, lo = _split3(a)
    return _dot(hi, b) + (_dot(mid, b) + _dot(lo, b))


def _mm_nt(a, b):
    return lax.dot_general(a.astype(BF16), b.astype(BF16), (((1,), (1,)), ((), ())),
                           preferred_element_type=F32)


def _mm_tn(a, b):
    return lax.dot_general(a.astype(BF16), b.astype(BF16), (((0,), (0,)), ((), ())),
                           preferred_element_type=F32)


def _rms(x, g, eps=NORM_EPS):
    return x * lax.rsqrt(jnp.mean(x * x, axis=-1, keepdims=True) + eps) * g


def _sigmoid(x):
    return 1.0 / (1.0 + jnp.exp(-x))


def _params(*sem):
    return pltpu.CompilerParams(dimension_semantics=sem, vmem_limit_bytes=VMEM_LIMIT)


def _const_spec(shape):
    nd = len(shape)
    return pl.BlockSpec(shape, lambda *_: (0,) * nd)


def _in_proj_kernel(x_ref, g_ref, wr_ref, wm_ref, zr_ref, zm_ref):
    h = _rms(x_ref[0], g_ref[...]).astype(BF16)
    zr_ref[0] = jnp.dot(h, wr_ref[...], preferred_element_type=F32)
    zm_ref[0] = jnp.dot(h, wm_ref[...], preferred_element_type=F32)


def _in_proj(x, g, w_r, w_m, tm):
    B, S, D = x.shape
    return pl.pallas_call(
        _in_proj_kernel,
        grid=(B, S // tm),
        in_specs=[pl.BlockSpec((1, tm, D), lambda b, i: (b, i, 0)),
                  _const_spec((1, D)),
                  _const_spec(w_r.shape),
                  _const_spec(w_m.shape)],
        out_specs=[pl.BlockSpec((1, tm, RWKV_TILE_COLS), lambda b, i: (b, i, 0)),
                   pl.BlockSpec((1, tm, MLA_TILE_COLS), lambda b, i: (b, i, 0))],
        out_shape=[jax.ShapeDtypeStruct((B, S, RWKV_TILE_COLS), F32),
                   jax.ShapeDtypeStruct((B, S, MLA_TILE_COLS), F32)],
        compiler_params=_params("parallel", "parallel"),
        name="in_proj",
    )(x, g, w_r, w_m)


def _chunk_unit(At, Rt, Bt, Kt, Bh, Kh, v, e_tot, m_strict, m_incl, eye):
    L = At.shape[0]
    X = jnp.concatenate([At, Rt], axis=0)
    Y = jnp.concatenate([Bt, Kt], axis=0)
    AA = _mm_nt(X, Y)
    A_ab = jnp.where(m_strict, AA[:L, :L], 0.0)
    A_ak = jnp.where(m_strict, AA[:L, L:], 0.0)
    A_rb = jnp.where(m_incl, AA[L:, :L], 0.0)
    A_rk = jnp.where(m_incl, AA[L:, L:], 0.0)
    AV = _mm(jnp.concatenate([A_ak, A_rk], axis=0), v)
    W = jnp.concatenate([At, AV[:L]], axis=1)
    Ak = A_ab
    W = W + _mm(Ak, W)
    span = 2
    while span < L:
        Ak = _mm(Ak, Ak)
        W = W + _mm(Ak, W)
        span *= 2
    Q = jnp.concatenate([Rt, AV[L:]], axis=1) + _mm(A_rb, W)
    GH = _mm_tn(Bh, W) + jnp.concatenate([eye * e_tot, _mm_tn(Kh, v)], axis=1)
    return Q, GH


def _rwkv_chunk_kernel(z_ref, zp_ref, zn_ref, conv_ref, w0_ref, w2_ref, a0_ref, a2_ref, g2_ref,
                       kk_ref, ka_ref, rk_ref, seg_ref, tri_ref,
                       qq_ref, gh_ref, bonus_ref, g_ref,
                       r_s, v_s, kk_s, b_s, kd_s, lw_s):
    i = pl.program_id(1)
    last = pl.num_programs(1) - 1
    tm = z_ref.shape[1]
    L = CHUNK

    z = z_ref[0]
    prev_row = jnp.where(i == 0, 0.0, zp_ref[0, 7:8, :])
    next_row = jnp.where(i == last, 0.0, zn_ref[0, 0:1, :])
    row = lax.broadcasted_iota(jnp.int32, (tm, 1), 0)
    z_dn = jnp.where(row == 0, prev_row, pltpu.roll(z, 1, 0))
    z_up = jnp.where(row == tm - 1, next_row, pltpu.roll(z, tm - 1, 0))
    zc = conv_ref[0:1, :] * z_dn + conv_ref[1:2, :] * z + conv_ref[2:3, :] * z_up

    r = zc[:, 0:RWKV_WIDTH]
    k = zc[:, RWKV_WIDTH:2 * RWKV_WIDTH]
    v = zc[:, 2 * RWKV_WIDTH:3 * RWKV_WIDTH]
    lora = zc[:, 3 * RWKV_WIDTH:]
    seg = seg_ref[...]

    def seg_sum(t):
        return _mm_exact_rhs(t, seg)

    kk = k * kk_ref[...]
    kk = kk * lax.rsqrt(jnp.maximum(seg_sum(kk * kk), 1e-24))
    tanh_lora = jnp.tanh(lora)
    kd_sum = jnp.zeros_like(k)
    for d in range(2):
        w_pre = w0_ref[d:d + 1, :] + _mm_x3(tanh_lora, w2_ref[d])
        lw_s[d] = -DECAY_SCALE * _sigmoid(w_pre)
        alpha = _sigmoid(a0_ref[d:d + 1, :] + _mm_x3(lora, a2_ref[d]))
        kd = k * (1.0 + (alpha - 1.0) * ka_ref[...])
        kd_s[d] = kd
        b_s[d] = kk * alpha
        kd_sum = kd_sum + kd
    r_s[...] = r
    v_s[...] = v
    kk_s[...] = kk
    g_ref[0] = _mm_x3(_sigmoid(lora), g2_ref[...])
    bonus_ref[0] = seg_sum(r * kd_sum * rk_ref[...]) * v

    ri = lax.broadcasted_iota(jnp.int32, (L, L), 0)
    ci = lax.broadcasted_iota(jnp.int32, (L, L), 1)
    eye = (ci == ri).astype(F32)
    m_strict = (ci < ri, ci > ri)
    m_incl = (ci <= ri, ci >= ri)

    def chunk_body(c, carry):
        rows = pl.ds(pl.multiple_of(c * L, L), L)
        rc = r_s[rows, :]
        vc = v_s[rows, :]
        kkc = kk_s[rows, :]
        for d in range(2):
            lw = lw_s[d, rows, :]
            bc = b_s[d, rows, :]
            kdc = kd_s[d, rows, :]
            cum = _mm_exact_lhs(tri_ref[d], lw)
            tot = cum[L - 1:L, :] if d == 0 else cum[0:1, :]
            e_in = jnp.exp(-cum)
            e_rem = jnp.exp(tot - cum)
            e_tot = jnp.exp(tot)
            At = -kkc * jnp.exp(cum - lw)
            Rt = rc * jnp.exp(cum)
            Bt = bc * e_in
            Kt = kdc * e_in
            Bh = bc * e_rem
            Kh = kdc * e_rem
            for h in range(RWKV_HEADS):
                sl = slice(h * HEAD_DIM, (h + 1) * HEAD_DIM)
                Q, GH = _chunk_unit(At[:, sl], Rt[:, sl], Bt[:, sl], Kt[:, sl], Bh[:, sl], Kh[:, sl],
                                    vc[:, sl], e_tot[:, sl], m_strict[d], m_incl[d], eye)
                col = (2 * h + d) * 2 * HEAD_DIM
                qq_ref[0, rows, col:col + 2 * HEAD_DIM] = Q
                gh_ref[0, rows, col:col + 2 * HEAD_DIM] = GH
        return carry

    lax.fori_loop(0, tm // L, chunk_body, 0)


def _rwkv_chunk(z_rwkv, conv_p, w0, w2_p, a0, a2_p, g2_p, k_k, k_a, r_k, seg, tri, tm):
    B, S, ZC = z_rwkv.shape
    C = RWKV_WIDTH
    halo = tm // 8
    n_halo = S // 8
    wide = 2 * RWKV_HEADS * 2 * HEAD_DIM
    row_spec = lambda cols: pl.BlockSpec((1, tm, cols), lambda b, i: (b, i, 0))
    return pl.pallas_call(
        _rwkv_chunk_kernel,
        grid=(B, S // tm),
        in_specs=[row_spec(ZC),
                  pl.BlockSpec((1, 8, ZC), lambda b, i: (b, jnp.maximum(i * halo - 1, 0), 0)),
                  pl.BlockSpec((1, 8, ZC), lambda b, i: (b, jnp.minimum((i + 1) * halo, n_halo - 1), 0)),
                  _const_spec(conv_p.shape), _const_spec(w0.shape), _const_spec(w2_p.shape),
                  _const_spec(a0.shape), _const_spec(a2_p.shape), _const_spec(g2_p.shape),
                  _const_spec(k_k.shape), _const_spec(k_a.shape), _const_spec(r_k.shape),
                  _const_spec(seg.shape), _const_spec(tri.shape)],
        out_specs=[row_spec(wide), row_spec(wide), row_spec(C), row_spec(C)],
        out_shape=[jax.ShapeDtypeStruct((B, S, wide), F32), jax.ShapeDtypeStruct((B, S, wide), F32),
                   jax.ShapeDtypeStruct((B, S, C), F32), jax.ShapeDtypeStruct((B, S, C), F32)],
        scratch_shapes=[pltpu.VMEM((tm, C), F32), pltpu.VMEM((tm, C), F32), pltpu.VMEM((tm, C), F32),
                        pltpu.VMEM((2, tm, C), F32), pltpu.VMEM((2, tm, C), F32), pltpu.VMEM((2, tm, C), F32)],
        compiler_params=_params("parallel", "parallel"),
        name="rwkv_chunk",
    )(z_rwkv, z_rwkv, z_rwkv, conv_p, w0, w2_p, a0, a2_p, g2_p, k_k, k_a, r_k, seg, tri)


def _rwkv_scan_kernel(qq_ref, gh_ref, bonus_ref, g_ref, lnw_ref, lnb_ref, segm_ref, y_ref, of_s, ob_s):
    S = qq_ref.shape[1]
    L, N = CHUNK, HEAD_DIM
    n_chunks = S // L
    out_s = (of_s, ob_s)

    def step(c, states):
        new_states = []
        for hh in range(2):
            for d in range(2):
                cc = c if d == 0 else n_chunks - 1 - c
                rows = pl.ds(pl.multiple_of(cc * L, L), L)
                col = (2 * hh + d) * 2 * N
                M = states[2 * hh + d]
                out = _mm_x3(qq_ref[0, rows, col:col + N], M) + qq_ref[0, rows, col + N:col + 2 * N]
                out_s[d][rows, hh * N:(hh + 1) * N] = out
                new_states.append(_mm_x3(gh_ref[0, rows, col:col + N], M) + gh_ref[0, rows, col + N:col + 2 * N])
        return tuple(new_states)

    lax.fori_loop(0, n_chunks, step, tuple(jnp.zeros((N, N), F32) for _ in range(4)))

    o = of_s[...] + ob_s[...]
    segm = segm_ref[...]
    mu = _mm_exact_rhs(o, segm)
    oc = o - mu
    var = _mm_exact_rhs(oc * oc, segm)
    o = oc * lax.rsqrt(var + LN_X_EPS) * lnw_ref[...] + lnb_ref[...]
    y_ref[0] = (o + bonus_ref[0]) * g_ref[0]


def _rwkv_scan(qq, gh, bonus, g, lnw, lnb, segm):
    B, S, _ = qq.shape
    pair = 2 * HEAD_DIM
    wide = 2 * 2 * pair
    return pl.pallas_call(
        _rwkv_scan_kernel,
        grid=(B, RWKV_HEADS // 2),
        in_specs=[pl.BlockSpec((1, S, wide), lambda b, p: (b, 0, p)),
                  pl.BlockSpec((1, S, wide), lambda b, p: (b, 0, p)),
                  pl.BlockSpec((1, S, pair), lambda b, p: (b, 0, p)),
                  pl.BlockSpec((1, S, pair), lambda b, p: (b, 0, p)),
                  pl.BlockSpec((1, pair), lambda b, p: (0, p)),
                  pl.BlockSpec((1, pair), lambda b, p: (0, p)),
                  _const_spec(segm.shape)],
        out_specs=pl.BlockSpec((1, S, pair), lambda b, p: (b, 0, p)),
        out_shape=jax.ShapeDtypeStruct((B, S, RWKV_WIDTH), F32),
        scratch_shapes=[pltpu.VMEM((S, pair), F32), pltpu.VMEM((S, pair), F32)],
        compiler_params=_params("parallel", "parallel"),
        name="rwkv_scan",
    )(qq, gh, bonus, g, lnw, lnb, segm)


def _rope_tiles(pos, freq, sign):
    ang = pos.astype(F32) * freq
    lane = lax.broadcasted_iota(jnp.int32, ang.shape, 1)
    cos_t = jnp.where(lane < QK_NOPE_DIM, 1.0, jnp.where(lane < QK_NOPE_DIM + QK_ROPE_DIM, jnp.cos(ang), 0.0))
    sin_t = jnp.sin(ang) * sign
    return cos_t, sin_t


def _mla_kernel(zq_ref, zkv_ref, pos_ref, freq_ref, sign_ref, qn_ref, wq_ref, wqs_ref, kvn_ref, wkv_ref,
                y_ref, k_s, v_s):
    i = pl.program_id(1)
    tq = zq_ref.shape[1]
    scale = (QK_NOPE_DIM + QK_ROPE_DIM) ** -0.5
    c_kv_lo, c_kv_hi = Q_LORA_RANK, Q_LORA_RANK + KV_LORA_RANK

    @pl.when(i == 0)
    def _():
        zkv = zkv_ref[0]
        kvn = _rms(zkv[:, c_kv_lo:c_kv_hi], kvn_ref[...]).astype(BF16)
        kvu = jnp.dot(kvn, wkv_ref[...], preferred_element_type=F32)
        cos_t, sin_t = _rope_tiles(pos_ref[0], freq_ref[...], sign_ref[...])
        k_rot = zkv[:, c_kv_hi:c_kv_hi + LANES] * cos_t + zkv[:, c_kv_hi + LANES:] * sin_t
        for h in range(MLA_HEADS):
            k_s[h] = (kvu[:, h * LANES:(h + 1) * LANES] + k_rot).astype(BF16)
        v_s[...] = kvu[:, MLA_HEADS * LANES:].astype(BF16)

    zq = zq_ref[0]
    qn = _rms(zq[:, :Q_LORA_RANK], qn_ref[...]).astype(BF16)
    q_a = jnp.dot(qn, wq_ref[...], preferred_element_type=F32)
    q_b = jnp.dot(qn, wqs_ref[...], preferred_element_type=F32)
    pos_q = pos_ref[0, pl.ds(pl.multiple_of(i * tq, tq), tq), :]
    cos_q, sin_q = _rope_tiles(pos_q, freq_ref[...], sign_ref[...])
    cos_q = cos_q * scale
    sin_q = sin_q * scale
    for h in range(MLA_HEADS):
        sl = slice(h * LANES, (h + 1) * LANES)
        qh = (q_a[:, sl] * cos_q + q_b[:, sl] * sin_q).astype(BF16)
        s = lax.dot_general(qh, k_s[h], (((1,), (1,)), ((), ())), preferred_element_type=F32)
        p = jnp.exp(s - jnp.max(s, axis=-1, keepdims=True))
        denom = jnp.sum(p, axis=-1, keepdims=True)
        o = jnp.dot(p.astype(BF16), v_s[:, h * V_HEAD_DIM:(h + 1) * V_HEAD_DIM], preferred_element_type=F32)
        y_ref[0, :, h * V_HEAD_DIM:(h + 1) * V_HEAD_DIM] = o / denom


def _mla_attn(z_mla, pos3, freq, sign, q_norm, wq_p, wq_sw, kv_norm, wkv_p, tq):
    B, S, ZC = z_mla.shape
    return pl.pallas_call(
        _mla_kernel,
        grid=(B, S // tq),
        in_specs=[pl.BlockSpec((1, tq, ZC), lambda b, i: (b, i, 0)),
                  pl.BlockSpec((1, S, ZC), lambda b, i: (b, 0, 0)),
                  pl.BlockSpec((1, S, 1), lambda b, i: (b, 0, 0)),
                  _const_spec(freq.shape), _const_spec(sign.shape), _const_spec(q_norm.shape),
                  _const_spec(wq_p.shape), _const_spec(wq_sw.shape), _const_spec(kv_norm.shape),
                  _const_spec(wkv_p.shape)],
        out_specs=pl.BlockSpec((1, tq, MLA_WIDTH), lambda b, i: (b, i, 0)),
        out_shape=jax.ShapeDtypeStruct((B, S, MLA_WIDTH), F32),
        scratch_shapes=[pltpu.VMEM((MLA_HEADS, S, LANES), BF16), pltpu.VMEM((S, MLA_WIDTH), BF16)],
        compiler_params=_params("parallel", "arbitrary"),
        name="mla_attn",
    )(z_mla, z_mla, pos3, freq, sign, q_norm, wq_p, wq_sw, kv_norm, wkv_p)


def _mix_out_kernel(x_ref, yr_ref, ym_ref, wr_ref, wm_ref, g_ref, o_ref):
    y = (jnp.dot(yr_ref[0].astype(BF16), wr_ref[...], preferred_element_type=F32)
         + jnp.dot(ym_ref[0].astype(BF16), wm_ref[...], preferred_element_type=F32))
    o_ref[0] = x_ref[0] + _rms(y, g_ref[...])


def _mix_out(x, y_rwkv, y_mla, w_r, w_m, g, tm):
    B, S, D = x.shape
    row_spec = lambda cols: pl.BlockSpec((1, tm, cols), lambda b, i: (b, i, 0))
    return pl.pallas_call(
        _mix_out_kernel,
        grid=(B, S // tm),
        in_specs=[row_spec(D), row_spec(RWKV_WIDTH), row_spec(MLA_WIDTH),
                  _const_spec(w_r.shape), _const_spec(w_m.shape), _const_spec(g.shape)],
        out_specs=row_spec(D),
        out_shape=jax.ShapeDtypeStruct((B, S, D), F32),
        compiler_params=_params("parallel", "parallel"),
        name="mix_out",
    )(x, y_rwkv, y_mla, w_r, w_m, g)


def _mem_kv_kernel(mem_ref, g_ref, w_ref, k_ref, v_ref):
    m = _rms(mem_ref[0], g_ref[...]).astype(BF16)
    kv = jnp.dot(m, w_ref[...], preferred_element_type=F32)
    k_ref[0] = kv[:, :D_MODEL].astype(BF16)
    v_ref[0] = kv[:, D_MODEL:].astype(BF16)


def _mem_kv(mem, g, wkv):
    B, T, D = mem.shape
    return pl.pallas_call(
        _mem_kv_kernel,
        grid=(B,),
        in_specs=[pl.BlockSpec((1, T, D), lambda b: (b, 0, 0)), _const_spec(g.shape), _const_spec(wkv.shape)],
        out_specs=[pl.BlockSpec((1, T, D), lambda b: (b, 0, 0)), pl.BlockSpec((1, T, D), lambda b: (b, 0, 0))],
        out_shape=[jax.ShapeDtypeStruct((B, T, D), BF16), jax.ShapeDtypeStruct((B, T, D), BF16)],
        compiler_params=_params("parallel"),
        name="mem_kv",
    )(mem, g, wkv)


def _mem_attn_kernel(x_ref, k_ref, v_ref, gpre_ref, wq_ref, wo_ref, gpost_ref, o_ref, att_s):
    x = x_ref[0]
    h = _rms(x, gpre_ref[...]).astype(BF16)
    q = jnp.dot(h, wq_ref[...], preferred_element_type=F32) * (MEM_HEAD_DIM ** -0.5)
    for hd in range(MEM_HEADS):
        sl = slice(hd * MEM_HEAD_DIM, (hd + 1) * MEM_HEAD_DIM)
        s = lax.dot_general(q[:, sl].astype(BF16), k_ref[0, :, sl], (((1,), (1,)), ((), ())),
                            preferred_element_type=F32)
        p = jnp.exp(s - jnp.max(s, axis=-1, keepdims=True))
        denom = jnp.sum(p, axis=-1, keepdims=True)
        o = jnp.dot(p.astype(BF16), v_ref[0, :, sl], preferred_element_type=F32)
        att_s[:, sl] = (o / denom).astype(BF16)
    y = jnp.dot(att_s[...], wo_ref[...], preferred_element_type=F32)
    o_ref[0] = x + _rms(y, gpost_ref[...])


def _mem_attn(x, k, v, g_pre, wq, wo, g_post, tm):
    B, S, D = x.shape
    T = k.shape[1]
    row_spec = pl.BlockSpec((1, tm, D), lambda b, i: (b, i, 0))
    kv_spec = pl.BlockSpec((1, T, D), lambda b, i: (b, 0, 0))
    return pl.pallas_call(
        _mem_attn_kernel,
        grid=(B, S // tm),
        in_specs=[row_spec, kv_spec, kv_spec, _const_spec(g_pre.shape), _const_spec(wq.shape),
                  _const_spec(wo.shape), _const_spec(g_post.shape)],
        out_specs=row_spec,
        out_shape=jax.ShapeDtypeStruct((B, S, D), F32),
        scratch_shapes=[pltpu.VMEM((tm, D), BF16)],
        compiler_params=_params("parallel", "parallel"),
        name="mem_attn",
    )(x, k, v, g_pre, wq, wo, g_post)


def _mlp_kernel(x_ref, gpre_ref, w1_ref, w2_ref, gpost_ref, o_ref):
    x = x_ref[0]
    h = _rms(x, gpre_ref[...]).astype(BF16)
    u = jnp.maximum(jnp.dot(h, w1_ref[...], preferred_element_type=F32), 0.0)
    y = jnp.dot((u * u).astype(BF16), w2_ref[...], preferred_element_type=F32)
    o_ref[0] = x + _rms(y, gpost_ref[...])


def _mlp(x, g_pre, w1, w2, g_post, tm):
    B, S, D = x.shape
    row_spec = pl.BlockSpec((1, tm, D), lambda b, i: (b, i, 0))
    return pl.pallas_call(
        _mlp_kernel,
        grid=(B, S // tm),
        in_specs=[row_spec, _const_spec(g_pre.shape), _const_spec(w1.shape), _const_spec(w2.shape),
                  _const_spec(g_post.shape)],
        out_specs=row_spec,
        out_shape=jax.ShapeDtypeStruct((B, S, D), F32),
        compiler_params=_params("parallel", "parallel"),
        name="mlp",
    )(x, g_pre, w1, w2, g_post)


def _pad_cols(w, n):
    return jnp.pad(w, ((0, 0), (0, n - w.shape[1])))


def _rope_tile(w):
    return jnp.pad(w, ((0, 0), (QK_NOPE_DIM, LANES - QK_NOPE_DIM - QK_ROPE_DIM)))


def _swap_halves(w):
    half = QK_ROPE_DIM // 2
    return jnp.concatenate([w[:, half:], w[:, :half]], axis=1)


def _lora_rows(w, lo, rows):
    return jnp.pad(w, ((lo, LORA_TILE - lo - rows), (0, 0)))


def kernel(x, mem, positions, norm_mix_pre, w_in, conv_rwkv, rwkv_w0, rwkv_w2, rwkv_a0, rwkv_a2, rwkv_g2, rwkv_k_k, rwkv_k_a, rwkv_r_k, rwkv_lnx_w, rwkv_lnx_b, mla_q_norm, mla_w_uq, mla_kv_norm, mla_w_ukv, w_out, norm_mix_post, norm_mem_pre, norm_memtok, mem_wq, mem_wkv, mem_wo, norm_mem_post, norm_mlp_pre, mlp_w1, mlp_w2, norm_mlp_post):
    depth = w_in.shape[0]
    C = RWKV_WIDTH
    head_of = jnp.arange(C) // HEAD_DIM
    seg = (head_of[:, None] == head_of[None, :]).astype(BF16)
    seg_mean = (head_of[:2 * HEAD_DIM, None] == head_of[None, :2 * HEAD_DIM]).astype(F32) / HEAD_DIM
    inv_freq = ROPE_THETA ** (-jnp.arange(0, QK_ROPE_DIM, 2, dtype=F32) / QK_ROPE_DIM)
    half = QK_ROPE_DIM // 2
    freq = _rope_tile(jnp.concatenate([inv_freq, inv_freq])[None, :])
    sign = _rope_tile(jnp.concatenate([-jnp.ones((half,), F32), jnp.ones((half,), F32)])[None, :])
    pos3 = positions[:, :, None]
    step = jnp.arange(CHUNK)
    tri = jnp.stack([step[None, :] <= step[:, None], step[None, :] >= step[:, None]]).astype(F32)
    row = lambda t: t.reshape(1, -1)

    for l in range(depth):
        w = w_in[l]
        mla0 = RWKV_COLS
        w_rope = w[:, mla0 + Q_LORA_RANK + KV_LORA_RANK:]
        w_r = _pad_cols(w[:, :RWKV_COLS], RWKV_TILE_COLS).astype(BF16)
        w_m = jnp.concatenate([w[:, mla0:mla0 + Q_LORA_RANK + KV_LORA_RANK], _rope_tile(w_rope),
                               _rope_tile(_swap_halves(w_rope))], axis=1).astype(BF16)
        z_rwkv, z_mla = _in_proj(x, row(norm_mix_pre[l]), w_r, w_m, tm=512)

        conv_p = _pad_cols(conv_rwkv[l], RWKV_TILE_COLS)
        w2_p = jnp.stack([_lora_rows(rwkv_w2[l, d], d * DECAY_LORA, DECAY_LORA) for d in range(2)])
        a_lo = 2 * DECAY_LORA
        a2_p = jnp.stack([_lora_rows(rwkv_a2[l, d], a_lo + d * ICLR_LORA, ICLR_LORA) for d in range(2)])
        g2_p = _lora_rows(rwkv_g2[l], a_lo + 2 * ICLR_LORA, GATE_LORA)
        qq, gh, bonus, gate = _rwkv_chunk(z_rwkv, conv_p, rwkv_w0[l], w2_p, rwkv_a0[l], a2_p, g2_p,
                                          row(rwkv_k_k[l]), row(rwkv_k_a[l]), row(rwkv_r_k[l]), seg, tri, tm=256)
        y_rwkv = _rwkv_scan(qq, gh, bonus, gate, row(rwkv_lnx_w[l]), row(rwkv_lnx_b[l]), seg_mean)

        qk = QK_NOPE_DIM + QK_ROPE_DIM
        w_uq = mla_w_uq[l].reshape(Q_LORA_RANK, MLA_HEADS, qk)
        wq_p = jnp.pad(w_uq, ((0, 0), (0, 0), (0, LANES - qk))).reshape(Q_LORA_RANK, MLA_HEADS * LANES)
        uq_rope = w_uq[:, :, QK_NOPE_DIM:]
        uq_sw = jnp.concatenate([uq_rope[:, :, half:], uq_rope[:, :, :half]], axis=2)
        wq_sw = jnp.pad(uq_sw, ((0, 0), (0, 0), (QK_NOPE_DIM, LANES - qk))).reshape(Q_LORA_RANK, MLA_HEADS * LANES)
        w_ukv = mla_w_ukv[l].reshape(KV_LORA_RANK, MLA_HEADS, QK_NOPE_DIM + V_HEAD_DIM)
        wk_p = jnp.pad(w_ukv[:, :, :QK_NOPE_DIM], ((0, 0), (0, 0), (0, LANES - QK_NOPE_DIM)))
        wkv_p = jnp.concatenate([wk_p.reshape(KV_LORA_RANK, MLA_HEADS * LANES),
                                 w_ukv[:, :, QK_NOPE_DIM:].reshape(KV_LORA_RANK, MLA_WIDTH)], axis=1)
        y_mla = _mla_attn(z_mla, pos3, freq, sign, row(mla_q_norm[l]), wq_p.astype(BF16), wq_sw.astype(BF16),
                          row(mla_kv_norm[l]), wkv_p.astype(BF16), tq=256)

        wo = w_out[l].astype(BF16)
        x = _mix_out(x, y_rwkv, y_mla, wo[:C], wo[C:], row(norm_mix_post[l]), tm=512)

        k_mem, v_mem = _mem_kv(mem, row(norm_memtok[l]), mem_wkv[l].astype(BF16))
        x = _mem_attn(x, k_mem, v_mem, row(norm_mem_pre[l]), mem_wq[l].astype(BF16), mem_wo[l].astype(BF16),
                      row(norm_mem_post[l]), tm=512)

        x = _mlp(x, row(norm_mlp_pre[l]), mlp_w1[l].astype(BF16), mlp_w2[l].astype(BF16),
                 row(norm_mlp_post[l]), tm=256)
    return x
```

```python
import functools
import math

import jax
import jax.numpy as jnp
from jax import lax
from jax.experimental import pallas as pl
from jax.experimental.pallas import tpu as pltpu

F32 = jnp.float32
BF16 = jnp.bfloat16

D_MODEL = 1024
NORM_EPS = 1e-6

RWKV_HEADS = 8
HEAD_DIM = 64
RWKV_WIDTH = RWKV_HEADS * HEAD_DIM
DECAY_LORA = 32
ICLR_LORA = 32
GATE_LORA = 96
LORA_COLS = 2 * DECAY_LORA + 2 * ICLR_LORA + GATE_LORA
LORA_TILE = 256
RWKV_COLS = 3 * RWKV_WIDTH + LORA_COLS
RWKV_TILE_COLS = 3 * RWKV_WIDTH + LORA_TILE
LN_X_EPS = 64e-5
CHUNK = 64
DECAY_SCALE = math.exp(-0.5)

MLA_HEADS = 8
QK_NOPE_DIM = 64
QK_ROPE_DIM = 32
V_HEAD_DIM = 64
MLA_WIDTH = MLA_HEADS * V_HEAD_DIM
Q_LORA_RANK = 256
KV_LORA_RANK = 128
ROPE_THETA = 10000.0
LANES = 128
MLA_TILE_COLS = Q_LORA_RANK + KV_LORA_RANK + 2 * LANES

MEM_HEADS = 4
MEM_HEAD_DIM = D_MODEL // MEM_HEADS
D_FF = 4 * D_MODEL

VMEM_LIMIT = 56 * 1024 * 1024


def _mm(a, b):
    return jnp.dot(a.astype(BF16), b.astype(BF16), preferred_element_type=F32)


def _split2(t):
    hi = t.astype(BF16)
    lo = (t - hi.astype(F32)).astype(BF16)
    return hi, lo


def _split3(t):
    hi = t.astype(BF16)
    rest = t - hi.astype(F32)
    mid = rest.astype(BF16)
    lo = (rest - mid.astype(F32)).astype(BF16)
    return hi, mid, lo


def _dot(a, b):
    return jnp.dot(a, b, preferred_element_type=F32)


def _mm_x3(a, b):
    a_hi, a_lo = _split2(a)
    b_hi, b_lo = _split2(b)
    return _dot(a_hi, b_hi) + (_dot(a_hi, b_lo) + _dot(a_lo, b_hi))


def _mm_exact_lhs(a, b):
    a = a.astype(BF16)
    hi, mid, lo = _split3(b)
    return _dot(a, hi) + (_dot(a, mid) + _dot(a, lo))


def _mm_exact_rhs(a, b):
    b = b.astype(BF16)
    hi, mid, lo = _split3(a)
    return _dot(hi, b) + (_dot(mid, b) + _dot(lo, b))


def _mm_nt(a, b):
    return lax.dot_general(a.astype(BF16), b.astype(BF16), (((1,), (1,)), ((), ())),
                           preferred_element_type=F32)


def _mm_tn(a, b):
    return lax.dot_general(a.astype(BF16), b.astype(BF16), (((0,), (0,)), ((), ())),
                           preferred_element_type=F32)


def _rms(x, g, eps=NORM_EPS):
    return x * lax.rsqrt(jnp.mean(x * x, axis=-1, keepdims=True) + eps) * g


def _sigmoid(x):
    return 1.0 / (1.0 + jnp.exp(-x))


def _params(*sem):
    return pltpu.CompilerParams(dimension_semantics=sem, vmem_limit_bytes=VMEM_LIMIT)


def _const_spec(shape):
    nd = len(shape)
    return pl.BlockSpec(shape, lambda *_: (0,) * nd)


def _in_proj_kernel(x_ref, g_ref, wr_ref, wm_ref, zr_ref, zm_ref):
    h = _rms(x_ref[0], g_ref[...]).astype(BF16)
    zr_ref[0] = jnp.dot(h, wr_ref[...], preferred_element_type=F32)
    zm_ref[0] = jnp.dot(h, wm_ref[...], preferred_element_type=F32)


def _in_proj(x, g, w_r, w_m, tm):
    B, S, D = x.shape
    return pl.pallas_call(
        _in_proj_kernel,
        grid=(B, S // tm),
        in_specs=[pl.BlockSpec((1, tm, D), lambda b, i: (b, i, 0)),
                  _const_spec((1, D)),
                  _const_spec(w_r.shape),
                  _const_spec(w_m.shape)],
        out_specs=[pl.BlockSpec((1, tm, RWKV_TILE_COLS), lambda b, i: (b, i, 0)),
                   pl.BlockSpec((1, tm, MLA_TILE_COLS), lambda b, i: (b, i, 0))],
        out_shape=[jax.ShapeDtypeStruct((B, S, RWKV_TILE_COLS), F32),
                   jax.ShapeDtypeStruct((B, S, MLA_TILE_COLS), F32)],
        compiler_params=_params("parallel", "parallel"),
        name="in_proj",
    )(x, g, w_r, w_m)


def _chunk_units(units, eye):
    L, N = units[0][0].shape
    zeros = jnp.zeros((L, N), F32)
    AA = [_mm_nt(jnp.concatenate([At, Rt], axis=0), jnp.concatenate([Bt, Kt], axis=0))
          for (At, Rt, Bt, Kt, *_) in units]
    Ak = [jnp.where(u[8], aa[:L, :L], 0.0) for u, aa in zip(units, AA)]
    AkV = [_mm(jnp.where(u[8], aa[:L, L:], 0.0), u[6]) for u, aa in zip(units, AA)]
    A_r = [jnp.where(u[9], aa[L:, :], 0.0) for u, aa in zip(units, AA)]
    W = [jnp.concatenate([u[0], akv], axis=1) for u, akv in zip(units, AkV)]
    W = [w + _mm(a, w) for a, w in zip(Ak, W)]
    span = 2
    while span < L:
        Ak = [_mm(a, a) for a in Ak]
        W = [w + _mm(a, w) for a, w in zip(Ak, W)]
        span *= 2
    Z = [jnp.concatenate([w, jnp.concatenate([zeros, u[6]], axis=1)], axis=0) for u, w in zip(units, W)]
    Q = [jnp.concatenate([u[1], zeros], axis=1) + _mm(a_r, z) for u, a_r, z in zip(units, A_r, Z)]
    GH = [jnp.concatenate([eye * u[7], zeros], axis=1) + _mm_tn(jnp.concatenate([u[4], u[5]], axis=0), z)
          for u, z in zip(units, Z)]
    return Q, GH


def _rwkv_chunk_kernel(z_ref, zp_ref, zn_ref, conv_ref, w0_ref, w2_ref, a0_ref, a2_ref, g2_ref,
                       kk_ref, ka_ref, rk_ref, seg_ref, tri_ref,
                       qq_ref, gh_ref, bonus_ref, g_ref,
                       r_s, v_s, kk_s, b_s, kd_s, lw_s):
    i = pl.program_id(1)
    last = pl.num_programs(1) - 1
    tm = z_ref.shape[1]
    L = CHUNK

    z = z_ref[0]
    prev_row = jnp.where(i == 0, 0.0, zp_ref[0, 7:8, :])
    next_row = jnp.where(i == last, 0.0, zn_ref[0, 0:1, :])
    row = lax.broadcasted_iota(jnp.int32, (tm, 1), 0)
    z_dn = jnp.where(row == 0, prev_row, pltpu.roll(z, 1, 0))
    z_up = jnp.where(row == tm - 1, next_row, pltpu.roll(z, tm - 1, 0))
    zc = conv_ref[0:1, :] * z_dn + conv_ref[1:2, :] * z + conv_ref[2:3, :] * z_up

    r = zc[:, 0:RWKV_WIDTH]
    k = zc[:, RWKV_WIDTH:2 * RWKV_WIDTH]
    v = zc[:, 2 * RWKV_WIDTH:3 * RWKV_WIDTH]
    lora = zc[:, 3 * RWKV_WIDTH:]
    seg = seg_ref[...]

    def seg_sum(t):
        return _mm_exact_rhs(t, seg)

    kk = k * kk_ref[...]
    kk = kk * lax.rsqrt(jnp.maximum(seg_sum(kk * kk), 1e-24))
    tanh_lora = jnp.tanh(lora)
    kd_sum = jnp.zeros_like(k)
    for d in range(2):
        w_pre = w0_ref[d:d + 1, :] + _mm_x3(tanh_lora, w2_ref[d])
        lw_s[d] = -DECAY_SCALE * _sigmoid(w_pre)
        alpha = _sigmoid(a0_ref[d:d + 1, :] + _mm_x3(lora, a2_ref[d]))
        kd = k * (1.0 + (alpha - 1.0) * ka_ref[...])
        kd_s[d] = kd
        b_s[d] = kk * alpha
        kd_sum = kd_sum + kd
    r_s[...] = r
    v_s[...] = v
    kk_s[...] = kk
    g_ref[0] = _mm_x3(_sigmoid(lora), g2_ref[...])
    bonus_ref[0] = seg_sum(r * kd_sum * rk_ref[...]) * v

    ri = lax.broadcasted_iota(jnp.int32, (L, L), 0)
    ci = lax.broadcasted_iota(jnp.int32, (L, L), 1)
    eye = (ci == ri).astype(F32)
    m_strict = (ci < ri, ci > ri)
    ri2 = lax.broadcasted_iota(jnp.int32, (L, 2 * L), 0)
    ci2 = lax.broadcasted_iota(jnp.int32, (L, 2 * L), 1)
    ci2 = jnp.where(ci2 >= L, ci2 - L, ci2)
    m_incl = (ci2 <= ri2, ci2 >= ri2)

    def chunk_body(c, carry):
        rows = pl.ds(pl.multiple_of(c * L, L), L)
        rc = r_s[rows, :]
        vc = v_s[rows, :]
        kkc = kk_s[rows, :]
        for d in range(2):
            lw = lw_s[d, rows, :]
            bc = b_s[d, rows, :]
            kdc = kd_s[d, rows, :]
            cum = _mm_exact_lhs(tri_ref[d], lw)
            tot = cum[L - 1:L, :] if d == 0 else cum[0:1, :]
            e_in = jnp.exp(-cum)
            e_rem = jnp.exp(tot - cum)
            e_tot = jnp.exp(tot)
            At = -kkc * jnp.exp(cum - lw)
            Rt = rc * jnp.exp(cum)
            Bt = bc * e_in
            Kt = kdc * e_in
            Bh = bc * e_rem
            Kh = kdc * e_rem
            units = []
            for h in range(RWKV_HEADS):
                sl = slice(h * HEAD_DIM, (h + 1) * HEAD_DIM)
                units.append((At[:, sl], Rt[:, sl], Bt[:, sl], Kt[:, sl], Bh[:, sl], Kh[:, sl],
                              vc[:, sl], e_tot[:, sl], m_strict[d], m_incl[d]))
            Q, GH = _chunk_units(units, eye)
            for h in range(RWKV_HEADS):
                col = (2 * h + d) * 2 * HEAD_DIM
                qq_ref[0, rows, col:col + 2 * HEAD_DIM] = Q[h]
                gh_ref[0, rows, col:col + 2 * HEAD_DIM] = GH[h]
        return carry

    lax.fori_loop(0, tm // L, chunk_body, 0)


def _rwkv_chunk(z_rwkv, conv_p, w0, w2_p, a0, a2_p, g2_p, k_k, k_a, r_k, seg, tri, tm):
    B, S, ZC = z_rwkv.shape
    C = RWKV_WIDTH
    halo = tm // 8
    n_halo = S // 8
    wide = 2 * RWKV_HEADS * 2 * HEAD_DIM
    row_spec = lambda cols: pl.BlockSpec((1, tm, cols), lambda b, i: (b, i, 0))
    return pl.pallas_call(
        _rwkv_chunk_kernel,
        grid=(B, S // tm),
        in_specs=[row_spec(ZC),
                  pl.BlockSpec((1, 8, ZC), lambda b, i: (b, jnp.maximum(i * halo - 1, 0), 0)),
                  pl.BlockSpec((1, 8, ZC), lambda b, i: (b, jnp.minimum((i + 1) * halo, n_halo - 1), 0)),
                  _const_spec(conv_p.shape), _const_spec(w0.shape), _const_spec(w2_p.shape),
                  _const_spec(a0.shape), _const_spec(a2_p.shape), _const_spec(g2_p.shape),
                  _const_spec(k_k.shape), _const_spec(k_a.shape), _const_spec(r_k.shape),
                  _const_spec(seg.shape), _const_spec(tri.shape)],
        out_specs=[row_spec(wide), row_spec(wide), row_spec(C), row_spec(C)],
        out_shape=[jax.ShapeDtypeStruct((B, S, wide), F32), jax.ShapeDtypeStruct((B, S, wide), F32),
                   jax.ShapeDtypeStruct((B, S, C), F32), jax.ShapeDtypeStruct((B, S, C), F32)],
        scratch_shapes=[pltpu.VMEM((tm, C), F32), pltpu.VMEM((tm, C), F32), pltpu.VMEM((tm, C), F32),
                        pltpu.VMEM((2, tm, C), F32), pltpu.VMEM((2, tm, C), F32), pltpu.VMEM((2, tm, C), F32)],
        compiler_params=_params("parallel", "parallel"),
        name="rwkv_chunk",
    )(z_rwkv, z_rwkv, z_rwkv, conv_p, w0, w2_p, a0, a2_p, g2_p, k_k, k_a, r_k, seg, tri)


def _rwkv_scan_kernel(qq_ref, gh_ref, bonus_ref, g_ref, lnw_ref, lnb_ref, segm_ref, y_ref, of_s, ob_s):
    S = qq_ref.shape[1]
    L, N = CHUNK, HEAD_DIM
    n_chunks = S // L
    out_s = (of_s, ob_s)

    def step(c, states):
        new_states = []
        for hh in range(2):
            for d in range(2):
                cc = c if d == 0 else n_chunks - 1 - c
                rows = pl.ds(pl.multiple_of(cc * L, L), L)
                col = (2 * hh + d) * 2 * N
                M = states[2 * hh + d]
                out = _mm_x3(qq_ref[0, rows, col:col + N], M) + qq_ref[0, rows, col + N:col + 2 * N]
                out_s[d][rows, hh * N:(hh + 1) * N] = out
                new_states.append(_mm_x3(gh_ref[0, rows, col:col + N], M) + gh_ref[0, rows, col + N:col + 2 * N])
        return tuple(new_states)

    lax.fori_loop(0, n_chunks, step, tuple(jnp.zeros((N, N), F32) for _ in range(4)))

    o = of_s[...] + ob_s[...]
    segm = segm_ref[...]
    mu = _mm_exact_rhs(o, segm)
    oc = o - mu
    var = _mm_exact_rhs(oc * oc, segm)
    o = oc * lax.rsqrt(var + LN_X_EPS) * lnw_ref[...] + lnb_ref[...]
    y_ref[0] = (o + bonus_ref[0]) * g_ref[0]


def _rwkv_scan(qq, gh, bonus, g, lnw, lnb, segm):
    B, S, _ = qq.shape
    pair = 2 * HEAD_DIM
    wide = 2 * 2 * pair
    return pl.pallas_call(
        _rwkv_scan_kernel,
        grid=(B, RWKV_HEADS // 2),
        in_specs=[pl.BlockSpec((1, S, wide), lambda b, p: (b, 0, p)),
                  pl.BlockSpec((1, S, wide), lambda b, p: (b, 0, p)),
                  pl.BlockSpec((1, S, pair), lambda b, p: (b, 0, p)),
                  pl.BlockSpec((1, S, pair), lambda b, p: (b, 0, p)),
                  pl.BlockSpec((1, pair), lambda b, p: (0, p)),
                  pl.BlockSpec((1, pair), lambda b, p: (0, p)),
                  _const_spec(segm.shape)],
        out_specs=pl.BlockSpec((1, S, pair), lambda b, p: (b, 0, p)),
        out_shape=jax.ShapeDtypeStruct((B, S, RWKV_WIDTH), F32),
        scratch_shapes=[pltpu.VMEM((S, pair), F32), pltpu.VMEM((S, pair), F32)],
        compiler_params=_params("parallel", "parallel"),
        name="rwkv_scan",
    )(qq, gh, bonus, g, lnw, lnb, segm)


def _rope_tiles(pos, freq, sign):
    ang = pos.astype(F32) * freq
    lane = lax.broadcasted_iota(jnp.int32, ang.shape, 1)
    cos_t = jnp.where(lane < QK_NOPE_DIM, 1.0, jnp.where(lane < QK_NOPE_DIM + QK_ROPE_DIM, jnp.cos(ang), 0.0))
    sin_t = jnp.sin(ang) * sign
    return cos_t, sin_t


def _mla_kernel(zq_ref, zkv_ref, pos_ref, freq_ref, sign_ref, qn_ref, wq_ref, wqs_ref, kvn_ref, wkv_ref,
                y_ref, k_s, v_s):
    i = pl.program_id(1)
    tq = zq_ref.shape[1]
    scale = (QK_NOPE_DIM + QK_ROPE_DIM) ** -0.5
    c_kv_lo, c_kv_hi = Q_LORA_RANK, Q_LORA_RANK + KV_LORA_RANK

    @pl.when(i == 0)
    def _():
        zkv = zkv_ref[0]
        kvn = _rms(zkv[:, c_kv_lo:c_kv_hi], kvn_ref[...]).astype(BF16)
        kvu = jnp.dot(kvn, wkv_ref[...], preferred_element_type=F32)
        cos_t, sin_t = _rope_tiles(pos_ref[0], freq_ref[...], sign_ref[...])
        k_rot = zkv[:, c_kv_hi:c_kv_hi + LANES] * cos_t + zkv[:, c_kv_hi + LANES:] * sin_t
        for h in range(MLA_HEADS):
            k_s[h] = (kvu[:, h * LANES:(h + 1) * LANES] + k_rot).astype(BF16)
        v_s[...] = kvu[:, MLA_HEADS * LANES:].astype(BF16)

    zq = zq_ref[0]
    qn = _rms(zq[:, :Q_LORA_RANK], qn_ref[...]).astype(BF16)
    q_a = jnp.dot(qn, wq_ref[...], preferred_element_type=F32)
    q_b = jnp.dot(qn, wqs_ref[...], preferred_element_type=F32)
    pos_q = pos_ref[0, pl.ds(pl.multiple_of(i * tq, tq), tq), :]
    cos_q, sin_q = _rope_tiles(pos_q, freq_ref[...], sign_ref[...])
    cos_q = cos_q * scale
    sin_q = sin_q * scale
    for h in range(MLA_HEADS):
        sl = slice(h * LANES, (h + 1) * LANES)
        qh = (q_a[:, sl] * cos_q + q_b[:, sl] * sin_q).astype(BF16)
        s = lax.dot_general(qh, k_s[h], (((1,), (1,)), ((), ())), preferred_element_type=F32)
        p = jnp.exp(s - jnp.max(s, axis=-1, keepdims=True))
        denom = jnp.sum(p, axis=-1, keepdims=True)
        o = jnp.dot(p.astype(BF16), v_s[:, h * V_HEAD_DIM:(h + 1) * V_HEAD_DIM], preferred_element_type=F32)
        y_ref[0, :, h * V_HEAD_DIM:(h + 1) * V_HEAD_DIM] = o / denom


def _mla_attn(z_mla, pos3, freq, sign, q_norm, wq_p, wq_sw, kv_norm, wkv_p, tq):
    B, S, ZC = z_mla.shape
    return pl.pallas_call(
        _mla_kernel,
        grid=(B, S // tq),
        in_specs=[pl.BlockSpec((1, tq, ZC), lambda b, i: (b, i, 0)),
                  pl.BlockSpec((1, S, ZC), lambda b, i: (b, 0, 0)),
                  pl.BlockSpec((1, S, 1), lambda b, i: (b, 0, 0)),
                  _const_spec(freq.shape), _const_spec(sign.shape), _const_spec(q_norm.shape),
                  _const_spec(wq_p.shape), _const_spec(wq_sw.shape), _const_spec(kv_norm.shape),
                  _const_spec(wkv_p.shape)],
        out_specs=pl.BlockSpec((1, tq, MLA_WIDTH), lambda b, i: (b, i, 0)),
        out_shape=jax.ShapeDtypeStruct((B, S, MLA_WIDTH), F32),
        scratch_shapes=[pltpu.VMEM((MLA_HEADS, S, LANES), BF16), pltpu.VMEM((S, MLA_WIDTH), BF16)],
        compiler_params=_params("parallel", "arbitrary"),
        name="mla_attn",
    )(z_mla, z_mla, pos3, freq, sign, q_norm, wq_p, wq_sw, kv_norm, wkv_p)


def _mix_out_kernel(x_ref, yr_ref, ym_ref, wr_ref, wm_ref, g_ref, o_ref):
    y = (jnp.dot(yr_ref[0].astype(BF16), wr_ref[...], preferred_element_type=F32)
         + jnp.dot(ym_ref[0].astype(BF16), wm_ref[...], preferred_element_type=F32))
    o_ref[0] = x_ref[0] + _rms(y, g_ref[...])


def _mix_out(x, y_rwkv, y_mla, w_r, w_m, g, tm):
    B, S, D = x.shape
    row_spec = lambda cols: pl.BlockSpec((1, tm, cols), lambda b, i: (b, i, 0))
    return pl.pallas_call(
        _mix_out_kernel,
        grid=(B, S // tm),
        in_specs=[row_spec(D), row_spec(RWKV_WIDTH), row_spec(MLA_WIDTH),
                  _const_spec(w_r.shape), _const_spec(w_m.shape), _const_spec(g.shape)],
        out_specs=row_spec(D),
        out_shape=jax.ShapeDtypeStruct((B, S, D), F32),
        compiler_params=_params("parallel", "parallel"),
        name="mix_out",
    )(x, y_rwkv, y_mla, w_r, w_m, g)


def _mem_kv_kernel(mem_ref, g_ref, w_ref, k_ref, v_ref):
    m = _rms(mem_ref[0], g_ref[...]).astype(BF16)
    kv = jnp.dot(m, w_ref[...], preferred_element_type=F32)
    k_ref[0] = kv[:, :D_MODEL].astype(BF16)
    v_ref[0] = kv[:, D_MODEL:].astype(BF16)


def _mem_kv(mem, g, wkv):
    B, T, D = mem.shape
    return pl.pallas_call(
        _mem_kv_kernel,
        grid=(B,),
        in_specs=[pl.BlockSpec((1, T, D), lambda b: (b, 0, 0)), _const_spec(g.shape), _const_spec(wkv.shape)],
        out_specs=[pl.BlockSpec((1, T, D), lambda b: (b, 0, 0)), pl.BlockSpec((1, T, D), lambda b: (b, 0, 0))],
        out_shape=[jax.ShapeDtypeStruct((B, T, D), BF16), jax.ShapeDtypeStruct((B, T, D), BF16)],
        compiler_params=_params("parallel"),
        name="mem_kv",
    )(mem, g, wkv)


def _mem_attn_kernel(x_ref, k_ref, v_ref, gpre_ref, wq_ref, wo_ref, gpost_ref, o_ref, att_s):
    x = x_ref[0]
    h = _rms(x, gpre_ref[...]).astype(BF16)
    q = jnp.dot(h, wq_ref[...], preferred_element_type=F32) * (MEM_HEAD_DIM ** -0.5)
    for hd in range(MEM_HEADS):
        sl = slice(hd * MEM_HEAD_DIM, (hd + 1) * MEM_HEAD_DIM)
        s = lax.dot_general(q[:, sl].astype(BF16), k_ref[0, :, sl], (((1,), (1,)), ((), ())),
                            preferred_element_type=F32)
        p = jnp.exp(s - jnp.max(s, axis=-1, keepdims=True))
        denom = jnp.sum(p, axis=-1, keepdims=True)
        o = jnp.dot(p.astype(BF16), v_ref[0, :, sl], preferred_element_type=F32)
        att_s[:, sl] = (o / denom).astype(BF16)
    y = jnp.dot(att_s[...], wo_ref[...], preferred_element_type=F32)
    o_ref[0] = x + _rms(y, gpost_ref[...])


def _mem_attn(x, k, v, g_pre, wq, wo, g_post, tm):
    B, S, D = x.shape
    T = k.shape[1]
    row_spec = pl.BlockSpec((1, tm, D), lambda b, i: (b, i, 0))
    kv_spec = pl.BlockSpec((1, T, D), lambda b, i: (b, 0, 0))
    return pl.pallas_call(
        _mem_attn_kernel,
        grid=(B, S // tm),
        in_specs=[row_spec, kv_spec, kv_spec, _const_spec(g_pre.shape), _const_spec(wq.shape),
                  _const_spec(wo.shape), _const_spec(g_post.shape)],
        out_specs=row_spec,
        out_shape=jax.ShapeDtypeStruct((B, S, D), F32),
        scratch_shapes=[pltpu.VMEM((tm, D), BF16)],
        compiler_params=_params("parallel", "parallel"),
        name="mem_attn",
    )(x, k, v, g_pre, wq, wo, g_post)


def _mlp_kernel(x_ref, gpre_ref, w1_ref, w2_ref, gpost_ref, o_ref):
    x = x_ref[0]
    h = _rms(x, gpre_ref[...]).astype(BF16)
    u = jnp.maximum(jnp.dot(h, w1_ref[...], preferred_element_type=F32), 0.0)
    y = jnp.dot((u * u).astype(BF16), w2_ref[...], preferred_element_type=F32)
    o_ref[0] = x + _rms(y, gpost_ref[...])


def _mlp(x, g_pre, w1, w2, g_post, tm):
    B, S, D = x.shape
    row_spec = pl.BlockSpec((1, tm, D), lambda b, i: (b, i, 0))
    return pl.pallas_call(
        _mlp_kernel,
        grid=(B, S // tm),
        in_specs=[row_spec, _const_spec(g_pre.shape), _const_spec(w1.shape), _const_spec(w2.shape),
                  _const_spec(g_post.shape)],
        out_specs=row_spec,
        out_shape=jax.ShapeDtypeStruct((B, S, D), F32),
        compiler_params=_params("parallel", "parallel"),
        name="mlp",
    )(x, g_pre, w1, w2, g_post)


def _pad_cols(w, n):
    return jnp.pad(w, ((0, 0), (0, n - w.shape[1])))


def _rope_tile(w):
    return jnp.pad(w, ((0, 0), (QK_NOPE_DIM, LANES - QK_NOPE_DIM - QK_ROPE_DIM)))


def _swap_halves(w):
    half = QK_ROPE_DIM // 2
    return jnp.concatenate([w[:, half:], w[:, :half]], axis=1)


def _lora_rows(w, lo, rows):
    return jnp.pad(w, ((lo, LORA_TILE - lo - rows), (0, 0)))


def kernel(x, mem, positions, norm_mix_pre, w_in, conv_rwkv, rwkv_w0, rwkv_w2, rwkv_a0, rwkv_a2, rwkv_g2, rwkv_k_k, rwkv_k_a, rwkv_r_k, rwkv_lnx_w, rwkv_lnx_b, mla_q_norm, mla_w_uq, mla_kv_norm, mla_w_ukv, w_out, norm_mix_post, norm_mem_pre, norm_memtok, mem_wq, mem_wkv, mem_wo, norm_mem_post, norm_mlp_pre, mlp_w1, mlp_w2, norm_mlp_post):
    depth = w_in.shape[0]
    C = RWKV_WIDTH
    head_of = jnp.arange(C) // HEAD_DIM
    seg = (head_of[:, None] == head_of[None, :]).astype(BF16)
    seg_mean = (head_of[:2 * HEAD_DIM, None] == head_of[None, :2 * HEAD_DIM]).astype(F32) / HEAD_DIM
    inv_freq = ROPE_THETA ** (-jnp.arange(0, QK_ROPE_DIM, 2, dtype=F32) / QK_ROPE_DIM)
    half = QK_ROPE_DIM // 2
    freq = _rope_tile(jnp.concatenate([inv_freq, inv_freq])[None, :])
    sign = _rope_tile(jnp.concatenate([-jnp.ones((half,), F32), jnp.ones((half,), F32)])[None, :])
    pos3 = positions[:, :, None]
    step = jnp.arange(CHUNK)
    tri = jnp.stack([step[None, :] <= step[:, None], step[None, :] >= step[:, None]]).astype(F32)
    row = lambda t: t.reshape(1, -1)

    for l in range(depth):
        w = w_in[l]
        mla0 = RWKV_COLS
        w_rope = w[:, mla0 + Q_LORA_RANK + KV_LORA_RANK:]
        w_r = _pad_cols(w[:, :RWKV_COLS], RWKV_TILE_COLS).astype(BF16)
        w_m = jnp.concatenate([w[:, mla0:mla0 + Q_LORA_RANK + KV_LORA_RANK], _rope_tile(w_rope),
                               _rope_tile(_swap_halves(w_rope))], axis=1).astype(BF16)
        z_rwkv, z_mla = _in_proj(x, row(norm_mix_pre[l]), w_r, w_m, tm=512)

        conv_p = _pad_cols(conv_rwkv[l], RWKV_TILE_COLS)
        w2_p = jnp.stack([_lora_rows(rwkv_w2[l, d], d * DECAY_LORA, DECAY_LORA) for d in range(2)])
        a_lo = 2 * DECAY_LORA
        a2_p = jnp.stack([_lora_rows(rwkv_a2[l, d], a_lo + d * ICLR_LORA, ICLR_LORA) for d in range(2)])
        g2_p = _lora_rows(rwkv_g2[l], a_lo + 2 * ICLR_LORA, GATE_LORA)
        qq, gh, bonus, gate = _rwkv_chunk(z_rwkv, conv_p, rwkv_w0[l], w2_p, rwkv_a0[l], a2_p, g2_p,
                                          row(rwkv_k_k[l]), row(rwkv_k_a[l]), row(rwkv_r_k[l]), seg, tri, tm=256)
        y_rwkv = _rwkv_scan(qq, gh, bonus, gate, row(rwkv_lnx_w[l]), row(rwkv_lnx_b[l]), seg_mean)

        qk = QK_NOPE_DIM + QK_ROPE_DIM
        w_uq = mla_w_uq[l].reshape(Q_LORA_RANK, MLA_HEADS, qk)
        wq_p = jnp.pad(w_uq, ((0, 0), (0, 0), (0, LANES - qk))).reshape(Q_LORA_RANK, MLA_HEADS * LANES)
        uq_rope = w_uq[:, :, QK_NOPE_DIM:]
        uq_sw = jnp.concatenate([uq_rope[:, :, half:], uq_rope[:, :, :half]], axis=2)
        wq_sw = jnp.pad(uq_sw, ((0, 0), (0, 0), (QK_NOPE_DIM, LANES - qk))).reshape(Q_LORA_RANK, MLA_HEADS * LANES)
        w_ukv = mla_w_ukv[l].reshape(KV_LORA_RANK, MLA_HEADS, QK_NOPE_DIM + V_HEAD_DIM)
        wk_p = jnp.pad(w_ukv[:, :, :QK_NOPE_DIM], ((0, 0), (0, 0), (0, LANES - QK_NOPE_DIM)))
        wkv_p = jnp.concatenate([wk_p.reshape(KV_LORA_RANK, MLA_HEADS * LANES),
                                 w_ukv[:, :, QK_NOPE_DIM:].reshape(KV_LORA_RANK, MLA_WIDTH)], axis=1)
        y_mla = _mla_attn(z_mla, pos3, freq, sign, row(mla_q_norm[l]), wq_p.astype(BF16), wq_sw.astype(BF16),
                          row(mla_kv_norm[l]), wkv_p.astype(BF16), tq=256)

        wo = w_out[l].astype(BF16)
        x = _mix_out(x, y_rwkv, y_mla, wo[:C], wo[C:], row(norm_mix_post[l]), tm=512)

        k_mem, v_mem = _mem_kv(mem, row(norm_memtok[l]), mem_wkv[l].astype(BF16))
        x = _mem_attn(x, k_mem, v_mem, row(norm_mem_pre[l]), mem_wq[l].astype(BF16), mem_wo[l].astype(BF16),
                      row(norm_mem_post[l]), tm=512)

        x = _mlp(x, row(norm_mlp_pre[l]), mlp_w1[l].astype(BF16), mlp_w2[l].astype(BF16),
                 row(norm_mlp_post[l]), tm=256)
    return x
```

```python
import functools
import math

import jax
import jax.numpy as jnp
from jax import lax
from jax.experimental import pallas as pl
from jax.experimental.pallas import tpu as pltpu

F32 = jnp.float32
BF16 = jnp.bfloat16

D_MODEL = 1024
NORM_EPS = 1e-6

RWKV_HEADS = 8
HEAD_DIM = 64
RWKV_WIDTH = RWKV_HEADS * HEAD_DIM
DECAY_LORA = 32
ICLR_LORA = 32
GATE_LORA = 96
LORA_COLS = 2 * DECAY_LORA + 2 * ICLR_LORA + GATE_LORA
LORA_TILE = 256
RWKV_COLS = 3 * RWKV_WIDTH + LORA_COLS
RWKV_TILE_COLS = 3 * RWKV_WIDTH + LORA_TILE
LN_X_EPS = 64e-5
CHUNK = 64
DECAY_SCALE = math.exp(-0.5)

MLA_HEADS = 8
QK_NOPE_DIM = 64
QK_ROPE_DIM = 32
V_HEAD_DIM = 64
MLA_WIDTH = MLA_HEADS * V_HEAD_DIM
Q_LORA_RANK = 256
KV_LORA_RANK = 128
ROPE_THETA = 10000.0
LANES = 128
MLA_TILE_COLS = Q_LORA_RANK + KV_LORA_RANK + 2 * LANES

MEM_HEADS = 4
MEM_HEAD_DIM = D_MODEL // MEM_HEADS
D_FF = 4 * D_MODEL

VMEM_LIMIT = 56 * 1024 * 1024


def _mm(a, b):
    return jnp.dot(a.astype(BF16), b.astype(BF16), preferred_element_type=F32)


def _split2(t):
    hi = t.astype(BF16)
    lo = (t - hi.astype(F32)).astype(BF16)
    return hi, lo


def _split3(t):
    hi = t.astype(BF16)
    rest = t - hi.astype(F32)
    mid = rest.astype(BF16)
    lo = (rest - mid.astype(F32)).astype(BF16)
    return hi, mid, lo


def _dot(a, b):
    return jnp.dot(a, b, preferred_element_type=F32)


def _mm_x3(a, b):
    a_hi, a_lo = _split2(a)
    b_hi, b_lo = _split2(b)
    return _dot(a_hi, b_hi) + (_dot(a_hi, b_lo) + _dot(a_lo, b_hi))


def _mm_exact_lhs(a, b):
    a = a.astype(BF16)
    hi, mid, lo = _split3(b)
    return _dot(a, hi) + (_dot(a, mid) + _dot(a, lo))


def _mm_exact_rhs(a, b):
    b = b.astype(BF16)
    hi, mid, lo = _split3(a)
    return _dot(hi, b) + (_dot(mid, b) + _dot(lo, b))


def _mm_nt(a, b):
    return lax.dot_general(a.astype(BF16), b.astype(BF16), (((1,), (1,)), ((), ())),
                           preferred_element_type=F32)


def _mm_tn(a, b):
    return lax.dot_general(a.astype(BF16), b.astype(BF16), (((0,), (0,)), ((), ())),
                           preferred_element_type=F32)


def _rms(x, g, eps=NORM_EPS):
    return x * lax.rsqrt(jnp.mean(x * x, axis=-1, keepdims=True) + eps) * g


def _sigmoid(x):
    return 0.5 * jnp.tanh(0.5 * x) + 0.5


def _params(*sem):
    return pltpu.CompilerParams(dimension_semantics=sem, vmem_limit_bytes=VMEM_LIMIT)


def _const_spec(shape):
    nd = len(shape)
    return pl.BlockSpec(shape, lambda *_: (0,) * nd)


def _in_proj_kernel(x_ref, g_ref, wr_ref, wm_ref, zr_ref, zm_ref):
    h = _rms(x_ref[0], g_ref[...]).astype(BF16)
    zr_ref[0] = jnp.dot(h, wr_ref[...], preferred_element_type=F32)
    zm_ref[0] = jnp.dot(h, wm_ref[...], preferred_element_type=F32)


def _in_proj(x, g, w_r, w_m, tm):
    B, S, D = x.shape
    return pl.pallas_call(
        _in_proj_kernel,
        grid=(B, S // tm),
        in_specs=[pl.BlockSpec((1, tm, D), lambda b, i: (b, i, 0)),
                  _const_spec((1, D)),
                  _const_spec(w_r.shape),
                  _const_spec(w_m.shape)],
        out_specs=[pl.BlockSpec((1, tm, RWKV_TILE_COLS), lambda b, i: (b, i, 0)),
                   pl.BlockSpec((1, tm, MLA_TILE_COLS), lambda b, i: (b, i, 0))],
        out_shape=[jax.ShapeDtypeStruct((B, S, RWKV_TILE_COLS), F32),
                   jax.ShapeDtypeStruct((B, S, MLA_TILE_COLS), F32)],
        compiler_params=_params("parallel", "parallel"),
        name="in_proj",
    )(x, g, w_r, w_m)


def _chunk_units(units, eye):
    L, N = units[0][0].shape
    zeros = jnp.zeros((L, N), F32)
    AA = [_mm_nt(jnp.concatenate([At, Rt], axis=0), jnp.concatenate([Bt, Kt], axis=0))
          for (At, Rt, Bt, Kt, *_) in units]
    Ak = [jnp.where(u[8], aa[:L, :L], 0.0) for u, aa in zip(units, AA)]
    AkV = [_mm(jnp.where(u[8], aa[:L, L:], 0.0), u[6]) for u, aa in zip(units, AA)]
    A_r = [jnp.where(u[9], aa[L:, :], 0.0) for u, aa in zip(units, AA)]
    W = [jnp.concatenate([u[0], akv], axis=1) for u, akv in zip(units, AkV)]
    span = 1
    while 2 * span < L:
        R = [_mm(a, jnp.concatenate([w, a], axis=1)) for a, w in zip(Ak, W)]
        W = [w + r[:, :2 * N] for w, r in zip(W, R)]
        Ak = [r[:, 2 * N:] for r in R]
        span *= 2
    W = [w + _mm(a, w) for a, w in zip(Ak, W)]
    Z = [jnp.concatenate([w, jnp.concatenate([zeros, u[6]], axis=1)], axis=0) for u, w in zip(units, W)]
    Q = [jnp.concatenate([u[1], zeros], axis=1) + _mm(a_r, z) for u, a_r, z in zip(units, A_r, Z)]
    GH = [jnp.concatenate([eye * u[7], zeros], axis=1) + _mm_tn(jnp.concatenate([u[4], u[5]], axis=0), z)
          for u, z in zip(units, Z)]
    return Q, GH


def _rwkv_chunk_kernel(z_ref, zp_ref, zn_ref, conv_ref, w0_ref, w2_ref, a0_ref, a2_ref, g2_ref,
                       kk_ref, ka_ref, rk_ref, seg_ref, tri_ref,
                       qq_ref, gh_ref, bonus_ref, g_ref,
                       r_s, v_s, kk_s, b_s, kd_s, lw_s):
    i = pl.program_id(1)
    last = pl.num_programs(1) - 1
    tm = z_ref.shape[1]
    L = CHUNK

    z = z_ref[0]
    prev_row = jnp.where(i == 0, 0.0, zp_ref[0, 7:8, :])
    next_row = jnp.where(i == last, 0.0, zn_ref[0, 0:1, :])
    row = lax.broadcasted_iota(jnp.int32, (tm, 1), 0)
    z_dn = jnp.where(row == 0, prev_row, pltpu.roll(z, 1, 0))
    z_up = jnp.where(row == tm - 1, next_row, pltpu.roll(z, tm - 1, 0))
    zc = conv_ref[0:1, :] * z_dn + conv_ref[1:2, :] * z + conv_ref[2:3, :] * z_up

    r = zc[:, 0:RWKV_WIDTH]
    k = zc[:, RWKV_WIDTH:2 * RWKV_WIDTH]
    v = zc[:, 2 * RWKV_WIDTH:3 * RWKV_WIDTH]
    lora = zc[:, 3 * RWKV_WIDTH:]
    seg = seg_ref[...]

    def seg_sum(t):
        hi, lo = _split2(t)
        return _dot(hi, seg) + _dot(lo, seg)

    kk = k * kk_ref[...]
    kk = kk * lax.rsqrt(jnp.maximum(seg_sum(kk * kk), 1e-24))
    tanh_lora = jnp.tanh(lora)
    kd_sum = jnp.zeros_like(k)
    for d in range(2):
        w_pre = w0_ref[d:d + 1, :] + _mm(tanh_lora, w2_ref[d])
        lw_s[d] = -DECAY_SCALE * _sigmoid(w_pre)
        alpha = _sigmoid(a0_ref[d:d + 1, :] + _mm(lora, a2_ref[d]))
        kd = k * (1.0 + (alpha - 1.0) * ka_ref[...])
        kd_s[d] = kd
        b_s[d] = kk * alpha
        kd_sum = kd_sum + kd
    r_s[...] = r
    v_s[...] = v
    kk_s[...] = kk
    g_ref[0] = _mm(_sigmoid(lora), g2_ref[...])
    bonus_ref[0] = seg_sum(r * kd_sum * rk_ref[...]) * v

    ri = lax.broadcasted_iota(jnp.int32, (L, L), 0)
    ci = lax.broadcasted_iota(jnp.int32, (L, L), 1)
    eye = (ci == ri).astype(F32)
    m_strict = (ci < ri, ci > ri)
    ri2 = lax.broadcasted_iota(jnp.int32, (L, 2 * L), 0)
    ci2 = lax.broadcasted_iota(jnp.int32, (L, 2 * L), 1)
    ci2 = jnp.where(ci2 >= L, ci2 - L, ci2)
    m_incl = (ci2 <= ri2, ci2 >= ri2)

    def chunk_body(c, carry):
        rows = pl.ds(pl.multiple_of(c * L, L), L)
        rc = r_s[rows, :]
        vc = v_s[rows, :]
        kkc = kk_s[rows, :]
        units = []
        for d in range(2):
            lw = lw_s[d, rows, :]
            bc = b_s[d, rows, :]
            kdc = kd_s[d, rows, :]
            cum = _mm_exact_lhs(tri_ref[d], lw)
            tot = cum[L - 1:L, :] if d == 0 else cum[0:1, :]
            e_in = jnp.exp(-cum)
            e_rem = jnp.exp(tot - cum)
            e_tot = jnp.exp(tot)
            At = -kkc * jnp.exp(cum - lw)
            Rt = rc * jnp.exp(cum)
            Bt = bc * e_in
            Kt = kdc * e_in
            Bh = bc * e_rem
            Kh = kdc * e_rem
            for h in range(RWKV_HEADS):
                sl = slice(h * HEAD_DIM, (h + 1) * HEAD_DIM)
                units.append((At[:, sl], Rt[:, sl], Bt[:, sl], Kt[:, sl], Bh[:, sl], Kh[:, sl],
                              vc[:, sl], e_tot[:, sl], m_strict[d], m_incl[d]))
        Q, GH = _chunk_units(units, eye)
        for d in range(2):
            for h in range(RWKV_HEADS):
                col = (2 * h + d) * 2 * HEAD_DIM
                qq_ref[0, rows, col:col + 2 * HEAD_DIM] = Q[d * RWKV_HEADS + h]
                gh_ref[0, rows, col:col + 2 * HEAD_DIM] = GH[d * RWKV_HEADS + h]
        return carry

    lax.fori_loop(0, tm // L, chunk_body, 0)


def _rwkv_chunk(z_rwkv, conv_p, w0, w2_p, a0, a2_p, g2_p, k_k, k_a, r_k, seg, tri, tm):
    B, S, ZC = z_rwkv.shape
    C = RWKV_WIDTH
    halo = tm // 8
    n_halo = S // 8
    wide = 2 * RWKV_HEADS * 2 * HEAD_DIM
    row_spec = lambda cols: pl.BlockSpec((1, tm, cols), lambda b, i: (b, i, 0))
    return pl.pallas_call(
        _rwkv_chunk_kernel,
        grid=(B, S // tm),
        in_specs=[row_spec(ZC),
                  pl.BlockSpec((1, 8, ZC), lambda b, i: (b, jnp.maximum(i * halo - 1, 0), 0)),
                  pl.BlockSpec((1, 8, ZC), lambda b, i: (b, jnp.minimum((i + 1) * halo, n_halo - 1), 0)),
                  _const_spec(conv_p.shape), _const_spec(w0.shape), _const_spec(w2_p.shape),
                  _const_spec(a0.shape), _const_spec(a2_p.shape), _const_spec(g2_p.shape),
                  _const_spec(k_k.shape), _const_spec(k_a.shape), _const_spec(r_k.shape),
                  _const_spec(seg.shape), _const_spec(tri.shape)],
        out_specs=[row_spec(wide), row_spec(wide), row_spec(C), row_spec(C)],
        out_shape=[jax.ShapeDtypeStruct((B, S, wide), F32), jax.ShapeDtypeStruct((B, S, wide), F32),
                   jax.ShapeDtypeStruct((B, S, C), F32), jax.ShapeDtypeStruct((B, S, C), F32)],
        scratch_shapes=[pltpu.VMEM((tm, C), F32), pltpu.VMEM((tm, C), F32), pltpu.VMEM((tm, C), F32),
                        pltpu.VMEM((2, tm, C), F32), pltpu.VMEM((2, tm, C), F32), pltpu.VMEM((2, tm, C), F32)],
        compiler_params=_params("parallel", "parallel"),
        name="rwkv_chunk",
    )(z_rwkv, z_rwkv, z_rwkv, conv_p, w0, w2_p, a0, a2_p, g2_p, k_k, k_a, r_k, seg, tri)


def _rwkv_scan_kernel(qq_ref, gh_ref, bonus_ref, g_ref, lnw_ref, lnb_ref, segm_ref, y_ref, of_s, ob_s):
    S = qq_ref.shape[1]
    L, N = CHUNK, HEAD_DIM
    n_chunks = S // L
    out_s = (of_s, ob_s)

    def step(c, states):
        new_states = []
        for hh in range(2):
            for d in range(2):
                cc = c if d == 0 else n_chunks - 1 - c
                rows = pl.ds(pl.multiple_of(cc * L, L), L)
                col = (2 * hh + d) * 2 * N
                M = states[2 * hh + d]
                out = _mm_x3(qq_ref[0, rows, col:col + N], M) + qq_ref[0, rows, col + N:col + 2 * N]
                out_s[d][rows, hh * N:(hh + 1) * N] = out
                new_states.append(_mm_x3(gh_ref[0, rows, col:col + N], M) + gh_ref[0, rows, col + N:col + 2 * N])
        return tuple(new_states)

    lax.fori_loop(0, n_chunks, step, tuple(jnp.zeros((N, N), F32) for _ in range(4)))

    o = of_s[...] + ob_s[...]
    segm = segm_ref[...]
    mu = _mm_exact_rhs(o, segm)
    oc = o - mu
    var = _mm_exact_rhs(oc * oc, segm)
    o = oc * lax.rsqrt(var + LN_X_EPS) * lnw_ref[...] + lnb_ref[...]
    y_ref[0] = (o + bonus_ref[0]) * g_ref[0]


def _rwkv_scan(qq, gh, bonus, g, lnw, lnb, segm):
    B, S, _ = qq.shape
    pair = 2 * HEAD_DIM
    wide = 2 * 2 * pair
    return pl.pallas_call(
        _rwkv_scan_kernel,
        grid=(B, RWKV_HEADS // 2),
        in_specs=[pl.BlockSpec((1, S, wide), lambda b, p: (b, 0, p)),
                  pl.BlockSpec((1, S, wide), lambda b, p: (b, 0, p)),
                  pl.BlockSpec((1, S, pair), lambda b, p: (b, 0, p)),
                  pl.BlockSpec((1, S, pair), lambda b, p: (b, 0, p)),
                  pl.BlockSpec((1, pair), lambda b, p: (0, p)),
                  pl.BlockSpec((1, pair), lambda b, p: (0, p)),
                  _const_spec(segm.shape)],
        out_specs=pl.BlockSpec((1, S, pair), lambda b, p: (b, 0, p)),
        out_shape=jax.ShapeDtypeStruct((B, S, RWKV_WIDTH), F32),
        scratch_shapes=[pltpu.VMEM((S, pair), F32), pltpu.VMEM((S, pair), F32)],
        compiler_params=_params("parallel", "parallel"),
        name="rwkv_scan",
    )(qq, gh, bonus, g, lnw, lnb, segm)


def _rope_tiles(pos, freq, sign):
    ang = pos.astype(F32) * freq
    lane = lax.broadcasted_iota(jnp.int32, ang.shape, 1)
    cos_t = jnp.where(lane < QK_NOPE_DIM, 1.0, jnp.where(lane < QK_NOPE_DIM + QK_ROPE_DIM, jnp.cos(ang), 0.0))
    sin_t = jnp.sin(ang) * sign
    return cos_t, sin_t


def _mla_kernel(zq_ref, zkv_ref, pos_ref, freq_ref, sign_ref, qn_ref, wq_ref, wqs_ref, kvn_ref, wkv_ref,
                y_ref, k_s, v_s):
    i = pl.program_id(1)
    tq = zq_ref.shape[1]
    scale = (QK_NOPE_DIM + QK_ROPE_DIM) ** -0.5
    c_kv_lo, c_kv_hi = Q_LORA_RANK, Q_LORA_RANK + KV_LORA_RANK

    @pl.when(i == 0)
    def _():
        zkv = zkv_ref[0]
        kvn = _rms(zkv[:, c_kv_lo:c_kv_hi], kvn_ref[...]).astype(BF16)
        kvu = jnp.dot(kvn, wkv_ref[...], preferred_element_type=F32)
        cos_t, sin_t = _rope_tiles(pos_ref[0], freq_ref[...], sign_ref[...])
        k_rot = zkv[:, c_kv_hi:c_kv_hi + LANES] * cos_t + zkv[:, c_kv_hi + LANES:] * sin_t
        for h in range(MLA_HEADS):
            k_s[h] = (kvu[:, h * LANES:(h + 1) * LANES] + k_rot).astype(BF16)
        v_s[...] = kvu[:, MLA_HEADS * LANES:].astype(BF16)

    zq = zq_ref[0]
    qn = _rms(zq[:, :Q_LORA_RANK], qn_ref[...]).astype(BF16)
    q_a = jnp.dot(qn, wq_ref[...], preferred_element_type=F32)
    q_b = jnp.dot(qn, wqs_ref[...], preferred_element_type=F32)
    pos_q = pos_ref[0, pl.ds(pl.multiple_of(i * tq, tq), tq), :]
    cos_q, sin_q = _rope_tiles(pos_q, freq_ref[...], sign_ref[...])
    cos_q = cos_q * scale
    sin_q = sin_q * scale
    for h in range(MLA_HEADS):
        sl = slice(h * LANES, (h + 1) * LANES)
        qh = (q_a[:, sl] * cos_q + q_b[:, sl] * sin_q).astype(BF16)
        s = lax.dot_general(qh, k_s[h], (((1,), (1,)), ((), ())), preferred_element_type=F32)
        p = jnp.exp(s - jnp.max(s, axis=-1, keepdims=True))
        denom = jnp.sum(p, axis=-1, keepdims=True)
        o = jnp.dot(p.astype(BF16), v_s[:, h * V_HEAD_DIM:(h + 1) * V_HEAD_DIM], preferred_element_type=F32)
        y_ref[0, :, h * V_HEAD_DIM:(h + 1) * V_HEAD_DIM] = o / denom


def _mla_attn(z_mla, pos3, freq, sign, q_norm, wq_p, wq_sw, kv_norm, wkv_p, tq):
    B, S, ZC = z_mla.shape
    return pl.pallas_call(
        _mla_kernel,
        grid=(B, S // tq),
        in_specs=[pl.BlockSpec((1, tq, ZC), lambda b, i: (b, i, 0)),
                  pl.BlockSpec((1, S, ZC), lambda b, i: (b, 0, 0)),
                  pl.BlockSpec((1, S, 1), lambda b, i: (b, 0, 0)),
                  _const_spec(freq.shape), _const_spec(sign.shape), _const_spec(q_norm.shape),
                  _const_spec(wq_p.shape), _const_spec(wq_sw.shape), _const_spec(kv_norm.shape),
                  _const_spec(wkv_p.shape)],
        out_specs=pl.BlockSpec((1, tq, MLA_WIDTH), lambda b, i: (b, i, 0)),
        out_shape=jax.ShapeDtypeStruct((B, S, MLA_WIDTH), F32),
        scratch_shapes=[pltpu.VMEM((MLA_HEADS, S, LANES), BF16), pltpu.VMEM((S, MLA_WIDTH), BF16)],
        compiler_params=_params("parallel", "arbitrary"),
        name="mla_attn",
    )(z_mla, z_mla, pos3, freq, sign, q_norm, wq_p, wq_sw, kv_norm, wkv_p)


def _mix_out_kernel(x_ref, yr_ref, ym_ref, wr_ref, wm_ref, g_ref, o_ref):
    y = (jnp.dot(yr_ref[0].astype(BF16), wr_ref[...], preferred_element_type=F32)
         + jnp.dot(ym_ref[0].astype(BF16), wm_ref[...], preferred_element_type=F32))
    o_ref[0] = x_ref[0] + _rms(y, g_ref[...])


def _mix_out(x, y_rwkv, y_mla, w_r, w_m, g, tm):
    B, S, D = x.shape
    row_spec = lambda cols: pl.BlockSpec((1, tm, cols), lambda b, i: (b, i, 0))
    return pl.pallas_call(
        _mix_out_kernel,
        grid=(B, S // tm),
        in_specs=[row_spec(D), row_spec(RWKV_WIDTH), row_spec(MLA_WIDTH),
                  _const_spec(w_r.shape), _const_spec(w_m.shape), _const_spec(g.shape)],
        out_specs=row_spec(D),
        out_shape=jax.ShapeDtypeStruct((B, S, D), F32),
        compiler_params=_params("parallel", "parallel"),
        name="mix_out",
    )(x, y_rwkv, y_mla, w_r, w_m, g)


def _mem_kv_kernel(mem_ref, g_ref, w_ref, k_ref, v_ref):
    m = _rms(mem_ref[0], g_ref[...]).astype(BF16)
    kv = jnp.dot(m, w_ref[...], preferred_element_type=F32)
    k_ref[0] = kv[:, :D_MODEL].astype(BF16)
    v_ref[0] = kv[:, D_MODEL:].astype(BF16)


def _mem_kv(mem, g, wkv):
    B, T, D = mem.shape
    return pl.pallas_call(
        _mem_kv_kernel,
        grid=(B,),
        in_specs=[pl.BlockSpec((1, T, D), lambda b: (b, 0, 0)), _const_spec(g.shape), _const_spec(wkv.shape)],
        out_specs=[pl.BlockSpec((1, T, D), lambda b: (b, 0, 0)), pl.BlockSpec((1, T, D), lambda b: (b, 0, 0))],
        out_shape=[jax.ShapeDtypeStruct((B, T, D), BF16), jax.ShapeDtypeStruct((B, T, D), BF16)],
        compiler_params=_params("parallel"),
        name="mem_kv",
    )(mem, g, wkv)


def _mem_attn_kernel(x_ref, k_ref, v_ref, gpre_ref, wq_ref, wo_ref, gpost_ref, o_ref, att_s):
    x = x_ref[0]
    h = _rms(x, gpre_ref[...]).astype(BF16)
    q = jnp.dot(h, wq_ref[...], preferred_element_type=F32) * (MEM_HEAD_DIM ** -0.5)
    for hd in range(MEM_HEADS):
        sl = slice(hd * MEM_HEAD_DIM, (hd + 1) * MEM_HEAD_DIM)
        s = lax.dot_general(q[:, sl].astype(BF16), k_ref[0, :, sl], (((1,), (1,)), ((), ())),
                            preferred_element_type=F32)
        p = jnp.exp(s - jnp.max(s, axis=-1, keepdims=True))
        denom = jnp.sum(p, axis=-1, keepdims=True)
        o = jnp.dot(p.astype(BF16), v_ref[0, :, sl], preferred_element_type=F32)
        att_s[:, sl] = (o / denom).astype(BF16)
    y = jnp.dot(att_s[...], wo_ref[...], preferred_element_type=F32)
    o_ref[0] = x + _rms(y, gpost_ref[...])


def _mem_attn(x, k, v, g_pre, wq, wo, g_post, tm):
    B, S, D = x.shape
    T = k.shape[1]
    row_spec = pl.BlockSpec((1, tm, D), lambda b, i: (b, i, 0))
    kv_spec = pl.BlockSpec((1, T, D), lambda b, i: (b, 0, 0))
    return pl.pallas_call(
        _mem_attn_kernel,
        grid=(B, S // tm),
        in_specs=[row_spec, kv_spec, kv_spec, _const_spec(g_pre.shape), _const_spec(wq.shape),
                  _const_spec(wo.shape), _const_spec(g_post.shape)],
        out_specs=row_spec,
        out_shape=jax.ShapeDtypeStruct((B, S, D), F32),
        scratch_shapes=[pltpu.VMEM((tm, D), BF16)],
        compiler_params=_params("parallel", "parallel"),
        name="mem_attn",
    )(x, k, v, g_pre, wq, wo, g_post)


def _mlp_kernel(x_ref, gpre_ref, w1_ref, w2_ref, gpost_ref, o_ref):
    x = x_ref[0]
    h = _rms(x, gpre_ref[...]).astype(BF16)
    u = jnp.maximum(jnp.dot(h, w1_ref[...], preferred_element_type=F32), 0.0)
    y = jnp.dot((u * u).astype(BF16), w2_ref[...], preferred_element_type=F32)
    o_ref[0] = x + _rms(y, gpost_ref[...])


def _mlp(x, g_pre, w1, w2, g_post, tm):
    B, S, D = x.shape
    row_spec = pl.BlockSpec((1, tm, D), lambda b, i: (b, i, 0))
    return pl.pallas_call(
        _mlp_kernel,
        grid=(B, S // tm),
        in_specs=[row_spec, _const_spec(g_pre.shape), _const_spec(w1.shape), _const_spec(w2.shape),
                  _const_spec(g_post.shape)],
        out_specs=row_spec,
        out_shape=jax.ShapeDtypeStruct((B, S, D), F32),
        compiler_params=_params("parallel", "parallel"),
        name="mlp",
    )(x, g_pre, w1, w2, g_post)


def _pad_cols(w, n):
    return jnp.pad(w, ((0, 0), (0, n - w.shape[1])))


def _rope_tile(w):
    return jnp.pad(w, ((0, 0), (QK_NOPE_DIM, LANES - QK_NOPE_DIM - QK_ROPE_DIM)))


def _swap_halves(w):
    half = QK_ROPE_DIM // 2
    return jnp.concatenate([w[:, half:], w[:, :half]], axis=1)


def _lora_rows(w, lo, rows):
    return jnp.pad(w, ((lo, LORA_TILE - lo - rows), (0, 0)))


def kernel(x, mem, positions, norm_mix_pre, w_in, conv_rwkv, rwkv_w0, rwkv_w2, rwkv_a0, rwkv_a2, rwkv_g2, rwkv_k_k, rwkv_k_a, rwkv_r_k, rwkv_lnx_w, rwkv_lnx_b, mla_q_norm, mla_w_uq, mla_kv_norm, mla_w_ukv, w_out, norm_mix_post, norm_mem_pre, norm_memtok, mem_wq, mem_wkv, mem_wo, norm_mem_post, norm_mlp_pre, mlp_w1, mlp_w2, norm_mlp_post):
    depth = w_in.shape[0]
    C = RWKV_WIDTH
    head_of = jnp.arange(C) // HEAD_DIM
    seg = (head_of[:, None] == head_of[None, :]).astype(BF16)
    seg_mean = (head_of[:2 * HEAD_DIM, None] == head_of[None, :2 * HEAD_DIM]).astype(F32) / HEAD_DIM
    inv_freq = ROPE_THETA ** (-jnp.arange(0, QK_ROPE_DIM, 2, dtype=F32) / QK_ROPE_DIM)
    half = QK_ROPE_DIM // 2
    freq = _rope_tile(jnp.concatenate([inv_freq, inv_freq])[None, :])
    sign = _rope_tile(jnp.concatenate([-jnp.ones((half,), F32), jnp.ones((half,), F32)])[None, :])
    pos3 = positions[:, :, None]
    step = jnp.arange(CHUNK)
    tri = jnp.stack([step[None, :] <= step[:, None], step[None, :] >= step[:, None]]).astype(F32)
    row = lambda t: t.reshape(1, -1)

    for l in range(depth):
        w = w_in[l]
        mla0 = RWKV_COLS
        w_rope = w[:, mla0 + Q_LORA_RANK + KV_LORA_RANK:]
        w_r = _pad_cols(w[:, :RWKV_COLS], RWKV_TILE_COLS).astype(BF16)
        w_m = jnp.concatenate([w[:, mla0:mla0 + Q_LORA_RANK + KV_LORA_RANK], _rope_tile(w_rope),
                               _rope_tile(_swap_halves(w_rope))], axis=1).astype(BF16)
        z_rwkv, z_mla = _in_proj(x, row(norm_mix_pre[l]), w_r, w_m, tm=512)

        conv_p = _pad_cols(conv_rwkv[l], RWKV_TILE_COLS)
        w2_p = jnp.stack([_lora_rows(rwkv_w2[l, d], d * DECAY_LORA, DECAY_LORA) for d in range(2)])
        a_lo = 2 * DECAY_LORA
        a2_p = jnp.stack([_lora_rows(rwkv_a2[l, d], a_lo + d * ICLR_LORA, ICLR_LORA) for d in range(2)])
        g2_p = _lora_rows(rwkv_g2[l], a_lo + 2 * ICLR_LORA, GATE_LORA)
        qq, gh, bonus, gate = _rwkv_chunk(z_rwkv, conv_p, rwkv_w0[l], w2_p, rwkv_a0[l], a2_p, g2_p,
                                          row(rwkv_k_k[l]), row(rwkv_k_a[l]), row(rwkv_r_k[l]), seg, tri, tm=256)
        y_rwkv = _rwkv_scan(qq, gh, bonus, gate, row(rwkv_lnx_w[l]), row(rwkv_lnx_b[l]), seg_mean)

        qk = QK_NOPE_DIM + QK_ROPE_DIM
        w_uq = mla_w_uq[l].reshape(Q_LORA_RANK, MLA_HEADS, qk)
        wq_p = jnp.pad(w_uq, ((0, 0), (0, 0), (0, LANES - qk))).reshape(Q_LORA_RANK, MLA_HEADS * LANES)
        uq_rope = w_uq[:, :, QK_NOPE_DIM:]
        uq_sw = jnp.concatenate([uq_rope[:, :, half:], uq_rope[:, :, :half]], axis=2)
        wq_sw = jnp.pad(uq_sw, ((0, 0), (0, 0), (QK_NOPE_DIM, LANES - qk))).reshape(Q_LORA_RANK, MLA_HEADS * LANES)
        w_ukv = mla_w_ukv[l].reshape(KV_LORA_RANK, MLA_HEADS, QK_NOPE_DIM + V_HEAD_DIM)
        wk_p = jnp.pad(w_ukv[:, :, :QK_NOPE_DIM], ((0, 0), (0, 0), (0, LANES - QK_NOPE_DIM)))
        wkv_p = jnp.concatenate([wk_p.reshape(KV_LORA_RANK, MLA_HEADS * LANES),
                                 w_ukv[:, :, QK_NOPE_DIM:].reshape(KV_LORA_RANK, MLA_WIDTH)], axis=1)
        y_mla = _mla_attn(z_mla, pos3, freq, sign, row(mla_q_norm[l]), wq_p.astype(BF16), wq_sw.astype(BF16),
                          row(mla_kv_norm[l]), wkv_p.astype(BF16), tq=256)

        wo = w_out[l].astype(BF16)
        x = _mix_out(x, y_rwkv, y_mla, wo[:C], wo[C:], row(norm_mix_post[l]), tm=512)

        k_mem, v_mem = _mem_kv(mem, row(norm_memtok[l]), mem_wkv[l].astype(BF16))
        x = _mem_attn(x, k_mem, v_mem, row(norm_mem_pre[l]), mem_wq[l].astype(BF16), mem_wo[l].astype(BF16),
                      row(norm_mem_post[l]), tm=512)

        x = _mlp(x, row(norm_mlp_pre[l]), mlp_w1[l].astype(BF16), mlp_w2[l].astype(BF16),
                 row(norm_mlp_post[l]), tm=256)
    return x
```

```python
import math

import jax
import jax.numpy as jnp
from jax import lax
from jax.experimental import pallas as pl
from jax.experimental.pallas import tpu as pltpu

F32 = jnp.float32
BF16 = jnp.bfloat16

D_MODEL = 1024
NORM_EPS = 1e-6

RWKV_HEADS = 8
HEAD_DIM = 64
RWKV_WIDTH = RWKV_HEADS * HEAD_DIM
DECAY_LORA = 32
ICLR_LORA = 32
GATE_LORA = 96
LORA_COLS = 2 * DECAY_LORA + 2 * ICLR_LORA + GATE_LORA
LORA_TILE = 256
RWKV_COLS = 3 * RWKV_WIDTH + LORA_COLS
RWKV_TILE_COLS = 3 * RWKV_WIDTH + LORA_TILE
LN_X_EPS = 64e-5
CHUNK = 64
DECAY_SCALE = math.exp(-0.5)

MLA_HEADS = 8
QK_NOPE_DIM = 64
QK_ROPE_DIM = 32
V_HEAD_DIM = 64
MLA_WIDTH = MLA_HEADS * V_HEAD_DIM
Q_LORA_RANK = 256
KV_LORA_RANK = 128
ROPE_THETA = 10000.0
LOG2_E = math.log2(math.e)
LANES = 128
MLA_TILE_COLS = Q_LORA_RANK + KV_LORA_RANK + 2 * LANES

MEM_HEADS = 4
MEM_HEAD_DIM = D_MODEL // MEM_HEADS
D_FF = 4 * D_MODEL

VMEM_LIMIT = 56 * 1024 * 1024


def _mm(a, b):
    return jnp.dot(a.astype(BF16), b.astype(BF16), preferred_element_type=F32)


def _split2(t):
    hi = t.astype(BF16)
    lo = (t - hi.astype(F32)).astype(BF16)
    return hi, lo


def _split3(t):
    hi = t.astype(BF16)
    rest = t - hi.astype(F32)
    mid = rest.astype(BF16)
    lo = (rest - mid.astype(F32)).astype(BF16)
    return hi, mid, lo


def _dot(a, b):
    return jnp.dot(a, b, preferred_element_type=F32)


def _mm_x3(a, b):
    a_hi, a_lo = _split2(a)
    b_hi, b_lo = _split2(b)
    return _dot(a_hi, b_hi) + (_dot(a_hi, b_lo) + _dot(a_lo, b_hi))


def _mm_exact_lhs(a, b):
    a = a.astype(BF16)
    hi, mid, lo = _split3(b)
    return _dot(a, hi) + (_dot(a, mid) + _dot(a, lo))


def _mm_nt(a, b):
    return lax.dot_general(a.astype(BF16), b.astype(BF16), (((1,), (1,)), ((), ())),
                           preferred_element_type=F32)


def _mm_tn(a, b):
    return lax.dot_general(a.astype(BF16), b.astype(BF16), (((0,), (0,)), ((), ())),
                           preferred_element_type=F32)


def _rms(x, g, eps=NORM_EPS):
    return x * lax.rsqrt(jnp.mean(x * x, axis=-1, keepdims=True) + eps) * g


def _sigmoid(x):
    return 0.5 * jnp.tanh(0.5 * x) + 0.5


def _params(*sem):
    return pltpu.CompilerParams(dimension_semantics=sem, vmem_limit_bytes=VMEM_LIMIT)


def _const_spec(shape):
    nd = len(shape)
    return pl.BlockSpec(shape, lambda *_: (0,) * nd)


def _in_proj_kernel(x_ref, g_ref, wr_ref, wm_ref, zr_ref, zm_ref):
    h = _rms(x_ref[0], g_ref[...]).astype(BF16)
    zr_ref[0] = jnp.dot(h, wr_ref[...], preferred_element_type=F32)
    zm_ref[0] = jnp.dot(h, wm_ref[...], preferred_element_type=F32)


def _in_proj(x, g, w_r, w_m, tm):
    B, S, D = x.shape
    return pl.pallas_call(
        _in_proj_kernel,
        grid=(B, S // tm),
        in_specs=[pl.BlockSpec((1, tm, D), lambda b, i: (b, i, 0)),
                  _const_spec((1, D)),
                  _const_spec(w_r.shape),
                  _const_spec(w_m.shape)],
        out_specs=[pl.BlockSpec((1, tm, RWKV_TILE_COLS), lambda b, i: (b, i, 0)),
                   pl.BlockSpec((1, tm, MLA_TILE_COLS), lambda b, i: (b, i, 0))],
        out_shape=[jax.ShapeDtypeStruct((B, S, RWKV_TILE_COLS), F32),
                   jax.ShapeDtypeStruct((B, S, MLA_TILE_COLS), F32)],
        compiler_params=_params("parallel", "parallel"),
        name="in_proj",
    )(x, g, w_r, w_m)


def _chunk_units(units, eye):
    L, N = units[0][0].shape
    zeros = jnp.zeros((L, N), F32)
    AA = [_mm_nt(jnp.concatenate([At, Rt], axis=0), jnp.concatenate([Bt, Kt], axis=0))
          for (At, Rt, Bt, Kt, *_) in units]
    Ak = [jnp.where(u[8], aa[:L, :L], 0.0) for u, aa in zip(units, AA)]
    AkV = [_mm(jnp.where(u[8], aa[:L, L:], 0.0), u[6]) for u, aa in zip(units, AA)]
    A_r = [jnp.where(u[9], aa[L:, :], 0.0) for u, aa in zip(units, AA)]
    W = [jnp.concatenate([u[0], akv], axis=1) for u, akv in zip(units, AkV)]
    span = 1
    while 2 * span < L:
        R = [_mm(a, jnp.concatenate([w, a], axis=1)) for a, w in zip(Ak, W)]
        W = [w + r[:, :2 * N] for w, r in zip(W, R)]
        Ak = [r[:, 2 * N:] for r in R]
        span *= 2
    W = [w + _mm(a, w) for a, w in zip(Ak, W)]
    Z = [jnp.concatenate([w, jnp.concatenate([zeros, u[6]], axis=1)], axis=0) for u, w in zip(units, W)]
    Q = [jnp.concatenate([u[1], zeros], axis=1) + _mm(a_r, z) for u, a_r, z in zip(units, A_r, Z)]
    GH = [jnp.concatenate([eye * u[7], zeros], axis=1) + _mm_tn(jnp.concatenate([u[4], u[5]], axis=0), z)
          for u, z in zip(units, Z)]
    return Q, GH


def _rwkv_kernel(zf_ref, zfp_ref, zfn_ref, zb_ref, zbp_ref, zbn_ref,
                 conv_ref, w0_ref, w2_ref, a0_ref, a2_ref, g2_ref, kk_ref, ka_ref, rk_ref, seg_ref, tri_ref,
                 of_ref, ob_ref, bonus_f_ref, bonus_b_ref, g_ref,
                 r_s, v_s, kk_s, b_s, kd_s, lw_s, state_s):
    i = pl.program_id(1)
    n_tiles = pl.num_programs(1)
    tm = zf_ref.shape[1]
    L, N = CHUNK, HEAD_DIM
    n_chunks = tm // L

    @pl.when(i == 0)
    def _():
        state_s[...] = jnp.zeros_like(state_s)

    seg = seg_ref[...]

    def seg_sum(t):
        hi, lo = _split2(t)
        return _dot(hi, seg) + _dot(lo, seg)

    row = lax.broadcasted_iota(jnp.int32, (tm, 1), 0)
    tiles = ((zf_ref, zfp_ref, zfn_ref, i, bonus_f_ref), (zb_ref, zbp_ref, zbn_ref, n_tiles - 1 - i, bonus_b_ref))
    for d, (z_ref, zp_ref, zn_ref, t, bonus_ref) in enumerate(tiles):
        z = z_ref[0]
        prev_row = jnp.where(t == 0, 0.0, zp_ref[0, 7:8, :])
        next_row = jnp.where(t == n_tiles - 1, 0.0, zn_ref[0, 0:1, :])
        z_dn = jnp.where(row == 0, prev_row, pltpu.roll(z, 1, 0))
        z_up = jnp.where(row == tm - 1, next_row, pltpu.roll(z, tm - 1, 0))
        zc = conv_ref[0:1, :] * z_dn + conv_ref[1:2, :] * z + conv_ref[2:3, :] * z_up
        r = zc[:, 0:RWKV_WIDTH]
        k = zc[:, RWKV_WIDTH:2 * RWKV_WIDTH]
        v = zc[:, 2 * RWKV_WIDTH:3 * RWKV_WIDTH]
        lora = zc[:, 3 * RWKV_WIDTH:]
        kk = k * kk_ref[...]
        kk = kk * lax.rsqrt(jnp.maximum(seg_sum(kk * kk), 1e-24))
        w_pre = w0_ref[d:d + 1, :] + _mm(jnp.tanh(lora), w2_ref[d])
        lw_s[d] = -DECAY_SCALE * _sigmoid(w_pre)
        alpha = _sigmoid(a0_ref[d:d + 1, :] + _mm(lora, a2_ref[d]))
        kd = k * (1.0 + (alpha - 1.0) * ka_ref[...])
        r_s[d] = r
        v_s[d] = v
        kk_s[d] = kk
        kd_s[d] = kd
        b_s[d] = kk * alpha
        bonus_ref[0] = seg_sum(r * kd * rk_ref[...]) * v
        if d == 0:
            g_ref[0] = _mm(_sigmoid(lora), g2_ref[...])

    ri = lax.broadcasted_iota(jnp.int32, (L, L), 0)
    ci = lax.broadcasted_iota(jnp.int32, (L, L), 1)
    eye = (ci == ri).astype(F32)
    m_strict = (ci < ri, ci > ri)
    ri2 = lax.broadcasted_iota(jnp.int32, (L, 2 * L), 0)
    ci2 = lax.broadcasted_iota(jnp.int32, (L, 2 * L), 1)
    ci2 = jnp.where(ci2 >= L, ci2 - L, ci2)
    m_incl = (ci2 <= ri2, ci2 >= ri2)
    out_refs = (of_ref, ob_ref)

    def chunk_body(c, carry):
        rows_d = (pl.ds(pl.multiple_of(c * L, L), L), pl.ds(pl.multiple_of((n_chunks - 1 - c) * L, L), L))
        units = []
        for d in range(2):
            rows = rows_d[d]
            lw = lw_s[d, rows, :]
            bc = b_s[d, rows, :]
            kdc = kd_s[d, rows, :]
            vc = v_s[d, rows, :]
            cum = _mm_exact_lhs(tri_ref[d], lw)
            tot = cum[L - 1:L, :] if d == 0 else cum[0:1, :]
            e_in = jnp.exp(-cum)
            e_rem = jnp.exp(tot - cum)
            e_tot = jnp.exp(tot)
            At = -kk_s[d, rows, :] * jnp.exp(cum - lw)
            Rt = r_s[d, rows, :] * jnp.exp(cum)
            Bt = bc * e_in
            Kt = kdc * e_in
            Bh = bc * e_rem
            Kh = kdc * e_rem
            for h in range(RWKV_HEADS):
                sl = slice(h * HEAD_DIM, (h + 1) * HEAD_DIM)
                units.append((At[:, sl], Rt[:, sl], Bt[:, sl], Kt[:, sl], Bh[:, sl], Kh[:, sl],
                              vc[:, sl], e_tot[:, sl], m_strict[d], m_incl[d]))
        Q, GH = _chunk_units(units, eye)
        T = [jnp.concatenate([q, gh], axis=0) for q, gh in zip(Q, GH)]
        res = [_mm_x3(t[:, :N], state_s[u]) + t[:, N:] for u, t in enumerate(T)]
        for d in range(2):
            for h in range(RWKV_HEADS):
                u = d * RWKV_HEADS + h
                out_refs[d][0, rows_d[d], h * N:(h + 1) * N] = res[u][:L]
                state_s[u] = res[u][L:]
        return carry

    lax.fori_loop(0, n_chunks, chunk_body, 0)


def _rwkv(z_rwkv, conv_p, w0, w2_p, a0, a2_p, g2_p, k_k, k_a, r_k, seg, tri, tm):
    B, S, ZC = z_rwkv.shape
    C = RWKV_WIDTH
    n_tiles = S // tm
    halo = tm // 8
    n_halo = S // 8
    fwd = lambda b, i: (b, i, 0)
    bwd = lambda b, i: (b, n_tiles - 1 - i, 0)
    prev_of = lambda t: jnp.maximum(t * halo - 1, 0)
    next_of = lambda t: jnp.minimum((t + 1) * halo, n_halo - 1)
    out_f = pl.BlockSpec((1, tm, C), fwd)
    out_b = pl.BlockSpec((1, tm, C), bwd)
    out_sds = jax.ShapeDtypeStruct((B, S, C), F32)
    return pl.pallas_call(
        _rwkv_kernel,
        grid=(B, n_tiles),
        in_specs=[pl.BlockSpec((1, tm, ZC), fwd),
                  pl.BlockSpec((1, 8, ZC), lambda b, i: (b, prev_of(i), 0)),
                  pl.BlockSpec((1, 8, ZC), lambda b, i: (b, next_of(i), 0)),
                  pl.BlockSpec((1, tm, ZC), bwd),
                  pl.BlockSpec((1, 8, ZC), lambda b, i: (b, prev_of(n_tiles - 1 - i), 0)),
                  pl.BlockSpec((1, 8, ZC), lambda b, i: (b, next_of(n_tiles - 1 - i), 0)),
                  _const_spec(conv_p.shape), _const_spec(w0.shape), _const_spec(w2_p.shape),
                  _const_spec(a0.shape), _const_spec(a2_p.shape), _const_spec(g2_p.shape),
                  _const_spec(k_k.shape), _const_spec(k_a.shape), _const_spec(r_k.shape),
                  _const_spec(seg.shape), _const_spec(tri.shape)],
        out_specs=[out_f, out_b, out_f, out_b, out_f],
        out_shape=[out_sds] * 5,
        scratch_shapes=[pltpu.VMEM((2, tm, C), F32)] * 6
                       + [pltpu.VMEM((2 * RWKV_HEADS, HEAD_DIM, HEAD_DIM), F32)],
        compiler_params=_params("parallel", "arbitrary"),
        name="rwkv",
    )(z_rwkv, z_rwkv, z_rwkv, z_rwkv, z_rwkv, z_rwkv, conv_p, w0, w2_p, a0, a2_p, g2_p, k_k, k_a, r_k, seg, tri)


def _rope_tiles(pos, freq, sign):
    ang = pos.astype(F32) * freq
    lane = lax.broadcasted_iota(jnp.int32, ang.shape, 1)
    cos_t = jnp.where(lane < QK_NOPE_DIM, 1.0, jnp.where(lane < QK_NOPE_DIM + QK_ROPE_DIM, jnp.cos(ang), 0.0))
    sin_t = jnp.sin(ang) * sign
    return cos_t, sin_t


def _mla_kernel(zq_ref, zkv_ref, pos_ref, freq_ref, sign_ref, qn_ref, wq_ref, wqs_ref, kvn_ref, wkv_ref,
                y_ref, k_s, v_s):
    i = pl.program_id(1)
    tq = zq_ref.shape[1]
    scale = (QK_NOPE_DIM + QK_ROPE_DIM) ** -0.5
    c_kv_lo, c_kv_hi = Q_LORA_RANK, Q_LORA_RANK + KV_LORA_RANK

    @pl.when(i == 0)
    def _():
        zkv = zkv_ref[0]
        kvn = _rms(zkv[:, c_kv_lo:c_kv_hi], kvn_ref[...]).astype(BF16)
        kvu = jnp.dot(kvn, wkv_ref[...], preferred_element_type=F32)
        cos_t, sin_t = _rope_tiles(pos_ref[0], freq_ref[...], sign_ref[...])
        k_rot = zkv[:, c_kv_hi:c_kv_hi + LANES] * cos_t + zkv[:, c_kv_hi + LANES:] * sin_t
        lane = lax.broadcasted_iota(jnp.int32, (1, LANES), 1)
        ones_col = (lane == V_HEAD_DIM).astype(F32)
        for h in range(MLA_HEADS):
            k_s[h] = (kvu[:, h * LANES:(h + 1) * LANES] + k_rot).astype(BF16)
            v_s[h] = (kvu[:, (MLA_HEADS + h) * LANES:(MLA_HEADS + h + 1) * LANES] + ones_col).astype(BF16)

    zq = zq_ref[0]
    qn = _rms(zq[:, :Q_LORA_RANK], qn_ref[...]).astype(BF16)
    q_a = jnp.dot(qn, wq_ref[...], preferred_element_type=F32)
    q_b = jnp.dot(qn, wqs_ref[...], preferred_element_type=F32)
    pos_q = pos_ref[0, pl.ds(pl.multiple_of(i * tq, tq), tq), :]
    cos_q, sin_q = _rope_tiles(pos_q, freq_ref[...], sign_ref[...])
    cos_q = cos_q * (scale * LOG2_E)
    sin_q = sin_q * (scale * LOG2_E)
    for h in range(MLA_HEADS):
        sl = slice(h * LANES, (h + 1) * LANES)
        qh = (q_a[:, sl] * cos_q + q_b[:, sl] * sin_q).astype(BF16)
        s = lax.dot_general(qh, k_s[h], (((1,), (1,)), ((), ())), preferred_element_type=F32)
        p = jnp.exp2((s - jnp.max(s, axis=-1, keepdims=True)).astype(BF16))
        o = jnp.dot(p, v_s[h], preferred_element_type=F32)
        y_ref[0, :, h * V_HEAD_DIM:(h + 1) * V_HEAD_DIM] = o[:, :V_HEAD_DIM] / o[:, V_HEAD_DIM:V_HEAD_DIM + 1]


def _mla_attn(z_mla, pos3, freq, sign, q_norm, wq_p, wq_sw, kv_norm, wkv_p, tq):
    B, S, ZC = z_mla.shape
    return pl.pallas_call(
        _mla_kernel,
        grid=(B, S // tq),
        in_specs=[pl.BlockSpec((1, tq, ZC), lambda b, i: (b, i, 0)),
                  pl.BlockSpec((1, S, ZC), lambda b, i: (b, 0, 0)),
                  pl.BlockSpec((1, S, 1), lambda b, i: (b, 0, 0)),
                  _const_spec(freq.shape), _const_spec(sign.shape), _const_spec(q_norm.shape),
                  _const_spec(wq_p.shape), _const_spec(wq_sw.shape), _const_spec(kv_norm.shape),
                  _const_spec(wkv_p.shape)],
        out_specs=pl.BlockSpec((1, tq, MLA_WIDTH), lambda b, i: (b, i, 0)),
        out_shape=jax.ShapeDtypeStruct((B, S, MLA_WIDTH), F32),
        scratch_shapes=[pltpu.VMEM((MLA_HEADS, S, LANES), BF16), pltpu.VMEM((MLA_HEADS, S, LANES), BF16)],
        compiler_params=_params("parallel", "arbitrary"),
        name="mla_attn",
    )(z_mla, z_mla, pos3, freq, sign, q_norm, wq_p, wq_sw, kv_norm, wkv_p)


def _mix_out_kernel(x_ref, of_ref, ob_ref, bf_ref, bb_ref, gate_ref, lnw_ref, lnb_ref, segm_ref, ym_ref,
                    wr_ref, wm_ref, g_ref, o_ref):
    segm = segm_ref[...]

    def seg_mean(t):
        hi, lo = _split2(t)
        return _dot(hi, segm) + _dot(lo, segm)

    o = of_ref[0] + ob_ref[0]
    oc = o - seg_mean(o)
    o = oc * lax.rsqrt(seg_mean(oc * oc) + LN_X_EPS) * lnw_ref[...] + lnb_ref[...]
    y_rwkv = (o + (bf_ref[0] + bb_ref[0])) * gate_ref[0]
    y = (jnp.dot(y_rwkv.astype(BF16), wr_ref[...], preferred_element_type=F32)
         + jnp.dot(ym_ref[0].astype(BF16), wm_ref[...], preferred_element_type=F32))
    o_ref[0] = x_ref[0] + _rms(y, g_ref[...])


def _mix_out(x, o_f, o_b, bonus_f, bonus_b, gate, lnw, lnb, segm, y_mla, w_r, w_m, g, tm):
    B, S, D = x.shape
    row_spec = lambda cols: pl.BlockSpec((1, tm, cols), lambda b, i: (b, i, 0))
    C = RWKV_WIDTH
    return pl.pallas_call(
        _mix_out_kernel,
        grid=(B, S // tm),
        in_specs=[row_spec(D), row_spec(C), row_spec(C), row_spec(C), row_spec(C), row_spec(C),
                  _const_spec(lnw.shape), _const_spec(lnb.shape), _const_spec(segm.shape), row_spec(MLA_WIDTH),
                  _const_spec(w_r.shape), _const_spec(w_m.shape), _const_spec(g.shape)],
        out_specs=row_spec(D),
        out_shape=jax.ShapeDtypeStruct((B, S, D), F32),
        compiler_params=_params("parallel", "parallel"),
        name="mix_out",
    )(x, o_f, o_b, bonus_f, bonus_b, gate, lnw, lnb, segm, y_mla, w_r, w_m, g)


def _mem_kv_kernel(mem_ref, g_ref, w_ref, k_ref, v_ref):
    m = _rms(mem_ref[0], g_ref[...]).astype(BF16)
    kv = jnp.dot(m, w_ref[...], preferred_element_type=F32)
    k_ref[0] = kv[:, :D_MODEL].astype(BF16)
    v_ref[0] = kv[:, D_MODEL:].astype(BF16)


def _mem_kv(mem, g, wkv):
    B, T, D = mem.shape
    return pl.pallas_call(
        _mem_kv_kernel,
        grid=(B,),
        in_specs=[pl.BlockSpec((1, T, D), lambda b: (b, 0, 0)), _const_spec(g.shape), _const_spec(wkv.shape)],
        out_specs=[pl.BlockSpec((1, T, D), lambda b: (b, 0, 0)), pl.BlockSpec((1, T, D), lambda b: (b, 0, 0))],
        out_shape=[jax.ShapeDtypeStruct((B, T, D), BF16), jax.ShapeDtypeStruct((B, T, D), BF16)],
        compiler_params=_params("parallel"),
        name="mem_kv",
    )(mem, g, wkv)


def _mem_attn_kernel(x_ref, k_ref, v_ref, gpre_ref, wq_ref, wo_ref, gpost_ref, o_ref, att_s):
    x = x_ref[0]
    h = _rms(x, gpre_ref[...]).astype(BF16)
    q = jnp.dot(h, wq_ref[...], preferred_element_type=F32) * (MEM_HEAD_DIM ** -0.5)
    for hd in range(MEM_HEADS):
        sl = slice(hd * MEM_HEAD_DIM, (hd + 1) * MEM_HEAD_DIM)
        s = lax.dot_general(q[:, sl].astype(BF16), k_ref[0, :, sl], (((1,), (1,)), ((), ())),
                            preferred_element_type=F32)
        p = jnp.exp(s - jnp.max(s, axis=-1, keepdims=True))
        denom = jnp.sum(p, axis=-1, keepdims=True)
        o = jnp.dot(p.astype(BF16), v_ref[0, :, sl], preferred_element_type=F32)
        att_s[:, sl] = (o / denom).astype(BF16)
    y = jnp.dot(att_s[...], wo_ref[...], preferred_element_type=F32)
    o_ref[0] = x + _rms(y, gpost_ref[...])


def _mem_attn(x, k, v, g_pre, wq, wo, g_post, tm):
    B, S, D = x.shape
    T = k.shape[1]
    row_spec = pl.BlockSpec((1, tm, D), lambda b, i: (b, i, 0))
    kv_spec = pl.BlockSpec((1, T, D), lambda b, i: (b, 0, 0))
    return pl.pallas_call(
        _mem_attn_kernel,
        grid=(B, S // tm),
        in_specs=[row_spec, kv_spec, kv_spec, _const_spec(g_pre.shape), _const_spec(wq.shape),
                  _const_spec(wo.shape), _const_spec(g_post.shape)],
        out_specs=row_spec,
        out_shape=jax.ShapeDtypeStruct((B, S, D), F32),
        scratch_shapes=[pltpu.VMEM((tm, D), BF16)],
        compiler_params=_params("parallel", "parallel"),
        name="mem_attn",
    )(x, k, v, g_pre, wq, wo, g_post)


def _mlp_kernel(x_ref, gpre_ref, w1_ref, w2_ref, gpost_ref, o_ref):
    x = x_ref[0]
    h = _rms(x, gpre_ref[...]).astype(BF16)
    u = jnp.maximum(jnp.dot(h, w1_ref[...], preferred_element_type=F32), 0.0)
    y = jnp.dot((u * u).astype(BF16), w2_ref[...], preferred_element_type=F32)
    o_ref[0] = x + _rms(y, gpost_ref[...])


def _mlp(x, g_pre, w1, w2, g_post, tm):
    B, S, D = x.shape
    row_spec = pl.BlockSpec((1, tm, D), lambda b, i: (b, i, 0))
    return pl.pallas_call(
        _mlp_kernel,
        grid=(B, S // tm),
        in_specs=[row_spec, _const_spec(g_pre.shape), _const_spec(w1.shape), _const_spec(w2.shape),
                  _const_spec(g_post.shape)],
        out_specs=row_spec,
        out_shape=jax.ShapeDtypeStruct((B, S, D), F32),
        compiler_params=_params("parallel", "parallel"),
        name="mlp",
    )(x, g_pre, w1, w2, g_post)


def _pad_cols(w, n):
    return jnp.pad(w, ((0, 0), (0, n - w.shape[1])))


def _rope_tile(w):
    return jnp.pad(w, ((0, 0), (QK_NOPE_DIM, LANES - QK_NOPE_DIM - QK_ROPE_DIM)))


def _swap_halves(w):
    half = QK_ROPE_DIM // 2
    return jnp.concatenate([w[:, half:], w[:, :half]], axis=1)


def _lora_rows(w, lo, rows):
    return jnp.pad(w, ((lo, LORA_TILE - lo - rows), (0, 0)))


def kernel(x, mem, positions, norm_mix_pre, w_in, conv_rwkv, rwkv_w0, rwkv_w2, rwkv_a0, rwkv_a2, rwkv_g2, rwkv_k_k, rwkv_k_a, rwkv_r_k, rwkv_lnx_w, rwkv_lnx_b, mla_q_norm, mla_w_uq, mla_kv_norm, mla_w_ukv, w_out, norm_mix_post, norm_mem_pre, norm_memtok, mem_wq, mem_wkv, mem_wo, norm_mem_post, norm_mlp_pre, mlp_w1, mlp_w2, norm_mlp_post):
    depth = w_in.shape[0]
    C = RWKV_WIDTH
    head_of = jnp.arange(C) // HEAD_DIM
    seg = (head_of[:, None] == head_of[None, :]).astype(BF16)
    seg_mean = (seg.astype(F32) / HEAD_DIM).astype(BF16)
    inv_freq = ROPE_THETA ** (-jnp.arange(0, QK_ROPE_DIM, 2, dtype=F32) / QK_ROPE_DIM)
    half = QK_ROPE_DIM // 2
    freq = _rope_tile(jnp.concatenate([inv_freq, inv_freq])[None, :])
    sign = _rope_tile(jnp.concatenate([-jnp.ones((half,), F32), jnp.ones((half,), F32)])[None, :])
    pos3 = positions[:, :, None]
    step = jnp.arange(CHUNK)
    tri = jnp.stack([step[None, :] <= step[:, None], step[None, :] >= step[:, None]]).astype(F32)
    row = lambda t: t.reshape(1, -1)

    for l in range(depth):
        w = w_in[l]
        mla0 = RWKV_COLS
        w_rope = w[:, mla0 + Q_LORA_RANK + KV_LORA_RANK:]
        w_r = _pad_cols(w[:, :RWKV_COLS], RWKV_TILE_COLS).astype(BF16)
        w_m = jnp.concatenate([w[:, mla0:mla0 + Q_LORA_RANK + KV_LORA_RANK], _rope_tile(w_rope),
                               _rope_tile(_swap_halves(w_rope))], axis=1).astype(BF16)
        z_rwkv, z_mla = _in_proj(x, row(norm_mix_pre[l]), w_r, w_m, tm=512)

        conv_p = _pad_cols(conv_rwkv[l], RWKV_TILE_COLS)
        w2_p = jnp.stack([_lora_rows(rwkv_w2[l, d], d * DECAY_LORA, DECAY_LORA) for d in range(2)])
        a_lo = 2 * DECAY_LORA
        a2_p = jnp.stack([_lora_rows(rwkv_a2[l, d], a_lo + d * ICLR_LORA, ICLR_LORA) for d in range(2)])
        g2_p = _lora_rows(rwkv_g2[l], a_lo + 2 * ICLR_LORA, GATE_LORA)
        o_f, o_b, bonus_f, bonus_b, gate = _rwkv(z_rwkv, conv_p, rwkv_w0[l], w2_p, rwkv_a0[l], a2_p, g2_p,
                                                 row(rwkv_k_k[l]), row(rwkv_k_a[l]), row(rwkv_r_k[l]), seg, tri, tm=256)

        qk = QK_NOPE_DIM + QK_ROPE_DIM
        w_uq = mla_w_uq[l].reshape(Q_LORA_RANK, MLA_HEADS, qk)
        wq_p = jnp.pad(w_uq, ((0, 0), (0, 0), (0, LANES - qk))).reshape(Q_LORA_RANK, MLA_HEADS * LANES)
        uq_rope = w_uq[:, :, QK_NOPE_DIM:]
        uq_sw = jnp.concatenate([uq_rope[:, :, half:], uq_rope[:, :, :half]], axis=2)
        wq_sw = jnp.pad(uq_sw, ((0, 0), (0, 0), (QK_NOPE_DIM, LANES - qk))).reshape(Q_LORA_RANK, MLA_HEADS * LANES)
        w_ukv = mla_w_ukv[l].reshape(KV_LORA_RANK, MLA_HEADS, QK_NOPE_DIM + V_HEAD_DIM)
        wk_p = jnp.pad(w_ukv[:, :, :QK_NOPE_DIM], ((0, 0), (0, 0), (0, LANES - QK_NOPE_DIM)))
        wv_p = jnp.pad(w_ukv[:, :, QK_NOPE_DIM:], ((0, 0), (0, 0), (0, LANES - V_HEAD_DIM)))
        wkv_p = jnp.concatenate([wk_p.reshape(KV_LORA_RANK, MLA_HEADS * LANES),
                                 wv_p.reshape(KV_LORA_RANK, MLA_HEADS * LANES)], axis=1)
        y_mla = _mla_attn(z_mla, pos3, freq, sign, row(mla_q_norm[l]), wq_p.astype(BF16), wq_sw.astype(BF16),
                          row(mla_kv_norm[l]), wkv_p.astype(BF16), tq=256)

        wo = w_out[l].astype(BF16)
        x = _mix_out(x, o_f, o_b, bonus_f, bonus_b, gate, row(rwkv_lnx_w[l]), row(rwkv_lnx_b[l]), seg_mean,
                     y_mla, wo[:C], wo[C:], row(norm_mix_post[l]), tm=512)

        k_mem, v_mem = _mem_kv(mem, row(norm_memtok[l]), mem_wkv[l].astype(BF16))
        x = _mem_attn(x, k_mem, v_mem, row(norm_mem_pre[l]), mem_wq[l].astype(BF16), mem_wo[l].astype(BF16),
                      row(norm_mem_post[l]), tm=512)

        x = _mlp(x, row(norm_mlp_pre[l]), mlp_w1[l].astype(BF16), mlp_w2[l].astype(BF16),
                 row(norm_mlp_post[l]), tm=256)
    return x
```

```python
import math

import jax
import jax.numpy as jnp
from jax import lax
from jax.experimental import pallas as pl
from jax.experimental.pallas import tpu as pltpu

F32 = jnp.float32
BF16 = jnp.bfloat16

D_MODEL = 1024
NORM_EPS = 1e-6

RWKV_HEADS = 8
HEAD_DIM = 64
RWKV_WIDTH = RWKV_HEADS * HEAD_DIM
DECAY_LORA = 32
ICLR_LORA = 32
GATE_LORA = 96
LORA_COLS = 2 * DECAY_LORA + 2 * ICLR_LORA + GATE_LORA
LORA_TILE = 256
RWKV_COLS = 3 * RWKV_WIDTH + LORA_COLS
RWKV_TILE_COLS = 3 * RWKV_WIDTH + LORA_TILE
LN_X_EPS = 64e-5
CHUNK = 64
CHUNK_GROUP = 2
DECAY_SCALE = math.exp(-0.5)

MLA_HEADS = 8
QK_NOPE_DIM = 64
QK_ROPE_DIM = 32
V_HEAD_DIM = 64
MLA_WIDTH = MLA_HEADS * V_HEAD_DIM
Q_LORA_RANK = 256
KV_LORA_RANK = 128
ROPE_THETA = 10000.0
LOG2_E = math.log2(math.e)
LANES = 128
MLA_TILE_COLS = Q_LORA_RANK + KV_LORA_RANK + 2 * LANES

MEM_HEADS = 4
MEM_HEAD_DIM = D_MODEL // MEM_HEADS
D_FF = 4 * D_MODEL

VMEM_LIMIT = 56 * 1024 * 1024


def _mm(a, b):
    return jnp.dot(a.astype(BF16), b.astype(BF16), preferred_element_type=F32)


def _split2(t):
    hi = t.astype(BF16)
    lo = (t - hi.astype(F32)).astype(BF16)
    return hi, lo


def _split3(t):
    hi = t.astype(BF16)
    rest = t - hi.astype(F32)
    mid = rest.astype(BF16)
    lo = (rest - mid.astype(F32)).astype(BF16)
    return hi, mid, lo


def _dot(a, b):
    return jnp.dot(a, b, preferred_element_type=F32)


def _mm_x3(a, b):
    a_hi, a_lo = _split2(a)
    b_hi, b_lo = _split2(b)
    return _dot(a_hi, b_hi) + (_dot(a_hi, b_lo) + _dot(a_lo, b_hi))


def _mm_exact_lhs(a, b):
    a = a.astype(BF16)
    hi, mid, lo = _split3(b)
    return _dot(a, hi) + (_dot(a, mid) + _dot(a, lo))


def _mm_nt(a, b):
    return lax.dot_general(a.astype(BF16), b.astype(BF16), (((1,), (1,)), ((), ())),
                           preferred_element_type=F32)


def _mm_tn(a, b):
    return lax.dot_general(a.astype(BF16), b.astype(BF16), (((0,), (0,)), ((), ())),
                           preferred_element_type=F32)


def _rms(x, g, eps=NORM_EPS):
    return x * lax.rsqrt(jnp.mean(x * x, axis=-1, keepdims=True) + eps) * g


def _sigmoid(x):
    return 0.5 * jnp.tanh(0.5 * x) + 0.5


def _params(*sem):
    return pltpu.CompilerParams(dimension_semantics=sem, vmem_limit_bytes=VMEM_LIMIT)


def _const_spec(shape):
    nd = len(shape)
    return pl.BlockSpec(shape, lambda *_: (0,) * nd)


def _in_proj_kernel(x_ref, xp_ref, xn_ref, g_ref, wr_ref, wm_ref, conv_ref, zr_ref, zm_ref):
    i = pl.program_id(1)
    last = pl.num_programs(1) - 1
    tm = x_ref.shape[1]
    g = g_ref[...]
    h = _rms(x_ref[0], g).astype(BF16)
    zm_ref[0] = jnp.dot(h, wm_ref[...], preferred_element_type=F32)
    z = jnp.dot(h, wr_ref[...], preferred_element_type=F32)
    halo = _rms(jnp.concatenate([xp_ref[0], xn_ref[0]], axis=0), g).astype(BF16)
    z_halo = jnp.dot(halo, wr_ref[...], preferred_element_type=F32)
    prev_row = jnp.where(i == 0, 0.0, z_halo[7:8, :])
    next_row = jnp.where(i == last, 0.0, z_halo[8:9, :])
    row = lax.broadcasted_iota(jnp.int32, (tm, 1), 0)
    z_dn = jnp.where(row == 0, prev_row, pltpu.roll(z, 1, 0))
    z_up = jnp.where(row == tm - 1, next_row, pltpu.roll(z, tm - 1, 0))
    zr_ref[0] = conv_ref[0:1, :] * z_dn + conv_ref[1:2, :] * z + conv_ref[2:3, :] * z_up


def _in_proj(x, g, w_r, w_m, conv_p, tm):
    B, S, D = x.shape
    halo = tm // 8
    n_halo = S // 8
    return pl.pallas_call(
        _in_proj_kernel,
        grid=(B, S // tm),
        in_specs=[pl.BlockSpec((1, tm, D), lambda b, i: (b, i, 0)),
                  pl.BlockSpec((1, 8, D), lambda b, i: (b, jnp.maximum(i * halo - 1, 0), 0)),
                  pl.BlockSpec((1, 8, D), lambda b, i: (b, jnp.minimum((i + 1) * halo, n_halo - 1), 0)),
                  _const_spec((1, D)),
                  _const_spec(w_r.shape),
                  _const_spec(w_m.shape),
                  _const_spec(conv_p.shape)],
        out_specs=[pl.BlockSpec((1, tm, RWKV_TILE_COLS), lambda b, i: (b, i, 0)),
                   pl.BlockSpec((1, tm, MLA_TILE_COLS), lambda b, i: (b, i, 0))],
        out_shape=[jax.ShapeDtypeStruct((B, S, RWKV_TILE_COLS), F32),
                   jax.ShapeDtypeStruct((B, S, MLA_TILE_COLS), F32)],
        compiler_params=_params("parallel", "parallel"),
        name="in_proj",
    )(x, x, x, g, w_r, w_m, conv_p)


def _chunk_units(units, eye, eye_hi):
    L, N = units[0][0].shape
    zeros = jnp.zeros((L, N), F32)
    AA = [_mm_nt(jnp.concatenate([At, Rt], axis=0), jnp.concatenate([Bt, Kt], axis=0))
          for (At, Rt, Bt, Kt, *_) in units]
    A_a = [jnp.where(u[8], aa[:L, :], 0.0) for u, aa in zip(units, AA)]
    A_r = [jnp.where(u[9], aa[L:, :], 0.0) for u, aa in zip(units, AA)]
    AkV = [_mm(a[:, L:], u[6]) for u, a in zip(units, A_a)]
    low = lax.broadcasted_iota(jnp.int32, (L, 2 * L), 1) < L
    S = [jnp.where(low, a, eye_hi) for a in A_a]
    span = 1
    while span < L:
        R = [_mm(s[:, :L], s) for s in S]
        S = [jnp.where(low, r, r + s) for r, s in zip(R, S)]
        span *= 2
    W = [_mm(s[:, L:], jnp.concatenate([u[0], akv], axis=1)) for s, u, akv in zip(S, units, AkV)]
    Z = [jnp.concatenate([w, jnp.concatenate([zeros, u[6]], axis=1)], axis=0) for u, w in zip(units, W)]
    Q = [jnp.concatenate([u[1], zeros], axis=1) + _mm(a_r, z) for u, a_r, z in zip(units, A_r, Z)]
    GH = [jnp.concatenate([eye * u[7], zeros], axis=1) + _mm_tn(jnp.concatenate([u[4], u[5]], axis=0), z)
          for u, z in zip(units, Z)]
    return Q, GH


def _rwkv_kernel(zf_ref, zb_ref,
                 w0_ref, w2_ref, a0_ref, a2_ref, g2_ref, kk_ref, ka_ref, rk_ref, seg_ref, tri_ref,
                 of_ref, ob_ref, bonus_f_ref, bonus_b_ref, g_ref,
                 r_s, v_s, kk_s, b_s, kd_s, lw_s, state_s):
    i = pl.program_id(1)
    n_tiles = pl.num_programs(1)
    tm = zf_ref.shape[1]
    L, N = CHUNK, HEAD_DIM
    n_chunks = tm // L

    @pl.when(i == 0)
    def _():
        state_s[...] = jnp.zeros_like(state_s)

    seg = seg_ref[...]

    def seg_sum(t):
        hi, lo = _split2(t)
        return _dot(hi, seg) + _dot(lo, seg)

    for d, (z_ref, bonus_ref) in enumerate(((zf_ref, bonus_f_ref), (zb_ref, bonus_b_ref))):
        r = z_ref[0, :, 0:RWKV_WIDTH]
        k = z_ref[0, :, RWKV_WIDTH:2 * RWKV_WIDTH]
        v = z_ref[0, :, 2 * RWKV_WIDTH:3 * RWKV_WIDTH]
        lora = z_ref[0, :, 3 * RWKV_WIDTH:]
        kk = k * kk_ref[...]
        kk = kk * lax.rsqrt(jnp.maximum(seg_sum(kk * kk), 1e-24))
        w_pre = w0_ref[d:d + 1, :] + _mm(jnp.tanh(lora), w2_ref[d])
        lw_s[d] = -DECAY_SCALE * _sigmoid(w_pre)
        alpha = _sigmoid(a0_ref[d:d + 1, :] + _mm(lora, a2_ref[d]))
        kd = k * (1.0 + (alpha - 1.0) * ka_ref[...])
        r_s[d] = r
        v_s[d] = v
        kk_s[d] = kk
        kd_s[d] = kd
        b_s[d] = kk * alpha
        bonus_ref[0] = seg_sum(r * kd * rk_ref[...]) * v
        if d == 0:
            g_ref[0] = _mm(_sigmoid(lora), g2_ref[...])

    ri = lax.broadcasted_iota(jnp.int32, (L, L), 0)
    ci = lax.broadcasted_iota(jnp.int32, (L, L), 1)
    eye = (ci == ri).astype(F32)
    ri2 = lax.broadcasted_iota(jnp.int32, (L, 2 * L), 0)
    ci2 = lax.broadcasted_iota(jnp.int32, (L, 2 * L), 1)
    eye_hi = (ci2 == ri2 + L).astype(F32)
    ci2 = jnp.where(ci2 >= L, ci2 - L, ci2)
    m_strict = (ci2 < ri2, ci2 > ri2)
    m_incl = (ci2 <= ri2, ci2 >= ri2)
    out_refs = (of_ref, ob_ref)

    def chunk_units(c):
        rows_d = (pl.ds(pl.multiple_of(c * L, L), L), pl.ds(pl.multiple_of((n_chunks - 1 - c) * L, L), L))
        units = []
        for d in range(2):
            rows = rows_d[d]
            lw = lw_s[d, rows, :]
            bc = b_s[d, rows, :]
            kdc = kd_s[d, rows, :]
            vc = v_s[d, rows, :]
            cum = _mm_exact_lhs(tri_ref[d], lw)
            tot = cum[L - 1:L, :] if d == 0 else cum[0:1, :]
            e_in = jnp.exp(-cum)
            e_rem = jnp.exp(tot - cum)
            e_tot = jnp.exp(tot)
            At = -kk_s[d, rows, :] * jnp.exp(cum - lw)
            Rt = r_s[d, rows, :] * jnp.exp(cum)
            Bt = bc * e_in
            Kt = kdc * e_in
            Bh = bc * e_rem
            Kh = kdc * e_rem
            for h in range(RWKV_HEADS):
                sl = slice(h * HEAD_DIM, (h + 1) * HEAD_DIM)
                units.append((At[:, sl], Rt[:, sl], Bt[:, sl], Kt[:, sl], Bh[:, sl], Kh[:, sl],
                              vc[:, sl], e_tot[:, sl], m_strict[d], m_incl[d]))
        return rows_d, units

    n_units = 2 * RWKV_HEADS

    def group_body(j, carry):
        rows, units = [], []
        for g in range(CHUNK_GROUP):
            rows_d, chunk = chunk_units(j * CHUNK_GROUP + g)
            rows.append(rows_d)
            units.extend(chunk)
        Q, GH = _chunk_units(units, eye, eye_hi)
        M = [state_s[u] for u in range(n_units)]
        for g in range(CHUNK_GROUP):
            Qg, GHg = Q[g * n_units:(g + 1) * n_units], GH[g * n_units:(g + 1) * n_units]
            out = [_mm(q[:, :N], m) + q[:, N:] for q, m in zip(Qg, M)]
            M = [_mm_x3(gh[:, :N], m) + gh[:, N:] for gh, m in zip(GHg, M)]
            for d in range(2):
                for h in range(RWKV_HEADS):
                    out_refs[d][0, rows[g][d], h * N:(h + 1) * N] = out[d * RWKV_HEADS + h]
        for u in range(n_units):
            state_s[u] = M[u]
        return carry

    lax.fori_loop(0, n_chunks // CHUNK_GROUP, group_body, 0)


def _rwkv(z_rwkv, w0, w2_p, a0, a2_p, g2_p, k_k, k_a, r_k, seg, tri, tm):
    B, S, ZC = z_rwkv.shape
    C = RWKV_WIDTH
    n_tiles = S // tm
    fwd = lambda b, i: (b, i, 0)
    bwd = lambda b, i: (b, n_tiles - 1 - i, 0)
    out_f = pl.BlockSpec((1, tm, C), fwd)
    out_b = pl.BlockSpec((1, tm, C), bwd)
    out_sds = jax.ShapeDtypeStruct((B, S, C), F32)
    return pl.pallas_call(
        _rwkv_kernel,
        grid=(B, n_tiles),
        in_specs=[pl.BlockSpec((1, tm, ZC), fwd),
                  pl.BlockSpec((1, tm, ZC), bwd),
                  _const_spec(w0.shape), _const_spec(w2_p.shape),
                  _const_spec(a0.shape), _const_spec(a2_p.shape), _const_spec(g2_p.shape),
                  _const_spec(k_k.shape), _const_spec(k_a.shape), _const_spec(r_k.shape),
                  _const_spec(seg.shape), _const_spec(tri.shape)],
        out_specs=[out_f, out_b, out_f, out_b, out_f],
        out_shape=[out_sds] * 5,
        scratch_shapes=[pltpu.VMEM((2, tm, C), F32)] * 6
                       + [pltpu.VMEM((2 * RWKV_HEADS, HEAD_DIM, HEAD_DIM), F32)],
        compiler_params=_params("parallel", "arbitrary"),
        name="rwkv",
    )(z_rwkv, z_rwkv, w0, w2_p, a0, a2_p, g2_p, k_k, k_a, r_k, seg, tri)


def _rope_tiles(pos, freq, sign):
    ang = pos.astype(F32) * freq
    lane = lax.broadcasted_iota(jnp.int32, ang.shape, 1)
    cos_t = jnp.where(lane < QK_NOPE_DIM, 1.0, jnp.where(lane < QK_NOPE_DIM + QK_ROPE_DIM, jnp.cos(ang), 0.0))
    sin_t = jnp.sin(ang) * sign
    return cos_t, sin_t


def _mla_kernel(zq_ref, zkv_ref, pos_ref, freq_ref, sign_ref, qn_ref, wq_ref, wqs_ref, kvn_ref, wkv_ref,
                y_ref, k_s, v_s):
    i = pl.program_id(1)
    tq = zq_ref.shape[1]
    scale = (QK_NOPE_DIM + QK_ROPE_DIM) ** -0.5
    c_kv_lo, c_kv_hi = Q_LORA_RANK, Q_LORA_RANK + KV_LORA_RANK

    @pl.when(i == 0)
    def _():
        zkv = zkv_ref[0]
        kvn = _rms(zkv[:, c_kv_lo:c_kv_hi], kvn_ref[...]).astype(BF16)
        kvu = jnp.dot(kvn, wkv_ref[...], preferred_element_type=F32)
        cos_t, sin_t = _rope_tiles(pos_ref[0], freq_ref[...], sign_ref[...])
        k_rot = zkv[:, c_kv_hi:c_kv_hi + LANES] * cos_t + zkv[:, c_kv_hi + LANES:] * sin_t
        lane = lax.broadcasted_iota(jnp.int32, (1, LANES), 1)
        ones_col = (lane == V_HEAD_DIM).astype(F32)
        for h in range(MLA_HEADS):
            k_s[h] = (kvu[:, h * LANES:(h + 1) * LANES] + k_rot).astype(BF16)
            v_s[h] = (kvu[:, (MLA_HEADS + h) * LANES:(MLA_HEADS + h + 1) * LANES] + ones_col).astype(BF16)

    zq = zq_ref[0]
    qn = _rms(zq[:, :Q_LORA_RANK], qn_ref[...]).astype(BF16)
    q_a = jnp.dot(qn, wq_ref[...], preferred_element_type=F32)
    q_b = jnp.dot(qn, wqs_ref[...], preferred_element_type=F32)
    pos_q = pos_ref[0, pl.ds(pl.multiple_of(i * tq, tq), tq), :]
    cos_q, sin_q = _rope_tiles(pos_q, freq_ref[...], sign_ref[...])
    cos_q = cos_q * (scale * LOG2_E)
    sin_q = sin_q * (scale * LOG2_E)
    for h in range(MLA_HEADS):
        sl = slice(h * LANES, (h + 1) * LANES)
        qh = (q_a[:, sl] * cos_q + q_b[:, sl] * sin_q).astype(BF16)
        s = lax.dot_general(qh, k_s[h], (((1,), (1,)), ((), ())), preferred_element_type=F32)
        p = jnp.exp2((s - jnp.max(s, axis=-1, keepdims=True)).astype(BF16))
        o = jnp.dot(p, v_s[h], preferred_element_type=F32)
        y_ref[0, :, h * V_HEAD_DIM:(h + 1) * V_HEAD_DIM] = o[:, :V_HEAD_DIM] / o[:, V_HEAD_DIM:V_HEAD_DIM + 1]


def _mla_attn(z_mla, pos3, freq, sign, q_norm, wq_p, wq_sw, kv_norm, wkv_p, tq):
    B, S, ZC = z_mla.shape
    return pl.pallas_call(
        _mla_kernel,
        grid=(B, S // tq),
        in_specs=[pl.BlockSpec((1, tq, ZC), lambda b, i: (b, i, 0)),
                  pl.BlockSpec((1, S, ZC), lambda b, i: (b, 0, 0)),
                  pl.BlockSpec((1, S, 1), lambda b, i: (b, 0, 0)),
                  _const_spec(freq.shape), _const_spec(sign.shape), _const_spec(q_norm.shape),
                  _const_spec(wq_p.shape), _const_spec(wq_sw.shape), _const_spec(kv_norm.shape),
                  _const_spec(wkv_p.shape)],
        out_specs=pl.BlockSpec((1, tq, MLA_WIDTH), lambda b, i: (b, i, 0)),
        out_shape=jax.ShapeDtypeStruct((B, S, MLA_WIDTH), F32),
        scratch_shapes=[pltpu.VMEM((MLA_HEADS, S, LANES), BF16), pltpu.VMEM((MLA_HEADS, S, LANES), BF16)],
        compiler_params=_params("parallel", "arbitrary"),
        name="mla_attn",
    )(z_mla, z_mla, pos3, freq, sign, q_norm, wq_p, wq_sw, kv_norm, wkv_p)


def _mix_out_kernel(x_ref, of_ref, ob_ref, bf_ref, bb_ref, gate_ref, lnw_ref, lnb_ref, segm_ref, ym_ref,
                    wr_ref, wm_ref, g_ref, o_ref):
    segm = segm_ref[...]

    def seg_mean(t):
        hi, lo = _split2(t)
        return _dot(hi, segm) + _dot(lo, segm)

    o = of_ref[0] + ob_ref[0]
    oc = o - seg_mean(o)
    o = oc * lax.rsqrt(seg_mean(oc * oc) + LN_X_EPS) * lnw_ref[...] + lnb_ref[...]
    y_rwkv = (o + (bf_ref[0] + bb_ref[0])) * gate_ref[0]
    y = (jnp.dot(y_rwkv.astype(BF16), wr_ref[...], preferred_element_type=F32)
         + jnp.dot(ym_ref[0].astype(BF16), wm_ref[...], preferred_element_type=F32))
    o_ref[0] = x_ref[0] + _rms(y, g_ref[...])


def _mix_out(x, o_f, o_b, bonus_f, bonus_b, gate, lnw, lnb, segm, y_mla, w_r, w_m, g, tm):
    B, S, D = x.shape
    row_spec = lambda cols: pl.BlockSpec((1, tm, cols), lambda b, i: (b, i, 0))
    C = RWKV_WIDTH
    return pl.pallas_call(
        _mix_out_kernel,
        grid=(B, S // tm),
        in_specs=[row_spec(D), row_spec(C), row_spec(C), row_spec(C), row_spec(C), row_spec(C),
                  _const_spec(lnw.shape), _const_spec(lnb.shape), _const_spec(segm.shape), row_spec(MLA_WIDTH),
                  _const_spec(w_r.shape), _const_spec(w_m.shape), _const_spec(g.shape)],
        out_specs=row_spec(D),
        out_shape=jax.ShapeDtypeStruct((B, S, D), F32),
        compiler_params=_params("parallel", "parallel"),
        name="mix_out",
    )(x, o_f, o_b, bonus_f, bonus_b, gate, lnw, lnb, segm, y_mla, w_r, w_m, g)


def _mem_kv_kernel(mem_ref, g_ref, w_ref, k_ref, v_ref):
    m = _rms(mem_ref[0], g_ref[...]).astype(BF16)
    kv = jnp.dot(m, w_ref[...], preferred_element_type=F32)
    k_ref[0] = kv[:, :D_MODEL].astype(BF16)
    v_ref[0] = kv[:, D_MODEL:].astype(BF16)


def _mem_kv(mem, g, wkv):
    B, T, D = mem.shape
    return pl.pallas_call(
        _mem_kv_kernel,
        grid=(B,),
        in_specs=[pl.BlockSpec((1, T, D), lambda b: (b, 0, 0)), _const_spec(g.shape), _const_spec(wkv.shape)],
        out_specs=[pl.BlockSpec((1, T, D), lambda b: (b, 0, 0)), pl.BlockSpec((1, T, D), lambda b: (b, 0, 0))],
        out_shape=[jax.ShapeDtypeStruct((B, T, D), BF16), jax.ShapeDtypeStruct((B, T, D), BF16)],
        compiler_params=_params("parallel"),
        name="mem_kv",
    )(mem, g, wkv)


def _mem_attn_kernel(x_ref, k_ref, v_ref, gpre_ref, wq_ref, wo_ref, gpost_ref, o_ref, att_s):
    x = x_ref[0]
    h = _rms(x, gpre_ref[...]).astype(BF16)
    q = jnp.dot(h, wq_ref[...], preferred_element_type=F32) * (MEM_HEAD_DIM ** -0.5)
    for hd in range(MEM_HEADS):
        sl = slice(hd * MEM_HEAD_DIM, (hd + 1) * MEM_HEAD_DIM)
        s = lax.dot_general(q[:, sl].astype(BF16), k_ref[0, :, sl], (((1,), (1,)), ((), ())),
                            preferred_element_type=F32)
        p = jnp.exp(s - jnp.max(s, axis=-1, keepdims=True))
        denom = jnp.sum(p, axis=-1, keepdims=True)
        o = jnp.dot(p.astype(BF16), v_ref[0, :, sl], preferred_element_type=F32)
        att_s[:, sl] = (o / denom).astype(BF16)
    y = jnp.dot(att_s[...], wo_ref[...], preferred_element_type=F32)
    o_ref[0] = x + _rms(y, gpost_ref[...])


def _mem_attn(x, k, v, g_pre, wq, wo, g_post, tm):
    B, S, D = x.shape
    T = k.shape[1]
    row_spec = pl.BlockSpec((1, tm, D), lambda b, i: (b, i, 0))
    kv_spec = pl.BlockSpec((1, T, D), lambda b, i: (b, 0, 0))
    return pl.pallas_call(
        _mem_attn_kernel,
        grid=(B, S // tm),
        in_specs=[row_spec, kv_spec, kv_spec, _const_spec(g_pre.shape), _const_spec(wq.shape),
                  _const_spec(wo.shape), _const_spec(g_post.shape)],
        out_specs=row_spec,
        out_shape=jax.ShapeDtypeStruct((B, S, D), F32),
        scratch_shapes=[pltpu.VMEM((tm, D), BF16)],
        compiler_params=_params("parallel", "parallel"),
        name="mem_attn",
    )(x, k, v, g_pre, wq, wo, g_post)


def _mlp_kernel(x_ref, gpre_ref, w1_ref, w2_ref, gpost_ref, o_ref):
    x = x_ref[0]
    h = _rms(x, gpre_ref[...]).astype(BF16)
    u = jnp.maximum(jnp.dot(h, w1_ref[...], preferred_element_type=F32), 0.0)
    y = jnp.dot((u * u).astype(BF16), w2_ref[...], preferred_element_type=F32)
    o_ref[0] = x + _rms(y, gpost_ref[...])


def _mlp(x, g_pre, w1, w2, g_post, tm):
    B, S, D = x.shape
    row_spec = pl.BlockSpec((1, tm, D), lambda b, i: (b, i, 0))
    return pl.pallas_call(
        _mlp_kernel,
        grid=(B, S // tm),
        in_specs=[row_spec, _const_spec(g_pre.shape), _const_spec(w1.shape), _const_spec(w2.shape),
                  _const_spec(g_post.shape)],
        out_specs=row_spec,
        out_shape=jax.ShapeDtypeStruct((B, S, D), F32),
        compiler_params=_params("parallel", "parallel"),
        name="mlp",
    )(x, g_pre, w1, w2, g_post)


def _pad_cols(w, n):
    return jnp.pad(w, ((0, 0), (0, n - w.shape[1])))


def _rope_tile(w):
    return jnp.pad(w, ((0, 0), (QK_NOPE_DIM, LANES - QK_NOPE_DIM - QK_ROPE_DIM)))


def _swap_halves(w):
    half = QK_ROPE_DIM // 2
    return jnp.concatenate([w[:, half:], w[:, :half]], axis=1)


def _lora_rows(w, lo, rows):
    return jnp.pad(w, ((lo, LORA_TILE - lo - rows), (0, 0)))


def kernel(x, mem, positions, norm_mix_pre, w_in, conv_rwkv, rwkv_w0, rwkv_w2, rwkv_a0, rwkv_a2, rwkv_g2, rwkv_k_k, rwkv_k_a, rwkv_r_k, rwkv_lnx_w, rwkv_lnx_b, mla_q_norm, mla_w_uq, mla_kv_norm, mla_w_ukv, w_out, norm_mix_post, norm_mem_pre, norm_memtok, mem_wq, mem_wkv, mem_wo, norm_mem_post, norm_mlp_pre, mlp_w1, mlp_w2, norm_mlp_post):
    depth = w_in.shape[0]
    C = RWKV_WIDTH
    head_of = jnp.arange(C) // HEAD_DIM
    seg = (head_of[:, None] == head_of[None, :]).astype(BF16)
    seg_mean = (seg.astype(F32) / HEAD_DIM).astype(BF16)
    inv_freq = ROPE_THETA ** (-jnp.arange(0, QK_ROPE_DIM, 2, dtype=F32) / QK_ROPE_DIM)
    half = QK_ROPE_DIM // 2
    freq = _rope_tile(jnp.concatenate([inv_freq, inv_freq])[None, :])
    sign = _rope_tile(jnp.concatenate([-jnp.ones((half,), F32), jnp.ones((half,), F32)])[None, :])
    pos3 = positions[:, :, None]
    step = jnp.arange(CHUNK)
    tri = jnp.stack([step[None, :] <= step[:, None], step[None, :] >= step[:, None]]).astype(F32)
    row = lambda t: t.reshape(1, -1)

    for l in range(depth):
        w = w_in[l]
        mla0 = RWKV_COLS
        w_rope = w[:, mla0 + Q_LORA_RANK + KV_LORA_RANK:]
        w_r = _pad_cols(w[:, :RWKV_COLS], RWKV_TILE_COLS).astype(BF16)
        w_m = jnp.concatenate([w[:, mla0:mla0 + Q_LORA_RANK + KV_LORA_RANK], _rope_tile(w_rope),
                               _rope_tile(_swap_halves(w_rope))], axis=1).astype(BF16)
        conv_p = _pad_cols(conv_rwkv[l], RWKV_TILE_COLS)
        z_rwkv, z_mla = _in_proj(x, row(norm_mix_pre[l]), w_r, w_m, conv_p, tm=512)

        w2_p = jnp.stack([_lora_rows(rwkv_w2[l, d], d * DECAY_LORA, DECAY_LORA) for d in range(2)])
        a_lo = 2 * DECAY_LORA
        a2_p = jnp.stack([_lora_rows(rwkv_a2[l, d], a_lo + d * ICLR_LORA, ICLR_LORA) for d in range(2)])
        g2_p = _lora_rows(rwkv_g2[l], a_lo + 2 * ICLR_LORA, GATE_LORA)
        o_f, o_b, bonus_f, bonus_b, gate = _rwkv(z_rwkv, rwkv_w0[l], w2_p, rwkv_a0[l], a2_p, g2_p,
                                                 row(rwkv_k_k[l]), row(rwkv_k_a[l]), row(rwkv_r_k[l]), seg, tri, tm=256)

        qk = QK_NOPE_DIM + QK_ROPE_DIM
        w_uq = mla_w_uq[l].reshape(Q_LORA_RANK, MLA_HEADS, qk)
        wq_p = jnp.pad(w_uq, ((0, 0), (0, 0), (0, LANES - qk))).reshape(Q_LORA_RANK, MLA_HEADS * LANES)
        uq_rope = w_uq[:, :, QK_NOPE_DIM:]
        uq_sw = jnp.concatenate([uq_rope[:, :, half:], uq_rope[:, :, :half]], axis=2)
        wq_sw = jnp.pad(uq_sw, ((0, 0), (0, 0), (QK_NOPE_DIM, LANES - qk))).reshape(Q_LORA_RANK, MLA_HEADS * LANES)
        w_ukv = mla_w_ukv[l].reshape(KV_LORA_RANK, MLA_HEADS, QK_NOPE_DIM + V_HEAD_DIM)
        wk_p = jnp.pad(w_ukv[:, :, :QK_NOPE_DIM], ((0, 0), (0, 0), (0, LANES - QK_NOPE_DIM)))
        wv_p = jnp.pad(w_ukv[:, :, QK_NOPE_DIM:], ((0, 0), (0, 0), (0, LANES - V_HEAD_DIM)))
        wkv_p = jnp.concatenate([wk_p.reshape(KV_LORA_RANK, MLA_HEADS * LANES),
                                 wv_p.reshape(KV_LORA_RANK, MLA_HEADS * LANES)], axis=1)
        y_mla = _mla_attn(z_mla, pos3, freq, sign, row(mla_q_norm[l]), wq_p.astype(BF16), wq_sw.astype(BF16),
                          row(mla_kv_norm[l]), wkv_p.astype(BF16), tq=256)

        wo = w_out[l].astype(BF16)
        x = _mix_out(x, o_f, o_b, bonus_f, bonus_b, gate, row(rwkv_lnx_w[l]), row(rwkv_lnx_b[l]), seg_mean,
                     y_mla, wo[:C], wo[C:], row(norm_mix_post[l]), tm=512)

        k_mem, v_mem = _mem_kv(mem, row(norm_memtok[l]), mem_wkv[l].astype(BF16))
        x = _mem_attn(x, k_mem, v_mem, row(norm_mem_pre[l]), mem_wq[l].astype(BF16), mem_wo[l].astype(BF16),
                      row(norm_mem_post[l]), tm=512)

        x = _mlp(x, row(norm_mlp_pre[l]), mlp_w1[l].astype(BF16), mlp_w2[l].astype(BF16),
                 row(norm_mlp_post[l]), tm=256)
    return x
```

```python
import math

import jax
import jax.numpy as jnp
from jax import lax
from jax.experimental import pallas as pl
from jax.experimental.pallas import tpu as pltpu

F32 = jnp.float32
BF16 = jnp.bfloat16

D_MODEL = 1024
NORM_EPS = 1e-6

RWKV_HEADS = 8
HEAD_DIM = 64
RWKV_WIDTH = RWKV_HEADS * HEAD_DIM
DECAY_LORA = 32
ICLR_LORA = 32
GATE_LORA = 96
LORA_COLS = 2 * DECAY_LORA + 2 * ICLR_LORA + GATE_LORA
LORA_TILE = 256
RWKV_COLS = 3 * RWKV_WIDTH + LORA_COLS
RWKV_TILE_COLS = 3 * RWKV_WIDTH + LORA_TILE
LN_X_EPS = 64e-5
CHUNK = 64
CHUNK_GROUP = 2
DECAY_SCALE = math.exp(-0.5)

MLA_HEADS = 8
QK_NOPE_DIM = 64
QK_ROPE_DIM = 32
V_HEAD_DIM = 64
MLA_WIDTH = MLA_HEADS * V_HEAD_DIM
Q_LORA_RANK = 256
KV_LORA_RANK = 128
ROPE_THETA = 10000.0
LOG2_E = math.log2(math.e)
LANES = 128
MLA_TILE_COLS = Q_LORA_RANK + KV_LORA_RANK + 2 * LANES

MEM_HEADS = 4
MEM_HEAD_DIM = D_MODEL // MEM_HEADS
D_FF = 4 * D_MODEL

VMEM_LIMIT = 56 * 1024 * 1024


def _mm(a, b):
    return jnp.dot(a.astype(BF16), b.astype(BF16), preferred_element_type=F32)


def _split2(t):
    hi = t.astype(BF16)
    lo = (t - hi.astype(F32)).astype(BF16)
    return hi, lo


def _split3(t):
    hi = t.astype(BF16)
    rest = t - hi.astype(F32)
    mid = rest.astype(BF16)
    lo = (rest - mid.astype(F32)).astype(BF16)
    return hi, mid, lo


def _dot(a, b):
    return jnp.dot(a, b, preferred_element_type=F32)


def _mm_x3(a, b):
    a_hi, a_lo = _split2(a)
    b_hi, b_lo = _split2(b)
    return _dot(a_hi, b_hi) + (_dot(a_hi, b_lo) + _dot(a_lo, b_hi))


def _mm_exact_lhs(a, b):
    a = a.astype(BF16)
    hi, mid, lo = _split3(b)
    return _dot(a, hi) + (_dot(a, mid) + _dot(a, lo))


def _mm_nt(a, b):
    return lax.dot_general(a.astype(BF16), b.astype(BF16), (((1,), (1,)), ((), ())),
                           preferred_element_type=F32)


def _mm_tn(a, b):
    return lax.dot_general(a.astype(BF16), b.astype(BF16), (((0,), (0,)), ((), ())),
                           preferred_element_type=F32)


def _rms(x, g, eps=NORM_EPS):
    return x * lax.rsqrt(jnp.mean(x * x, axis=-1, keepdims=True) + eps) * g


def _sigmoid(x):
    return 0.5 * jnp.tanh(0.5 * x) + 0.5


def _params(*sem):
    return pltpu.CompilerParams(dimension_semantics=sem, vmem_limit_bytes=VMEM_LIMIT)


def _const_spec(shape):
    nd = len(shape)
    return pl.BlockSpec(shape, lambda *_: (0,) * nd)


def _in_proj_kernel(x_ref, xp_ref, xn_ref, g_ref, wr_ref, wm_ref, conv_ref, zr_ref, zm_ref):
    i = pl.program_id(1)
    last = pl.num_programs(1) - 1
    tm = x_ref.shape[1]
    g = g_ref[...]
    h = _rms(x_ref[0], g).astype(BF16)
    zm_ref[0] = jnp.dot(h, wm_ref[...], preferred_element_type=F32)
    z = jnp.dot(h, wr_ref[...], preferred_element_type=F32)
    halo = _rms(jnp.concatenate([xp_ref[0], xn_ref[0]], axis=0), g).astype(BF16)
    z_halo = jnp.dot(halo, wr_ref[...], preferred_element_type=F32)
    prev_row = jnp.where(i == 0, 0.0, z_halo[7:8, :])
    next_row = jnp.where(i == last, 0.0, z_halo[8:9, :])
    row = lax.broadcasted_iota(jnp.int32, (tm, 1), 0)
    z_dn = jnp.where(row == 0, prev_row, pltpu.roll(z, 1, 0))
    z_up = jnp.where(row == tm - 1, next_row, pltpu.roll(z, tm - 1, 0))
    zr_ref[0] = conv_ref[0:1, :] * z_dn + conv_ref[1:2, :] * z + conv_ref[2:3, :] * z_up


def _in_proj(x, g, w_r, w_m, conv_p, tm):
    B, S, D = x.shape
    halo = tm // 8
    n_halo = S // 8
    return pl.pallas_call(
        _in_proj_kernel,
        grid=(B, S // tm),
        in_specs=[pl.BlockSpec((1, tm, D), lambda b, i: (b, i, 0)),
                  pl.BlockSpec((1, 8, D), lambda b, i: (b, jnp.maximum(i * halo - 1, 0), 0)),
                  pl.BlockSpec((1, 8, D), lambda b, i: (b, jnp.minimum((i + 1) * halo, n_halo - 1), 0)),
                  _const_spec((1, D)),
                  _const_spec(w_r.shape),
                  _const_spec(w_m.shape),
                  _const_spec(conv_p.shape)],
        out_specs=[pl.BlockSpec((1, tm, RWKV_TILE_COLS), lambda b, i: (b, i, 0)),
                   pl.BlockSpec((1, tm, MLA_TILE_COLS), lambda b, i: (b, i, 0))],
        out_shape=[jax.ShapeDtypeStruct((B, S, RWKV_TILE_COLS), F32),
                   jax.ShapeDtypeStruct((B, S, MLA_TILE_COLS), F32)],
        compiler_params=_params("parallel", "parallel"),
        name="in_proj",
    )(x, x, x, g, w_r, w_m, conv_p)


def _chunk_units(units, eye, eye_hi):
    L, N = units[0][0].shape
    zeros = jnp.zeros((L, N), F32)
    AA = [_mm_nt(jnp.concatenate([At, Rt], axis=0), jnp.concatenate([Bt, Kt], axis=0))
          for (At, Rt, Bt, Kt, *_) in units]
    A_a = [jnp.where(u[8], aa[:L, :], 0.0) for u, aa in zip(units, AA)]
    A_r = [jnp.where(u[9], aa[L:, :], 0.0) for u, aa in zip(units, AA)]
    AkV = [_mm(a[:, L:], u[6]) for u, a in zip(units, A_a)]
    low = lax.broadcasted_iota(jnp.int32, (L, 2 * L), 1) < L
    S = [jnp.where(low, a, eye_hi) for a in A_a]
    span = 1
    while span < L:
        R = [_mm(s[:, :L], s) for s in S]
        S = [jnp.where(low, r, r + s) for r, s in zip(R, S)]
        span *= 2
    W = [_mm(s[:, L:], jnp.concatenate([u[0], akv], axis=1)) for s, u, akv in zip(S, units, AkV)]
    Z = [jnp.concatenate([w, jnp.concatenate([zeros, u[6]], axis=1)], axis=0) for u, w in zip(units, W)]
    Q = [jnp.concatenate([u[1], zeros], axis=1) + _mm(a_r, z) for u, a_r, z in zip(units, A_r, Z)]
    GH = [jnp.concatenate([eye * u[7], zeros], axis=1) + _mm_tn(jnp.concatenate([u[4], u[5]], axis=0), z)
          for u, z in zip(units, Z)]
    return Q, GH


def _rwkv_kernel(zf_ref, zb_ref,
                 w0_ref, w2_ref, a0_ref, a2_ref, g2_ref, kk_ref, ka_ref, rk_ref, seg_ref, tri_ref,
                 of_ref, ob_ref, bonus_f_ref, bonus_b_ref, g_ref,
                 r_s, v_s, kk_s, b_s, kd_s, lw_s, state_s):
    i = pl.program_id(1)
    n_tiles = pl.num_programs(1)
    tm = zf_ref.shape[1]
    L, N = CHUNK, HEAD_DIM
    n_chunks = tm // L

    @pl.when(i == 0)
    def _():
        state_s[...] = jnp.zeros_like(state_s)

    seg = seg_ref[...]

    def seg_sum(t):
        hi, lo = _split2(t)
        return _dot(hi, seg) + _dot(lo, seg)

    for d, (z_ref, bonus_ref) in enumerate(((zf_ref, bonus_f_ref), (zb_ref, bonus_b_ref))):
        r = z_ref[0, :, 0:RWKV_WIDTH]
        k = z_ref[0, :, RWKV_WIDTH:2 * RWKV_WIDTH]
        v = z_ref[0, :, 2 * RWKV_WIDTH:3 * RWKV_WIDTH]
        lora = z_ref[0, :, 3 * RWKV_WIDTH:]
        kk = k * kk_ref[...]
        kk = kk * lax.rsqrt(jnp.maximum(seg_sum(kk * kk), 1e-24))
        w_pre = w0_ref[d:d + 1, :] + _mm(jnp.tanh(lora), w2_ref[d])
        lw_s[d] = -DECAY_SCALE * _sigmoid(w_pre)
        alpha = _sigmoid(a0_ref[d:d + 1, :] + _mm(lora, a2_ref[d]))
        kd = k * (1.0 + (alpha - 1.0) * ka_ref[...])
        r_s[d] = r
        v_s[d] = v
        kk_s[d] = kk
        kd_s[d] = kd
        b_s[d] = kk * alpha
        bonus_ref[0] = seg_sum(r * kd * rk_ref[...]) * v
        if d == 0:
            g_ref[0] = _mm(_sigmoid(lora), g2_ref[...])

    ri = lax.broadcasted_iota(jnp.int32, (L, L), 0)
    ci = lax.broadcasted_iota(jnp.int32, (L, L), 1)
    eye = (ci == ri).astype(F32)
    ri2 = lax.broadcasted_iota(jnp.int32, (L, 2 * L), 0)
    ci2 = lax.broadcasted_iota(jnp.int32, (L, 2 * L), 1)
    eye_hi = (ci2 == ri2 + L).astype(F32)
    ci2 = jnp.where(ci2 >= L, ci2 - L, ci2)
    m_strict = (ci2 < ri2, ci2 > ri2)
    m_incl = (ci2 <= ri2, ci2 >= ri2)
    out_refs = (of_ref, ob_ref)

    def chunk_units(c):
        rows_d = (pl.ds(pl.multiple_of(c * L, L), L), pl.ds(pl.multiple_of((n_chunks - 1 - c) * L, L), L))
        units = []
        for d in range(2):
            rows = rows_d[d]
            lw = lw_s[d, rows, :]
            bc = b_s[d, rows, :]
            kdc = kd_s[d, rows, :]
            vc = v_s[d, rows, :]
            cum = _mm_exact_lhs(tri_ref[d], lw)
            tot = cum[L - 1:L, :] if d == 0 else cum[0:1, :]
            e_in = jnp.exp(-cum)
            e_rem = jnp.exp(tot - cum)
            e_tot = jnp.exp(tot)
            At = -kk_s[d, rows, :] * jnp.exp(cum - lw)
            Rt = r_s[d, rows, :] * jnp.exp(cum)
            Bt = bc * e_in
            Kt = kdc * e_in
            Bh = bc * e_rem
            Kh = kdc * e_rem
            for h in range(RWKV_HEADS):
                sl = slice(h * HEAD_DIM, (h + 1) * HEAD_DIM)
                units.append((At[:, sl], Rt[:, sl], Bt[:, sl], Kt[:, sl], Bh[:, sl], Kh[:, sl],
                              vc[:, sl], e_tot[:, sl], m_strict[d], m_incl[d]))
        return rows_d, units

    n_units = 2 * RWKV_HEADS

    def group_body(j, carry):
        rows, units = [], []
        for g in range(CHUNK_GROUP):
            rows_d, chunk = chunk_units(j * CHUNK_GROUP + g)
            rows.append(rows_d)
            units.extend(chunk)
        Q, GH = _chunk_units(units, eye, eye_hi)
        M = [state_s[u] for u in range(n_units)]
        for g in range(CHUNK_GROUP):
            Qg, GHg = Q[g * n_units:(g + 1) * n_units], GH[g * n_units:(g + 1) * n_units]
            out = [_mm(q[:, :N], m) + q[:, N:] for q, m in zip(Qg, M)]
            M = [_mm_x3(gh[:, :N], m) + gh[:, N:] for gh, m in zip(GHg, M)]
            for d in range(2):
                for h in range(RWKV_HEADS):
                    out_refs[d][0, rows[g][d], h * N:(h + 1) * N] = out[d * RWKV_HEADS + h]
        for u in range(n_units):
            state_s[u] = M[u]
        return carry

    lax.fori_loop(0, n_chunks // CHUNK_GROUP, group_body, 0)


def _rwkv(z_rwkv, w0, w2_p, a0, a2_p, g2_p, k_k, k_a, r_k, seg, tri, tm):
    B, S, ZC = z_rwkv.shape
    C = RWKV_WIDTH
    n_tiles = S // tm
    fwd = lambda b, i: (b, i, 0)
    bwd = lambda b, i: (b, n_tiles - 1 - i, 0)
    out_f = pl.BlockSpec((1, tm, C), fwd)
    out_b = pl.BlockSpec((1, tm, C), bwd)
    out_sds = jax.ShapeDtypeStruct((B, S, C), F32)
    return pl.pallas_call(
        _rwkv_kernel,
        grid=(B, n_tiles),
        in_specs=[pl.BlockSpec((1, tm, ZC), fwd),
                  pl.BlockSpec((1, tm, ZC), bwd),
                  _const_spec(w0.shape), _const_spec(w2_p.shape),
                  _const_spec(a0.shape), _const_spec(a2_p.shape), _const_spec(g2_p.shape),
                  _const_spec(k_k.shape), _const_spec(k_a.shape), _const_spec(r_k.shape),
                  _const_spec(seg.shape), _const_spec(tri.shape)],
        out_specs=[out_f, out_b, out_f, out_b, out_f],
        out_shape=[out_sds] * 5,
        scratch_shapes=[pltpu.VMEM((2, tm, C), F32)] * 6
                       + [pltpu.VMEM((2 * RWKV_HEADS, HEAD_DIM, HEAD_DIM), F32)],
        compiler_params=_params("parallel", "arbitrary"),
        name="rwkv",
    )(z_rwkv, z_rwkv, w0, w2_p, a0, a2_p, g2_p, k_k, k_a, r_k, seg, tri)


def _rope_tiles(pos, freq, sign):
    ang = pos.astype(F32) * freq
    lane = lax.broadcasted_iota(jnp.int32, ang.shape, 1)
    cos_t = jnp.where(lane < QK_NOPE_DIM, 1.0, jnp.where(lane < QK_NOPE_DIM + QK_ROPE_DIM, jnp.cos(ang), 0.0))
    sin_t = jnp.sin(ang) * sign
    return cos_t, sin_t


def _mla_kernel(zq_ref, zkv_ref, pos_ref, freq_ref, sign_ref, qn_ref, wq_ref, wqs_ref, kvn_ref, wkv_ref,
                y_ref, k_s, v_s, cos_s, sin_s):
    i = pl.program_id(1)
    tq = zq_ref.shape[1]
    scale = (QK_NOPE_DIM + QK_ROPE_DIM) ** -0.5
    c_kv_lo, c_kv_hi = Q_LORA_RANK, Q_LORA_RANK + KV_LORA_RANK

    @pl.when(i == 0)
    def _():
        zkv = zkv_ref[0]
        kvn = _rms(zkv[:, c_kv_lo:c_kv_hi], kvn_ref[...]).astype(BF16)
        kvu = jnp.dot(kvn, wkv_ref[...], preferred_element_type=F32)
        cos_t, sin_t = _rope_tiles(pos_ref[0], freq_ref[...], sign_ref[...])
        cos_s[...] = cos_t
        sin_s[...] = sin_t
        k_rot = zkv[:, c_kv_hi:c_kv_hi + LANES] * cos_t + zkv[:, c_kv_hi + LANES:] * sin_t
        lane = lax.broadcasted_iota(jnp.int32, (1, LANES), 1)
        ones_col = (lane == V_HEAD_DIM).astype(F32)
        for h in range(MLA_HEADS):
            k_s[h] = (kvu[:, h * LANES:(h + 1) * LANES] + k_rot).astype(BF16)
            v_s[h] = (kvu[:, (MLA_HEADS + h) * LANES:(MLA_HEADS + h + 1) * LANES] + ones_col).astype(BF16)

    zq = zq_ref[0]
    qn = _rms(zq[:, :Q_LORA_RANK], qn_ref[...]).astype(BF16)
    q_a = jnp.dot(qn, wq_ref[...], preferred_element_type=F32)
    q_b = jnp.dot(qn, wqs_ref[...], preferred_element_type=F32)
    rows = pl.ds(pl.multiple_of(i * tq, tq), tq)
    cos_q = cos_s[rows, :] * (scale * LOG2_E)
    sin_q = sin_s[rows, :] * (scale * LOG2_E)

    def scores(h):
        sl = slice(h * LANES, (h + 1) * LANES)
        qh = (q_a[:, sl] * cos_q + q_b[:, sl] * sin_q).astype(BF16)
        return lax.dot_general(qh, k_s[h], (((1,), (1,)), ((), ())), preferred_element_type=F32)

    s = scores(0)
    for h in range(MLA_HEADS):
        s_next = scores(h + 1) if h + 1 < MLA_HEADS else None
        p = jnp.exp2((s - jnp.max(s, axis=-1, keepdims=True)).astype(BF16))
        o = jnp.dot(p, v_s[h], preferred_element_type=F32)
        y_ref[0, :, h * V_HEAD_DIM:(h + 1) * V_HEAD_DIM] = o[:, :V_HEAD_DIM] / o[:, V_HEAD_DIM:V_HEAD_DIM + 1]
        s = s_next


def _mla_attn(z_mla, pos3, freq, sign, q_norm, wq_p, wq_sw, kv_norm, wkv_p, tq):
    B, S, ZC = z_mla.shape
    return pl.pallas_call(
        _mla_kernel,
        grid=(B, S // tq),
        in_specs=[pl.BlockSpec((1, tq, ZC), lambda b, i: (b, i, 0)),
                  pl.BlockSpec((1, S, ZC), lambda b, i: (b, 0, 0)),
                  pl.BlockSpec((1, S, 1), lambda b, i: (b, 0, 0)),
                  _const_spec(freq.shape), _const_spec(sign.shape), _const_spec(q_norm.shape),
                  _const_spec(wq_p.shape), _const_spec(wq_sw.shape), _const_spec(kv_norm.shape),
                  _const_spec(wkv_p.shape)],
        out_specs=pl.BlockSpec((1, tq, MLA_WIDTH), lambda b, i: (b, i, 0)),
        out_shape=jax.ShapeDtypeStruct((B, S, MLA_WIDTH), F32),
        scratch_shapes=[pltpu.VMEM((MLA_HEADS, S, LANES), BF16), pltpu.VMEM((MLA_HEADS, S, LANES), BF16),
                        pltpu.VMEM((S, LANES), F32), pltpu.VMEM((S, LANES), F32)],
        compiler_params=_params("parallel", "arbitrary"),
        name="mla_attn",
    )(z_mla, z_mla, pos3, freq, sign, q_norm, wq_p, wq_sw, kv_norm, wkv_p)


def _mix_out_kernel(x_ref, of_ref, ob_ref, bf_ref, bb_ref, gate_ref, lnw_ref, lnb_ref, segm_ref, ym_ref,
                    wr_ref, wm_ref, g_ref, o_ref):
    segm = segm_ref[...]

    def seg_mean(t):
        hi, lo = _split2(t)
        return _dot(hi, segm) + _dot(lo, segm)

    o = of_ref[0] + ob_ref[0]
    oc = o - seg_mean(o)
    o = oc * lax.rsqrt(seg_mean(oc * oc) + LN_X_EPS) * lnw_ref[...] + lnb_ref[...]
    y_rwkv = (o + (bf_ref[0] + bb_ref[0])) * gate_ref[0]
    y = (jnp.dot(y_rwkv.astype(BF16), wr_ref[...], preferred_element_type=F32)
         + jnp.dot(ym_ref[0].astype(BF16), wm_ref[...], preferred_element_type=F32))
    o_ref[0] = x_ref[0] + _rms(y, g_ref[...])


def _mix_out(x, o_f, o_b, bonus_f, bonus_b, gate, lnw, lnb, segm, y_mla, w_r, w_m, g, tm):
    B, S, D = x.shape
    row_spec = lambda cols: pl.BlockSpec((1, tm, cols), lambda b, i: (b, i, 0))
    C = RWKV_WIDTH
    return pl.pallas_call(
        _mix_out_kernel,
        grid=(B, S // tm),
        in_specs=[row_spec(D), row_spec(C), row_spec(C), row_spec(C), row_spec(C), row_spec(C),
                  _const_spec(lnw.shape), _const_spec(lnb.shape), _const_spec(segm.shape), row_spec(MLA_WIDTH),
                  _const_spec(w_r.shape), _const_spec(w_m.shape), _const_spec(g.shape)],
        out_specs=row_spec(D),
        out_shape=jax.ShapeDtypeStruct((B, S, D), F32),
        compiler_params=_params("parallel", "parallel"),
        name="mix_out",
    )(x, o_f, o_b, bonus_f, bonus_b, gate, lnw, lnb, segm, y_mla, w_r, w_m, g)


def _mem_kv_kernel(mem_ref, g_ref, w_ref, k_ref, v_ref):
    m = _rms(mem_ref[0], g_ref[...]).astype(BF16)
    kv = jnp.dot(m, w_ref[...], preferred_element_type=F32)
    k_ref[0] = kv[:, :D_MODEL].astype(BF16)
    v_ref[0] = kv[:, D_MODEL:].astype(BF16)


def _mem_kv(mem, g, wkv):
    B, T, D = mem.shape
    return pl.pallas_call(
        _mem_kv_kernel,
        grid=(B,),
        in_specs=[pl.BlockSpec((1, T, D), lambda b: (b, 0, 0)), _const_spec(g.shape), _const_spec(wkv.shape)],
        out_specs=[pl.BlockSpec((1, T, D), lambda b: (b, 0, 0)), pl.BlockSpec((1, T, D), lambda b: (b, 0, 0))],
        out_shape=[jax.ShapeDtypeStruct((B, T, D), BF16), jax.ShapeDtypeStruct((B, T, D), BF16)],
        compiler_params=_params("parallel"),
        name="mem_kv",
    )(mem, g, wkv)


def _mem_attn_kernel(x_ref, k_ref, v_ref, gpre_ref, wq_ref, wo_ref, gpost_ref, o_ref, att_s):
    x = x_ref[0]
    h = _rms(x, gpre_ref[...]).astype(BF16)
    q = jnp.dot(h, wq_ref[...], preferred_element_type=F32) * (MEM_HEAD_DIM ** -0.5)
    for hd in range(MEM_HEADS):
        sl = slice(hd * MEM_HEAD_DIM, (hd + 1) * MEM_HEAD_DIM)
        s = lax.dot_general(q[:, sl].astype(BF16), k_ref[0, :, sl], (((1,), (1,)), ((), ())),
                            preferred_element_type=F32)
        p = jnp.exp(s - jnp.max(s, axis=-1, keepdims=True))
        denom = jnp.sum(p, axis=-1, keepdims=True)
        o = jnp.dot(p.astype(BF16), v_ref[0, :, sl], preferred_element_type=F32)
        att_s[:, sl] = (o / denom).astype(BF16)
    y = jnp.dot(att_s[...], wo_ref[...], preferred_element_type=F32)
    o_ref[0] = x + _rms(y, gpost_ref[...])


def _mem_attn(x, k, v, g_pre, wq, wo, g_post, tm):
    B, S, D = x.shape
    T = k.shape[1]
    row_spec = pl.BlockSpec((1, tm, D), lambda b, i: (b, i, 0))
    kv_spec = pl.BlockSpec((1, T, D), lambda b, i: (b, 0, 0))
    return pl.pallas_call(
        _mem_attn_kernel,
        grid=(B, S // tm),
        in_specs=[row_spec, kv_spec, kv_spec, _const_spec(g_pre.shape), _const_spec(wq.shape),
                  _const_spec(wo.shape), _const_spec(g_post.shape)],
        out_specs=row_spec,
        out_shape=jax.ShapeDtypeStruct((B, S, D), F32),
        scratch_shapes=[pltpu.VMEM((tm, D), BF16)],
        compiler_params=_params("parallel", "parallel"),
        name="mem_attn",
    )(x, k, v, g_pre, wq, wo, g_post)


def _mlp_kernel(x_ref, gpre_ref, w1_ref, w2_ref, gpost_ref, o_ref):
    x = x_ref[0]
    h = _rms(x, gpre_ref[...]).astype(BF16)
    u = jnp.maximum(jnp.dot(h, w1_ref[...], preferred_element_type=F32), 0.0)
    y = jnp.dot((u * u).astype(BF16), w2_ref[...], preferred_element_type=F32)
    o_ref[0] = x + _rms(y, gpost_ref[...])


def _mlp(x, g_pre, w1, w2, g_post, tm):
    B, S, D = x.shape
    row_spec = pl.BlockSpec((1, tm, D), lambda b, i: (b, i, 0))
    return pl.pallas_call(
        _mlp_kernel,
        grid=(B, S // tm),
        in_specs=[row_spec, _const_spec(g_pre.shape), _const_spec(w1.shape), _const_spec(w2.shape),
                  _const_spec(g_post.shape)],
        out_specs=row_spec,
        out_shape=jax.ShapeDtypeStruct((B, S, D), F32),
        compiler_params=_params("parallel", "parallel"),
        name="mlp",
    )(x, g_pre, w1, w2, g_post)


def _pad_cols(w, n):
    return jnp.pad(w, ((0, 0), (0, n - w.shape[1])))


def _rope_tile(w):
    return jnp.pad(w, ((0, 0), (QK_NOPE_DIM, LANES - QK_NOPE_DIM - QK_ROPE_DIM)))


def _swap_halves(w):
    half = QK_ROPE_DIM // 2
    return jnp.concatenate([w[:, half:], w[:, :half]], axis=1)


def _lora_rows(w, lo, rows):
    return jnp.pad(w, ((lo, LORA_TILE - lo - rows), (0, 0)))


def kernel(x, mem, positions, norm_mix_pre, w_in, conv_rwkv, rwkv_w0, rwkv_w2, rwkv_a0, rwkv_a2, rwkv_g2, rwkv_k_k, rwkv_k_a, rwkv_r_k, rwkv_lnx_w, rwkv_lnx_b, mla_q_norm, mla_w_uq, mla_kv_norm, mla_w_ukv, w_out, norm_mix_post, norm_mem_pre, norm_memtok, mem_wq, mem_wkv, mem_wo, norm_mem_post, norm_mlp_pre, mlp_w1, mlp_w2, norm_mlp_post):
    depth = w_in.shape[0]
    C = RWKV_WIDTH
    head_of = jnp.arange(C) // HEAD_DIM
    seg = (head_of[:, None] == head_of[None, :]).astype(BF16)
    seg_mean = (seg.astype(F32) / HEAD_DIM).astype(BF16)
    inv_freq = ROPE_THETA ** (-jnp.arange(0, QK_ROPE_DIM, 2, dtype=F32) / QK_ROPE_DIM)
    half = QK_ROPE_DIM // 2
    freq = _rope_tile(jnp.concatenate([inv_freq, inv_freq])[None, :])
    sign = _rope_tile(jnp.concatenate([-jnp.ones((half,), F32), jnp.ones((half,), F32)])[None, :])
    pos3 = positions[:, :, None]
    step = jnp.arange(CHUNK)
    tri = jnp.stack([step[None, :] <= step[:, None], step[None, :] >= step[:, None]]).astype(F32)
    row = lambda t: t.reshape(1, -1)

    for l in range(depth):
        w = w_in[l]
        mla0 = RWKV_COLS
        w_rope = w[:, mla0 + Q_LORA_RANK + KV_LORA_RANK:]
        w_r = _pad_cols(w[:, :RWKV_COLS], RWKV_TILE_COLS).astype(BF16)
        w_m = jnp.concatenate([w[:, mla0:mla0 + Q_LORA_RANK + KV_LORA_RANK], _rope_tile(w_rope),
                               _rope_tile(_swap_halves(w_rope))], axis=1).astype(BF16)
        conv_p = _pad_cols(conv_rwkv[l], RWKV_TILE_COLS)
        z_rwkv, z_mla = _in_proj(x, row(norm_mix_pre[l]), w_r, w_m, conv_p, tm=512)

        w2_p = jnp.stack([_lora_rows(rwkv_w2[l, d], d * DECAY_LORA, DECAY_LORA) for d in range(2)])
        a_lo = 2 * DECAY_LORA
        a2_p = jnp.stack([_lora_rows(rwkv_a2[l, d], a_lo + d * ICLR_LORA, ICLR_LORA) for d in range(2)])
        g2_p = _lora_rows(rwkv_g2[l], a_lo + 2 * ICLR_LORA, GATE_LORA)
        o_f, o_b, bonus_f, bonus_b, gate = _rwkv(z_rwkv, rwkv_w0[l], w2_p, rwkv_a0[l], a2_p, g2_p,
                                                 row(rwkv_k_k[l]), row(rwkv_k_a[l]), row(rwkv_r_k[l]), seg, tri, tm=256)

        qk = QK_NOPE_DIM + QK_ROPE_DIM
        w_uq = mla_w_uq[l].reshape(Q_LORA_RANK, MLA_HEADS, qk)
        wq_p = jnp.pad(w_uq, ((0, 0), (0, 0), (0, LANES - qk))).reshape(Q_LORA_RANK, MLA_HEADS * LANES)
        uq_rope = w_uq[:, :, QK_NOPE_DIM:]
        uq_sw = jnp.concatenate([uq_rope[:, :, half:], uq_rope[:, :, :half]], axis=2)
        wq_sw = jnp.pad(uq_sw, ((0, 0), (0, 0), (QK_NOPE_DIM, LANES - qk))).reshape(Q_LORA_RANK, MLA_HEADS * LANES)
        w_ukv = mla_w_ukv[l].reshape(KV_LORA_RANK, MLA_HEADS, QK_NOPE_DIM + V_HEAD_DIM)
        wk_p = jnp.pad(w_ukv[:, :, :QK_NOPE_DIM], ((0, 0), (0, 0), (0, LANES - QK_NOPE_DIM)))
        wv_p = jnp.pad(w_ukv[:, :, QK_NOPE_DIM:], ((0, 0), (0, 0), (0, LANES - V_HEAD_DIM)))
        wkv_p = jnp.concatenate([wk_p.reshape(KV_LORA_RANK, MLA_HEADS * LANES),
                                 wv_p.reshape(KV_LORA_RANK, MLA_HEADS * LANES)], axis=1)
        y_mla = _mla_attn(z_mla, pos3, freq, sign, row(mla_q_norm[l]), wq_p.astype(BF16), wq_sw.astype(BF16),
                          row(mla_kv_norm[l]), wkv_p.astype(BF16), tq=512)

        wo = w_out[l].astype(BF16)
        x = _mix_out(x, o_f, o_b, bonus_f, bonus_b, gate, row(rwkv_lnx_w[l]), row(rwkv_lnx_b[l]), seg_mean,
                     y_mla, wo[:C], wo[C:], row(norm_mix_post[l]), tm=512)

        k_mem, v_mem = _mem_kv(mem, row(norm_memtok[l]), mem_wkv[l].astype(BF16))
        x = _mem_attn(x, k_mem, v_mem, row(norm_mem_pre[l]), mem_wq[l].astype(BF16), mem_wo[l].astype(BF16),
                      row(norm_mem_post[l]), tm=512)

        x = _mlp(x, row(norm_mlp_pre[l]), mlp_w1[l].astype(BF16), mlp_w2[l].astype(BF16),
                 row(norm_mlp_post[l]), tm=256)
    return x
```

```python
import math

import jax
import jax.numpy as jnp
from jax import lax
from jax.experimental import pallas as pl
from jax.experimental.pallas import tpu as pltpu

F32 = jnp.float32
BF16 = jnp.bfloat16

D_MODEL = 1024
NORM_EPS = 1e-6

RWKV_HEADS = 8
HEAD_DIM = 64
RWKV_WIDTH = RWKV_HEADS * HEAD_DIM
DECAY_LORA = 32
ICLR_LORA = 32
GATE_LORA = 96
LORA_COLS = 2 * DECAY_LORA + 2 * ICLR_LORA + GATE_LORA
LORA_TILE = 256
RWKV_COLS = 3 * RWKV_WIDTH + LORA_COLS
RWKV_TILE_COLS = 3 * RWKV_WIDTH + LORA_TILE
LN_X_EPS = 64e-5
CHUNK = 64
CHUNK_GROUP = 2
DECAY_SCALE = math.exp(-0.5)

MLA_HEADS = 8
QK_NOPE_DIM = 64
QK_ROPE_DIM = 32
V_HEAD_DIM = 64
MLA_WIDTH = MLA_HEADS * V_HEAD_DIM
Q_LORA_RANK = 256
KV_LORA_RANK = 128
ROPE_THETA = 10000.0
LOG2_E = math.log2(math.e)
LANES = 128
MLA_TILE_COLS = Q_LORA_RANK + KV_LORA_RANK + 2 * LANES

MEM_HEADS = 4
MEM_HEAD_DIM = D_MODEL // MEM_HEADS
D_FF = 4 * D_MODEL

VMEM_LIMIT = 56 * 1024 * 1024


def _mm(a, b):
    return jnp.dot(a.astype(BF16), b.astype(BF16), preferred_element_type=F32)


def _split2(t):
    hi = t.astype(BF16)
    lo = (t - hi.astype(F32)).astype(BF16)
    return hi, lo


def _split3(t):
    hi = t.astype(BF16)
    rest = t - hi.astype(F32)
    mid = rest.astype(BF16)
    lo = (rest - mid.astype(F32)).astype(BF16)
    return hi, mid, lo


def _dot(a, b):
    return jnp.dot(a, b, preferred_element_type=F32)


def _mm_x3(a, b):
    a_hi, a_lo = _split2(a)
    b_hi, b_lo = _split2(b)
    return _dot(a_hi, b_hi) + (_dot(a_hi, b_lo) + _dot(a_lo, b_hi))


def _mm_exact_lhs(a, b):
    a = a.astype(BF16)
    hi, mid, lo = _split3(b)
    return _dot(a, hi) + (_dot(a, mid) + _dot(a, lo))


def _mm_nt(a, b):
    return lax.dot_general(a.astype(BF16), b.astype(BF16), (((1,), (1,)), ((), ())),
                           preferred_element_type=F32)


def _mm_tn(a, b):
    return lax.dot_general(a.astype(BF16), b.astype(BF16), (((0,), (0,)), ((), ())),
                           preferred_element_type=F32)


def _rms(x, g, eps=NORM_EPS):
    return x * lax.rsqrt(jnp.mean(x * x, axis=-1, keepdims=True) + eps) * g


def _sigmoid(x):
    return 0.5 * jnp.tanh(0.5 * x) + 0.5


def _params(*sem):
    return pltpu.CompilerParams(dimension_semantics=sem, vmem_limit_bytes=VMEM_LIMIT)


def _const_spec(shape):
    nd = len(shape)
    return pl.BlockSpec(shape, lambda *_: (0,) * nd)


def _in_proj_kernel(x_ref, xp_ref, xn_ref, g_ref, wr_ref, wm_ref, conv_ref, zr_ref, zm_ref):
    i = pl.program_id(1)
    last = pl.num_programs(1) - 1
    tm = x_ref.shape[1]
    g = g_ref[...]
    h = _rms(x_ref[0], g).astype(BF16)
    zm_ref[0] = jnp.dot(h, wm_ref[...], preferred_element_type=F32)
    z = jnp.dot(h, wr_ref[...], preferred_element_type=F32)
    halo = _rms(jnp.concatenate([xp_ref[0], xn_ref[0]], axis=0), g).astype(BF16)
    z_halo = jnp.dot(halo, wr_ref[...], preferred_element_type=F32)
    prev_row = jnp.where(i == 0, 0.0, z_halo[7:8, :])
    next_row = jnp.where(i == last, 0.0, z_halo[8:9, :])
    row = lax.broadcasted_iota(jnp.int32, (tm, 1), 0)
    z_dn = jnp.where(row == 0, prev_row, pltpu.roll(z, 1, 0))
    z_up = jnp.where(row == tm - 1, next_row, pltpu.roll(z, tm - 1, 0))
    zr_ref[0] = conv_ref[0:1, :] * z_dn + conv_ref[1:2, :] * z + conv_ref[2:3, :] * z_up


def _in_proj(x, g, w_r, w_m, conv_p, tm):
    B, S, D = x.shape
    halo = tm // 8
    n_halo = S // 8
    return pl.pallas_call(
        _in_proj_kernel,
        grid=(B, S // tm),
        in_specs=[pl.BlockSpec((1, tm, D), lambda b, i: (b, i, 0)),
                  pl.BlockSpec((1, 8, D), lambda b, i: (b, jnp.maximum(i * halo - 1, 0), 0)),
                  pl.BlockSpec((1, 8, D), lambda b, i: (b, jnp.minimum((i + 1) * halo, n_halo - 1), 0)),
                  _const_spec((1, D)),
                  _const_spec(w_r.shape),
                  _const_spec(w_m.shape),
                  _const_spec(conv_p.shape)],
        out_specs=[pl.BlockSpec((1, tm, RWKV_TILE_COLS), lambda b, i: (b, i, 0)),
                   pl.BlockSpec((1, tm, MLA_TILE_COLS), lambda b, i: (b, i, 0))],
        out_shape=[jax.ShapeDtypeStruct((B, S, RWKV_TILE_COLS), F32),
                   jax.ShapeDtypeStruct((B, S, MLA_TILE_COLS), F32)],
        compiler_params=_params("parallel", "parallel"),
        name="in_proj",
    )(x, x, x, g, w_r, w_m, conv_p)


def _chunk_units(units, eye, eye_hi):
    L, N = units[0][0].shape
    zeros = jnp.zeros((L, N), F32)
    AA = [_mm_nt(jnp.concatenate([At, Rt], axis=0), jnp.concatenate([Bt, Kt], axis=0))
          for (At, Rt, Bt, Kt, *_) in units]
    A_a = [jnp.where(u[8], aa[:L, :], 0.0) for u, aa in zip(units, AA)]
    A_r = [jnp.where(u[9], aa[L:, :], 0.0) for u, aa in zip(units, AA)]
    AkV = [_mm(a[:, L:], u[6]) for u, a in zip(units, A_a)]
    low = lax.broadcasted_iota(jnp.int32, (L, 2 * L), 1) < L
    S = [jnp.where(low, a, eye_hi) for a in A_a]
    span = 1
    while span < L:
        R = [_mm(s[:, :L], s) for s in S]
        S = [jnp.where(low, r, r + s) for r, s in zip(R, S)]
        span *= 2
    W = [_mm(s[:, L:], jnp.concatenate([u[0], akv], axis=1)) for s, u, akv in zip(S, units, AkV)]
    Z = [jnp.concatenate([w, jnp.concatenate([zeros, u[6]], axis=1)], axis=0) for u, w in zip(units, W)]
    Q = [jnp.concatenate([u[1], zeros], axis=1) + _mm(a_r, z) for u, a_r, z in zip(units, A_r, Z)]
    GH = [jnp.concatenate([eye * u[7], zeros], axis=1) + _mm_tn(jnp.concatenate([u[4], u[5]], axis=0), z)
          for u, z in zip(units, Z)]
    return Q, GH


def _rwkv_kernel(zf_ref, zb_ref,
                 w0_ref, w2_ref, a0_ref, a2_ref, g2_ref, kk_ref, ka_ref, rk_ref, seg_ref, tri_ref,
                 of_ref, ob_ref, bonus_f_ref, bonus_b_ref, g_ref,
                 r_s, v_s, kk_s, b_s, kd_s, lw_s, state_s):
    i = pl.program_id(1)
    n_tiles = pl.num_programs(1)
    tm = zf_ref.shape[1]
    L, N = CHUNK, HEAD_DIM
    n_chunks = tm // L

    @pl.when(i == 0)
    def _():
        state_s[...] = jnp.zeros_like(state_s)

    seg = seg_ref[...]

    def seg_sum(t):
        return _dot(t.astype(BF16), seg)

    for d, (z_ref, bonus_ref) in enumerate(((zf_ref, bonus_f_ref), (zb_ref, bonus_b_ref))):
        r = z_ref[0, :, 0:RWKV_WIDTH]
        k = z_ref[0, :, RWKV_WIDTH:2 * RWKV_WIDTH]
        v = z_ref[0, :, 2 * RWKV_WIDTH:3 * RWKV_WIDTH]
        lora = z_ref[0, :, 3 * RWKV_WIDTH:]
        kk = k * kk_ref[...]
        kk = kk * lax.rsqrt(jnp.maximum(seg_sum(kk * kk), 1e-24))
        w_pre = w0_ref[d:d + 1, :] + _mm(jnp.tanh(lora), w2_ref[d])
        lw_s[d] = -DECAY_SCALE * _sigmoid(w_pre)
        alpha = _sigmoid(a0_ref[d:d + 1, :] + _mm(lora, a2_ref[d]))
        kd = k * (1.0 + (alpha - 1.0) * ka_ref[...])
        r_s[d] = r
        v_s[d] = v
        kk_s[d] = kk
        kd_s[d] = kd
        b_s[d] = kk * alpha
        bonus_ref[0] = seg_sum(r * kd * rk_ref[...]) * v
        if d == 0:
            g_ref[0] = _mm(_sigmoid(lora), g2_ref[...])

    ri = lax.broadcasted_iota(jnp.int32, (L, L), 0)
    ci = lax.broadcasted_iota(jnp.int32, (L, L), 1)
    eye = (ci == ri).astype(F32)
    ri2 = lax.broadcasted_iota(jnp.int32, (L, 2 * L), 0)
    ci2 = lax.broadcasted_iota(jnp.int32, (L, 2 * L), 1)
    eye_hi = (ci2 == ri2 + L).astype(F32)
    ci2 = jnp.where(ci2 >= L, ci2 - L, ci2)
    m_strict = (ci2 < ri2, ci2 > ri2)
    m_incl = (ci2 <= ri2, ci2 >= ri2)
    out_refs = (of_ref, ob_ref)

    def chunk_units(c):
        rows_d = (pl.ds(pl.multiple_of(c * L, L), L), pl.ds(pl.multiple_of((n_chunks - 1 - c) * L, L), L))
        units = []
        for d in range(2):
            rows = rows_d[d]
            lw = lw_s[d, rows, :]
            bc = b_s[d, rows, :]
            kdc = kd_s[d, rows, :]
            vc = v_s[d, rows, :]
            cum = _mm_exact_lhs(tri_ref[d], lw)
            tot = cum[L - 1:L, :] if d == 0 else cum[0:1, :]
            e_in = jnp.exp(-cum)
            e_rem = jnp.exp(tot - cum)
            e_tot = jnp.exp(tot)
            At = -kk_s[d, rows, :] * jnp.exp(cum - lw)
            Rt = r_s[d, rows, :] * jnp.exp(cum)
            Bt = bc * e_in
            Kt = kdc * e_in
            Bh = bc * e_rem
            Kh = kdc * e_rem
            for h in range(RWKV_HEADS):
                sl = slice(h * HEAD_DIM, (h + 1) * HEAD_DIM)
                units.append((At[:, sl], Rt[:, sl], Bt[:, sl], Kt[:, sl], Bh[:, sl], Kh[:, sl],
                              vc[:, sl], e_tot[:, sl], m_strict[d], m_incl[d]))
        return rows_d, units

    n_units = 2 * RWKV_HEADS

    def group_body(j, carry):
        rows, units = [], []
        for g in range(CHUNK_GROUP):
            rows_d, chunk = chunk_units(j * CHUNK_GROUP + g)
            rows.append(rows_d)
            units.extend(chunk)
        Q, GH = _chunk_units(units, eye, eye_hi)
        M = [state_s[u] for u in range(n_units)]
        for g in range(CHUNK_GROUP):
            Qg, GHg = Q[g * n_units:(g + 1) * n_units], GH[g * n_units:(g + 1) * n_units]
            out = [_mm(q[:, :N], m) + q[:, N:] for q, m in zip(Qg, M)]
            M = [_mm_x3(gh[:, :N], m) + gh[:, N:] for gh, m in zip(GHg, M)]
            for d in range(2):
                for h in range(RWKV_HEADS):
                    out_refs[d][0, rows[g][d], h * N:(h + 1) * N] = out[d * RWKV_HEADS + h]
        for u in range(n_units):
            state_s[u] = M[u]
        return carry

    lax.fori_loop(0, n_chunks // CHUNK_GROUP, group_body, 0)


def _rwkv(z_rwkv, w0, w2_p, a0, a2_p, g2_p, k_k, k_a, r_k, seg, tri, tm):
    B, S, ZC = z_rwkv.shape
    C = RWKV_WIDTH
    n_tiles = S // tm
    fwd = lambda b, i: (b, i, 0)
    bwd = lambda b, i: (b, n_tiles - 1 - i, 0)
    out_f = pl.BlockSpec((1, tm, C), fwd)
    out_b = pl.BlockSpec((1, tm, C), bwd)
    out_sds = jax.ShapeDtypeStruct((B, S, C), F32)
    return pl.pallas_call(
        _rwkv_kernel,
        grid=(B, n_tiles),
        in_specs=[pl.BlockSpec((1, tm, ZC), fwd),
                  pl.BlockSpec((1, tm, ZC), bwd),
                  _const_spec(w0.shape), _const_spec(w2_p.shape),
                  _const_spec(a0.shape), _const_spec(a2_p.shape), _const_spec(g2_p.shape),
                  _const_spec(k_k.shape), _const_spec(k_a.shape), _const_spec(r_k.shape),
                  _const_spec(seg.shape), _const_spec(tri.shape)],
        out_specs=[out_f, out_b, out_f, out_b, out_f],
        out_shape=[out_sds] * 5,
        scratch_shapes=[pltpu.VMEM((2, tm, C), F32)] * 6
                       + [pltpu.VMEM((2 * RWKV_HEADS, HEAD_DIM, HEAD_DIM), F32)],
        compiler_params=_params("parallel", "arbitrary"),
        name="rwkv",
    )(z_rwkv, z_rwkv, w0, w2_p, a0, a2_p, g2_p, k_k, k_a, r_k, seg, tri)


def _rope_tiles(pos, freq, sign):
    ang = pos.astype(F32) * freq
    lane = lax.broadcasted_iota(jnp.int32, ang.shape, 1)
    cos_t = jnp.where(lane < QK_NOPE_DIM, 1.0, jnp.where(lane < QK_NOPE_DIM + QK_ROPE_DIM, jnp.cos(ang), 0.0))
    sin_t = jnp.sin(ang) * sign
    return cos_t, sin_t


def _mla_kernel(zq_ref, zkv_ref, pos_ref, freq_ref, sign_ref, qn_ref, wq_ref, wqs_ref, kvn_ref, wkv_ref,
                y_ref, k_s, v_s, cos_s, sin_s):
    i = pl.program_id(1)
    tq = zq_ref.shape[1]
    scale = (QK_NOPE_DIM + QK_ROPE_DIM) ** -0.5
    c_kv_lo, c_kv_hi = Q_LORA_RANK, Q_LORA_RANK + KV_LORA_RANK

    @pl.when(i == 0)
    def _():
        zkv = zkv_ref[0]
        kvn = _rms(zkv[:, c_kv_lo:c_kv_hi], kvn_ref[...]).astype(BF16)
        kvu = jnp.dot(kvn, wkv_ref[...], preferred_element_type=F32)
        cos_t, sin_t = _rope_tiles(pos_ref[0], freq_ref[...], sign_ref[...])
        cos_s[...] = cos_t
        sin_s[...] = sin_t
        k_rot = zkv[:, c_kv_hi:c_kv_hi + LANES] * cos_t + zkv[:, c_kv_hi + LANES:] * sin_t
        lane = lax.broadcasted_iota(jnp.int32, (1, LANES), 1)
        ones_col = (lane == V_HEAD_DIM).astype(F32)
        for h in range(MLA_HEADS):
            k_s[h] = (kvu[:, h * LANES:(h + 1) * LANES] + k_rot).astype(BF16)
            v_s[h] = (kvu[:, (MLA_HEADS + h) * LANES:(MLA_HEADS + h + 1) * LANES] + ones_col).astype(BF16)

    zq = zq_ref[0]
    qn = _rms(zq[:, :Q_LORA_RANK], qn_ref[...]).astype(BF16)
    q_a = jnp.dot(qn, wq_ref[...], preferred_element_type=F32)
    q_b = jnp.dot(qn, wqs_ref[...], preferred_element_type=F32)
    rows = pl.ds(pl.multiple_of(i * tq, tq), tq)
    cos_q = cos_s[rows, :] * (scale * LOG2_E)
    sin_q = sin_s[rows, :] * (scale * LOG2_E)

    def scores(h):
        sl = slice(h * LANES, (h + 1) * LANES)
        qh = (q_a[:, sl] * cos_q + q_b[:, sl] * sin_q).astype(BF16)
        return lax.dot_general(qh, k_s[h], (((1,), (1,)), ((), ())), preferred_element_type=F32)

    s = scores(0)
    for h in range(MLA_HEADS):
        s_next = scores(h + 1) if h + 1 < MLA_HEADS else None
        p = jnp.exp2((s - jnp.max(s, axis=-1, keepdims=True)).astype(BF16))
        o = jnp.dot(p, v_s[h], preferred_element_type=F32)
        y_ref[0, :, h * V_HEAD_DIM:(h + 1) * V_HEAD_DIM] = o[:, :V_HEAD_DIM] / o[:, V_HEAD_DIM:V_HEAD_DIM + 1]
        s = s_next


def _mla_attn(z_mla, pos3, freq, sign, q_norm, wq_p, wq_sw, kv_norm, wkv_p, tq):
    B, S, ZC = z_mla.shape
    return pl.pallas_call(
        _mla_kernel,
        grid=(B, S // tq),
        in_specs=[pl.BlockSpec((1, tq, ZC), lambda b, i: (b, i, 0)),
                  pl.BlockSpec((1, S, ZC), lambda b, i: (b, 0, 0)),
                  pl.BlockSpec((1, S, 1), lambda b, i: (b, 0, 0)),
                  _const_spec(freq.shape), _const_spec(sign.shape), _const_spec(q_norm.shape),
                  _const_spec(wq_p.shape), _const_spec(wq_sw.shape), _const_spec(kv_norm.shape),
                  _const_spec(wkv_p.shape)],
        out_specs=pl.BlockSpec((1, tq, MLA_WIDTH), lambda b, i: (b, i, 0)),
        out_shape=jax.ShapeDtypeStruct((B, S, MLA_WIDTH), F32),
        scratch_shapes=[pltpu.VMEM((MLA_HEADS, S, LANES), BF16), pltpu.VMEM((MLA_HEADS, S, LANES), BF16),
                        pltpu.VMEM((S, LANES), F32), pltpu.VMEM((S, LANES), F32)],
        compiler_params=_params("parallel", "arbitrary"),
        name="mla_attn",
    )(z_mla, z_mla, pos3, freq, sign, q_norm, wq_p, wq_sw, kv_norm, wkv_p)


def _mix_out_kernel(x_ref, of_ref, ob_ref, bf_ref, bb_ref, gate_ref, lnw_ref, lnb_ref, segm_ref, ym_ref,
                    wr_ref, wm_ref, g_ref, o_ref):
    segm = segm_ref[...]

    o = of_ref[0] + ob_ref[0]
    o_hi, o_lo = _split2(o)
    oc = o - (_dot(o_hi, segm) + _dot(o_lo, segm))
    var = _dot((oc * oc).astype(BF16), segm)
    o = oc * lax.rsqrt(var + LN_X_EPS) * lnw_ref[...] + lnb_ref[...]
    y_rwkv = (o + (bf_ref[0] + bb_ref[0])) * gate_ref[0]
    y = (jnp.dot(y_rwkv.astype(BF16), wr_ref[...], preferred_element_type=F32)
         + jnp.dot(ym_ref[0].astype(BF16), wm_ref[...], preferred_element_type=F32))
    o_ref[0] = x_ref[0] + _rms(y, g_ref[...])


def _mix_out(x, o_f, o_b, bonus_f, bonus_b, gate, lnw, lnb, segm, y_mla, w_r, w_m, g, tm):
    B, S, D = x.shape
    row_spec = lambda cols: pl.BlockSpec((1, tm, cols), lambda b, i: (b, i, 0))
    C = RWKV_WIDTH
    return pl.pallas_call(
        _mix_out_kernel,
        grid=(B, S // tm),
        in_specs=[row_spec(D), row_spec(C), row_spec(C), row_spec(C), row_spec(C), row_spec(C),
                  _const_spec(lnw.shape), _const_spec(lnb.shape), _const_spec(segm.shape), row_spec(MLA_WIDTH),
                  _const_spec(w_r.shape), _const_spec(w_m.shape), _const_spec(g.shape)],
        out_specs=row_spec(D),
        out_shape=jax.ShapeDtypeStruct((B, S, D), F32),
        compiler_params=_params("parallel", "parallel"),
        name="mix_out",
    )(x, o_f, o_b, bonus_f, bonus_b, gate, lnw, lnb, segm, y_mla, w_r, w_m, g)


def _mem_kv_kernel(mem_ref, g_ref, w_ref, k_ref, v_ref):
    m = _rms(mem_ref[0], g_ref[...]).astype(BF16)
    kv = jnp.dot(m, w_ref[...], preferred_element_type=F32)
    k_ref[0] = kv[:, :D_MODEL].astype(BF16)
    v_ref[0] = kv[:, D_MODEL:].astype(BF16)


def _mem_kv(mem, g, wkv):
    B, T, D = mem.shape
    return pl.pallas_call(
        _mem_kv_kernel,
        grid=(B,),
        in_specs=[pl.BlockSpec((1, T, D), lambda b: (b, 0, 0)), _const_spec(g.shape), _const_spec(wkv.shape)],
        out_specs=[pl.BlockSpec((1, T, D), lambda b: (b, 0, 0)), pl.BlockSpec((1, T, D), lambda b: (b, 0, 0))],
        out_shape=[jax.ShapeDtypeStruct((B, T, D), BF16), jax.ShapeDtypeStruct((B, T, D), BF16)],
        compiler_params=_params("parallel"),
        name="mem_kv",
    )(mem, g, wkv)


def _mem_attn_kernel(x_ref, k_ref, v_ref, gpre_ref, wq_ref, wo_ref, gpost_ref, o_ref, att_s):
    x = x_ref[0]
    h = _rms(x, gpre_ref[...]).astype(BF16)
    q = jnp.dot(h, wq_ref[...], preferred_element_type=F32) * (MEM_HEAD_DIM ** -0.5)
    for hd in range(MEM_HEADS):
        sl = slice(hd * MEM_HEAD_DIM, (hd + 1) * MEM_HEAD_DIM)
        s = lax.dot_general(q[:, sl].astype(BF16), k_ref[0, :, sl], (((1,), (1,)), ((), ())),
                            preferred_element_type=F32)
        p = jnp.exp(s - jnp.max(s, axis=-1, keepdims=True))
        denom = jnp.sum(p, axis=-1, keepdims=True)
        o = jnp.dot(p.astype(BF16), v_ref[0, :, sl], preferred_element_type=F32)
        att_s[:, sl] = (o / denom).astype(BF16)
    y = jnp.dot(att_s[...], wo_ref[...], preferred_element_type=F32)
    o_ref[0] = x + _rms(y, gpost_ref[...])


def _mem_attn(x, k, v, g_pre, wq, wo, g_post, tm):
    B, S, D = x.shape
    T = k.shape[1]
    row_spec = pl.BlockSpec((1, tm, D), lambda b, i: (b, i, 0))
    kv_spec = pl.BlockSpec((1, T, D), lambda b, i: (b, 0, 0))
    return pl.pallas_call(
        _mem_attn_kernel,
        grid=(B, S // tm),
        in_specs=[row_spec, kv_spec, kv_spec, _const_spec(g_pre.shape), _const_spec(wq.shape),
                  _const_spec(wo.shape), _const_spec(g_post.shape)],
        out_specs=row_spec,
        out_shape=jax.ShapeDtypeStruct((B, S, D), F32),
        scratch_shapes=[pltpu.VMEM((tm, D), BF16)],
        compiler_params=_params("parallel", "parallel"),
        name="mem_attn",
    )(x, k, v, g_pre, wq, wo, g_post)


def _mlp_kernel(x_ref, gpre_ref, w1_ref, w2_ref, gpost_ref, o_ref):
    x = x_ref[0]
    h = _rms(x, gpre_ref[...]).astype(BF16)
    u = jnp.maximum(jnp.dot(h, w1_ref[...], preferred_element_type=F32), 0.0)
    y = jnp.dot((u * u).astype(BF16), w2_ref[...], preferred_element_type=F32)
    o_ref[0] = x + _rms(y, gpost_ref[...])


def _mlp(x, g_pre, w1, w2, g_post, tm):
    B, S, D = x.shape
    row_spec = pl.BlockSpec((1, tm, D), lambda b, i: (b, i, 0))
    return pl.pallas_call(
        _mlp_kernel,
        grid=(B, S // tm),
        in_specs=[row_spec, _const_spec(g_pre.shape), _const_spec(w1.shape), _const_spec(w2.shape),
                  _const_spec(g_post.shape)],
        out_specs=row_spec,
        out_shape=jax.ShapeDtypeStruct((B, S, D), F32),
        compiler_params=_params("parallel", "parallel"),
        name="mlp",
    )(x, g_pre, w1, w2, g_post)


def _pad_cols(w, n):
    return jnp.pad(w, ((0, 0), (0, n - w.shape[1])))


def _rope_tile(w):
    return jnp.pad(w, ((0, 0), (QK_NOPE_DIM, LANES - QK_NOPE_DIM - QK_ROPE_DIM)))


def _swap_halves(w):
    half = QK_ROPE_DIM // 2
    return jnp.concatenate([w[:, half:], w[:, :half]], axis=1)


def _lora_rows(w, lo, rows):
    return jnp.pad(w, ((lo, LORA_TILE - lo - rows), (0, 0)))


def kernel(x, mem, positions, norm_mix_pre, w_in, conv_rwkv, rwkv_w0, rwkv_w2, rwkv_a0, rwkv_a2, rwkv_g2, rwkv_k_k, rwkv_k_a, rwkv_r_k, rwkv_lnx_w, rwkv_lnx_b, mla_q_norm, mla_w_uq, mla_kv_norm, mla_w_ukv, w_out, norm_mix_post, norm_mem_pre, norm_memtok, mem_wq, mem_wkv, mem_wo, norm_mem_post, norm_mlp_pre, mlp_w1, mlp_w2, norm_mlp_post):
    depth = w_in.shape[0]
    C = RWKV_WIDTH
    head_of = jnp.arange(C) // HEAD_DIM
    seg = (head_of[:, None] == head_of[None, :]).astype(BF16)
    seg_mean = (seg.astype(F32) / HEAD_DIM).astype(BF16)
    inv_freq = ROPE_THETA ** (-jnp.arange(0, QK_ROPE_DIM, 2, dtype=F32) / QK_ROPE_DIM)
    half = QK_ROPE_DIM // 2
    freq = _rope_tile(jnp.concatenate([inv_freq, inv_freq])[None, :])
    sign = _rope_tile(jnp.concatenate([-jnp.ones((half,), F32), jnp.ones((half,), F32)])[None, :])
    pos3 = positions[:, :, None]
    step = jnp.arange(CHUNK)
    tri = jnp.stack([step[None, :] <= step[:, None], step[None, :] >= step[:, None]]).astype(F32)
    row = lambda t: t.reshape(1, -1)

    for l in range(depth):
        w = w_in[l]
        mla0 = RWKV_COLS
        w_rope = w[:, mla0 + Q_LORA_RANK + KV_LORA_RANK:]
        w_r = _pad_cols(w[:, :RWKV_COLS], RWKV_TILE_COLS).astype(BF16)
        w_m = jnp.concatenate([w[:, mla0:mla0 + Q_LORA_RANK + KV_LORA_RANK], _rope_tile(w_rope),
                               _rope_tile(_swap_halves(w_rope))], axis=1).astype(BF16)
        conv_p = _pad_cols(conv_rwkv[l], RWKV_TILE_COLS)
        z_rwkv, z_mla = _in_proj(x, row(norm_mix_pre[l]), w_r, w_m, conv_p, tm=512)

        w2_p = jnp.stack([_lora_rows(rwkv_w2[l, d], d * DECAY_LORA, DECAY_LORA) for d in range(2)])
        a_lo = 2 * DECAY_LORA
        a2_p = jnp.stack([_lora_rows(rwkv_a2[l, d], a_lo + d * ICLR_LORA, ICLR_LORA) for d in range(2)])
        g2_p = _lora_rows(rwkv_g2[l], a_lo + 2 * ICLR_LORA, GATE_LORA)
        o_f, o_b, bonus_f, bonus_b, gate = _rwkv(z_rwkv, rwkv_w0[l], w2_p, rwkv_a0[l], a2_p, g2_p,
                                                 row(rwkv_k_k[l]), row(rwkv_k_a[l]), row(rwkv_r_k[l]), seg, tri, tm=256)

        qk = QK_NOPE_DIM + QK_ROPE_DIM
        w_uq = mla_w_uq[l].reshape(Q_LORA_RANK, MLA_HEADS, qk)
        wq_p = jnp.pad(w_uq, ((0, 0), (0, 0), (0, LANES - qk))).reshape(Q_LORA_RANK, MLA_HEADS * LANES)
        uq_rope = w_uq[:, :, QK_NOPE_DIM:]
        uq_sw = jnp.concatenate([uq_rope[:, :, half:], uq_rope[:, :, :half]], axis=2)
        wq_sw = jnp.pad(uq_sw, ((0, 0), (0, 0), (QK_NOPE_DIM, LANES - qk))).reshape(Q_LORA_RANK, MLA_HEADS * LANES)
        w_ukv = mla_w_ukv[l].reshape(KV_LORA_RANK, MLA_HEADS, QK_NOPE_DIM + V_HEAD_DIM)
        wk_p = jnp.pad(w_ukv[:, :, :QK_NOPE_DIM], ((0, 0), (0, 0), (0, LANES - QK_NOPE_DIM)))
        wv_p = jnp.pad(w_ukv[:, :, QK_NOPE_DIM:], ((0, 0), (0, 0), (0, LANES - V_HEAD_DIM)))
        wkv_p = jnp.concatenate([wk_p.reshape(KV_LORA_RANK, MLA_HEADS * LANES),
                                 wv_p.reshape(KV_LORA_RANK, MLA_HEADS * LANES)], axis=1)
        y_mla = _mla_attn(z_mla, pos3, freq, sign, row(mla_q_norm[l]), wq_p.astype(BF16), wq_sw.astype(BF16),
                          row(mla_kv_norm[l]), wkv_p.astype(BF16), tq=512)

        wo = w_out[l].astype(BF16)
        x = _mix_out(x, o_f, o_b, bonus_f, bonus_b, gate, row(rwkv_lnx_w[l]), row(rwkv_lnx_b[l]), seg_mean,
                     y_mla, wo[:C], wo[C:], row(norm_mix_post[l]), tm=512)

        k_mem, v_mem = _mem_kv(mem, row(norm_memtok[l]), mem_wkv[l].astype(BF16))
        x = _mem_attn(x, k_mem, v_mem, row(norm_mem_pre[l]), mem_wq[l].astype(BF16), mem_wo[l].astype(BF16),
                      row(norm_mem_post[l]), tm=512)

        x = _mlp(x, row(norm_mlp_pre[l]), mlp_w1[l].astype(BF16), mlp_w2[l].astype(BF16),
                 row(norm_mlp_post[l]), tm=256)
    return x
```

```python
import math

import jax
import jax.numpy as jnp
from jax import lax
from jax.experimental import pallas as pl
from jax.experimental.pallas import tpu as pltpu

F32 = jnp.float32
BF16 = jnp.bfloat16

D_MODEL = 1024
NORM_EPS = 1e-6

RWKV_HEADS = 8
HEAD_DIM = 64
RWKV_WIDTH = RWKV_HEADS * HEAD_DIM
DECAY_LORA = 32
ICLR_LORA = 32
GATE_LORA = 96
LORA_COLS = 2 * DECAY_LORA + 2 * ICLR_LORA + GATE_LORA
LORA_TILE = 256
RWKV_COLS = 3 * RWKV_WIDTH + LORA_COLS
RWKV_TILE_COLS = 3 * RWKV_WIDTH + LORA_TILE
LN_X_EPS = 64e-5
CHUNK = 64
CHUNK_GROUP = 2
DECAY_SCALE = math.exp(-0.5)

MLA_HEADS = 8
QK_NOPE_DIM = 64
QK_ROPE_DIM = 32
V_HEAD_DIM = 64
MLA_WIDTH = MLA_HEADS * V_HEAD_DIM
Q_LORA_RANK = 256
KV_LORA_RANK = 128
ROPE_THETA = 10000.0
LOG2_E = math.log2(math.e)
LANES = 128
MLA_TILE_COLS = Q_LORA_RANK + KV_LORA_RANK + 2 * LANES

MEM_HEADS = 4
MEM_HEAD_DIM = D_MODEL // MEM_HEADS
D_FF = 4 * D_MODEL

VMEM_LIMIT = 56 * 1024 * 1024


def _mm(a, b):
    return jnp.dot(a.astype(BF16), b.astype(BF16), preferred_element_type=F32)


def _split2(t):
    hi = t.astype(BF16)
    lo = (t - hi.astype(F32)).astype(BF16)
    return hi, lo


def _split3(t):
    hi = t.astype(BF16)
    rest = t - hi.astype(F32)
    mid = rest.astype(BF16)
    lo = (rest - mid.astype(F32)).astype(BF16)
    return hi, mid, lo


def _dot(a, b):
    return jnp.dot(a, b, preferred_element_type=F32)


def _mm_x3(a, b):
    a_hi, a_lo = _split2(a)
    b_hi, b_lo = _split2(b)
    return _dot(a_hi, b_hi) + (_dot(a_hi, b_lo) + _dot(a_lo, b_hi))


def _mm_exact_lhs(a, b):
    a = a.astype(BF16)
    hi, mid, lo = _split3(b)
    return _dot(a, hi) + (_dot(a, mid) + _dot(a, lo))


def _mm_nt(a, b):
    return lax.dot_general(a.astype(BF16), b.astype(BF16), (((1,), (1,)), ((), ())),
                           preferred_element_type=F32)


def _mm_tn(a, b):
    return lax.dot_general(a.astype(BF16), b.astype(BF16), (((0,), (0,)), ((), ())),
                           preferred_element_type=F32)


def _rms(x, g, eps=NORM_EPS):
    return x * lax.rsqrt(jnp.mean(x * x, axis=-1, keepdims=True) + eps) * g


def _sigmoid(x):
    return 0.5 * jnp.tanh(0.5 * x) + 0.5


def _params(*sem):
    return pltpu.CompilerParams(dimension_semantics=sem, vmem_limit_bytes=VMEM_LIMIT)


def _const_spec(shape):
    nd = len(shape)
    return pl.BlockSpec(shape, lambda *_: (0,) * nd)


def _in_proj_kernel(x_ref, xp_ref, xn_ref, g_ref, wr_ref, wm_ref, conv_ref, zr_ref, zm_ref):
    i = pl.program_id(1)
    last = pl.num_programs(1) - 1
    tm = x_ref.shape[1]
    g = g_ref[...]
    h = _rms(x_ref[0], g).astype(BF16)
    zm_ref[0] = jnp.dot(h, wm_ref[...], preferred_element_type=F32)
    z = jnp.dot(h, wr_ref[...], preferred_element_type=F32)
    halo = _rms(jnp.concatenate([xp_ref[0], xn_ref[0]], axis=0), g).astype(BF16)
    z_halo = jnp.dot(halo, wr_ref[...], preferred_element_type=F32)
    prev_row = jnp.where(i == 0, 0.0, z_halo[7:8, :])
    next_row = jnp.where(i == last, 0.0, z_halo[8:9, :])
    row = lax.broadcasted_iota(jnp.int32, (tm, 1), 0)
    z_dn = jnp.where(row == 0, prev_row, pltpu.roll(z, 1, 0))
    z_up = jnp.where(row == tm - 1, next_row, pltpu.roll(z, tm - 1, 0))
    zr_ref[0] = conv_ref[0:1, :] * z_dn + conv_ref[1:2, :] * z + conv_ref[2:3, :] * z_up


def _in_proj(x, g, w_r, w_m, conv_p, tm):
    B, S, D = x.shape
    halo = tm // 8
    n_halo = S // 8
    return pl.pallas_call(
        _in_proj_kernel,
        grid=(B, S // tm),
        in_specs=[pl.BlockSpec((1, tm, D), lambda b, i: (b, i, 0)),
                  pl.BlockSpec((1, 8, D), lambda b, i: (b, jnp.maximum(i * halo - 1, 0), 0)),
                  pl.BlockSpec((1, 8, D), lambda b, i: (b, jnp.minimum((i + 1) * halo, n_halo - 1), 0)),
                  _const_spec((1, D)),
                  _const_spec(w_r.shape),
                  _const_spec(w_m.shape),
                  _const_spec(conv_p.shape)],
        out_specs=[pl.BlockSpec((1, tm, RWKV_TILE_COLS), lambda b, i: (b, i, 0)),
                   pl.BlockSpec((1, tm, MLA_TILE_COLS), lambda b, i: (b, i, 0))],
        out_shape=[jax.ShapeDtypeStruct((B, S, RWKV_TILE_COLS), F32),
                   jax.ShapeDtypeStruct((B, S, MLA_TILE_COLS), F32)],
        compiler_params=_params("parallel", "parallel"),
        name="in_proj",
    )(x, x, x, g, w_r, w_m, conv_p)


def _chunk_units(units, eye, eye_hi):
    L, N = units[0][0].shape
    zeros = jnp.zeros((L, N), F32)
    AA = [_mm_nt(jnp.concatenate([At, Rt], axis=0), jnp.concatenate([Bt, Kt], axis=0))
          for (At, Rt, Bt, Kt, *_) in units]
    A_a = [jnp.where(u[8], aa[:L, :], 0.0) for u, aa in zip(units, AA)]
    A_r = [jnp.where(u[9], aa[L:, :], 0.0) for u, aa in zip(units, AA)]
    AkV = [_mm(a[:, L:], u[6]) for u, a in zip(units, A_a)]
    low = lax.broadcasted_iota(jnp.int32, (L, 2 * L), 1) < L
    S = [jnp.where(low, a, eye_hi) for a in A_a]
    span = 1
    while span < L:
        R = [_mm(s[:, :L], s) for s in S]
        S = [jnp.where(low, r, r + s) for r, s in zip(R, S)]
        span *= 2
    W = [_mm(s[:, L:], jnp.concatenate([u[0], akv], axis=1)) for s, u, akv in zip(S, units, AkV)]
    Z = [jnp.concatenate([w, jnp.concatenate([zeros, u[6]], axis=1)], axis=0) for u, w in zip(units, W)]
    lhs = [jnp.concatenate([a_r, jnp.concatenate([u[4], u[5]], axis=0).T], axis=0) for u, a_r in zip(units, A_r)]
    QGH = [_mm(l, z) for l, z in zip(lhs, Z)]
    Q = [jnp.concatenate([u[1], zeros], axis=1) + t[:L] for u, t in zip(units, QGH)]
    GH = [jnp.concatenate([eye * u[7], zeros], axis=1) + t[L:] for u, t in zip(units, QGH)]
    return Q, GH


def _rwkv_kernel(zf_ref, zb_ref,
                 w0_ref, w2_ref, a0_ref, a2_ref, g2_ref, kk_ref, ka_ref, rk_ref, seg_ref, tri_ref,
                 of_ref, ob_ref, bonus_f_ref, bonus_b_ref, g_ref,
                 r_s, v_s, kk_s, b_s, kd_s, lw_s, state_s):
    i = pl.program_id(1)
    n_tiles = pl.num_programs(1)
    tm = zf_ref.shape[1]
    L, N = CHUNK, HEAD_DIM
    n_chunks = tm // L

    @pl.when(i == 0)
    def _():
        state_s[...] = jnp.zeros_like(state_s)

    seg = seg_ref[...]

    def seg_sum(t):
        return _dot(t.astype(BF16), seg)

    for d, (z_ref, bonus_ref) in enumerate(((zf_ref, bonus_f_ref), (zb_ref, bonus_b_ref))):
        r = z_ref[0, :, 0:RWKV_WIDTH]
        k = z_ref[0, :, RWKV_WIDTH:2 * RWKV_WIDTH]
        v = z_ref[0, :, 2 * RWKV_WIDTH:3 * RWKV_WIDTH]
        lora = z_ref[0, :, 3 * RWKV_WIDTH:]
        kk = k * kk_ref[...]
        kk = kk * lax.rsqrt(jnp.maximum(seg_sum(kk * kk), 1e-24))
        w_pre = w0_ref[d:d + 1, :] + _mm(jnp.tanh(lora), w2_ref[d])
        lw_s[d] = -DECAY_SCALE * _sigmoid(w_pre)
        alpha = _sigmoid(a0_ref[d:d + 1, :] + _mm(lora, a2_ref[d]))
        kd = k * (1.0 + (alpha - 1.0) * ka_ref[...])
        r_s[d] = r
        v_s[d] = v
        kk_s[d] = kk
        kd_s[d] = kd
        b_s[d] = kk * alpha
        bonus_ref[0] = seg_sum(r * kd * rk_ref[...]) * v
        if d == 0:
            g_ref[0] = _mm(_sigmoid(lora), g2_ref[...])

    ri = lax.broadcasted_iota(jnp.int32, (L, L), 0)
    ci = lax.broadcasted_iota(jnp.int32, (L, L), 1)
    eye = (ci == ri).astype(F32)
    ri2 = lax.broadcasted_iota(jnp.int32, (L, 2 * L), 0)
    ci2 = lax.broadcasted_iota(jnp.int32, (L, 2 * L), 1)
    eye_hi = (ci2 == ri2 + L).astype(F32)
    ci2 = jnp.where(ci2 >= L, ci2 - L, ci2)
    m_strict = (ci2 < ri2, ci2 > ri2)
    m_incl = (ci2 <= ri2, ci2 >= ri2)
    out_refs = (of_ref, ob_ref)

    def chunk_units(c):
        rows_d = (pl.ds(pl.multiple_of(c * L, L), L), pl.ds(pl.multiple_of((n_chunks - 1 - c) * L, L), L))
        units = []
        for d in range(2):
            rows = rows_d[d]
            lw = lw_s[d, rows, :]
            bc = b_s[d, rows, :]
            kdc = kd_s[d, rows, :]
            vc = v_s[d, rows, :]
            cum = _mm_exact_lhs(tri_ref[d], lw)
            tot = cum[L - 1:L, :] if d == 0 else cum[0:1, :]
            e_in = jnp.exp(-cum)
            e_rem = jnp.exp(tot - cum)
            e_tot = jnp.exp(tot)
            At = -kk_s[d, rows, :] * jnp.exp(cum - lw)
            Rt = r_s[d, rows, :] * jnp.exp(cum)
            Bt = bc * e_in
            Kt = kdc * e_in
            Bh = bc * e_rem
            Kh = kdc * e_rem
            for h in range(RWKV_HEADS):
                sl = slice(h * HEAD_DIM, (h + 1) * HEAD_DIM)
                units.append((At[:, sl], Rt[:, sl], Bt[:, sl], Kt[:, sl], Bh[:, sl], Kh[:, sl],
                              vc[:, sl], e_tot[:, sl], m_strict[d], m_incl[d]))
        return rows_d, units

    n_units = 2 * RWKV_HEADS

    def group_body(j, carry):
        rows, units = [], []
        for g in range(CHUNK_GROUP):
            rows_d, chunk = chunk_units(j * CHUNK_GROUP + g)
            rows.append(rows_d)
            units.extend(chunk)
        Q, GH = _chunk_units(units, eye, eye_hi)
        M = [state_s[u] for u in range(n_units)]
        for g in range(CHUNK_GROUP):
            Qg, GHg = Q[g * n_units:(g + 1) * n_units], GH[g * n_units:(g + 1) * n_units]
            M_split = [_split2(m) for m in M]
            G_split = [_split2(gh[:, :N]) for gh in GHg]
            prod = [_dot(jnp.concatenate([q[:, :N].astype(BF16), g_hi, g_lo], axis=0), m_hi)
                    for q, (g_hi, g_lo), (m_hi, _) in zip(Qg, G_split, M_split)]
            out = [p[:L] + q[:, N:] for p, q in zip(prod, Qg)]
            M = [p[L:L + N] + (p[L + N:] + _dot(g_hi, m_lo)) + gh[:, N:]
                 for p, (g_hi, _), (_, m_lo), gh in zip(prod, G_split, M_split, GHg)]
            for d in range(2):
                for h in range(RWKV_HEADS):
                    out_refs[d][0, rows[g][d], h * N:(h + 1) * N] = out[d * RWKV_HEADS + h]
        for u in range(n_units):
            state_s[u] = M[u]
        return carry

    lax.fori_loop(0, n_chunks // CHUNK_GROUP, group_body, 0)


def _rwkv(z_rwkv, w0, w2_p, a0, a2_p, g2_p, k_k, k_a, r_k, seg, tri, tm):
    B, S, ZC = z_rwkv.shape
    C = RWKV_WIDTH
    n_tiles = S // tm
    fwd = lambda b, i: (b, i, 0)
    bwd = lambda b, i: (b, n_tiles - 1 - i, 0)
    out_f = pl.BlockSpec((1, tm, C), fwd)
    out_b = pl.BlockSpec((1, tm, C), bwd)
    out_sds = jax.ShapeDtypeStruct((B, S, C), F32)
    return pl.pallas_call(
        _rwkv_kernel,
        grid=(B, n_tiles),
        in_specs=[pl.BlockSpec((1, tm, ZC), fwd),
                  pl.BlockSpec((1, tm, ZC), bwd),
                  _const_spec(w0.shape), _const_spec(w2_p.shape),
                  _const_spec(a0.shape), _const_spec(a2_p.shape), _const_spec(g2_p.shape),
                  _const_spec(k_k.shape), _const_spec(k_a.shape), _const_spec(r_k.shape),
                  _const_spec(seg.shape), _const_spec(tri.shape)],
        out_specs=[out_f, out_b, out_f, out_b, out_f],
        out_shape=[out_sds] * 5,
        scratch_shapes=[pltpu.VMEM((2, tm, C), F32)] * 6
                       + [pltpu.VMEM((2 * RWKV_HEADS, HEAD_DIM, HEAD_DIM), F32)],
        compiler_params=_params("parallel", "arbitrary"),
        name="rwkv",
    )(z_rwkv, z_rwkv, w0, w2_p, a0, a2_p, g2_p, k_k, k_a, r_k, seg, tri)


def _rope_tiles(pos, freq, sign):
    ang = pos.astype(F32) * freq
    lane = lax.broadcasted_iota(jnp.int32, ang.shape, 1)
    cos_t = jnp.where(lane < QK_NOPE_DIM, 1.0, jnp.where(lane < QK_NOPE_DIM + QK_ROPE_DIM, jnp.cos(ang), 0.0))
    sin_t = jnp.sin(ang) * sign
    return cos_t, sin_t


def _mla_kernel(zq_ref, zkv_ref, pos_ref, freq_ref, sign_ref, qn_ref, wq_ref, wqs_ref, kvn_ref, wkv_ref,
                y_ref, k_s, v_s, cos_s, sin_s):
    i = pl.program_id(1)
    tq = zq_ref.shape[1]
    scale = (QK_NOPE_DIM + QK_ROPE_DIM) ** -0.5
    c_kv_lo, c_kv_hi = Q_LORA_RANK, Q_LORA_RANK + KV_LORA_RANK

    @pl.when(i == 0)
    def _():
        zkv = zkv_ref[0]
        kvn = _rms(zkv[:, c_kv_lo:c_kv_hi], kvn_ref[...]).astype(BF16)
        kvu = jnp.dot(kvn, wkv_ref[...], preferred_element_type=F32)
        cos_t, sin_t = _rope_tiles(pos_ref[0], freq_ref[...], sign_ref[...])
        cos_s[...] = cos_t
        sin_s[...] = sin_t
        k_rot = zkv[:, c_kv_hi:c_kv_hi + LANES] * cos_t + zkv[:, c_kv_hi + LANES:] * sin_t
        lane = lax.broadcasted_iota(jnp.int32, (1, LANES), 1)
        ones_col = (lane == V_HEAD_DIM).astype(F32)
        for h in range(MLA_HEADS):
            k_s[h] = (kvu[:, h * LANES:(h + 1) * LANES] + k_rot).astype(BF16)
            v_s[h] = (kvu[:, (MLA_HEADS + h) * LANES:(MLA_HEADS + h + 1) * LANES] + ones_col).astype(BF16)

    zq = zq_ref[0]
    qn = _rms(zq[:, :Q_LORA_RANK], qn_ref[...]).astype(BF16)
    q_a = jnp.dot(qn, wq_ref[...], preferred_element_type=F32)
    q_b = jnp.dot(qn, wqs_ref[...], preferred_element_type=F32)
    rows = pl.ds(pl.multiple_of(i * tq, tq), tq)
    cos_q = cos_s[rows, :] * (scale * LOG2_E)
    sin_q = sin_s[rows, :] * (scale * LOG2_E)

    def scores(h):
        sl = slice(h * LANES, (h + 1) * LANES)
        qh = (q_a[:, sl] * cos_q + q_b[:, sl] * sin_q).astype(BF16)
        return lax.dot_general(qh, k_s[h], (((1,), (1,)), ((), ())), preferred_element_type=F32)

    s = scores(0)
    for h in range(MLA_HEADS):
        s_next = scores(h + 1) if h + 1 < MLA_HEADS else None
        p = jnp.exp2((s - jnp.max(s, axis=-1, keepdims=True)).astype(BF16))
        o = jnp.dot(p, v_s[h], preferred_element_type=F32)
        y_ref[0, :, h * V_HEAD_DIM:(h + 1) * V_HEAD_DIM] = o[:, :V_HEAD_DIM] / o[:, V_HEAD_DIM:V_HEAD_DIM + 1]
        s = s_next


def _mla_attn(z_mla, pos3, freq, sign, q_norm, wq_p, wq_sw, kv_norm, wkv_p, tq):
    B, S, ZC = z_mla.shape
    return pl.pallas_call(
        _mla_kernel,
        grid=(B, S // tq),
        in_specs=[pl.BlockSpec((1, tq, ZC), lambda b, i: (b, i, 0)),
                  pl.BlockSpec((1, S, ZC), lambda b, i: (b, 0, 0)),
                  pl.BlockSpec((1, S, 1), lambda b, i: (b, 0, 0)),
                  _const_spec(freq.shape), _const_spec(sign.shape), _const_spec(q_norm.shape),
                  _const_spec(wq_p.shape), _const_spec(wq_sw.shape), _const_spec(kv_norm.shape),
                  _const_spec(wkv_p.shape)],
        out_specs=pl.BlockSpec((1, tq, MLA_WIDTH), lambda b, i: (b, i, 0)),
        out_shape=jax.ShapeDtypeStruct((B, S, MLA_WIDTH), F32),
        scratch_shapes=[pltpu.VMEM((MLA_HEADS, S, LANES), BF16), pltpu.VMEM((MLA_HEADS, S, LANES), BF16),
                        pltpu.VMEM((S, LANES), F32), pltpu.VMEM((S, LANES), F32)],
        compiler_params=_params("parallel", "arbitrary"),
        name="mla_attn",
    )(z_mla, z_mla, pos3, freq, sign, q_norm, wq_p, wq_sw, kv_norm, wkv_p)


def _mix_out_kernel(x_ref, of_ref, ob_ref, bf_ref, bb_ref, gate_ref, lnw_ref, lnb_ref, segm_ref, ym_ref,
                    wr_ref, wm_ref, g_ref, o_ref):
    segm = segm_ref[...]

    o = of_ref[0] + ob_ref[0]
    o_hi, o_lo = _split2(o)
    oc = o - (_dot(o_hi, segm) + _dot(o_lo, segm))
    var = _dot((oc * oc).astype(BF16), segm)
    o = oc * lax.rsqrt(var + LN_X_EPS) * lnw_ref[...] + lnb_ref[...]
    y_rwkv = (o + (bf_ref[0] + bb_ref[0])) * gate_ref[0]
    y = (jnp.dot(y_rwkv.astype(BF16), wr_ref[...], preferred_element_type=F32)
         + jnp.dot(ym_ref[0].astype(BF16), wm_ref[...], preferred_element_type=F32))
    o_ref[0] = x_ref[0] + _rms(y, g_ref[...])


def _mix_out(x, o_f, o_b, bonus_f, bonus_b, gate, lnw, lnb, segm, y_mla, w_r, w_m, g, tm):
    B, S, D = x.shape
    row_spec = lambda cols: pl.BlockSpec((1, tm, cols), lambda b, i: (b, i, 0))
    C = RWKV_WIDTH
    return pl.pallas_call(
        _mix_out_kernel,
        grid=(B, S // tm),
        in_specs=[row_spec(D), row_spec(C), row_spec(C), row_spec(C), row_spec(C), row_spec(C),
                  _const_spec(lnw.shape), _const_spec(lnb.shape), _const_spec(segm.shape), row_spec(MLA_WIDTH),
                  _const_spec(w_r.shape), _const_spec(w_m.shape), _const_spec(g.shape)],
        out_specs=row_spec(D),
        out_shape=jax.ShapeDtypeStruct((B, S, D), F32),
        compiler_params=_params("parallel", "parallel"),
        name="mix_out",
    )(x, o_f, o_b, bonus_f, bonus_b, gate, lnw, lnb, segm, y_mla, w_r, w_m, g)


def _mem_kv_kernel(mem_ref, g_ref, w_ref, k_ref, v_ref):
    m = _rms(mem_ref[0], g_ref[...]).astype(BF16)
    kv = jnp.dot(m, w_ref[...], preferred_element_type=F32)
    k_ref[0] = kv[:, :D_MODEL].astype(BF16)
    v_ref[0] = kv[:, D_MODEL:].astype(BF16)


def _mem_kv(mem, g, wkv):
    B, T, D = mem.shape
    return pl.pallas_call(
        _mem_kv_kernel,
        grid=(B,),
        in_specs=[pl.BlockSpec((1, T, D), lambda b: (b, 0, 0)), _const_spec(g.shape), _const_spec(wkv.shape)],
        out_specs=[pl.BlockSpec((1, T, D), lambda b: (b, 0, 0)), pl.BlockSpec((1, T, D), lambda b: (b, 0, 0))],
        out_shape=[jax.ShapeDtypeStruct((B, T, D), BF16), jax.ShapeDtypeStruct((B, T, D), BF16)],
        compiler_params=_params("parallel"),
        name="mem_kv",
    )(mem, g, wkv)


def _mem_attn_kernel(x_ref, k_ref, v_ref, gpre_ref, wq_ref, wo_ref, gpost_ref, o_ref, att_s):
    x = x_ref[0]
    h = _rms(x, gpre_ref[...]).astype(BF16)
    q = jnp.dot(h, wq_ref[...], preferred_element_type=F32) * (MEM_HEAD_DIM ** -0.5)
    for hd in range(MEM_HEADS):
        sl = slice(hd * MEM_HEAD_DIM, (hd + 1) * MEM_HEAD_DIM)
        s = lax.dot_general(q[:, sl].astype(BF16), k_ref[0, :, sl], (((1,), (1,)), ((), ())),
                            preferred_element_type=F32)
        p = jnp.exp(s - jnp.max(s, axis=-1, keepdims=True))
        denom = jnp.sum(p, axis=-1, keepdims=True)
        o = jnp.dot(p.astype(BF16), v_ref[0, :, sl], preferred_element_type=F32)
        att_s[:, sl] = (o / denom).astype(BF16)
    y = jnp.dot(att_s[...], wo_ref[...], preferred_element_type=F32)
    o_ref[0] = x + _rms(y, gpost_ref[...])


def _mem_attn(x, k, v, g_pre, wq, wo, g_post, tm):
    B, S, D = x.shape
    T = k.shape[1]
    row_spec = pl.BlockSpec((1, tm, D), lambda b, i: (b, i, 0))
    kv_spec = pl.BlockSpec((1, T, D), lambda b, i: (b, 0, 0))
    return pl.pallas_call(
        _mem_attn_kernel,
        grid=(B, S // tm),
        in_specs=[row_spec, kv_spec, kv_spec, _const_spec(g_pre.shape), _const_spec(wq.shape),
                  _const_spec(wo.shape), _const_spec(g_post.shape)],
        out_specs=row_spec,
        out_shape=jax.ShapeDtypeStruct((B, S, D), F32),
        scratch_shapes=[pltpu.VMEM((tm, D), BF16)],
        compiler_params=_params("parallel", "parallel"),
        name="mem_attn",
    )(x, k, v, g_pre, wq, wo, g_post)


def _mlp_kernel(x_ref, gpre_ref, w1_ref, w2_ref, gpost_ref, o_ref):
    x = x_ref[0]
    h = _rms(x, gpre_ref[...]).astype(BF16)
    u = jnp.maximum(jnp.dot(h, w1_ref[...], preferred_element_type=F32), 0.0)
    y = jnp.dot((u * u).astype(BF16), w2_ref[...], preferred_element_type=F32)
    o_ref[0] = x + _rms(y, gpost_ref[...])


def _mlp(x, g_pre, w1, w2, g_post, tm):
    B, S, D = x.shape
    row_spec = pl.BlockSpec((1, tm, D), lambda b, i: (b, i, 0))
    return pl.pallas_call(
        _mlp_kernel,
        grid=(B, S // tm),
        in_specs=[row_spec, _const_spec(g_pre.shape), _const_spec(w1.shape), _const_spec(w2.shape),
                  _const_spec(g_post.shape)],
        out_specs=row_spec,
        out_shape=jax.ShapeDtypeStruct((B, S, D), F32),
        compiler_params=_params("parallel", "parallel"),
        name="mlp",
    )(x, g_pre, w1, w2, g_post)


def _pad_cols(w, n):
    return jnp.pad(w, ((0, 0), (0, n - w.shape[1])))


def _rope_tile(w):
    return jnp.pad(w, ((0, 0), (QK_NOPE_DIM, LANES - QK_NOPE_DIM - QK_ROPE_DIM)))


def _swap_halves(w):
    half = QK_ROPE_DIM // 2
    return jnp.concatenate([w[:, half:], w[:, :half]], axis=1)


def _lora_rows(w, lo, rows):
    return jnp.pad(w, ((lo, LORA_TILE - lo - rows), (0, 0)))


def kernel(x, mem, positions, norm_mix_pre, w_in, conv_rwkv, rwkv_w0, rwkv_w2, rwkv_a0, rwkv_a2, rwkv_g2, rwkv_k_k, rwkv_k_a, rwkv_r_k, rwkv_lnx_w, rwkv_lnx_b, mla_q_norm, mla_w_uq, mla_kv_norm, mla_w_ukv, w_out, norm_mix_post, norm_mem_pre, norm_memtok, mem_wq, mem_wkv, mem_wo, norm_mem_post, norm_mlp_pre, mlp_w1, mlp_w2, norm_mlp_post):
    depth = w_in.shape[0]
    C = RWKV_WIDTH
    head_of = jnp.arange(C) // HEAD_DIM
    seg = (head_of[:, None] == head_of[None, :]).astype(BF16)
    seg_mean = (seg.astype(F32) / HEAD_DIM).astype(BF16)
    inv_freq = ROPE_THETA ** (-jnp.arange(0, QK_ROPE_DIM, 2, dtype=F32) / QK_ROPE_DIM)
    half = QK_ROPE_DIM // 2
    freq = _rope_tile(jnp.concatenate([inv_freq, inv_freq])[None, :])
    sign = _rope_tile(jnp.concatenate([-jnp.ones((half,), F32), jnp.ones((half,), F32)])[None, :])
    pos3 = positions[:, :, None]
    step = jnp.arange(CHUNK)
    tri = jnp.stack([step[None, :] <= step[:, None], step[None, :] >= step[:, None]]).astype(F32)
    row = lambda t: t.reshape(1, -1)

    for l in range(depth):
        w = w_in[l]
        mla0 = RWKV_COLS
        w_rope = w[:, mla0 + Q_LORA_RANK + KV_LORA_RANK:]
        w_r = _pad_cols(w[:, :RWKV_COLS], RWKV_TILE_COLS).astype(BF16)
        w_m = jnp.concatenate([w[:, mla0:mla0 + Q_LORA_RANK + KV_LORA_RANK], _rope_tile(w_rope),
                               _rope_tile(_swap_halves(w_rope))], axis=1).astype(BF16)
        conv_p = _pad_cols(conv_rwkv[l], RWKV_TILE_COLS)
        z_rwkv, z_mla = _in_proj(x, row(norm_mix_pre[l]), w_r, w_m, conv_p, tm=512)

        w2_p = jnp.stack([_lora_rows(rwkv_w2[l, d], d * DECAY_LORA, DECAY_LORA) for d in range(2)])
        a_lo = 2 * DECAY_LORA
        a2_p = jnp.stack([_lora_rows(rwkv_a2[l, d], a_lo + d * ICLR_LORA, ICLR_LORA) for d in range(2)])
        g2_p = _lora_rows(rwkv_g2[l], a_lo + 2 * ICLR_LORA, GATE_LORA)
        o_f, o_b, bonus_f, bonus_b, gate = _rwkv(z_rwkv, rwkv_w0[l], w2_p, rwkv_a0[l], a2_p, g2_p,
                                                 row(rwkv_k_k[l]), row(rwkv_k_a[l]), row(rwkv_r_k[l]), seg, tri, tm=256)

        qk = QK_NOPE_DIM + QK_ROPE_DIM
        w_uq = mla_w_uq[l].reshape(Q_LORA_RANK, MLA_HEADS, qk)
        wq_p = jnp.pad(w_uq, ((0, 0), (0, 0), (0, LANES - qk))).reshape(Q_LORA_RANK, MLA_HEADS * LANES)
        uq_rope = w_uq[:, :, QK_NOPE_DIM:]
        uq_sw = jnp.concatenate([uq_rope[:, :, half:], uq_rope[:, :, :half]], axis=2)
        wq_sw = jnp.pad(uq_sw, ((0, 0), (0, 0), (QK_NOPE_DIM, LANES - qk))).reshape(Q_LORA_RANK, MLA_HEADS * LANES)
        w_ukv = mla_w_ukv[l].reshape(KV_LORA_RANK, MLA_HEADS, QK_NOPE_DIM + V_HEAD_DIM)
        wk_p = jnp.pad(w_ukv[:, :, :QK_NOPE_DIM], ((0, 0), (0, 0), (0, LANES - QK_NOPE_DIM)))
        wv_p = jnp.pad(w_ukv[:, :, QK_NOPE_DIM:], ((0, 0), (0, 0), (0, LANES - V_HEAD_DIM)))
        wkv_p = jnp.concatenate([wk_p.reshape(KV_LORA_RANK, MLA_HEADS * LANES),
                                 wv_p.reshape(KV_LORA_RANK, MLA_HEADS * LANES)], axis=1)
        y_mla = _mla_attn(z_mla, pos3, freq, sign, row(mla_q_norm[l]), wq_p.astype(BF16), wq_sw.astype(BF16),
                          row(mla_kv_norm[l]), wkv_p.astype(BF16), tq=512)

        wo = w_out[l].astype(BF16)
        x = _mix_out(x, o_f, o_b, bonus_f, bonus_b, gate, row(rwkv_lnx_w[l]), row(rwkv_lnx_b[l]), seg_mean,
                     y_mla, wo[:C], wo[C:], row(norm_mix_post[l]), tm=512)

        k_mem, v_mem = _mem_kv(mem, row(norm_memtok[l]), mem_wkv[l].astype(BF16))
        x = _mem_attn(x, k_mem, v_mem, row(norm_mem_pre[l]), mem_wq[l].astype(BF16), mem_wo[l].astype(BF16),
                      row(norm_mem_post[l]), tm=512)

        x = _mlp(x, row(norm_mlp_pre[l]), mlp_w1[l].astype(BF16), mlp_w2[l].astype(BF16),
                 row(norm_mlp_post[l]), tm=256)
    return x
```

```python
import math

import jax
import jax.numpy as jnp
from jax import lax
from jax.experimental import pallas as pl
from jax.experimental.pallas import tpu as pltpu

F32 = jnp.float32
BF16 = jnp.bfloat16

D_MODEL = 1024
NORM_EPS = 1e-6

RWKV_HEADS = 8
HEAD_DIM = 64
RWKV_WIDTH = RWKV_HEADS * HEAD_DIM
DECAY_LORA = 32
ICLR_LORA = 32
GATE_LORA = 96
LORA_COLS = 2 * DECAY_LORA + 2 * ICLR_LORA + GATE_LORA
LORA_TILE = 256
RWKV_COLS = 3 * RWKV_WIDTH + LORA_COLS
RWKV_TILE_COLS = 3 * RWKV_WIDTH + LORA_TILE
LN_X_EPS = 64e-5
CHUNK = 64
CHUNK_GROUP = 2
DECAY_SCALE = math.exp(-0.5)

MLA_HEADS = 8
QK_NOPE_DIM = 64
QK_ROPE_DIM = 32
V_HEAD_DIM = 64
MLA_WIDTH = MLA_HEADS * V_HEAD_DIM
Q_LORA_RANK = 256
KV_LORA_RANK = 128
ROPE_THETA = 10000.0
LOG2_E = math.log2(math.e)
LANES = 128
MLA_TILE_COLS = Q_LORA_RANK + KV_LORA_RANK + 2 * LANES

MEM_HEADS = 4
MEM_HEAD_DIM = D_MODEL // MEM_HEADS
D_FF = 4 * D_MODEL

VMEM_LIMIT = 56 * 1024 * 1024


def _mm(a, b):
    return jnp.dot(a.astype(BF16), b.astype(BF16), preferred_element_type=F32)


def _split2(t):
    hi = t.astype(BF16)
    lo = (t - hi.astype(F32)).astype(BF16)
    return hi, lo


def _split3(t):
    hi = t.astype(BF16)
    rest = t - hi.astype(F32)
    mid = rest.astype(BF16)
    lo = (rest - mid.astype(F32)).astype(BF16)
    return hi, mid, lo


def _dot(a, b):
    return jnp.dot(a, b, preferred_element_type=F32)


def _mm_x3(a, b):
    a_hi, a_lo = _split2(a)
    b_hi, b_lo = _split2(b)
    return _dot(a_hi, b_hi) + (_dot(a_hi, b_lo) + _dot(a_lo, b_hi))


def _mm_exact_lhs(a, b):
    a = a.astype(BF16)
    hi, mid, lo = _split3(b)
    return _dot(a, hi) + (_dot(a, mid) + _dot(a, lo))


def _mm_nt(a, b):
    return lax.dot_general(a.astype(BF16), b.astype(BF16), (((1,), (1,)), ((), ())),
                           preferred_element_type=F32)


def _mm_tn(a, b):
    return lax.dot_general(a.astype(BF16), b.astype(BF16), (((0,), (0,)), ((), ())),
                           preferred_element_type=F32)


def _rms(x, g, eps=NORM_EPS):
    return x * lax.rsqrt(jnp.mean(x * x, axis=-1, keepdims=True) + eps) * g


def _sigmoid(x):
    return 0.5 * jnp.tanh(0.5 * x) + 0.5


def _params(*sem):
    return pltpu.CompilerParams(dimension_semantics=sem, vmem_limit_bytes=VMEM_LIMIT)


def _const_spec(shape):
    nd = len(shape)
    return pl.BlockSpec(shape, lambda *_: (0,) * nd)


def _in_proj_kernel(x_ref, xp_ref, xn_ref, g_ref, wr_ref, wm_ref, conv_ref, zr_ref, zm_ref):
    i = pl.program_id(1)
    last = pl.num_programs(1) - 1
    tm = x_ref.shape[1]
    g = g_ref[...]
    h = _rms(x_ref[0], g).astype(BF16)
    zm_ref[0] = jnp.dot(h, wm_ref[...], preferred_element_type=F32)
    z = jnp.dot(h, wr_ref[...], preferred_element_type=F32)
    halo = _rms(jnp.concatenate([xp_ref[0], xn_ref[0]], axis=0), g).astype(BF16)
    z_halo = jnp.dot(halo, wr_ref[...], preferred_element_type=F32)
    prev_row = jnp.where(i == 0, 0.0, z_halo[7:8, :])
    next_row = jnp.where(i == last, 0.0, z_halo[8:9, :])
    row = lax.broadcasted_iota(jnp.int32, (tm, 1), 0)
    z_dn = jnp.where(row == 0, prev_row, pltpu.roll(z, 1, 0))
    z_up = jnp.where(row == tm - 1, next_row, pltpu.roll(z, tm - 1, 0))
    zr_ref[0] = conv_ref[0:1, :] * z_dn + conv_ref[1:2, :] * z + conv_ref[2:3, :] * z_up


def _in_proj(x, g, w_r, w_m, conv_p, tm):
    B, S, D = x.shape
    halo = tm // 8
    n_halo = S // 8
    return pl.pallas_call(
        _in_proj_kernel,
        grid=(B, S // tm),
        in_specs=[pl.BlockSpec((1, tm, D), lambda b, i: (b, i, 0)),
                  pl.BlockSpec((1, 8, D), lambda b, i: (b, jnp.maximum(i * halo - 1, 0), 0)),
                  pl.BlockSpec((1, 8, D), lambda b, i: (b, jnp.minimum((i + 1) * halo, n_halo - 1), 0)),
                  _const_spec((1, D)),
                  _const_spec(w_r.shape),
                  _const_spec(w_m.shape),
                  _const_spec(conv_p.shape)],
        out_specs=[pl.BlockSpec((1, tm, RWKV_TILE_COLS), lambda b, i: (b, i, 0)),
                   pl.BlockSpec((1, tm, MLA_TILE_COLS), lambda b, i: (b, i, 0))],
        out_shape=[jax.ShapeDtypeStruct((B, S, RWKV_TILE_COLS), F32),
                   jax.ShapeDtypeStruct((B, S, MLA_TILE_COLS), F32)],
        compiler_params=_params("parallel", "parallel"),
        name="in_proj",
    )(x, x, x, g, w_r, w_m, conv_p)


def _chunk_units(units, eye, eye_hi):
    L, N = units[0][0].shape
    zeros = jnp.zeros((L, N), F32)
    AA = [_mm_nt(jnp.concatenate([At, Rt], axis=0), jnp.concatenate([Bt, Kt], axis=0))
          for (At, Rt, Bt, Kt, *_) in units]
    A_a = [jnp.where(u[8], aa[:L, :], 0.0) for u, aa in zip(units, AA)]
    A_r = [jnp.where(u[9], aa[L:, :], 0.0) for u, aa in zip(units, AA)]
    AkV = [_mm(a[:, L:], u[6]) for u, a in zip(units, A_a)]
    low = lax.broadcasted_iota(jnp.int32, (L, 2 * L), 1) < L
    S = [jnp.where(low, a, eye_hi) for a in A_a]
    span = 1
    while span < L:
        R = [_mm(s[:, :L], s) for s in S]
        S = [jnp.where(low, r, r + s) for r, s in zip(R, S)]
        span *= 2
    W = [_mm(s[:, L:], jnp.concatenate([u[0], akv], axis=1)) for s, u, akv in zip(S, units, AkV)]
    Z = [jnp.concatenate([w, jnp.concatenate([zeros, u[6]], axis=1)], axis=0) for u, w in zip(units, W)]
    lhs = [jnp.concatenate([a_r, jnp.concatenate([u[4], u[5]], axis=0).T], axis=0) for u, a_r in zip(units, A_r)]
    QGH = [_mm(l, z) for l, z in zip(lhs, Z)]
    Q = [jnp.concatenate([u[1], zeros], axis=1) + t[:L] for u, t in zip(units, QGH)]
    GH = [jnp.concatenate([eye * u[7], zeros], axis=1) + t[L:] for u, t in zip(units, QGH)]
    return Q, GH


def _rwkv_kernel(zf_ref, zb_ref,
                 w0_ref, w2_ref, a0_ref, a2_ref, g2_ref, kk_ref, ka_ref, rk_ref, seg_ref, tri_ref,
                 of_ref, ob_ref, bonus_f_ref, bonus_b_ref, g_ref,
                 r_s, v_s, kk_s, b_s, kd_s, lw_s, state_s):
    i = pl.program_id(1)
    n_tiles = pl.num_programs(1)
    tm = zf_ref.shape[1]
    L, N = CHUNK, HEAD_DIM
    n_chunks = tm // L

    @pl.when(i == 0)
    def _():
        state_s[...] = jnp.zeros_like(state_s)

    seg = seg_ref[...]

    def seg_sum(t):
        return _dot(t.astype(BF16), seg)

    for d, (z_ref, bonus_ref) in enumerate(((zf_ref, bonus_f_ref), (zb_ref, bonus_b_ref))):
        r = z_ref[0, :, 0:RWKV_WIDTH]
        k = z_ref[0, :, RWKV_WIDTH:2 * RWKV_WIDTH]
        v = z_ref[0, :, 2 * RWKV_WIDTH:3 * RWKV_WIDTH]
        lora = z_ref[0, :, 3 * RWKV_WIDTH:]
        kk = k * kk_ref[...]
        kk = kk * lax.rsqrt(jnp.maximum(seg_sum(kk * kk), 1e-24))
        w_pre = w0_ref[d:d + 1, :] + _mm(jnp.tanh(lora), w2_ref[d])
        lw_s[d] = -DECAY_SCALE * _sigmoid(w_pre)
        alpha = _sigmoid(a0_ref[d:d + 1, :] + _mm(lora, a2_ref[d]))
        kd = k * (1.0 + (alpha - 1.0) * ka_ref[...])
        r_s[d] = r
        v_s[d] = v
        kk_s[d] = kk
        kd_s[d] = kd
        b_s[d] = kk * alpha
        bonus_ref[0] = seg_sum(r * kd * rk_ref[...]) * v
        if d == 0:
            g_ref[0] = _mm(_sigmoid(lora), g2_ref[...])

    ri = lax.broadcasted_iota(jnp.int32, (L, L), 0)
    ci = lax.broadcasted_iota(jnp.int32, (L, L), 1)
    eye = (ci == ri).astype(F32)
    ri2 = lax.broadcasted_iota(jnp.int32, (L, 2 * L), 0)
    ci2 = lax.broadcasted_iota(jnp.int32, (L, 2 * L), 1)
    eye_hi = (ci2 == ri2 + L).astype(F32)
    ci2 = jnp.where(ci2 >= L, ci2 - L, ci2)
    m_strict = (ci2 < ri2, ci2 > ri2)
    m_incl = (ci2 <= ri2, ci2 >= ri2)
    out_refs = (of_ref, ob_ref)

    def chunk_units(c):
        rows_d = (pl.ds(pl.multiple_of(c * L, L), L), pl.ds(pl.multiple_of((n_chunks - 1 - c) * L, L), L))
        units = []
        for d in range(2):
            rows = rows_d[d]
            lw = lw_s[d, rows, :]
            bc = b_s[d, rows, :]
            kdc = kd_s[d, rows, :]
            vc = v_s[d, rows, :]
            cum = _mm_exact_lhs(tri_ref[d], lw)
            tot = cum[L - 1:L, :] if d == 0 else cum[0:1, :]
            e_in = jnp.exp(-cum)
            e_rem = jnp.exp(tot - cum)
            e_tot = jnp.exp(tot)
            At = -kk_s[d, rows, :] * jnp.exp(cum - lw)
            Rt = r_s[d, rows, :] * jnp.exp(cum)
            Bt = bc * e_in
            Kt = kdc * e_in
            Bh = bc * e_rem
            Kh = kdc * e_rem
            for h in range(RWKV_HEADS):
                sl = slice(h * HEAD_DIM, (h + 1) * HEAD_DIM)
                units.append((At[:, sl], Rt[:, sl], Bt[:, sl], Kt[:, sl], Bh[:, sl], Kh[:, sl],
                              vc[:, sl], e_tot[:, sl], m_strict[d], m_incl[d]))
        return rows_d, units

    n_units = 2 * RWKV_HEADS

    def group_body(j, carry):
        rows, units = [], []
        for g in range(CHUNK_GROUP):
            rows_d, chunk = chunk_units(j * CHUNK_GROUP + g)
            rows.append(rows_d)
            units.extend(chunk)
        Q, GH = _chunk_units(units, eye, eye_hi)
        M = [state_s[u] for u in range(n_units)]
        for g in range(CHUNK_GROUP):
            Qg, GHg = Q[g * n_units:(g + 1) * n_units], GH[g * n_units:(g + 1) * n_units]
            M_split = [_split2(m) for m in M]
            G_split = [_split2(gh[:, :N]) for gh in GHg]
            prod = [_dot(jnp.concatenate([q[:, :N].astype(BF16), g_hi, g_lo], axis=0), m_hi)
                    for q, (g_hi, g_lo), (m_hi, _) in zip(Qg, G_split, M_split)]
            out = [p[:L] + q[:, N:] for p, q in zip(prod, Qg)]
            M = [p[L:L + N] + (p[L + N:] + _dot(g_hi, m_lo)) + gh[:, N:]
                 for p, (g_hi, _), (_, m_lo), gh in zip(prod, G_split, M_split, GHg)]
            for d in range(2):
                for h in range(RWKV_HEADS):
                    out_refs[d][0, rows[g][d], h * N:(h + 1) * N] = out[d * RWKV_HEADS + h]
        for u in range(n_units):
            state_s[u] = M[u]
        return carry

    lax.fori_loop(0, n_chunks // CHUNK_GROUP, group_body, 0)


def _rwkv(z_rwkv, w0, w2_p, a0, a2_p, g2_p, k_k, k_a, r_k, seg, tri, tm):
    B, S, ZC = z_rwkv.shape
    C = RWKV_WIDTH
    n_tiles = S // tm
    fwd = lambda b, i: (b, i, 0)
    bwd = lambda b, i: (b, n_tiles - 1 - i, 0)
    out_f = pl.BlockSpec((1, tm, C), fwd)
    out_b = pl.BlockSpec((1, tm, C), bwd)
    out_sds = jax.ShapeDtypeStruct((B, S, C), F32)
    return pl.pallas_call(
        _rwkv_kernel,
        grid=(B, n_tiles),
        in_specs=[pl.BlockSpec((1, tm, ZC), fwd),
                  pl.BlockSpec((1, tm, ZC), bwd),
                  _const_spec(w0.shape), _const_spec(w2_p.shape),
                  _const_spec(a0.shape), _const_spec(a2_p.shape), _const_spec(g2_p.shape),
                  _const_spec(k_k.shape), _const_spec(k_a.shape), _const_spec(r_k.shape),
                  _const_spec(seg.shape), _const_spec(tri.shape)],
        out_specs=[out_f, out_b, out_f, out_b, out_f],
        out_shape=[out_sds] * 5,
        scratch_shapes=[pltpu.VMEM((2, tm, C), F32)] * 6
                       + [pltpu.VMEM((2 * RWKV_HEADS, HEAD_DIM, HEAD_DIM), F32)],
        compiler_params=_params("parallel", "arbitrary"),
        name="rwkv",
    )(z_rwkv, z_rwkv, w0, w2_p, a0, a2_p, g2_p, k_k, k_a, r_k, seg, tri)


def _rope_tiles(pos_row, freq_col, place_cos, place_sin, base):
    ang = freq_col * pos_row.astype(F32)

    def place(t, p):
        return sum(lax.dot_general(part, p, (((0,), (0,)), ((), ())), preferred_element_type=F32)
                   for part in _split3(t))

    return place(jnp.cos(ang), place_cos) + base, place(jnp.sin(ang), place_sin)


def _mla_kernel(zq_ref, zkv_ref, pos_ref, freq_ref, pcos_ref, psin_ref, base_ref, qn_ref, wq_ref, wqs_ref, kvn_ref, wkv_ref,
                y_ref, k_s, v_s, cos_s, sin_s):
    i = pl.program_id(1)
    tq = zq_ref.shape[1]
    scale = (QK_NOPE_DIM + QK_ROPE_DIM) ** -0.5
    c_kv_lo, c_kv_hi = Q_LORA_RANK, Q_LORA_RANK + KV_LORA_RANK

    @pl.when(i == 0)
    def _():
        zkv = zkv_ref[0]
        kvn = _rms(zkv[:, c_kv_lo:c_kv_hi], kvn_ref[...]).astype(BF16)
        kvu = jnp.dot(kvn, wkv_ref[...], preferred_element_type=F32)
        cos_t, sin_t = _rope_tiles(pos_ref[0], freq_ref[...], pcos_ref[...], psin_ref[...], base_ref[...])
        cos_s[...] = cos_t
        sin_s[...] = sin_t
        k_rot = zkv[:, c_kv_hi:c_kv_hi + LANES] * cos_t + zkv[:, c_kv_hi + LANES:] * sin_t
        lane = lax.broadcasted_iota(jnp.int32, (1, LANES), 1)
        ones_col = (lane == V_HEAD_DIM).astype(F32)
        for h in range(MLA_HEADS):
            k_s[h] = (kvu[:, h * LANES:(h + 1) * LANES] + k_rot).astype(BF16)
            v_s[h] = (kvu[:, (MLA_HEADS + h) * LANES:(MLA_HEADS + h + 1) * LANES] + ones_col).astype(BF16)

    zq = zq_ref[0]
    qn = _rms(zq[:, :Q_LORA_RANK], qn_ref[...]).astype(BF16)
    q_a = jnp.dot(qn, wq_ref[...], preferred_element_type=F32)
    q_b = jnp.dot(qn, wqs_ref[...], preferred_element_type=F32)
    rows = pl.ds(pl.multiple_of(i * tq, tq), tq)
    cos_q = cos_s[rows, :] * (scale * LOG2_E)
    sin_q = sin_s[rows, :] * (scale * LOG2_E)

    def scores(h):
        sl = slice(h * LANES, (h + 1) * LANES)
        qh = (q_a[:, sl] * cos_q + q_b[:, sl] * sin_q).astype(BF16)
        return lax.dot_general(qh, k_s[h], (((1,), (1,)), ((), ())), preferred_element_type=F32)

    s = scores(0)
    for h in range(MLA_HEADS):
        s_next = scores(h + 1) if h + 1 < MLA_HEADS else None
        p = jnp.exp2((s - jnp.max(s, axis=-1, keepdims=True)).astype(BF16))
        o = jnp.dot(p, v_s[h], preferred_element_type=F32)
        y_ref[0, :, h * V_HEAD_DIM:(h + 1) * V_HEAD_DIM] = o[:, :V_HEAD_DIM] / o[:, V_HEAD_DIM:V_HEAD_DIM + 1]
        s = s_next


def _mla_attn(z_mla, pos_row, freq_col, place_cos, place_sin, base, q_norm, wq_p, wq_sw, kv_norm, wkv_p, tq):
    B, S, ZC = z_mla.shape
    return pl.pallas_call(
        _mla_kernel,
        grid=(B, S // tq),
        in_specs=[pl.BlockSpec((1, tq, ZC), lambda b, i: (b, i, 0)),
                  pl.BlockSpec((1, S, ZC), lambda b, i: (b, 0, 0)),
                  pl.BlockSpec((1, 1, S), lambda b, i: (b, 0, 0)),
                  _const_spec(freq_col.shape), _const_spec(place_cos.shape), _const_spec(place_sin.shape),
                  _const_spec(base.shape), _const_spec(q_norm.shape),
                  _const_spec(wq_p.shape), _const_spec(wq_sw.shape), _const_spec(kv_norm.shape),
                  _const_spec(wkv_p.shape)],
        out_specs=pl.BlockSpec((1, tq, MLA_WIDTH), lambda b, i: (b, i, 0)),
        out_shape=jax.ShapeDtypeStruct((B, S, MLA_WIDTH), F32),
        scratch_shapes=[pltpu.VMEM((MLA_HEADS, S, LANES), BF16), pltpu.VMEM((MLA_HEADS, S, LANES), BF16),
                        pltpu.VMEM((S, LANES), F32), pltpu.VMEM((S, LANES), F32)],
        compiler_params=_params("parallel", "arbitrary"),
        name="mla_attn",
    )(z_mla, z_mla, pos_row, freq_col, place_cos, place_sin, base, q_norm, wq_p, wq_sw, kv_norm, wkv_p)


def _mix_out_kernel(x_ref, of_ref, ob_ref, bf_ref, bb_ref, gate_ref, lnw_ref, lnb_ref, segm_ref, ym_ref,
                    wr_ref, wm_ref, g_ref, o_ref):
    segm = segm_ref[...]

    o = of_ref[0] + ob_ref[0]
    o_hi, o_lo = _split2(o)
    oc = o - (_dot(o_hi, segm) + _dot(o_lo, segm))
    var = _dot((oc * oc).astype(BF16), segm)
    o = oc * lax.rsqrt(var + LN_X_EPS) * lnw_ref[...] + lnb_ref[...]
    y_rwkv = (o + (bf_ref[0] + bb_ref[0])) * gate_ref[0]
    y = (jnp.dot(y_rwkv.astype(BF16), wr_ref[...], preferred_element_type=F32)
         + jnp.dot(ym_ref[0].astype(BF16), wm_ref[...], preferred_element_type=F32))
    o_ref[0] = x_ref[0] + _rms(y, g_ref[...])


def _mix_out(x, o_f, o_b, bonus_f, bonus_b, gate, lnw, lnb, segm, y_mla, w_r, w_m, g, tm):
    B, S, D = x.shape
    row_spec = lambda cols: pl.BlockSpec((1, tm, cols), lambda b, i: (b, i, 0))
    C = RWKV_WIDTH
    return pl.pallas_call(
        _mix_out_kernel,
        grid=(B, S // tm),
        in_specs=[row_spec(D), row_spec(C), row_spec(C), row_spec(C), row_spec(C), row_spec(C),
                  _const_spec(lnw.shape), _const_spec(lnb.shape), _const_spec(segm.shape), row_spec(MLA_WIDTH),
                  _const_spec(w_r.shape), _const_spec(w_m.shape), _const_spec(g.shape)],
        out_specs=row_spec(D),
        out_shape=jax.ShapeDtypeStruct((B, S, D), F32),
        compiler_params=_params("parallel", "parallel"),
        name="mix_out",
    )(x, o_f, o_b, bonus_f, bonus_b, gate, lnw, lnb, segm, y_mla, w_r, w_m, g)


def _mem_kv_kernel(mem_ref, g_ref, w_ref, k_ref, v_ref):
    m = _rms(mem_ref[0], g_ref[...]).astype(BF16)
    kv = jnp.dot(m, w_ref[...], preferred_element_type=F32)
    k_ref[0] = kv[:, :D_MODEL].astype(BF16)
    v_ref[0] = kv[:, D_MODEL:].astype(BF16)


def _mem_kv(mem, g, wkv):
    B, T, D = mem.shape
    return pl.pallas_call(
        _mem_kv_kernel,
        grid=(B,),
        in_specs=[pl.BlockSpec((1, T, D), lambda b: (b, 0, 0)), _const_spec(g.shape), _const_spec(wkv.shape)],
        out_specs=[pl.BlockSpec((1, T, D), lambda b: (b, 0, 0)), pl.BlockSpec((1, T, D), lambda b: (b, 0, 0))],
        out_shape=[jax.ShapeDtypeStruct((B, T, D), BF16), jax.ShapeDtypeStruct((B, T, D), BF16)],
        compiler_params=_params("parallel"),
        name="mem_kv",
    )(mem, g, wkv)


def _mem_attn_kernel(x_ref, k_ref, v_ref, gpre_ref, wq_ref, wo_ref, gpost_ref, o_ref, att_s):
    x = x_ref[0]
    h = _rms(x, gpre_ref[...]).astype(BF16)
    q = jnp.dot(h, wq_ref[...], preferred_element_type=F32) * (MEM_HEAD_DIM ** -0.5)
    for hd in range(MEM_HEADS):
        sl = slice(hd * MEM_HEAD_DIM, (hd + 1) * MEM_HEAD_DIM)
        s = lax.dot_general(q[:, sl].astype(BF16), k_ref[0, :, sl], (((1,), (1,)), ((), ())),
                            preferred_element_type=F32)
        p = jnp.exp(s - jnp.max(s, axis=-1, keepdims=True))
        denom = jnp.sum(p, axis=-1, keepdims=True)
        o = jnp.dot(p.astype(BF16), v_ref[0, :, sl], preferred_element_type=F32)
        att_s[:, sl] = (o / denom).astype(BF16)
    y = jnp.dot(att_s[...], wo_ref[...], preferred_element_type=F32)
    o_ref[0] = x + _rms(y, gpost_ref[...])


def _mem_attn(x, k, v, g_pre, wq, wo, g_post, tm):
    B, S, D = x.shape
    T = k.shape[1]
    row_spec = pl.BlockSpec((1, tm, D), lambda b, i: (b, i, 0))
    kv_spec = pl.BlockSpec((1, T, D), lambda b, i: (b, 0, 0))
    return pl.pallas_call(
        _mem_attn_kernel,
        grid=(B, S // tm),
        in_specs=[row_spec, kv_spec, kv_spec, _const_spec(g_pre.shape), _const_spec(wq.shape),
                  _const_spec(wo.shape), _const_spec(g_post.shape)],
        out_specs=row_spec,
        out_shape=jax.ShapeDtypeStruct((B, S, D), F32),
        scratch_shapes=[pltpu.VMEM((tm, D), BF16)],
        compiler_params=_params("parallel", "parallel"),
        name="mem_attn",
    )(x, k, v, g_pre, wq, wo, g_post)


def _mlp_kernel(x_ref, gpre_ref, w1_ref, w2_ref, gpost_ref, o_ref):
    x = x_ref[0]
    h = _rms(x, gpre_ref[...]).astype(BF16)
    u = jnp.maximum(jnp.dot(h, w1_ref[...], preferred_element_type=F32), 0.0)
    y = jnp.dot((u * u).astype(BF16), w2_ref[...], preferred_element_type=F32)
    o_ref[0] = x + _rms(y, gpost_ref[...])


def _mlp(x, g_pre, w1, w2, g_post, tm):
    B, S, D = x.shape
    row_spec = pl.BlockSpec((1, tm, D), lambda b, i: (b, i, 0))
    return pl.pallas_call(
        _mlp_kernel,
        grid=(B, S // tm),
        in_specs=[row_spec, _const_spec(g_pre.shape), _const_spec(w1.shape), _const_spec(w2.shape),
                  _const_spec(g_post.shape)],
        out_specs=row_spec,
        out_shape=jax.ShapeDtypeStruct((B, S, D), F32),
        compiler_params=_params("parallel", "parallel"),
        name="mlp",
    )(x, g_pre, w1, w2, g_post)


def _pad_cols(w, n):
    return jnp.pad(w, ((0, 0), (0, n - w.shape[1])))


def _rope_tile(w):
    return jnp.pad(w, ((0, 0), (QK_NOPE_DIM, LANES - QK_NOPE_DIM - QK_ROPE_DIM)))


def _swap_halves(w):
    half = QK_ROPE_DIM // 2
    return jnp.concatenate([w[:, half:], w[:, :half]], axis=1)


def _lora_rows(w, lo, rows):
    return jnp.pad(w, ((lo, LORA_TILE - lo - rows), (0, 0)))


def kernel(x, mem, positions, norm_mix_pre, w_in, conv_rwkv, rwkv_w0, rwkv_w2, rwkv_a0, rwkv_a2, rwkv_g2, rwkv_k_k, rwkv_k_a, rwkv_r_k, rwkv_lnx_w, rwkv_lnx_b, mla_q_norm, mla_w_uq, mla_kv_norm, mla_w_ukv, w_out, norm_mix_post, norm_mem_pre, norm_memtok, mem_wq, mem_wkv, mem_wo, norm_mem_post, norm_mlp_pre, mlp_w1, mlp_w2, norm_mlp_post):
    depth = w_in.shape[0]
    C = RWKV_WIDTH
    head_of = jnp.arange(C) // HEAD_DIM
    seg = (head_of[:, None] == head_of[None, :]).astype(BF16)
    seg_mean = (seg.astype(F32) / HEAD_DIM).astype(BF16)
    inv_freq = ROPE_THETA ** (-jnp.arange(0, QK_ROPE_DIM, 2, dtype=F32) / QK_ROPE_DIM)
    half = QK_ROPE_DIM // 2
    lane = jnp.arange(LANES)[None, :]
    f_idx = jnp.arange(half)[:, None]
    first, second = lane == QK_NOPE_DIM + f_idx, lane == QK_NOPE_DIM + half + f_idx
    place_cos = (first | second).astype(BF16)
    place_sin = (second.astype(F32) - first.astype(F32)).astype(BF16)
    rope_base = (lane < QK_NOPE_DIM).astype(F32)
    freq_col = inv_freq[:, None]
    pos_row = positions[:, None, :]
    step = jnp.arange(CHUNK)
    tri = jnp.stack([step[None, :] <= step[:, None], step[None, :] >= step[:, None]]).astype(F32)
    row = lambda t: t.reshape(1, -1)

    for l in range(depth):
        w = w_in[l]
        mla0 = RWKV_COLS
        w_rope = w[:, mla0 + Q_LORA_RANK + KV_LORA_RANK:]
        w_r = _pad_cols(w[:, :RWKV_COLS], RWKV_TILE_COLS).astype(BF16)
        w_m = jnp.concatenate([w[:, mla0:mla0 + Q_LORA_RANK + KV_LORA_RANK], _rope_tile(w_rope),
                               _rope_tile(_swap_halves(w_rope))], axis=1).astype(BF16)
        conv_p = _pad_cols(conv_rwkv[l], RWKV_TILE_COLS)
        z_rwkv, z_mla = _in_proj(x, row(norm_mix_pre[l]), w_r, w_m, conv_p, tm=512)

        w2_p = jnp.stack([_lora_rows(rwkv_w2[l, d], d * DECAY_LORA, DECAY_LORA) for d in range(2)])
        a_lo = 2 * DECAY_LORA
        a2_p = jnp.stack([_lora_rows(rwkv_a2[l, d], a_lo + d * ICLR_LORA, ICLR_LORA) for d in range(2)])
        g2_p = _lora_rows(rwkv_g2[l], a_lo + 2 * ICLR_LORA, GATE_LORA)
        o_f, o_b, bonus_f, bonus_b, gate = _rwkv(z_rwkv, rwkv_w0[l], w2_p, rwkv_a0[l], a2_p, g2_p,
                                                 row(rwkv_k_k[l]), row(rwkv_k_a[l]), row(rwkv_r_k[l]), seg, tri, tm=256)

        qk = QK_NOPE_DIM + QK_ROPE_DIM
        w_uq = mla_w_uq[l].reshape(Q_LORA_RANK, MLA_HEADS, qk)
        wq_p = jnp.pad(w_uq, ((0, 0), (0, 0), (0, LANES - qk))).reshape(Q_LORA_RANK, MLA_HEADS * LANES)
        uq_rope = w_uq[:, :, QK_NOPE_DIM:]
        uq_sw = jnp.concatenate([uq_rope[:, :, half:], uq_rope[:, :, :half]], axis=2)
        wq_sw = jnp.pad(uq_sw, ((0, 0), (0, 0), (QK_NOPE_DIM, LANES - qk))).reshape(Q_LORA_RANK, MLA_HEADS * LANES)
        w_ukv = mla_w_ukv[l].reshape(KV_LORA_RANK, MLA_HEADS, QK_NOPE_DIM + V_HEAD_DIM)
        wk_p = jnp.pad(w_ukv[:, :, :QK_NOPE_DIM], ((0, 0), (0, 0), (0, LANES - QK_NOPE_DIM)))
        wv_p = jnp.pad(w_ukv[:, :, QK_NOPE_DIM:], ((0, 0), (0, 0), (0, LANES - V_HEAD_DIM)))
        wkv_p = jnp.concatenate([wk_p.reshape(KV_LORA_RANK, MLA_HEADS * LANES),
                                 wv_p.reshape(KV_LORA_RANK, MLA_HEADS * LANES)], axis=1)
        y_mla = _mla_attn(z_mla, pos_row, freq_col, place_cos, place_sin, rope_base, row(mla_q_norm[l]), wq_p.astype(BF16), wq_sw.astype(BF16),
                          row(mla_kv_norm[l]), wkv_p.astype(BF16), tq=512)

        wo = w_out[l].astype(BF16)
        x = _mix_out(x, o_f, o_b, bonus_f, bonus_b, gate, row(rwkv_lnx_w[l]), row(rwkv_lnx_b[l]), seg_mean,
                     y_mla, wo[:C], wo[C:], row(norm_mix_post[l]), tm=512)

        k_mem, v_mem = _mem_kv(mem, row(norm_memtok[l]), mem_wkv[l].astype(BF16))
        x = _mem_attn(x, k_mem, v_mem, row(norm_mem_pre[l]), mem_wq[l].astype(BF16), mem_wo[l].astype(BF16),
                      row(norm_mem_post[l]), tm=512)

        x = _mlp(x, row(norm_mlp_pre[l]), mlp_w1[l].astype(BF16), mlp_w2[l].astype(BF16),
                 row(norm_mlp_post[l]), tm=256)
    return x
```

```python
import math

import jax
import jax.numpy as jnp
from jax import lax
from jax.experimental import pallas as pl
from jax.experimental.pallas import tpu as pltpu

F32 = jnp.float32
BF16 = jnp.bfloat16

D_MODEL = 1024
NORM_EPS = 1e-6

RWKV_HEADS = 8
HEAD_DIM = 64
RWKV_WIDTH = RWKV_HEADS * HEAD_DIM
DECAY_LORA = 32
ICLR_LORA = 32
GATE_LORA = 96
LORA_COLS = 2 * DECAY_LORA + 2 * ICLR_LORA + GATE_LORA
LORA_TILE = 256
RWKV_COLS = 3 * RWKV_WIDTH + LORA_COLS
RWKV_TILE_COLS = 3 * RWKV_WIDTH + LORA_TILE
LN_X_EPS = 64e-5
CHUNK = 64
CHUNK_GROUP = 2
DECAY_SCALE = math.exp(-0.5)

MLA_HEADS = 8
QK_NOPE_DIM = 64
QK_ROPE_DIM = 32
V_HEAD_DIM = 64
MLA_WIDTH = MLA_HEADS * V_HEAD_DIM
Q_LORA_RANK = 256
KV_LORA_RANK = 128
ROPE_THETA = 10000.0
LOG2_E = math.log2(math.e)
LANES = 128
MLA_TILE_COLS = Q_LORA_RANK + KV_LORA_RANK + 2 * LANES

MEM_HEADS = 4
MEM_HEAD_DIM = D_MODEL // MEM_HEADS
D_FF = 4 * D_MODEL

VMEM_LIMIT = 56 * 1024 * 1024


def _mm(a, b):
    return jnp.dot(a.astype(BF16), b.astype(BF16), preferred_element_type=F32)


def _split2(t):
    hi = t.astype(BF16)
    lo = (t - hi.astype(F32)).astype(BF16)
    return hi, lo


def _split3(t):
    hi = t.astype(BF16)
    rest = t - hi.astype(F32)
    mid = rest.astype(BF16)
    lo = (rest - mid.astype(F32)).astype(BF16)
    return hi, mid, lo


def _dot(a, b):
    return jnp.dot(a, b, preferred_element_type=F32)


def _mm_x3(a, b):
    a_hi, a_lo = _split2(a)
    b_hi, b_lo = _split2(b)
    return _dot(a_hi, b_hi) + (_dot(a_hi, b_lo) + _dot(a_lo, b_hi))


def _mm_exact_lhs(a, b):
    a = a.astype(BF16)
    hi, mid, lo = _split3(b)
    return _dot(a, hi) + (_dot(a, mid) + _dot(a, lo))


def _mm_nt(a, b):
    return lax.dot_general(a.astype(BF16), b.astype(BF16), (((1,), (1,)), ((), ())),
                           preferred_element_type=F32)


def _mm_tn(a, b):
    return lax.dot_general(a.astype(BF16), b.astype(BF16), (((0,), (0,)), ((), ())),
                           preferred_element_type=F32)


def _rms(x, g, eps=NORM_EPS):
    return x * lax.rsqrt(jnp.mean(x * x, axis=-1, keepdims=True) + eps) * g


def _sigmoid(x):
    return 0.5 * jnp.tanh(0.5 * x) + 0.5


def _params(*sem):
    return pltpu.CompilerParams(dimension_semantics=sem, vmem_limit_bytes=VMEM_LIMIT)


def _const_spec(shape):
    nd = len(shape)
    return pl.BlockSpec(shape, lambda *_: (0,) * nd)


def _in_proj_kernel(x_ref, xp_ref, xn_ref, g_ref, wr_ref, wm_ref, conv_ref, zr_ref, zm_ref):
    i = pl.program_id(1)
    last = pl.num_programs(1) - 1
    tm = x_ref.shape[1]
    g = g_ref[...]
    h = _rms(x_ref[0], g).astype(BF16)
    zm_ref[0] = jnp.dot(h, wm_ref[...], preferred_element_type=F32)
    z = jnp.dot(h, wr_ref[...], preferred_element_type=F32)
    halo = _rms(jnp.concatenate([xp_ref[0], xn_ref[0]], axis=0), g).astype(BF16)
    z_halo = jnp.dot(halo, wr_ref[...], preferred_element_type=F32)
    prev_row = jnp.where(i == 0, 0.0, z_halo[7:8, :])
    next_row = jnp.where(i == last, 0.0, z_halo[8:9, :])
    row = lax.broadcasted_iota(jnp.int32, (tm, 1), 0)
    z_dn = jnp.where(row == 0, prev_row, pltpu.roll(z, 1, 0))
    z_up = jnp.where(row == tm - 1, next_row, pltpu.roll(z, tm - 1, 0))
    zr_ref[0] = conv_ref[0:1, :] * z_dn + conv_ref[1:2, :] * z + conv_ref[2:3, :] * z_up


def _in_proj(x, g, w_r, w_m, conv_p, tm):
    B, S, D = x.shape
    halo = tm // 8
    n_halo = S // 8
    return pl.pallas_call(
        _in_proj_kernel,
        grid=(B, S // tm),
        in_specs=[pl.BlockSpec((1, tm, D), lambda b, i: (b, i, 0)),
                  pl.BlockSpec((1, 8, D), lambda b, i: (b, jnp.maximum(i * halo - 1, 0), 0)),
                  pl.BlockSpec((1, 8, D), lambda b, i: (b, jnp.minimum((i + 1) * halo, n_halo - 1), 0)),
                  _const_spec((1, D)),
                  _const_spec(w_r.shape),
                  _const_spec(w_m.shape),
                  _const_spec(conv_p.shape)],
        out_specs=[pl.BlockSpec((1, tm, RWKV_TILE_COLS), lambda b, i: (b, i, 0)),
                   pl.BlockSpec((1, tm, MLA_TILE_COLS), lambda b, i: (b, i, 0))],
        out_shape=[jax.ShapeDtypeStruct((B, S, RWKV_TILE_COLS), F32),
                   jax.ShapeDtypeStruct((B, S, MLA_TILE_COLS), F32)],
        compiler_params=_params("parallel", "parallel"),
        name="in_proj",
    )(x, x, x, g, w_r, w_m, conv_p)


def _chunk_units(units, eye, eye_hi):
    L, N = units[0][0].shape
    zeros = jnp.zeros((L, N), F32)
    AA = [_mm_nt(jnp.concatenate([At, Rt], axis=0), jnp.concatenate([Bt, Kt], axis=0))
          for (At, Rt, Bt, Kt, *_) in units]
    A_a = [jnp.where(u[8], aa[:L, :], 0.0) for u, aa in zip(units, AA)]
    A_r = [jnp.where(u[9], aa[L:, :], 0.0) for u, aa in zip(units, AA)]
    AkV = [_mm(a[:, L:], u[6]) for u, a in zip(units, A_a)]
    low = lax.broadcasted_iota(jnp.int32, (L, 2 * L), 1) < L
    S = [jnp.where(low, a, eye_hi) for a in A_a]
    span = 1
    while span < L:
        R = [_mm(s[:, :L], s) for s in S]
        S = [jnp.where(low, r, r + s) for r, s in zip(R, S)]
        span *= 2
    W = [_mm(s[:, L:], jnp.concatenate([u[0], akv], axis=1)) for s, u, akv in zip(S, units, AkV)]
    Z = [jnp.concatenate([w, jnp.concatenate([zeros, u[6]], axis=1)], axis=0) for u, w in zip(units, W)]
    lhs = [jnp.concatenate([a_r, jnp.concatenate([u[4], u[5]], axis=0).T], axis=0) for u, a_r in zip(units, A_r)]
    QGH = [_mm(l, z) for l, z in zip(lhs, Z)]
    Q = [jnp.concatenate([u[1], zeros], axis=1) + t[:L] for u, t in zip(units, QGH)]
    GH = [jnp.concatenate([eye * u[7], zeros], axis=1) + t[L:] for u, t in zip(units, QGH)]
    return Q, GH


def _rwkv_kernel(zf_ref, zb_ref,
                 w0_ref, w2_ref, a0_ref, a2_ref, g2_ref, kk_ref, ka_ref, rk_ref, seg_ref, tri_ref,
                 of_ref, ob_ref, bonus_f_ref, bonus_b_ref, g_ref,
                 r_s, v_s, kk_s, b_s, kd_s, lw_s, state_s):
    i = pl.program_id(1)
    n_tiles = pl.num_programs(1)
    tm = zf_ref.shape[1]
    L, N = CHUNK, HEAD_DIM
    n_chunks = tm // L

    @pl.when(i == 0)
    def _():
        state_s[...] = jnp.zeros_like(state_s)

    seg = seg_ref[...]

    def seg_sum(t):
        return _dot(t.astype(BF16), seg)

    for d, (z_ref, bonus_ref) in enumerate(((zf_ref, bonus_f_ref), (zb_ref, bonus_b_ref))):
        r = z_ref[0, :, 0:RWKV_WIDTH]
        k = z_ref[0, :, RWKV_WIDTH:2 * RWKV_WIDTH]
        v = z_ref[0, :, 2 * RWKV_WIDTH:3 * RWKV_WIDTH]
        lora = z_ref[0, :, 3 * RWKV_WIDTH:]
        kk = k * kk_ref[...]
        kk = kk * lax.rsqrt(jnp.maximum(seg_sum(kk * kk), 1e-24))
        w_pre = w0_ref[d:d + 1, :] + _mm(jnp.tanh(lora), w2_ref[d])
        lw_s[d] = -DECAY_SCALE * _sigmoid(w_pre)
        alpha = _sigmoid(a0_ref[d:d + 1, :] + _mm(lora, a2_ref[d]))
        kd = k * (1.0 + (alpha - 1.0) * ka_ref[...])
        r_s[d] = r
        v_s[d] = v
        kk_s[d] = kk
        kd_s[d] = kd
        b_s[d] = kk * alpha
        bonus_ref[0] = (seg_sum(r * kd * rk_ref[...]) * v).astype(bonus_ref.dtype)
        if d == 0:
            g_ref[0] = _mm(_sigmoid(lora), g2_ref[...]).astype(g_ref.dtype)

    ri = lax.broadcasted_iota(jnp.int32, (L, L), 0)
    ci = lax.broadcasted_iota(jnp.int32, (L, L), 1)
    eye = (ci == ri).astype(F32)
    ri2 = lax.broadcasted_iota(jnp.int32, (L, 2 * L), 0)
    ci2 = lax.broadcasted_iota(jnp.int32, (L, 2 * L), 1)
    eye_hi = (ci2 == ri2 + L).astype(F32)
    ci2 = jnp.where(ci2 >= L, ci2 - L, ci2)
    m_strict = (ci2 < ri2, ci2 > ri2)
    m_incl = (ci2 <= ri2, ci2 >= ri2)
    out_refs = (of_ref, ob_ref)

    def chunk_units(c):
        rows_d = (pl.ds(pl.multiple_of(c * L, L), L), pl.ds(pl.multiple_of((n_chunks - 1 - c) * L, L), L))
        units = []
        for d in range(2):
            rows = rows_d[d]
            lw = lw_s[d, rows, :]
            bc = b_s[d, rows, :]
            kdc = kd_s[d, rows, :]
            vc = v_s[d, rows, :]
            cum = _mm_exact_lhs(tri_ref[d], lw)
            tot = cum[L - 1:L, :] if d == 0 else cum[0:1, :]
            e_in = jnp.exp(-cum)
            e_rem = jnp.exp(tot - cum)
            e_tot = jnp.exp(tot)
            At = -kk_s[d, rows, :] * jnp.exp(cum - lw)
            Rt = r_s[d, rows, :] * jnp.exp(cum)
            Bt = bc * e_in
            Kt = kdc * e_in
            Bh = bc * e_rem
            Kh = kdc * e_rem
            for h in range(RWKV_HEADS):
                sl = slice(h * HEAD_DIM, (h + 1) * HEAD_DIM)
                units.append((At[:, sl], Rt[:, sl], Bt[:, sl], Kt[:, sl], Bh[:, sl], Kh[:, sl],
                              vc[:, sl], e_tot[:, sl], m_strict[d], m_incl[d]))
        return rows_d, units

    n_units = 2 * RWKV_HEADS

    def group_body(j, carry):
        rows, units = [], []
        for g in range(CHUNK_GROUP):
            rows_d, chunk = chunk_units(j * CHUNK_GROUP + g)
            rows.append(rows_d)
            units.extend(chunk)
        Q, GH = _chunk_units(units, eye, eye_hi)
        M = [state_s[u] for u in range(n_units)]
        for g in range(CHUNK_GROUP):
            Qg, GHg = Q[g * n_units:(g + 1) * n_units], GH[g * n_units:(g + 1) * n_units]
            M_split = [_split2(m) for m in M]
            G_split = [_split2(gh[:, :N]) for gh in GHg]
            prod = [_dot(jnp.concatenate([q[:, :N].astype(BF16), g_hi, g_lo], axis=0), m_hi)
                    for q, (g_hi, g_lo), (m_hi, _) in zip(Qg, G_split, M_split)]
            out = [p[:L] + q[:, N:] for p, q in zip(prod, Qg)]
            M = [p[L:L + N] + (p[L + N:] + _dot(g_hi, m_lo)) + gh[:, N:]
                 for p, (g_hi, _), (_, m_lo), gh in zip(prod, G_split, M_split, GHg)]
            for d in range(2):
                for h in range(RWKV_HEADS):
                    out_refs[d][0, rows[g][d], h * N:(h + 1) * N] = out[d * RWKV_HEADS + h]
        for u in range(n_units):
            state_s[u] = M[u]
        return carry

    lax.fori_loop(0, n_chunks // CHUNK_GROUP, group_body, 0)


def _rwkv(z_rwkv, w0, w2_p, a0, a2_p, g2_p, k_k, k_a, r_k, seg, tri, tm):
    B, S, ZC = z_rwkv.shape
    C = RWKV_WIDTH
    n_tiles = S // tm
    fwd = lambda b, i: (b, i, 0)
    bwd = lambda b, i: (b, n_tiles - 1 - i, 0)
    out_f = pl.BlockSpec((1, tm, C), fwd)
    out_b = pl.BlockSpec((1, tm, C), bwd)
    out_sds = lambda dtype: jax.ShapeDtypeStruct((B, S, C), dtype)
    return pl.pallas_call(
        _rwkv_kernel,
        grid=(B, n_tiles),
        in_specs=[pl.BlockSpec((1, tm, ZC), fwd),
                  pl.BlockSpec((1, tm, ZC), bwd),
                  _const_spec(w0.shape), _const_spec(w2_p.shape),
                  _const_spec(a0.shape), _const_spec(a2_p.shape), _const_spec(g2_p.shape),
                  _const_spec(k_k.shape), _const_spec(k_a.shape), _const_spec(r_k.shape),
                  _const_spec(seg.shape), _const_spec(tri.shape)],
        out_specs=[out_f, out_b, out_f, out_b, out_f],
        out_shape=[out_sds(F32), out_sds(F32), out_sds(BF16), out_sds(BF16), out_sds(BF16)],
        scratch_shapes=[pltpu.VMEM((2, tm, C), F32)] * 6
                       + [pltpu.VMEM((2 * RWKV_HEADS, HEAD_DIM, HEAD_DIM), F32)],
        compiler_params=_params("parallel", "arbitrary"),
        name="rwkv",
    )(z_rwkv, z_rwkv, w0, w2_p, a0, a2_p, g2_p, k_k, k_a, r_k, seg, tri)


def _rope_tiles(pos_row, freq_col, place_cos, place_sin, base):
    ang = freq_col * pos_row.astype(F32)

    def place(t, p):
        return sum(lax.dot_general(part, p, (((0,), (0,)), ((), ())), preferred_element_type=F32)
                   for part in _split3(t))

    return place(jnp.cos(ang), place_cos) + base, place(jnp.sin(ang), place_sin)


def _mla_kernel(zq_ref, zkv_ref, pos_ref, freq_ref, pcos_ref, psin_ref, base_ref, qn_ref, wq_ref, wqs_ref, kvn_ref, wkv_ref,
                y_ref, k_s, v_s, cos_s, sin_s):
    i = pl.program_id(1)
    tq = zq_ref.shape[1]
    scale = (QK_NOPE_DIM + QK_ROPE_DIM) ** -0.5
    c_kv_lo, c_kv_hi = Q_LORA_RANK, Q_LORA_RANK + KV_LORA_RANK

    @pl.when(i == 0)
    def _():
        zkv = zkv_ref[0]
        kvn = _rms(zkv[:, c_kv_lo:c_kv_hi], kvn_ref[...]).astype(BF16)
        kvu = jnp.dot(kvn, wkv_ref[...], preferred_element_type=F32)
        cos_t, sin_t = _rope_tiles(pos_ref[0], freq_ref[...], pcos_ref[...], psin_ref[...], base_ref[...])
        cos_s[...] = cos_t
        sin_s[...] = sin_t
        k_rot = zkv[:, c_kv_hi:c_kv_hi + LANES] * cos_t + zkv[:, c_kv_hi + LANES:] * sin_t
        lane = lax.broadcasted_iota(jnp.int32, (1, LANES), 1)
        ones_col = (lane == V_HEAD_DIM).astype(F32)
        for h in range(MLA_HEADS):
            k_s[h] = (kvu[:, h * LANES:(h + 1) * LANES] + k_rot).astype(BF16)
            v_s[h] = (kvu[:, (MLA_HEADS + h) * LANES:(MLA_HEADS + h + 1) * LANES] + ones_col).astype(BF16)

    zq = zq_ref[0]
    qn = _rms(zq[:, :Q_LORA_RANK], qn_ref[...]).astype(BF16)
    q_a = jnp.dot(qn, wq_ref[...], preferred_element_type=F32)
    q_b = jnp.dot(qn, wqs_ref[...], preferred_element_type=F32)
    rows = pl.ds(pl.multiple_of(i * tq, tq), tq)
    cos_q = cos_s[rows, :] * (scale * LOG2_E)
    sin_q = sin_s[rows, :] * (scale * LOG2_E)

    def scores(h):
        sl = slice(h * LANES, (h + 1) * LANES)
        qh = (q_a[:, sl] * cos_q + q_b[:, sl] * sin_q).astype(BF16)
        return lax.dot_general(qh, k_s[h], (((1,), (1,)), ((), ())), preferred_element_type=F32)

    s = scores(0)
    for h in range(MLA_HEADS):
        s_next = scores(h + 1) if h + 1 < MLA_HEADS else None
        p = jnp.exp2((s - jnp.max(s, axis=-1, keepdims=True)).astype(BF16))
        o = jnp.dot(p, v_s[h], preferred_element_type=F32)
        y_ref[0, :, h * V_HEAD_DIM:(h + 1) * V_HEAD_DIM] = (
            o[:, :V_HEAD_DIM] / o[:, V_HEAD_DIM:V_HEAD_DIM + 1]).astype(y_ref.dtype)
        s = s_next


def _mla_attn(z_mla, pos_row, freq_col, place_cos, place_sin, base, q_norm, wq_p, wq_sw, kv_norm, wkv_p, tq):
    B, S, ZC = z_mla.shape
    return pl.pallas_call(
        _mla_kernel,
        grid=(B, S // tq),
        in_specs=[pl.BlockSpec((1, tq, ZC), lambda b, i: (b, i, 0)),
                  pl.BlockSpec((1, S, ZC), lambda b, i: (b, 0, 0)),
                  pl.BlockSpec((1, 1, S), lambda b, i: (b, 0, 0)),
                  _const_spec(freq_col.shape), _const_spec(place_cos.shape), _const_spec(place_sin.shape),
                  _const_spec(base.shape), _const_spec(q_norm.shape),
                  _const_spec(wq_p.shape), _const_spec(wq_sw.shape), _const_spec(kv_norm.shape),
                  _const_spec(wkv_p.shape)],
        out_specs=pl.BlockSpec((1, tq, MLA_WIDTH), lambda b, i: (b, i, 0)),
        out_shape=jax.ShapeDtypeStruct((B, S, MLA_WIDTH), BF16),
        scratch_shapes=[pltpu.VMEM((MLA_HEADS, S, LANES), BF16), pltpu.VMEM((MLA_HEADS, S, LANES), BF16),
                        pltpu.VMEM((S, LANES), F32), pltpu.VMEM((S, LANES), F32)],
        compiler_params=_params("parallel", "arbitrary"),
        name="mla_attn",
    )(z_mla, z_mla, pos_row, freq_col, place_cos, place_sin, base, q_norm, wq_p, wq_sw, kv_norm, wkv_p)


def _mix_out_kernel(x_ref, of_ref, ob_ref, bf_ref, bb_ref, gate_ref, lnw_ref, lnb_ref, segm_ref, ym_ref,
                    wr_ref, wm_ref, g_ref, o_ref):
    segm = segm_ref[...]

    o = of_ref[0] + ob_ref[0]
    o_hi, o_lo = _split2(o)
    oc = o - (_dot(o_hi, segm) + _dot(o_lo, segm))
    var = _dot((oc * oc).astype(BF16), segm)
    o = oc * lax.rsqrt(var + LN_X_EPS) * lnw_ref[...] + lnb_ref[...]
    y_rwkv = (o + (bf_ref[0].astype(F32) + bb_ref[0].astype(F32))) * gate_ref[0].astype(F32)
    y = (jnp.dot(y_rwkv.astype(BF16), wr_ref[...], preferred_element_type=F32)
         + jnp.dot(ym_ref[0].astype(BF16), wm_ref[...], preferred_element_type=F32))
    o_ref[0] = x_ref[0] + _rms(y, g_ref[...])


def _mix_out(x, o_f, o_b, bonus_f, bonus_b, gate, lnw, lnb, segm, y_mla, w_r, w_m, g, tm):
    B, S, D = x.shape
    row_spec = lambda cols: pl.BlockSpec((1, tm, cols), lambda b, i: (b, i, 0))
    C = RWKV_WIDTH
    return pl.pallas_call(
        _mix_out_kernel,
        grid=(B, S // tm),
        in_specs=[row_spec(D), row_spec(C), row_spec(C), row_spec(C), row_spec(C), row_spec(C),
                  _const_spec(lnw.shape), _const_spec(lnb.shape), _const_spec(segm.shape), row_spec(MLA_WIDTH),
                  _const_spec(w_r.shape), _const_spec(w_m.shape), _const_spec(g.shape)],
        out_specs=row_spec(D),
        out_shape=jax.ShapeDtypeStruct((B, S, D), F32),
        compiler_params=_params("parallel", "parallel"),
        name="mix_out",
    )(x, o_f, o_b, bonus_f, bonus_b, gate, lnw, lnb, segm, y_mla, w_r, w_m, g)


def _mem_kv_kernel(mem_ref, g_ref, w_ref, k_ref, v_ref):
    m = _rms(mem_ref[0], g_ref[...]).astype(BF16)
    kv = jnp.dot(m, w_ref[...], preferred_element_type=F32)
    k_ref[0] = kv[:, :D_MODEL].astype(BF16)
    v_ref[0] = kv[:, D_MODEL:].astype(BF16)


def _mem_kv(mem, g, wkv):
    B, T, D = mem.shape
    return pl.pallas_call(
        _mem_kv_kernel,
        grid=(B,),
        in_specs=[pl.BlockSpec((1, T, D), lambda b: (b, 0, 0)), _const_spec(g.shape), _const_spec(wkv.shape)],
        out_specs=[pl.BlockSpec((1, T, D), lambda b: (b, 0, 0)), pl.BlockSpec((1, T, D), lambda b: (b, 0, 0))],
        out_shape=[jax.ShapeDtypeStruct((B, T, D), BF16), jax.ShapeDtypeStruct((B, T, D), BF16)],
        compiler_params=_params("parallel"),
        name="mem_kv",
    )(mem, g, wkv)


def _mem_attn_kernel(x_ref, k_ref, v_ref, gpre_ref, wq_ref, wo_ref, gpost_ref, o_ref, att_s):
    x = x_ref[0]
    h = _rms(x, gpre_ref[...]).astype(BF16)
    q = jnp.dot(h, wq_ref[...], preferred_element_type=F32) * (MEM_HEAD_DIM ** -0.5)
    for hd in range(MEM_HEADS):
        sl = slice(hd * MEM_HEAD_DIM, (hd + 1) * MEM_HEAD_DIM)
        s = lax.dot_general(q[:, sl].astype(BF16), k_ref[0, :, sl], (((1,), (1,)), ((), ())),
                            preferred_element_type=F32)
        p = jnp.exp(s - jnp.max(s, axis=-1, keepdims=True))
        denom = jnp.sum(p, axis=-1, keepdims=True)
        o = jnp.dot(p.astype(BF16), v_ref[0, :, sl], preferred_element_type=F32)
        att_s[:, sl] = (o / denom).astype(BF16)
    y = jnp.dot(att_s[...], wo_ref[...], preferred_element_type=F32)
    o_ref[0] = x + _rms(y, gpost_ref[...])


def _mem_attn(x, k, v, g_pre, wq, wo, g_post, tm):
    B, S, D = x.shape
    T = k.shape[1]
    row_spec = pl.BlockSpec((1, tm, D), lambda b, i: (b, i, 0))
    kv_spec = pl.BlockSpec((1, T, D), lambda b, i: (b, 0, 0))
    return pl.pallas_call(
        _mem_attn_kernel,
        grid=(B, S // tm),
        in_specs=[row_spec, kv_spec, kv_spec, _const_spec(g_pre.shape), _const_spec(wq.shape),
                  _const_spec(wo.shape), _const_spec(g_post.shape)],
        out_specs=row_spec,
        out_shape=jax.ShapeDtypeStruct((B, S, D), F32),
        scratch_shapes=[pltpu.VMEM((tm, D), BF16)],
        compiler_params=_params("parallel", "parallel"),
        name="mem_attn",
    )(x, k, v, g_pre, wq, wo, g_post)


def _mlp_kernel(x_ref, gpre_ref, w1_ref, w2_ref, gpost_ref, o_ref):
    x = x_ref[0]
    h = _rms(x, gpre_ref[...]).astype(BF16)
    u = jnp.maximum(jnp.dot(h, w1_ref[...], preferred_element_type=F32), 0.0)
    y = jnp.dot((u * u).astype(BF16), w2_ref[...], preferred_element_type=F32)
    o_ref[0] = x + _rms(y, gpost_ref[...])


def _mlp(x, g_pre, w1, w2, g_post, tm):
    B, S, D = x.shape
    row_spec = pl.BlockSpec((1, tm, D), lambda b, i: (b, i, 0))
    return pl.pallas_call(
        _mlp_kernel,
        grid=(B, S // tm),
        in_specs=[row_spec, _const_spec(g_pre.shape), _const_spec(w1.shape), _const_spec(w2.shape),
                  _const_spec(g_post.shape)],
        out_specs=row_spec,
        out_shape=jax.ShapeDtypeStruct((B, S, D), F32),
        compiler_params=_params("parallel", "parallel"),
        name="mlp",
    )(x, g_pre, w1, w2, g_post)


def _pad_cols(w, n):
    return jnp.pad(w, ((0, 0), (0, n - w.shape[1])))


def _rope_tile(w):
    return jnp.pad(w, ((0, 0), (QK_NOPE_DIM, LANES - QK_NOPE_DIM - QK_ROPE_DIM)))


def _swap_halves(w):
    half = QK_ROPE_DIM // 2
    return jnp.concatenate([w[:, half:], w[:, :half]], axis=1)


def _lora_rows(w, lo, rows):
    return jnp.pad(w, ((lo, LORA_TILE - lo - rows), (0, 0)))


def kernel(x, mem, positions, norm_mix_pre, w_in, conv_rwkv, rwkv_w0, rwkv_w2, rwkv_a0, rwkv_a2, rwkv_g2, rwkv_k_k, rwkv_k_a, rwkv_r_k, rwkv_lnx_w, rwkv_lnx_b, mla_q_norm, mla_w_uq, mla_kv_norm, mla_w_ukv, w_out, norm_mix_post, norm_mem_pre, norm_memtok, mem_wq, mem_wkv, mem_wo, norm_mem_post, norm_mlp_pre, mlp_w1, mlp_w2, norm_mlp_post):
    depth = w_in.shape[0]
    C = RWKV_WIDTH
    head_of = jnp.arange(C) // HEAD_DIM
    seg = (head_of[:, None] == head_of[None, :]).astype(BF16)
    seg_mean = (seg.astype(F32) / HEAD_DIM).astype(BF16)
    inv_freq = ROPE_THETA ** (-jnp.arange(0, QK_ROPE_DIM, 2, dtype=F32) / QK_ROPE_DIM)
    half = QK_ROPE_DIM // 2
    lane = jnp.arange(LANES)[None, :]
    f_idx = jnp.arange(half)[:, None]
    first, second = lane == QK_NOPE_DIM + f_idx, lane == QK_NOPE_DIM + half + f_idx
    place_cos = (first | second).astype(BF16)
    place_sin = (second.astype(F32) - first.astype(F32)).astype(BF16)
    rope_base = (lane < QK_NOPE_DIM).astype(F32)
    freq_col = inv_freq[:, None]
    pos_row = positions[:, None, :]
    step = jnp.arange(CHUNK)
    tri = jnp.stack([step[None, :] <= step[:, None], step[None, :] >= step[:, None]]).astype(F32)
    row = lambda t: t.reshape(1, -1)

    for l in range(depth):
        w = w_in[l]
        mla0 = RWKV_COLS
        w_rope = w[:, mla0 + Q_LORA_RANK + KV_LORA_RANK:]
        w_r = _pad_cols(w[:, :RWKV_COLS], RWKV_TILE_COLS).astype(BF16)
        w_m = jnp.concatenate([w[:, mla0:mla0 + Q_LORA_RANK + KV_LORA_RANK], _rope_tile(w_rope),
                               _rope_tile(_swap_halves(w_rope))], axis=1).astype(BF16)
        conv_p = _pad_cols(conv_rwkv[l], RWKV_TILE_COLS)
        z_rwkv, z_mla = _in_proj(x, row(norm_mix_pre[l]), w_r, w_m, conv_p, tm=512)

        w2_p = jnp.stack([_lora_rows(rwkv_w2[l, d], d * DECAY_LORA, DECAY_LORA) for d in range(2)])
        a_lo = 2 * DECAY_LORA
        a2_p = jnp.stack([_lora_rows(rwkv_a2[l, d], a_lo + d * ICLR_LORA, ICLR_LORA) for d in range(2)])
        g2_p = _lora_rows(rwkv_g2[l], a_lo + 2 * ICLR_LORA, GATE_LORA)
        o_f, o_b, bonus_f, bonus_b, gate = _rwkv(z_rwkv, rwkv_w0[l], w2_p, rwkv_a0[l], a2_p, g2_p,
                                                 row(rwkv_k_k[l]), row(rwkv_k_a[l]), row(rwkv_r_k[l]), seg, tri, tm=256)

        qk = QK_NOPE_DIM + QK_ROPE_DIM
        w_uq = mla_w_uq[l].reshape(Q_LORA_RANK, MLA_HEADS, qk)
        wq_p = jnp.pad(w_uq, ((0, 0), (0, 0), (0, LANES - qk))).reshape(Q_LORA_RANK, MLA_HEADS * LANES)
        uq_rope = w_uq[:, :, QK_NOPE_DIM:]
        uq_sw = jnp.concatenate([uq_rope[:, :, half:], uq_rope[:, :, :half]], axis=2)
        wq_sw = jnp.pad(uq_sw, ((0, 0), (0, 0), (QK_NOPE_DIM, LANES - qk))).reshape(Q_LORA_RANK, MLA_HEADS * LANES)
        w_ukv = mla_w_ukv[l].reshape(KV_LORA_RANK, MLA_HEADS, QK_NOPE_DIM + V_HEAD_DIM)
        wk_p = jnp.pad(w_ukv[:, :, :QK_NOPE_DIM], ((0, 0), (0, 0), (0, LANES - QK_NOPE_DIM)))
        wv_p = jnp.pad(w_ukv[:, :, QK_NOPE_DIM:], ((0, 0), (0, 0), (0, LANES - V_HEAD_DIM)))
        wkv_p = jnp.concatenate([wk_p.reshape(KV_LORA_RANK, MLA_HEADS * LANES),
                                 wv_p.reshape(KV_LORA_RANK, MLA_HEADS * LANES)], axis=1)
        y_mla = _mla_attn(z_mla, pos_row, freq_col, place_cos, place_sin, rope_base, row(mla_q_norm[l]), wq_p.astype(BF16), wq_sw.astype(BF16),
                          row(mla_kv_norm[l]), wkv_p.astype(BF16), tq=512)

        wo = w_out[l].astype(BF16)
        x = _mix_out(x, o_f, o_b, bonus_f, bonus_b, gate, row(rwkv_lnx_w[l]), row(rwkv_lnx_b[l]), seg_mean,
                     y_mla, wo[:C], wo[C:], row(norm_mix_post[l]), tm=512)

        k_mem, v_mem = _mem_kv(mem, row(norm_memtok[l]), mem_wkv[l].astype(BF16))
        x = _mem_attn(x, k_mem, v_mem, row(norm_mem_pre[l]), mem_wq[l].astype(BF16), mem_wo[l].astype(BF16),
                      row(norm_mem_post[l]), tm=512)

        x = _mlp(x, row(norm_mlp_pre[l]), mlp_w1[l].astype(BF16), mlp_w2[l].astype(BF16),
                 row(norm_mlp_post[l]), tm=512)
    return x
```

```python
import math

import jax
import jax.numpy as jnp
from jax import lax
from jax.experimental import pallas as pl
from jax.experimental.pallas import tpu as pltpu

F32 = jnp.float32
BF16 = jnp.bfloat16

D_MODEL = 1024
NORM_EPS = 1e-6

RWKV_HEADS = 8
HEAD_DIM = 64
RWKV_WIDTH = RWKV_HEADS * HEAD_DIM
DECAY_LORA = 32
ICLR_LORA = 32
GATE_LORA = 96
LORA_COLS = 2 * DECAY_LORA + 2 * ICLR_LORA + GATE_LORA
LORA_TILE = 256
RWKV_COLS = 3 * RWKV_WIDTH + LORA_COLS
RWKV_TILE_COLS = 3 * RWKV_WIDTH + LORA_TILE
LN_X_EPS = 64e-5
CHUNK = 64
CHUNK_GROUP = 2
DECAY_SCALE = math.exp(-0.5)

MLA_HEADS = 8
QK_NOPE_DIM = 64
QK_ROPE_DIM = 32
V_HEAD_DIM = 64
MLA_WIDTH = MLA_HEADS * V_HEAD_DIM
Q_LORA_RANK = 256
KV_LORA_RANK = 128
ROPE_THETA = 10000.0
LOG2_E = math.log2(math.e)
LANES = 128
MLA_TILE_COLS = Q_LORA_RANK + KV_LORA_RANK + 2 * LANES

MEM_HEADS = 4
MEM_HEAD_DIM = D_MODEL // MEM_HEADS
D_FF = 4 * D_MODEL

VMEM_LIMIT = 56 * 1024 * 1024


def _mm(a, b):
    return jnp.dot(a.astype(BF16), b.astype(BF16), preferred_element_type=F32)


def _split2(t):
    hi = t.astype(BF16)
    lo = (t - hi.astype(F32)).astype(BF16)
    return hi, lo


def _split3(t):
    hi = t.astype(BF16)
    rest = t - hi.astype(F32)
    mid = rest.astype(BF16)
    lo = (rest - mid.astype(F32)).astype(BF16)
    return hi, mid, lo


def _dot(a, b):
    return jnp.dot(a, b, preferred_element_type=F32)


def _mm_x3(a, b):
    a_hi, a_lo = _split2(a)
    b_hi, b_lo = _split2(b)
    return _dot(a_hi, b_hi) + (_dot(a_hi, b_lo) + _dot(a_lo, b_hi))


def _mm_exact_lhs(a, b):
    a = a.astype(BF16)
    hi, lo = _split2(b)
    return _dot(a, hi) + _dot(a, lo)


def _mm_nt(a, b):
    return lax.dot_general(a.astype(BF16), b.astype(BF16), (((1,), (1,)), ((), ())),
                           preferred_element_type=F32)


def _mm_tn(a, b):
    return lax.dot_general(a.astype(BF16), b.astype(BF16), (((0,), (0,)), ((), ())),
                           preferred_element_type=F32)


def _rms(x, g, eps=NORM_EPS):
    return x * lax.rsqrt(jnp.mean(x * x, axis=-1, keepdims=True) + eps) * g


def _sigmoid(x):
    return 0.5 * jnp.tanh(0.5 * x) + 0.5


def _params(*sem):
    return pltpu.CompilerParams(dimension_semantics=sem, vmem_limit_bytes=VMEM_LIMIT)


def _const_spec(shape):
    nd = len(shape)
    return pl.BlockSpec(shape, lambda *_: (0,) * nd)


def _in_proj_kernel(x_ref, xp_ref, xn_ref, g_ref, wr_ref, wm_ref, conv_ref, zr_ref, zm_ref):
    i = pl.program_id(1)
    last = pl.num_programs(1) - 1
    tm = x_ref.shape[1]
    g = g_ref[...]
    h = _rms(x_ref[0], g).astype(BF16)
    zm_ref[0] = jnp.dot(h, wm_ref[...], preferred_element_type=F32)
    z = jnp.dot(h, wr_ref[...], preferred_element_type=F32)
    halo = _rms(jnp.concatenate([xp_ref[0], xn_ref[0]], axis=0), g).astype(BF16)
    z_halo = jnp.dot(halo, wr_ref[...], preferred_element_type=F32)
    prev_row = jnp.where(i == 0, 0.0, z_halo[7:8, :])
    next_row = jnp.where(i == last, 0.0, z_halo[8:9, :])
    row = lax.broadcasted_iota(jnp.int32, (tm, 1), 0)
    z_dn = jnp.where(row == 0, prev_row, pltpu.roll(z, 1, 0))
    z_up = jnp.where(row == tm - 1, next_row, pltpu.roll(z, tm - 1, 0))
    zr_ref[0] = conv_ref[0:1, :] * z_dn + conv_ref[1:2, :] * z + conv_ref[2:3, :] * z_up


def _in_proj(x, g, w_r, w_m, conv_p, tm):
    B, S, D = x.shape
    halo = tm // 8
    n_halo = S // 8
    return pl.pallas_call(
        _in_proj_kernel,
        grid=(B, S // tm),
        in_specs=[pl.BlockSpec((1, tm, D), lambda b, i: (b, i, 0)),
                  pl.BlockSpec((1, 8, D), lambda b, i: (b, jnp.maximum(i * halo - 1, 0), 0)),
                  pl.BlockSpec((1, 8, D), lambda b, i: (b, jnp.minimum((i + 1) * halo, n_halo - 1), 0)),
                  _const_spec((1, D)),
                  _const_spec(w_r.shape),
                  _const_spec(w_m.shape),
                  _const_spec(conv_p.shape)],
        out_specs=[pl.BlockSpec((1, tm, RWKV_TILE_COLS), lambda b, i: (b, i, 0)),
                   pl.BlockSpec((1, tm, MLA_TILE_COLS), lambda b, i: (b, i, 0))],
        out_shape=[jax.ShapeDtypeStruct((B, S, RWKV_TILE_COLS), F32),
                   jax.ShapeDtypeStruct((B, S, MLA_TILE_COLS), F32)],
        compiler_params=_params("parallel", "parallel"),
        name="in_proj",
    )(x, x, x, g, w_r, w_m, conv_p)


def _chunk_units(units, eye, eye_hi):
    L, N = units[0][0].shape
    zeros = jnp.zeros((L, N), BF16)
    AA = [_mm_nt(jnp.concatenate([At, Rt], axis=0), jnp.concatenate([Bt, Kt], axis=0))
          for (At, Rt, Bt, Kt, *_) in units]
    A_a = [jnp.where(u[8], aa[:L, :], 0.0) for u, aa in zip(units, AA)]
    A_r = [jnp.where(u[9], aa[L:, :], 0.0) for u, aa in zip(units, AA)]
    AkV = [_mm(a.astype(BF16)[:, L:], u[6]) for u, a in zip(units, A_a)]
    low = lax.broadcasted_iota(jnp.int32, (L, 2 * L), 1) < L
    S = [jnp.where(low, a, eye_hi) for a in A_a]
    span = 1
    while span < L:
        R = [_mm(s[:, :L], s) for s in S]
        S = [jnp.where(low, r, r + s) for r, s in zip(R, S)]
        span *= 2
    W = [_mm(s.astype(BF16)[:, L:], jnp.concatenate([u[0], akv.astype(BF16)], axis=1))
         for s, u, akv in zip(S, units, AkV)]
    Z = [jnp.concatenate([w.astype(BF16), jnp.concatenate([zeros, u[6]], axis=1)], axis=0)
         for u, w in zip(units, W)]
    lhs = [jnp.concatenate([a_r.astype(BF16), jnp.concatenate([u[4], u[5]], axis=0).T], axis=0)
           for u, a_r in zip(units, A_r)]
    QGH = [_mm(l, z) for l, z in zip(lhs, Z)]
    Q = [jnp.concatenate([u[1], zeros], axis=1) + t[:L] for u, t in zip(units, QGH)]
    GH = [jnp.concatenate([eye * u[7], zeros.astype(F32)], axis=1) + t[L:] for u, t in zip(units, QGH)]
    return Q, GH


def _rwkv_kernel(zf_ref, zb_ref,
                 w0_ref, w2_ref, a0_ref, a2_ref, g2_ref, kk_ref, ka_ref, rk_ref, seg_ref, tri_ref,
                 of_ref, ob_ref, bonus_f_ref, bonus_b_ref, g_ref,
                 r_s, v_s, kk_s, b_s, kd_s, lw_s, state_s):
    i = pl.program_id(1)
    n_tiles = pl.num_programs(1)
    tm = zf_ref.shape[1]
    L, N = CHUNK, HEAD_DIM
    n_chunks = tm // L

    @pl.when(i == 0)
    def _():
        state_s[...] = jnp.zeros_like(state_s)

    seg = seg_ref[...]

    def seg_sum(t):
        return _dot(t.astype(BF16), seg)

    for d, (z_ref, bonus_ref) in enumerate(((zf_ref, bonus_f_ref), (zb_ref, bonus_b_ref))):
        r = z_ref[0, :, 0:RWKV_WIDTH]
        k = z_ref[0, :, RWKV_WIDTH:2 * RWKV_WIDTH]
        v = z_ref[0, :, 2 * RWKV_WIDTH:3 * RWKV_WIDTH]
        lora = z_ref[0, :, 3 * RWKV_WIDTH:]
        kk = k * kk_ref[...]
        kk = kk * lax.rsqrt(jnp.maximum(seg_sum(kk * kk), 1e-24))
        w_pre = w0_ref[d:d + 1, :] + _mm(jnp.tanh(lora), w2_ref[d])
        lw_s[d] = -DECAY_SCALE * _sigmoid(w_pre)
        alpha = _sigmoid(a0_ref[d:d + 1, :] + _mm(lora, a2_ref[d]))
        kd = k * (1.0 + (alpha - 1.0) * ka_ref[...])
        r_s[d] = r
        v_s[d] = v
        kk_s[d] = kk
        kd_s[d] = kd
        b_s[d] = kk * alpha
        bonus_ref[0] = (seg_sum(r * kd * rk_ref[...]) * v).astype(bonus_ref.dtype)
        if d == 0:
            g_ref[0] = _mm(_sigmoid(lora), g2_ref[...]).astype(g_ref.dtype)

    ri = lax.broadcasted_iota(jnp.int32, (L, L), 0)
    ci = lax.broadcasted_iota(jnp.int32, (L, L), 1)
    eye = (ci == ri).astype(F32)
    ri2 = lax.broadcasted_iota(jnp.int32, (L, 2 * L), 0)
    ci2 = lax.broadcasted_iota(jnp.int32, (L, 2 * L), 1)
    eye_hi = (ci2 == ri2 + L).astype(F32)
    ci2 = jnp.where(ci2 >= L, ci2 - L, ci2)
    m_strict = (ci2 < ri2, ci2 > ri2)
    m_incl = (ci2 <= ri2, ci2 >= ri2)
    out_refs = (of_ref, ob_ref)

    def chunk_units(c):
        rows_d = (pl.ds(pl.multiple_of(c * L, L), L), pl.ds(pl.multiple_of((n_chunks - 1 - c) * L, L), L))
        units = []
        for d in range(2):
            rows = rows_d[d]
            lw = lw_s[d, rows, :]
            bc = b_s[d, rows, :]
            kdc = kd_s[d, rows, :]
            vc = v_s[d, rows, :]
            cum = _mm_exact_lhs(tri_ref[d], lw)
            tot = cum[L - 1:L, :] if d == 0 else cum[0:1, :]
            e_in = jnp.exp(-cum)
            e_rem = jnp.exp(tot - cum)
            e_tot = jnp.exp(tot)
            At = (-kk_s[d, rows, :] * jnp.exp(cum - lw)).astype(BF16)
            Rt = (r_s[d, rows, :] * jnp.exp(cum)).astype(BF16)
            Bt = (bc * e_in).astype(BF16)
            Kt = (kdc * e_in).astype(BF16)
            Bh = (bc * e_rem).astype(BF16)
            Kh = (kdc * e_rem).astype(BF16)
            vc = vc.astype(BF16)
            for h in range(RWKV_HEADS):
                sl = slice(h * HEAD_DIM, (h + 1) * HEAD_DIM)
                units.append((At[:, sl], Rt[:, sl], Bt[:, sl], Kt[:, sl], Bh[:, sl], Kh[:, sl],
                              vc[:, sl], e_tot[:, sl], m_strict[d], m_incl[d]))
        return rows_d, units

    n_units = 2 * RWKV_HEADS

    def group_body(j, carry):
        rows, units = [], []
        for g in range(CHUNK_GROUP):
            rows_d, chunk = chunk_units(j * CHUNK_GROUP + g)
            rows.append(rows_d)
            units.extend(chunk)
        Q, GH = _chunk_units(units, eye, eye_hi)
        M = [state_s[u] for u in range(n_units)]
        for g in range(CHUNK_GROUP):
            Qg, GHg = Q[g * n_units:(g + 1) * n_units], GH[g * n_units:(g + 1) * n_units]
            M_split = [_split2(m) for m in M]
            G_split = [_split2(gh[:, :N]) for gh in GHg]
            prod = [_dot(jnp.concatenate([q[:, :N].astype(BF16), g_hi, g_lo], axis=0), m_hi)
                    for q, (g_hi, g_lo), (m_hi, _) in zip(Qg, G_split, M_split)]
            out = [p[:L] + q[:, N:] for p, q in zip(prod, Qg)]
            M = [p[L:L + N] + (p[L + N:] + _dot(g_hi, m_lo)) + gh[:, N:]
                 for p, (g_hi, _), (_, m_lo), gh in zip(prod, G_split, M_split, GHg)]
            for d in range(2):
                for h in range(RWKV_HEADS):
                    out_refs[d][0, rows[g][d], h * N:(h + 1) * N] = out[d * RWKV_HEADS + h]
        for u in range(n_units):
            state_s[u] = M[u]
        return carry

    lax.fori_loop(0, n_chunks // CHUNK_GROUP, group_body, 0)


def _rwkv(z_rwkv, w0, w2_p, a0, a2_p, g2_p, k_k, k_a, r_k, seg, tri, tm):
    B, S, ZC = z_rwkv.shape
    C = RWKV_WIDTH
    n_tiles = S // tm
    fwd = lambda b, i: (b, i, 0)
    bwd = lambda b, i: (b, n_tiles - 1 - i, 0)
    out_f = pl.BlockSpec((1, tm, C), fwd)
    out_b = pl.BlockSpec((1, tm, C), bwd)
    out_sds = lambda dtype: jax.ShapeDtypeStruct((B, S, C), dtype)
    return pl.pallas_call(
        _rwkv_kernel,
        grid=(B, n_tiles),
        in_specs=[pl.BlockSpec((1, tm, ZC), fwd),
                  pl.BlockSpec((1, tm, ZC), bwd),
                  _const_spec(w0.shape), _const_spec(w2_p.shape),
                  _const_spec(a0.shape), _const_spec(a2_p.shape), _const_spec(g2_p.shape),
                  _const_spec(k_k.shape), _const_spec(k_a.shape), _const_spec(r_k.shape),
                  _const_spec(seg.shape), _const_spec(tri.shape)],
        out_specs=[out_f, out_b, out_f, out_b, out_f],
        out_shape=[out_sds(F32), out_sds(F32), out_sds(BF16), out_sds(BF16), out_sds(BF16)],
        scratch_shapes=[pltpu.VMEM((2, tm, C), F32)] * 6
                       + [pltpu.VMEM((2 * RWKV_HEADS, HEAD_DIM, HEAD_DIM), F32)],
        compiler_params=_params("parallel", "arbitrary"),
        name="rwkv",
    )(z_rwkv, z_rwkv, w0, w2_p, a0, a2_p, g2_p, k_k, k_a, r_k, seg, tri)


def _rope_tiles(pos_row, freq_col, place_cos, place_sin, base):
    ang = freq_col * pos_row.astype(F32)

    def place(t, p):
        return sum(lax.dot_general(part, p, (((0,), (0,)), ((), ())), preferred_element_type=F32)
                   for part in _split3(t))

    return place(jnp.cos(ang), place_cos) + base, place(jnp.sin(ang), place_sin)


def _mla_kernel(zq_ref, zkv_ref, pos_ref, freq_ref, pcos_ref, psin_ref, base_ref, qn_ref, wq_ref, wqs_ref, kvn_ref, wkv_ref,
                y_ref, k_s, v_s, cos_s, sin_s):
    i = pl.program_id(1)
    tq = zq_ref.shape[1]
    scale = (QK_NOPE_DIM + QK_ROPE_DIM) ** -0.5
    c_kv_lo, c_kv_hi = Q_LORA_RANK, Q_LORA_RANK + KV_LORA_RANK

    @pl.when(i == 0)
    def _():
        zkv = zkv_ref[0]
        kvn = _rms(zkv[:, c_kv_lo:c_kv_hi], kvn_ref[...]).astype(BF16)
        kvu = jnp.dot(kvn, wkv_ref[...], preferred_element_type=F32)
        cos_t, sin_t = _rope_tiles(pos_ref[0], freq_ref[...], pcos_ref[...], psin_ref[...], base_ref[...])
        cos_s[...] = cos_t
        sin_s[...] = sin_t
        k_rot = zkv[:, c_kv_hi:c_kv_hi + LANES] * cos_t + zkv[:, c_kv_hi + LANES:] * sin_t
        lane = lax.broadcasted_iota(jnp.int32, (1, LANES), 1)
        ones_col = (lane == V_HEAD_DIM).astype(F32)
        for h in range(MLA_HEADS):
            k_s[h] = (kvu[:, h * LANES:(h + 1) * LANES] + k_rot).astype(BF16)
            v_s[h] = (kvu[:, (MLA_HEADS + h) * LANES:(MLA_HEADS + h + 1) * LANES] + ones_col).astype(BF16)

    zq = zq_ref[0]
    qn = _rms(zq[:, :Q_LORA_RANK], qn_ref[...]).astype(BF16)
    q_a = jnp.dot(qn, wq_ref[...], preferred_element_type=F32)
    q_b = jnp.dot(qn, wqs_ref[...], preferred_element_type=F32)
    rows = pl.ds(pl.multiple_of(i * tq, tq), tq)
    cos_q = cos_s[rows, :] * (scale * LOG2_E)
    sin_q = sin_s[rows, :] * (scale * LOG2_E)

    def scores(h):
        sl = slice(h * LANES, (h + 1) * LANES)
        qh = (q_a[:, sl] * cos_q + q_b[:, sl] * sin_q).astype(BF16)
        return lax.dot_general(qh, k_s[h], (((1,), (1,)), ((), ())), preferred_element_type=F32)

    s = scores(0)
    for h in range(MLA_HEADS):
        s_next = scores(h + 1) if h + 1 < MLA_HEADS else None
        p = jnp.exp2((s - jnp.max(s, axis=-1, keepdims=True)).astype(BF16))
        o = jnp.dot(p, v_s[h], preferred_element_type=F32)
        y_ref[0, :, h * V_HEAD_DIM:(h + 1) * V_HEAD_DIM] = (
            o[:, :V_HEAD_DIM] / o[:, V_HEAD_DIM:V_HEAD_DIM + 1]).astype(y_ref.dtype)
        s = s_next


def _mla_attn(z_mla, pos_row, freq_col, place_cos, place_sin, base, q_norm, wq_p, wq_sw, kv_norm, wkv_p, tq):
    B, S, ZC = z_mla.shape
    return pl.pallas_call(
        _mla_kernel,
        grid=(B, S // tq),
        in_specs=[pl.BlockSpec((1, tq, ZC), lambda b, i: (b, i, 0)),
                  pl.BlockSpec((1, S, ZC), lambda b, i: (b, 0, 0)),
                  pl.BlockSpec((1, 1, S), lambda b, i: (b, 0, 0)),
                  _const_spec(freq_col.shape), _const_spec(place_cos.shape), _const_spec(place_sin.shape),
                  _const_spec(base.shape), _const_spec(q_norm.shape),
                  _const_spec(wq_p.shape), _const_spec(wq_sw.shape), _const_spec(kv_norm.shape),
                  _const_spec(wkv_p.shape)],
        out_specs=pl.BlockSpec((1, tq, MLA_WIDTH), lambda b, i: (b, i, 0)),
        out_shape=jax.ShapeDtypeStruct((B, S, MLA_WIDTH), BF16),
        scratch_shapes=[pltpu.VMEM((MLA_HEADS, S, LANES), BF16), pltpu.VMEM((MLA_HEADS, S, LANES), BF16),
                        pltpu.VMEM((S, LANES), F32), pltpu.VMEM((S, LANES), F32)],
        compiler_params=_params("parallel", "arbitrary"),
        name="mla_attn",
    )(z_mla, z_mla, pos_row, freq_col, place_cos, place_sin, base, q_norm, wq_p, wq_sw, kv_norm, wkv_p)


def _mix_out_kernel(x_ref, of_ref, ob_ref, bf_ref, bb_ref, gate_ref, lnw_ref, lnb_ref, segm_ref, ym_ref,
                    wr_ref, wm_ref, g_ref, o_ref):
    segm = segm_ref[...]

    o = of_ref[0] + ob_ref[0]
    o_hi, o_lo = _split2(o)
    oc = o - (_dot(o_hi, segm) + _dot(o_lo, segm))
    var = _dot((oc * oc).astype(BF16), segm)
    o = oc * lax.rsqrt(var + LN_X_EPS) * lnw_ref[...] + lnb_ref[...]
    y_rwkv = (o + (bf_ref[0].astype(F32) + bb_ref[0].astype(F32))) * gate_ref[0].astype(F32)
    y = (jnp.dot(y_rwkv.astype(BF16), wr_ref[...], preferred_element_type=F32)
         + jnp.dot(ym_ref[0].astype(BF16), wm_ref[...], preferred_element_type=F32))
    o_ref[0] = x_ref[0] + _rms(y, g_ref[...])


def _mix_out(x, o_f, o_b, bonus_f, bonus_b, gate, lnw, lnb, segm, y_mla, w_r, w_m, g, tm):
    B, S, D = x.shape
    row_spec = lambda cols: pl.BlockSpec((1, tm, cols), lambda b, i: (b, i, 0))
    C = RWKV_WIDTH
    return pl.pallas_call(
        _mix_out_kernel,
        grid=(B, S // tm),
        in_specs=[row_spec(D), row_spec(C), row_spec(C), row_spec(C), row_spec(C), row_spec(C),
                  _const_spec(lnw.shape), _const_spec(lnb.shape), _const_spec(segm.shape), row_spec(MLA_WIDTH),
                  _const_spec(w_r.shape), _const_spec(w_m.shape), _const_spec(g.shape)],
        out_specs=row_spec(D),
        out_shape=jax.ShapeDtypeStruct((B, S, D), F32),
        compiler_params=_params("parallel", "parallel"),
        name="mix_out",
    )(x, o_f, o_b, bonus_f, bonus_b, gate, lnw, lnb, segm, y_mla, w_r, w_m, g)


def _mem_kv_kernel(mem_ref, g_ref, w_ref, k_ref, v_ref):
    m = _rms(mem_ref[0], g_ref[...]).astype(BF16)
    kv = jnp.dot(m, w_ref[...], preferred_element_type=F32)
    k_ref[0] = kv[:, :D_MODEL].astype(BF16)
    v_ref[0] = kv[:, D_MODEL:].astype(BF16)


def _mem_kv(mem, g, wkv):
    B, T, D = mem.shape
    return pl.pallas_call(
        _mem_kv_kernel,
        grid=(B,),
        in_specs=[pl.BlockSpec((1, T, D), lambda b: (b, 0, 0)), _const_spec(g.shape), _const_spec(wkv.shape)],
        out_specs=[pl.BlockSpec((1, T, D), lambda b: (b, 0, 0)), pl.BlockSpec((1, T, D), lambda b: (b, 0, 0))],
        out_shape=[jax.ShapeDtypeStruct((B, T, D), BF16), jax.ShapeDtypeStruct((B, T, D), BF16)],
        compiler_params=_params("parallel"),
        name="mem_kv",
    )(mem, g, wkv)


def _mem_attn_kernel(x_ref, k_ref, v_ref, gpre_ref, wq_ref, wo_ref, gpost_ref, o_ref, att_s):
    x = x_ref[0]
    h = _rms(x, gpre_ref[...]).astype(BF16)
    q = (jnp.dot(h, wq_ref[...], preferred_element_type=F32) * (MEM_HEAD_DIM ** -0.5 * LOG2_E)).astype(BF16)

    def scores(hd):
        sl = slice(hd * MEM_HEAD_DIM, (hd + 1) * MEM_HEAD_DIM)
        return lax.dot_general(q[:, sl], k_ref[0, :, sl], (((1,), (1,)), ((), ())), preferred_element_type=F32)

    s = scores(0)
    for hd in range(MEM_HEADS):
        sl = slice(hd * MEM_HEAD_DIM, (hd + 1) * MEM_HEAD_DIM)
        s_next = scores(hd + 1) if hd + 1 < MEM_HEADS else None
        p = jnp.exp2(s - jnp.max(s, axis=-1, keepdims=True))
        denom = jnp.sum(p, axis=-1, keepdims=True)
        o = jnp.dot(p.astype(BF16), v_ref[0, :, sl], preferred_element_type=F32)
        att_s[:, sl] = (o / denom).astype(BF16)
        s = s_next
    y = jnp.dot(att_s[...], wo_ref[...], preferred_element_type=F32)
    o_ref[0] = x + _rms(y, gpost_ref[...])


def _mem_attn(x, k, v, g_pre, wq, wo, g_post, tm):
    B, S, D = x.shape
    T = k.shape[1]
    row_spec = pl.BlockSpec((1, tm, D), lambda b, i: (b, i, 0))
    kv_spec = pl.BlockSpec((1, T, D), lambda b, i: (b, 0, 0))
    return pl.pallas_call(
        _mem_attn_kernel,
        grid=(B, S // tm),
        in_specs=[row_spec, kv_spec, kv_spec, _const_spec(g_pre.shape), _const_spec(wq.shape),
                  _const_spec(wo.shape), _const_spec(g_post.shape)],
        out_specs=row_spec,
        out_shape=jax.ShapeDtypeStruct((B, S, D), F32),
        scratch_shapes=[pltpu.VMEM((tm, D), BF16)],
        compiler_params=_params("parallel", "parallel"),
        name="mem_attn",
    )(x, k, v, g_pre, wq, wo, g_post)


def _mlp_kernel(x_ref, gpre_ref, w1_ref, w2_ref, gpost_ref, o_ref):
    x = x_ref[0]
    h = _rms(x, gpre_ref[...]).astype(BF16)
    u = jnp.maximum(jnp.dot(h, w1_ref[...], preferred_element_type=F32), 0.0)
    y = jnp.dot((u * u).astype(BF16), w2_ref[...], preferred_element_type=F32)
    o_ref[0] = x + _rms(y, gpost_ref[...])


def _mlp(x, g_pre, w1, w2, g_post, tm):
    B, S, D = x.shape
    row_spec = pl.BlockSpec((1, tm, D), lambda b, i: (b, i, 0))
    return pl.pallas_call(
        _mlp_kernel,
        grid=(B, S // tm),
        in_specs=[row_spec, _const_spec(g_pre.shape), _const_spec(w1.shape), _const_spec(w2.shape),
                  _const_spec(g_post.shape)],
        out_specs=row_spec,
        out_shape=jax.ShapeDtypeStruct((B, S, D), F32),
        compiler_params=_params("parallel", "parallel"),
        name="mlp",
    )(x, g_pre, w1, w2, g_post)


def _pad_cols(w, n):
    return jnp.pad(w, ((0, 0), (0, n - w.shape[1])))


def _rope_tile(w):
    return jnp.pad(w, ((0, 0), (QK_NOPE_DIM, LANES - QK_NOPE_DIM - QK_ROPE_DIM)))


def _swap_halves(w):
    half = QK_ROPE_DIM // 2
    return jnp.concatenate([w[:, half:], w[:, :half]], axis=1)


def _lora_rows(w, lo, rows):
    return jnp.pad(w, ((lo, LORA_TILE - lo - rows), (0, 0)))


def kernel(x, mem, positions, norm_mix_pre, w_in, conv_rwkv, rwkv_w0, rwkv_w2, rwkv_a0, rwkv_a2, rwkv_g2, rwkv_k_k, rwkv_k_a, rwkv_r_k, rwkv_lnx_w, rwkv_lnx_b, mla_q_norm, mla_w_uq, mla_kv_norm, mla_w_ukv, w_out, norm_mix_post, norm_mem_pre, norm_memtok, mem_wq, mem_wkv, mem_wo, norm_mem_post, norm_mlp_pre, mlp_w1, mlp_w2, norm_mlp_post):
    depth = w_in.shape[0]
    C = RWKV_WIDTH
    head_of = jnp.arange(C) // HEAD_DIM
    seg = (head_of[:, None] == head_of[None, :]).astype(BF16)
    seg_mean = (seg.astype(F32) / HEAD_DIM).astype(BF16)
    inv_freq = ROPE_THETA ** (-jnp.arange(0, QK_ROPE_DIM, 2, dtype=F32) / QK_ROPE_DIM)
    half = QK_ROPE_DIM // 2
    lane = jnp.arange(LANES)[None, :]
    f_idx = jnp.arange(half)[:, None]
    first, second = lane == QK_NOPE_DIM + f_idx, lane == QK_NOPE_DIM + half + f_idx
    place_cos = (first | second).astype(BF16)
    place_sin = (second.astype(F32) - first.astype(F32)).astype(BF16)
    rope_base = (lane < QK_NOPE_DIM).astype(F32)
    freq_col = inv_freq[:, None]
    pos_row = positions[:, None, :]
    step = jnp.arange(CHUNK)
    tri = jnp.stack([step[None, :] <= step[:, None], step[None, :] >= step[:, None]]).astype(F32)
    row = lambda t: t.reshape(1, -1)

    for l in range(depth):
        w = w_in[l]
        mla0 = RWKV_COLS
        w_rope = w[:, mla0 + Q_LORA_RANK + KV_LORA_RANK:]
        w_r = _pad_cols(w[:, :RWKV_COLS], RWKV_TILE_COLS).astype(BF16)
        w_m = jnp.concatenate([w[:, mla0:mla0 + Q_LORA_RANK + KV_LORA_RANK], _rope_tile(w_rope),
                               _rope_tile(_swap_halves(w_rope))], axis=1).astype(BF16)
        conv_p = _pad_cols(conv_rwkv[l], RWKV_TILE_COLS)
        z_rwkv, z_mla = _in_proj(x, row(norm_mix_pre[l]), w_r, w_m, conv_p, tm=512)

        w2_p = jnp.stack([_lora_rows(rwkv_w2[l, d], d * DECAY_LORA, DECAY_LORA) for d in range(2)])
        a_lo = 2 * DECAY_LORA
        a2_p = jnp.stack([_lora_rows(rwkv_a2[l, d], a_lo + d * ICLR_LORA, ICLR_LORA) for d in range(2)])
        g2_p = _lora_rows(rwkv_g2[l], a_lo + 2 * ICLR_LORA, GATE_LORA)
        o_f, o_b, bonus_f, bonus_b, gate = _rwkv(z_rwkv, rwkv_w0[l], w2_p, rwkv_a0[l], a2_p, g2_p,
                                                 row(rwkv_k_k[l]), row(rwkv_k_a[l]), row(rwkv_r_k[l]), seg, tri, tm=256)

        qk = QK_NOPE_DIM + QK_ROPE_DIM
        w_uq = mla_w_uq[l].reshape(Q_LORA_RANK, MLA_HEADS, qk)
        wq_p = jnp.pad(w_uq, ((0, 0), (0, 0), (0, LANES - qk))).reshape(Q_LORA_RANK, MLA_HEADS * LANES)
        uq_rope = w_uq[:, :, QK_NOPE_DIM:]
        uq_sw = jnp.concatenate([uq_rope[:, :, half:], uq_rope[:, :, :half]], axis=2)
        wq_sw = jnp.pad(uq_sw, ((0, 0), (0, 0), (QK_NOPE_DIM, LANES - qk))).reshape(Q_LORA_RANK, MLA_HEADS * LANES)
        w_ukv = mla_w_ukv[l].reshape(KV_LORA_RANK, MLA_HEADS, QK_NOPE_DIM + V_HEAD_DIM)
        wk_p = jnp.pad(w_ukv[:, :, :QK_NOPE_DIM], ((0, 0), (0, 0), (0, LANES - QK_NOPE_DIM)))
        wv_p = jnp.pad(w_ukv[:, :, QK_NOPE_DIM:], ((0, 0), (0, 0), (0, LANES - V_HEAD_DIM)))
        wkv_p = jnp.concatenate([wk_p.reshape(KV_LORA_RANK, MLA_HEADS * LANES),
                                 wv_p.reshape(KV_LORA_RANK, MLA_HEADS * LANES)], axis=1)
        y_mla = _mla_attn(z_mla, pos_row, freq_col, place_cos, place_sin, rope_base, row(mla_q_norm[l]), wq_p.astype(BF16), wq_sw.astype(BF16),
                          row(mla_kv_norm[l]), wkv_p.astype(BF16), tq=512)

        wo = w_out[l].astype(BF16)
        x = _mix_out(x, o_f, o_b, bonus_f, bonus_b, gate, row(rwkv_lnx_w[l]), row(rwkv_lnx_b[l]), seg_mean,
                     y_mla, wo[:C], wo[C:], row(norm_mix_post[l]), tm=512)

        k_mem, v_mem = _mem_kv(mem, row(norm_memtok[l]), mem_wkv[l].astype(BF16))
        x = _mem_attn(x, k_mem, v_mem, row(norm_mem_pre[l]), mem_wq[l].astype(BF16), mem_wo[l].astype(BF16),
                      row(norm_mem_post[l]), tm=512)

        x = _mlp(x, row(norm_mlp_pre[l]), mlp_w1[l].astype(BF16), mlp_w2[l].astype(BF16),
                 row(norm_mlp_post[l]), tm=512)
    return x
```

```python
import math

import jax
import jax.numpy as jnp
from jax import lax
from jax.experimental import pallas as pl
from jax.experimental.pallas import tpu as pltpu

F32 = jnp.float32
BF16 = jnp.bfloat16

D_MODEL = 1024
NORM_EPS = 1e-6

RWKV_HEADS = 8
HEAD_DIM = 64
RWKV_WIDTH = RWKV_HEADS * HEAD_DIM
DECAY_LORA = 32
ICLR_LORA = 32
GATE_LORA = 96
LORA_COLS = 2 * DECAY_LORA + 2 * ICLR_LORA + GATE_LORA
LORA_TILE = 256
RWKV_COLS = 3 * RWKV_WIDTH + LORA_COLS
RWKV_TILE_COLS = 3 * RWKV_WIDTH + LORA_TILE
LN_X_EPS = 64e-5
CHUNK = 64
CHUNK_GROUP = 2
DECAY_SCALE = math.exp(-0.5)

MLA_HEADS = 8
QK_NOPE_DIM = 64
QK_ROPE_DIM = 32
V_HEAD_DIM = 64
MLA_WIDTH = MLA_HEADS * V_HEAD_DIM
Q_LORA_RANK = 256
KV_LORA_RANK = 128
ROPE_THETA = 10000.0
LOG2_E = math.log2(math.e)
LANES = 128
MLA_TILE_COLS = Q_LORA_RANK + KV_LORA_RANK + 2 * LANES

MEM_HEADS = 4
MEM_HEAD_DIM = D_MODEL // MEM_HEADS
D_FF = 4 * D_MODEL

VMEM_LIMIT = 56 * 1024 * 1024


def _mm(a, b):
    return jnp.dot(a.astype(BF16), b.astype(BF16), preferred_element_type=F32)


def _split2(t):
    hi = t.astype(BF16)
    lo = (t - hi.astype(F32)).astype(BF16)
    return hi, lo


def _split3(t):
    hi = t.astype(BF16)
    rest = t - hi.astype(F32)
    mid = rest.astype(BF16)
    lo = (rest - mid.astype(F32)).astype(BF16)
    return hi, mid, lo


def _dot(a, b):
    return jnp.dot(a, b, preferred_element_type=F32)


def _mm_x3(a, b):
    a_hi, a_lo = _split2(a)
    b_hi, b_lo = _split2(b)
    return _dot(a_hi, b_hi) + (_dot(a_hi, b_lo) + _dot(a_lo, b_hi))


def _mm_exact_lhs(a, b):
    a = a.astype(BF16)
    hi, lo = _split2(b)
    return _dot(a, hi) + _dot(a, lo)


def _mm_nt(a, b):
    return lax.dot_general(a.astype(BF16), b.astype(BF16), (((1,), (1,)), ((), ())),
                           preferred_element_type=F32)


def _mm_tn(a, b):
    return lax.dot_general(a.astype(BF16), b.astype(BF16), (((0,), (0,)), ((), ())),
                           preferred_element_type=F32)


def _rms(x, g, eps=NORM_EPS):
    return x * lax.rsqrt(jnp.mean(x * x, axis=-1, keepdims=True) + eps) * g


def _sigmoid(x):
    return 0.5 * jnp.tanh(0.5 * x) + 0.5


def _params(*sem):
    return pltpu.CompilerParams(dimension_semantics=sem, vmem_limit_bytes=VMEM_LIMIT)


def _const_spec(shape):
    nd = len(shape)
    return pl.BlockSpec(shape, lambda *_: (0,) * nd)


def _in_proj_kernel(x_ref, xp_ref, xn_ref, g_ref, wr_ref, wm_ref, conv_ref, zr_ref, zm_ref):
    i = pl.program_id(1)
    last = pl.num_programs(1) - 1
    tm = x_ref.shape[1]
    g = g_ref[...]
    h = _rms(x_ref[0], g).astype(BF16)
    zm_ref[0] = jnp.dot(h, wm_ref[...], preferred_element_type=F32)
    z = jnp.dot(h, wr_ref[...], preferred_element_type=F32)
    halo = _rms(jnp.concatenate([xp_ref[0], xn_ref[0]], axis=0), g).astype(BF16)
    z_halo = jnp.dot(halo, wr_ref[...], preferred_element_type=F32)
    prev_row = jnp.where(i == 0, 0.0, z_halo[7:8, :])
    next_row = jnp.where(i == last, 0.0, z_halo[8:9, :])
    row = lax.broadcasted_iota(jnp.int32, (tm, 1), 0)
    z_dn = jnp.where(row == 0, prev_row, pltpu.roll(z, 1, 0))
    z_up = jnp.where(row == tm - 1, next_row, pltpu.roll(z, tm - 1, 0))
    zr_ref[0] = conv_ref[0:1, :] * z_dn + conv_ref[1:2, :] * z + conv_ref[2:3, :] * z_up


def _in_proj(x, g, w_r, w_m, conv_p, tm):
    B, S, D = x.shape
    halo = tm // 8
    n_halo = S // 8
    return pl.pallas_call(
        _in_proj_kernel,
        grid=(B, S // tm),
        in_specs=[pl.BlockSpec((1, tm, D), lambda b, i: (b, i, 0)),
                  pl.BlockSpec((1, 8, D), lambda b, i: (b, jnp.maximum(i * halo - 1, 0), 0)),
                  pl.BlockSpec((1, 8, D), lambda b, i: (b, jnp.minimum((i + 1) * halo, n_halo - 1), 0)),
                  _const_spec((1, D)),
                  _const_spec(w_r.shape),
                  _const_spec(w_m.shape),
                  _const_spec(conv_p.shape)],
        out_specs=[pl.BlockSpec((1, tm, RWKV_TILE_COLS), lambda b, i: (b, i, 0)),
                   pl.BlockSpec((1, tm, MLA_TILE_COLS), lambda b, i: (b, i, 0))],
        out_shape=[jax.ShapeDtypeStruct((B, S, RWKV_TILE_COLS), F32),
                   jax.ShapeDtypeStruct((B, S, MLA_TILE_COLS), F32)],
        compiler_params=_params("parallel", "parallel"),
        name="in_proj",
    )(x, x, x, g, w_r, w_m, conv_p)


def _chunk_units(units, eye, eye_hi):
    L, N = units[0][0].shape
    zeros = jnp.zeros((L, N), BF16)
    AA = [_mm_nt(jnp.concatenate([At, Rt], axis=0), jnp.concatenate([Bt, Kt], axis=0))
          for (At, Rt, Bt, Kt, *_) in units]
    A_a = [jnp.where(u[8], aa[:L, :], 0.0) for u, aa in zip(units, AA)]
    A_r = [jnp.where(u[9], aa[L:, :], 0.0) for u, aa in zip(units, AA)]
    AkV = [_mm(a.astype(BF16)[:, L:], u[6]) for u, a in zip(units, A_a)]
    low = lax.broadcasted_iota(jnp.int32, (L, 2 * L), 1) < L
    S = [jnp.where(low, a, eye_hi) for a in A_a]
    span = 1
    while span < L:
        R = [_mm(s[:, :L], s) for s in S]
        S = [jnp.where(low, r, r + s) for r, s in zip(R, S)]
        span *= 2
    W = [_mm(s.astype(BF16)[:, L:], jnp.concatenate([u[0], akv.astype(BF16)], axis=1))
         for s, u, akv in zip(S, units, AkV)]
    Z = [jnp.concatenate([w.astype(BF16), jnp.concatenate([zeros, u[6]], axis=1)], axis=0)
         for u, w in zip(units, W)]
    lhs = [jnp.concatenate([a_r.astype(BF16), jnp.concatenate([u[4], u[5]], axis=0).T], axis=0)
           for u, a_r in zip(units, A_r)]
    QGH = [_mm(l, z) for l, z in zip(lhs, Z)]
    Q = [jnp.concatenate([u[1], zeros], axis=1) + t[:L] for u, t in zip(units, QGH)]
    GH = [jnp.concatenate([eye * u[7], zeros.astype(F32)], axis=1) + t[L:] for u, t in zip(units, QGH)]
    return Q, GH


def _rwkv_kernel(zf_ref, zb_ref,
                 w0_ref, w2_ref, a0_ref, a2_ref, g2_ref, kk_ref, ka_ref, rk_ref, seg_ref, tri_ref,
                 of_ref, ob_ref, bonus_f_ref, bonus_b_ref, g_ref,
                 r_s, v_s, kk_s, b_s, kd_s, lw_s, state_s):
    i = pl.program_id(1)
    n_tiles = pl.num_programs(1)
    tm = zf_ref.shape[1]
    L, N = CHUNK, HEAD_DIM
    n_chunks = tm // L

    @pl.when(i == 0)
    def _():
        state_s[...] = jnp.zeros_like(state_s)

    seg = seg_ref[...]

    def seg_sum(t):
        t = t.astype(BF16)
        half = seg.shape[0]
        return jnp.concatenate([_dot(t[:, :half], seg), _dot(t[:, half:], seg)], axis=1)

    for d, (z_ref, bonus_ref) in enumerate(((zf_ref, bonus_f_ref), (zb_ref, bonus_b_ref))):
        r = z_ref[0, :, 0:RWKV_WIDTH]
        k = z_ref[0, :, RWKV_WIDTH:2 * RWKV_WIDTH]
        v = z_ref[0, :, 2 * RWKV_WIDTH:3 * RWKV_WIDTH]
        lora = z_ref[0, :, 3 * RWKV_WIDTH:]
        kk = k * kk_ref[...]
        kk = kk * lax.rsqrt(jnp.maximum(seg_sum(kk * kk), 1e-24))
        w_pre = w0_ref[d:d + 1, :] + _mm(jnp.tanh(lora), w2_ref[d])
        lw_s[d] = -DECAY_SCALE * _sigmoid(w_pre)
        alpha = _sigmoid(a0_ref[d:d + 1, :] + _mm(lora, a2_ref[d]))
        kd = k * (1.0 + (alpha - 1.0) * ka_ref[...])
        r_s[d] = r
        v_s[d] = v
        kk_s[d] = kk
        kd_s[d] = kd
        b_s[d] = kk * alpha
        bonus_ref[0] = (seg_sum(r * kd * rk_ref[...]) * v).astype(bonus_ref.dtype)
        if d == 0:
            g_ref[0] = _mm(_sigmoid(lora), g2_ref[...]).astype(g_ref.dtype)

    ri = lax.broadcasted_iota(jnp.int32, (L, L), 0)
    ci = lax.broadcasted_iota(jnp.int32, (L, L), 1)
    eye = (ci == ri).astype(F32)
    ri2 = lax.broadcasted_iota(jnp.int32, (L, 2 * L), 0)
    ci2 = lax.broadcasted_iota(jnp.int32, (L, 2 * L), 1)
    eye_hi = (ci2 == ri2 + L).astype(F32)
    ci2 = jnp.where(ci2 >= L, ci2 - L, ci2)
    m_strict = (ci2 < ri2, ci2 > ri2)
    m_incl = (ci2 <= ri2, ci2 >= ri2)
    out_refs = (of_ref, ob_ref)

    def chunk_units(c):
        rows_d = (pl.ds(pl.multiple_of(c * L, L), L), pl.ds(pl.multiple_of((n_chunks - 1 - c) * L, L), L))
        units = []
        for d in range(2):
            rows = rows_d[d]
            lw = lw_s[d, rows, :]
            bc = b_s[d, rows, :]
            kdc = kd_s[d, rows, :]
            vc = v_s[d, rows, :]
            cum = _mm_exact_lhs(tri_ref[d], lw)
            tot = cum[L - 1:L, :] if d == 0 else cum[0:1, :]
            e_in = jnp.exp(-cum)
            e_rem = jnp.exp(tot - cum)
            e_tot = jnp.exp(tot)
            At = (-kk_s[d, rows, :] * jnp.exp(cum - lw)).astype(BF16)
            Rt = (r_s[d, rows, :] * jnp.exp(cum)).astype(BF16)
            Bt = (bc * e_in).astype(BF16)
            Kt = (kdc * e_in).astype(BF16)
            Bh = (bc * e_rem).astype(BF16)
            Kh = (kdc * e_rem).astype(BF16)
            vc = vc.astype(BF16)
            for h in range(RWKV_HEADS):
                sl = slice(h * HEAD_DIM, (h + 1) * HEAD_DIM)
                units.append((At[:, sl], Rt[:, sl], Bt[:, sl], Kt[:, sl], Bh[:, sl], Kh[:, sl],
                              vc[:, sl], e_tot[:, sl], m_strict[d], m_incl[d]))
        return rows_d, units

    n_units = 2 * RWKV_HEADS

    def group_body(j, carry):
        rows, units = [], []
        for g in range(CHUNK_GROUP):
            rows_d, chunk = chunk_units(j * CHUNK_GROUP + g)
            rows.append(rows_d)
            units.extend(chunk)
        Q, GH = _chunk_units(units, eye, eye_hi)
        M = [state_s[u] for u in range(n_units)]
        for g in range(CHUNK_GROUP):
            Qg, GHg = Q[g * n_units:(g + 1) * n_units], GH[g * n_units:(g + 1) * n_units]
            M_split = [_split2(m) for m in M]
            G_split = [_split2(gh[:, :N]) for gh in GHg]
            prod = [_dot(jnp.concatenate([q[:, :N].astype(BF16), g_hi, g_lo], axis=0), m_hi)
                    for q, (g_hi, g_lo), (m_hi, _) in zip(Qg, G_split, M_split)]
            out = [p[:L] + q[:, N:] for p, q in zip(prod, Qg)]
            M = [p[L:L + N] + (p[L + N:] + _dot(g_hi, m_lo)) + gh[:, N:]
                 for p, (g_hi, _), (_, m_lo), gh in zip(prod, G_split, M_split, GHg)]
            for d in range(2):
                for h in range(RWKV_HEADS):
                    out_refs[d][0, rows[g][d], h * N:(h + 1) * N] = out[d * RWKV_HEADS + h]
        for u in range(n_units):
            state_s[u] = M[u]
        return carry

    lax.fori_loop(0, n_chunks // CHUNK_GROUP, group_body, 0)


def _rwkv(z_rwkv, w0, w2_p, a0, a2_p, g2_p, k_k, k_a, r_k, seg, tri, tm):
    B, S, ZC = z_rwkv.shape
    C = RWKV_WIDTH
    n_tiles = S // tm
    fwd = lambda b, i: (b, i, 0)
    bwd = lambda b, i: (b, n_tiles - 1 - i, 0)
    out_f = pl.BlockSpec((1, tm, C), fwd)
    out_b = pl.BlockSpec((1, tm, C), bwd)
    out_sds = lambda dtype: jax.ShapeDtypeStruct((B, S, C), dtype)
    return pl.pallas_call(
        _rwkv_kernel,
        grid=(B, n_tiles),
        in_specs=[pl.BlockSpec((1, tm, ZC), fwd),
                  pl.BlockSpec((1, tm, ZC), bwd),
                  _const_spec(w0.shape), _const_spec(w2_p.shape),
                  _const_spec(a0.shape), _const_spec(a2_p.shape), _const_spec(g2_p.shape),
                  _const_spec(k_k.shape), _const_spec(k_a.shape), _const_spec(r_k.shape),
                  _const_spec(seg.shape), _const_spec(tri.shape)],
        out_specs=[out_f, out_b, out_f, out_b, out_f],
        out_shape=[out_sds(F32), out_sds(F32), out_sds(BF16), out_sds(BF16), out_sds(BF16)],
        scratch_shapes=[pltpu.VMEM((2, tm, C), F32)] * 6
                       + [pltpu.VMEM((2 * RWKV_HEADS, HEAD_DIM, HEAD_DIM), F32)],
        compiler_params=_params("parallel", "arbitrary"),
        name="rwkv",
    )(z_rwkv, z_rwkv, w0, w2_p, a0, a2_p, g2_p, k_k, k_a, r_k, seg, tri)


def _rope_tiles(pos_row, freq_col, place_cos, place_sin, base):
    ang = freq_col * pos_row.astype(F32)

    def place(t, p):
        return sum(lax.dot_general(part, p, (((0,), (0,)), ((), ())), preferred_element_type=F32)
                   for part in _split3(t))

    return place(jnp.cos(ang), place_cos) + base, place(jnp.sin(ang), place_sin)


def _mla_kernel(zq_ref, zkv_ref, pos_ref, freq_ref, pcos_ref, psin_ref, base_ref, qn_ref, wq_ref, wqs_ref, kvn_ref, wkv_ref,
                y_ref, k_s, v_s, cos_s, sin_s):
    i = pl.program_id(1)
    tq = zq_ref.shape[1]
    scale = (QK_NOPE_DIM + QK_ROPE_DIM) ** -0.5
    c_kv_lo, c_kv_hi = Q_LORA_RANK, Q_LORA_RANK + KV_LORA_RANK

    @pl.when(i == 0)
    def _():
        zkv = zkv_ref[0]
        kvn = _rms(zkv[:, c_kv_lo:c_kv_hi], kvn_ref[...]).astype(BF16)
        kvu = jnp.dot(kvn, wkv_ref[...], preferred_element_type=F32)
        cos_t, sin_t = _rope_tiles(pos_ref[0], freq_ref[...], pcos_ref[...], psin_ref[...], base_ref[...])
        cos_s[...] = cos_t
        sin_s[...] = sin_t
        k_rot = zkv[:, c_kv_hi:c_kv_hi + LANES] * cos_t + zkv[:, c_kv_hi + LANES:] * sin_t
        lane = lax.broadcasted_iota(jnp.int32, (1, LANES), 1)
        ones_col = (lane == V_HEAD_DIM).astype(F32)
        for h in range(MLA_HEADS):
            k_s[h] = (kvu[:, h * LANES:(h + 1) * LANES] + k_rot).astype(BF16)
            v_s[h] = (kvu[:, (MLA_HEADS + h) * LANES:(MLA_HEADS + h + 1) * LANES] + ones_col).astype(BF16)

    zq = zq_ref[0]
    qn = _rms(zq[:, :Q_LORA_RANK], qn_ref[...]).astype(BF16)
    q_a = jnp.dot(qn, wq_ref[...], preferred_element_type=F32)
    q_b = jnp.dot(qn, wqs_ref[...], preferred_element_type=F32)
    rows = pl.ds(pl.multiple_of(i * tq, tq), tq)
    cos_q = cos_s[rows, :] * (scale * LOG2_E)
    sin_q = sin_s[rows, :] * (scale * LOG2_E)

    def scores(h):
        sl = slice(h * LANES, (h + 1) * LANES)
        qh = (q_a[:, sl] * cos_q + q_b[:, sl] * sin_q).astype(BF16)
        return lax.dot_general(qh, k_s[h], (((1,), (1,)), ((), ())), preferred_element_type=F32)

    s = scores(0)
    for h in range(MLA_HEADS):
        s_next = scores(h + 1) if h + 1 < MLA_HEADS else None
        p = jnp.exp2((s - jnp.max(s, axis=-1, keepdims=True)).astype(BF16))
        o = jnp.dot(p, v_s[h], preferred_element_type=F32)
        y_ref[0, :, h * V_HEAD_DIM:(h + 1) * V_HEAD_DIM] = (
            o[:, :V_HEAD_DIM] / o[:, V_HEAD_DIM:V_HEAD_DIM + 1]).astype(y_ref.dtype)
        s = s_next


def _mla_attn(z_mla, pos_row, freq_col, place_cos, place_sin, base, q_norm, wq_p, wq_sw, kv_norm, wkv_p, tq):
    B, S, ZC = z_mla.shape
    return pl.pallas_call(
        _mla_kernel,
        grid=(B, S // tq),
        in_specs=[pl.BlockSpec((1, tq, ZC), lambda b, i: (b, i, 0)),
                  pl.BlockSpec((1, S, ZC), lambda b, i: (b, 0, 0)),
                  pl.BlockSpec((1, 1, S), lambda b, i: (b, 0, 0)),
                  _const_spec(freq_col.shape), _const_spec(place_cos.shape), _const_spec(place_sin.shape),
                  _const_spec(base.shape), _const_spec(q_norm.shape),
                  _const_spec(wq_p.shape), _const_spec(wq_sw.shape), _const_spec(kv_norm.shape),
                  _const_spec(wkv_p.shape)],
        out_specs=pl.BlockSpec((1, tq, MLA_WIDTH), lambda b, i: (b, i, 0)),
        out_shape=jax.ShapeDtypeStruct((B, S, MLA_WIDTH), BF16),
        scratch_shapes=[pltpu.VMEM((MLA_HEADS, S, LANES), BF16), pltpu.VMEM((MLA_HEADS, S, LANES), BF16),
                        pltpu.VMEM((S, LANES), F32), pltpu.VMEM((S, LANES), F32)],
        compiler_params=_params("parallel", "arbitrary"),
        name="mla_attn",
    )(z_mla, z_mla, pos_row, freq_col, place_cos, place_sin, base, q_norm, wq_p, wq_sw, kv_norm, wkv_p)


def _mix_out_kernel(x_ref, of_ref, ob_ref, bf_ref, bb_ref, gate_ref, lnw_ref, lnb_ref, segm_ref, ym_ref,
                    wr_ref, wm_ref, g_ref, o_ref):
    segm = segm_ref[...]

    o = of_ref[0] + ob_ref[0]
    o_hi, o_lo = _split2(o)
    oc = o - (_dot(o_hi, segm) + _dot(o_lo, segm))
    var = _dot((oc * oc).astype(BF16), segm)
    o = oc * lax.rsqrt(var + LN_X_EPS) * lnw_ref[...] + lnb_ref[...]
    y_rwkv = (o + (bf_ref[0].astype(F32) + bb_ref[0].astype(F32))) * gate_ref[0].astype(F32)
    y = (jnp.dot(y_rwkv.astype(BF16), wr_ref[...], preferred_element_type=F32)
         + jnp.dot(ym_ref[0].astype(BF16), wm_ref[...], preferred_element_type=F32))
    o_ref[0] = x_ref[0] + _rms(y, g_ref[...])


def _mix_out(x, o_f, o_b, bonus_f, bonus_b, gate, lnw, lnb, segm, y_mla, w_r, w_m, g, tm):
    B, S, D = x.shape
    row_spec = lambda cols: pl.BlockSpec((1, tm, cols), lambda b, i: (b, i, 0))
    C = RWKV_WIDTH
    return pl.pallas_call(
        _mix_out_kernel,
        grid=(B, S // tm),
        in_specs=[row_spec(D), row_spec(C), row_spec(C), row_spec(C), row_spec(C), row_spec(C),
                  _const_spec(lnw.shape), _const_spec(lnb.shape), _const_spec(segm.shape), row_spec(MLA_WIDTH),
                  _const_spec(w_r.shape), _const_spec(w_m.shape), _const_spec(g.shape)],
        out_specs=row_spec(D),
        out_shape=jax.ShapeDtypeStruct((B, S, D), F32),
        compiler_params=_params("parallel", "parallel"),
        name="mix_out",
    )(x, o_f, o_b, bonus_f, bonus_b, gate, lnw, lnb, segm, y_mla, w_r, w_m, g)


def _mem_kv_kernel(mem_ref, g_ref, w_ref, k_ref, v_ref):
    m = _rms(mem_ref[0], g_ref[...]).astype(BF16)
    kv = jnp.dot(m, w_ref[...], preferred_element_type=F32)
    k_ref[0] = kv[:, :D_MODEL].astype(BF16)
    v_ref[0] = kv[:, D_MODEL:].astype(BF16)


def _mem_kv(mem, g, wkv):
    B, T, D = mem.shape
    return pl.pallas_call(
        _mem_kv_kernel,
        grid=(B,),
        in_specs=[pl.BlockSpec((1, T, D), lambda b: (b, 0, 0)), _const_spec(g.shape), _const_spec(wkv.shape)],
        out_specs=[pl.BlockSpec((1, T, D), lambda b: (b, 0, 0)), pl.BlockSpec((1, T, D), lambda b: (b, 0, 0))],
        out_shape=[jax.ShapeDtypeStruct((B, T, D), BF16), jax.ShapeDtypeStruct((B, T, D), BF16)],
        compiler_params=_params("parallel"),
        name="mem_kv",
    )(mem, g, wkv)


def _mem_attn_kernel(x_ref, k_ref, v_ref, gpre_ref, wq_ref, wo_ref, gpost_ref, o_ref, att_s):
    x = x_ref[0]
    h = _rms(x, gpre_ref[...]).astype(BF16)
    q = (jnp.dot(h, wq_ref[...], preferred_element_type=F32) * (MEM_HEAD_DIM ** -0.5 * LOG2_E)).astype(BF16)

    def scores(hd):
        sl = slice(hd * MEM_HEAD_DIM, (hd + 1) * MEM_HEAD_DIM)
        return lax.dot_general(q[:, sl], k_ref[0, :, sl], (((1,), (1,)), ((), ())), preferred_element_type=F32)

    s = scores(0)
    for hd in range(MEM_HEADS):
        sl = slice(hd * MEM_HEAD_DIM, (hd + 1) * MEM_HEAD_DIM)
        s_next = scores(hd + 1) if hd + 1 < MEM_HEADS else None
        p = jnp.exp2(s - jnp.max(s, axis=-1, keepdims=True))
        denom = jnp.sum(p, axis=-1, keepdims=True)
        o = jnp.dot(p.astype(BF16), v_ref[0, :, sl], preferred_element_type=F32)
        att_s[:, sl] = (o / denom).astype(BF16)
        s = s_next
    y = jnp.dot(att_s[...], wo_ref[...], preferred_element_type=F32)
    o_ref[0] = x + _rms(y, gpost_ref[...])


def _mem_attn(x, k, v, g_pre, wq, wo, g_post, tm):
    B, S, D = x.shape
    T = k.shape[1]
    row_spec = pl.BlockSpec((1, tm, D), lambda b, i: (b, i, 0))
    kv_spec = pl.BlockSpec((1, T, D), lambda b, i: (b, 0, 0))
    return pl.pallas_call(
        _mem_attn_kernel,
        grid=(B, S // tm),
        in_specs=[row_spec, kv_spec, kv_spec, _const_spec(g_pre.shape), _const_spec(wq.shape),
                  _const_spec(wo.shape), _const_spec(g_post.shape)],
        out_specs=row_spec,
        out_shape=jax.ShapeDtypeStruct((B, S, D), F32),
        scratch_shapes=[pltpu.VMEM((tm, D), BF16)],
        compiler_params=_params("parallel", "parallel"),
        name="mem_attn",
    )(x, k, v, g_pre, wq, wo, g_post)


def _mlp_kernel(x_ref, gpre_ref, w1_ref, w2_ref, gpost_ref, o_ref):
    x = x_ref[0]
    h = _rms(x, gpre_ref[...]).astype(BF16)
    u = jnp.maximum(jnp.dot(h, w1_ref[...], preferred_element_type=F32), 0.0)
    y = jnp.dot((u * u).astype(BF16), w2_ref[...], preferred_element_type=F32)
    o_ref[0] = x + _rms(y, gpost_ref[...])


def _mlp(x, g_pre, w1, w2, g_post, tm):
    B, S, D = x.shape
    row_spec = pl.BlockSpec((1, tm, D), lambda b, i: (b, i, 0))
    return pl.pallas_call(
        _mlp_kernel,
        grid=(B, S // tm),
        in_specs=[row_spec, _const_spec(g_pre.shape), _const_spec(w1.shape), _const_spec(w2.shape),
                  _const_spec(g_post.shape)],
        out_specs=row_spec,
        out_shape=jax.ShapeDtypeStruct((B, S, D), F32),
        compiler_params=_params("parallel", "parallel"),
        name="mlp",
    )(x, g_pre, w1, w2, g_post)


def _pad_cols(w, n):
    return jnp.pad(w, ((0, 0), (0, n - w.shape[1])))


def _rope_tile(w):
    return jnp.pad(w, ((0, 0), (QK_NOPE_DIM, LANES - QK_NOPE_DIM - QK_ROPE_DIM)))


def _swap_halves(w):
    half = QK_ROPE_DIM // 2
    return jnp.concatenate([w[:, half:], w[:, :half]], axis=1)


def _lora_rows(w, lo, rows):
    return jnp.pad(w, ((lo, LORA_TILE - lo - rows), (0, 0)))


def kernel(x, mem, positions, norm_mix_pre, w_in, conv_rwkv, rwkv_w0, rwkv_w2, rwkv_a0, rwkv_a2, rwkv_g2, rwkv_k_k, rwkv_k_a, rwkv_r_k, rwkv_lnx_w, rwkv_lnx_b, mla_q_norm, mla_w_uq, mla_kv_norm, mla_w_ukv, w_out, norm_mix_post, norm_mem_pre, norm_memtok, mem_wq, mem_wkv, mem_wo, norm_mem_post, norm_mlp_pre, mlp_w1, mlp_w2, norm_mlp_post):
    depth = w_in.shape[0]
    C = RWKV_WIDTH
    head_of = jnp.arange(C) // HEAD_DIM
    seg = (head_of[:, None] == head_of[None, :]).astype(BF16)
    seg_mean = (seg.astype(F32) / HEAD_DIM).astype(BF16)
    inv_freq = ROPE_THETA ** (-jnp.arange(0, QK_ROPE_DIM, 2, dtype=F32) / QK_ROPE_DIM)
    half = QK_ROPE_DIM // 2
    lane = jnp.arange(LANES)[None, :]
    f_idx = jnp.arange(half)[:, None]
    first, second = lane == QK_NOPE_DIM + f_idx, lane == QK_NOPE_DIM + half + f_idx
    place_cos = (first | second).astype(BF16)
    place_sin = (second.astype(F32) - first.astype(F32)).astype(BF16)
    rope_base = (lane < QK_NOPE_DIM).astype(F32)
    freq_col = inv_freq[:, None]
    pos_row = positions[:, None, :]
    step = jnp.arange(CHUNK)
    tri = jnp.stack([step[None, :] <= step[:, None], step[None, :] >= step[:, None]]).astype(F32)
    row = lambda t: t.reshape(1, -1)

    for l in range(depth):
        w = w_in[l]
        mla0 = RWKV_COLS
        w_rope = w[:, mla0 + Q_LORA_RANK + KV_LORA_RANK:]
        w_r = _pad_cols(w[:, :RWKV_COLS], RWKV_TILE_COLS).astype(BF16)
        w_m = jnp.concatenate([w[:, mla0:mla0 + Q_LORA_RANK + KV_LORA_RANK], _rope_tile(w_rope),
                               _rope_tile(_swap_halves(w_rope))], axis=1).astype(BF16)
        conv_p = _pad_cols(conv_rwkv[l], RWKV_TILE_COLS)
        z_rwkv, z_mla = _in_proj(x, row(norm_mix_pre[l]), w_r, w_m, conv_p, tm=512)

        w2_p = jnp.stack([_lora_rows(rwkv_w2[l, d], d * DECAY_LORA, DECAY_LORA) for d in range(2)])
        a_lo = 2 * DECAY_LORA
        a2_p = jnp.stack([_lora_rows(rwkv_a2[l, d], a_lo + d * ICLR_LORA, ICLR_LORA) for d in range(2)])
        g2_p = _lora_rows(rwkv_g2[l], a_lo + 2 * ICLR_LORA, GATE_LORA)
        o_f, o_b, bonus_f, bonus_b, gate = _rwkv(z_rwkv, rwkv_w0[l], w2_p, rwkv_a0[l], a2_p, g2_p,
                                                 row(rwkv_k_k[l]), row(rwkv_k_a[l]), row(rwkv_r_k[l]),
                                                 seg[:C // 2, :C // 2], tri, tm=512)

        qk = QK_NOPE_DIM + QK_ROPE_DIM
        w_uq = mla_w_uq[l].reshape(Q_LORA_RANK, MLA_HEADS, qk)
        wq_p = jnp.pad(w_uq, ((0, 0), (0, 0), (0, LANES - qk))).reshape(Q_LORA_RANK, MLA_HEADS * LANES)
        uq_rope = w_uq[:, :, QK_NOPE_DIM:]
        uq_sw = jnp.concatenate([uq_rope[:, :, half:], uq_rope[:, :, :half]], axis=2)
        wq_sw = jnp.pad(uq_sw, ((0, 0), (0, 0), (QK_NOPE_DIM, LANES - qk))).reshape(Q_LORA_RANK, MLA_HEADS * LANES)
        w_ukv = mla_w_ukv[l].reshape(KV_LORA_RANK, MLA_HEADS, QK_NOPE_DIM + V_HEAD_DIM)
        wk_p = jnp.pad(w_ukv[:, :, :QK_NOPE_DIM], ((0, 0), (0, 0), (0, LANES - QK_NOPE_DIM)))
        wv_p = jnp.pad(w_ukv[:, :, QK_NOPE_DIM:], ((0, 0), (0, 0), (0, LANES - V_HEAD_DIM)))
        wkv_p = jnp.concatenate([wk_p.reshape(KV_LORA_RANK, MLA_HEADS * LANES),
                                 wv_p.reshape(KV_LORA_RANK, MLA_HEADS * LANES)], axis=1)
        y_mla = _mla_attn(z_mla, pos_row, freq_col, place_cos, place_sin, rope_base, row(mla_q_norm[l]), wq_p.astype(BF16), wq_sw.astype(BF16),
                          row(mla_kv_norm[l]), wkv_p.astype(BF16), tq=512)

        wo = w_out[l].astype(BF16)
        x = _mix_out(x, o_f, o_b, bonus_f, bonus_b, gate, row(rwkv_lnx_w[l]), row(rwkv_lnx_b[l]), seg_mean,
                     y_mla, wo[:C], wo[C:], row(norm_mix_post[l]), tm=512)

        k_mem, v_mem = _mem_kv(mem, row(norm_memtok[l]), mem_wkv[l].astype(BF16))
        x = _mem_attn(x, k_mem, v_mem, row(norm_mem_pre[l]), mem_wq[l].astype(BF16), mem_wo[l].astype(BF16),
                      row(norm_mem_post[l]), tm=512)

        x = _mlp(x, row(norm_mlp_pre[l]), mlp_w1[l].astype(BF16), mlp_w2[l].astype(BF16),
                 row(norm_mlp_post[l]), tm=512)
    return x
```

```python
import math

import jax
import jax.numpy as jnp
from jax import lax
from jax.experimental import pallas as pl
from jax.experimental.pallas import tpu as pltpu

F32 = jnp.float32
BF16 = jnp.bfloat16

D_MODEL = 1024
NORM_EPS = 1e-6

RWKV_HEADS = 8
HEAD_DIM = 64
RWKV_WIDTH = RWKV_HEADS * HEAD_DIM
DECAY_LORA = 32
ICLR_LORA = 32
GATE_LORA = 96
LORA_COLS = 2 * DECAY_LORA + 2 * ICLR_LORA + GATE_LORA
LORA_TILE = 256
RWKV_COLS = 3 * RWKV_WIDTH + LORA_COLS
RWKV_TILE_COLS = 3 * RWKV_WIDTH + LORA_TILE
LN_X_EPS = 64e-5
CHUNK = 64
CHUNK_GROUP = 2
DECAY_SCALE = math.exp(-0.5)

MLA_HEADS = 8
QK_NOPE_DIM = 64
QK_ROPE_DIM = 32
V_HEAD_DIM = 64
MLA_WIDTH = MLA_HEADS * V_HEAD_DIM
Q_LORA_RANK = 256
KV_LORA_RANK = 128
ROPE_THETA = 10000.0
LOG2_E = math.log2(math.e)
LANES = 128
MLA_TILE_COLS = Q_LORA_RANK + KV_LORA_RANK + 2 * LANES

MEM_HEADS = 4
MEM_HEAD_DIM = D_MODEL // MEM_HEADS
D_FF = 4 * D_MODEL

VMEM_LIMIT = 56 * 1024 * 1024


def _mm(a, b):
    return jnp.dot(a.astype(BF16), b.astype(BF16), preferred_element_type=F32)


def _split2(t):
    hi = t.astype(BF16)
    lo = (t - hi.astype(F32)).astype(BF16)
    return hi, lo


def _split3(t):
    hi = t.astype(BF16)
    rest = t - hi.astype(F32)
    mid = rest.astype(BF16)
    lo = (rest - mid.astype(F32)).astype(BF16)
    return hi, mid, lo


def _dot(a, b):
    return jnp.dot(a, b, preferred_element_type=F32)


def _mm_x3(a, b):
    a_hi, a_lo = _split2(a)
    b_hi, b_lo = _split2(b)
    return _dot(a_hi, b_hi) + (_dot(a_hi, b_lo) + _dot(a_lo, b_hi))


def _mm_exact_lhs(a, b):
    a = a.astype(BF16)
    hi, lo = _split2(b)
    return _dot(a, hi) + _dot(a, lo)


def _mm_nt(a, b):
    return lax.dot_general(a.astype(BF16), b.astype(BF16), (((1,), (1,)), ((), ())),
                           preferred_element_type=F32)


def _mm_tn(a, b):
    return lax.dot_general(a.astype(BF16), b.astype(BF16), (((0,), (0,)), ((), ())),
                           preferred_element_type=F32)


def _rms(x, g, eps=NORM_EPS):
    return x * lax.rsqrt(jnp.mean(x * x, axis=-1, keepdims=True) + eps) * g


def _sigmoid(x):
    return 0.5 * jnp.tanh(0.5 * x) + 0.5


def _params(*sem):
    return pltpu.CompilerParams(dimension_semantics=sem, vmem_limit_bytes=VMEM_LIMIT)


def _const_spec(shape):
    nd = len(shape)
    return pl.BlockSpec(shape, lambda *_: (0,) * nd)


def _in_proj_kernel(x_ref, xp_ref, xn_ref, g_ref, wr_ref, wm_ref, conv_ref, zr_ref, zm_ref):
    i = pl.program_id(1)
    last = pl.num_programs(1) - 1
    tm = x_ref.shape[1]
    g = g_ref[...]
    h = _rms(x_ref[0], g).astype(BF16)
    zm_ref[0] = jnp.dot(h, wm_ref[...], preferred_element_type=F32)
    z = jnp.dot(h, wr_ref[...], preferred_element_type=F32)
    halo = _rms(jnp.concatenate([xp_ref[0], xn_ref[0]], axis=0), g).astype(BF16)
    z_halo = jnp.dot(halo, wr_ref[...], preferred_element_type=F32)
    prev_row = jnp.where(i == 0, 0.0, z_halo[7:8, :])
    next_row = jnp.where(i == last, 0.0, z_halo[8:9, :])
    row = lax.broadcasted_iota(jnp.int32, (tm, 1), 0)
    z_dn = jnp.where(row == 0, prev_row, pltpu.roll(z, 1, 0))
    z_up = jnp.where(row == tm - 1, next_row, pltpu.roll(z, tm - 1, 0))
    zr_ref[0] = conv_ref[0:1, :] * z_dn + conv_ref[1:2, :] * z + conv_ref[2:3, :] * z_up


def _in_proj(x, g, w_r, w_m, conv_p, tm):
    B, S, D = x.shape
    halo = tm // 8
    n_halo = S // 8
    return pl.pallas_call(
        _in_proj_kernel,
        grid=(B, S // tm),
        in_specs=[pl.BlockSpec((1, tm, D), lambda b, i: (b, i, 0)),
                  pl.BlockSpec((1, 8, D), lambda b, i: (b, jnp.maximum(i * halo - 1, 0), 0)),
                  pl.BlockSpec((1, 8, D), lambda b, i: (b, jnp.minimum((i + 1) * halo, n_halo - 1), 0)),
                  _const_spec((1, D)),
                  _const_spec(w_r.shape),
                  _const_spec(w_m.shape),
                  _const_spec(conv_p.shape)],
        out_specs=[pl.BlockSpec((1, tm, RWKV_TILE_COLS), lambda b, i: (b, i, 0)),
                   pl.BlockSpec((1, tm, MLA_TILE_COLS), lambda b, i: (b, i, 0))],
        out_shape=[jax.ShapeDtypeStruct((B, S, RWKV_TILE_COLS), F32),
                   jax.ShapeDtypeStruct((B, S, MLA_TILE_COLS), F32)],
        compiler_params=_params("parallel", "parallel"),
        name="in_proj",
    )(x, x, x, g, w_r, w_m, conv_p)


def _chunk_units(units, eye, eye_hi):
    L, N = units[0][0].shape
    zeros = jnp.zeros((L, N), BF16)
    AA = [_mm_nt(jnp.concatenate([At, Rt], axis=0), jnp.concatenate([Bt, Kt], axis=0))
          for (At, Rt, Bt, Kt, *_) in units]
    A_a = [jnp.where(u[8], aa[:L, :], 0.0) for u, aa in zip(units, AA)]
    A_r = [jnp.where(u[9], aa[L:, :], 0.0) for u, aa in zip(units, AA)]
    AkV = [_mm(a.astype(BF16)[:, L:], u[6]) for u, a in zip(units, A_a)]
    low = lax.broadcasted_iota(jnp.int32, (L, 2 * L), 1) < L
    S = [jnp.where(low, a, eye_hi) for a in A_a]
    span = 1
    while span < L:
        R = [_mm(s[:, :L], s) for s in S]
        S = [jnp.where(low, r, r + s) for r, s in zip(R, S)]
        span *= 2
    W = [_mm(s.astype(BF16)[:, L:], jnp.concatenate([u[0], akv.astype(BF16)], axis=1))
         for s, u, akv in zip(S, units, AkV)]
    Z = [jnp.concatenate([w.astype(BF16), jnp.concatenate([zeros, u[6]], axis=1)], axis=0)
         for u, w in zip(units, W)]
    lhs = [jnp.concatenate([a_r.astype(BF16), jnp.concatenate([u[4], u[5]], axis=0).T], axis=0)
           for u, a_r in zip(units, A_r)]
    QGH = [_mm(l, z) for l, z in zip(lhs, Z)]
    Q = [jnp.concatenate([u[1], zeros], axis=1) + t[:L] for u, t in zip(units, QGH)]
    GH = [jnp.concatenate([eye * u[7], zeros.astype(F32)], axis=1) + t[L:] for u, t in zip(units, QGH)]
    return Q, GH


def _rwkv_kernel(zf_ref, zb_ref,
                 w0_ref, w2_ref, a0_ref, a2_ref, g2_ref, kk_ref, ka_ref, rk_ref, seg_ref, tri_ref,
                 of_ref, ob_ref, bonus_f_ref, bonus_b_ref, g_ref,
                 r_s, v_s, kk_s, b_s, kd_s, lw_s, state_s):
    i = pl.program_id(1)
    n_tiles = pl.num_programs(1)
    tm = zf_ref.shape[1]
    L, N = CHUNK, HEAD_DIM
    n_chunks = tm // L

    @pl.when(i == 0)
    def _():
        state_s[...] = jnp.zeros_like(state_s)

    seg = seg_ref[...]

    def seg_sum(t):
        t = t.astype(BF16)
        half = seg.shape[0]
        return jnp.concatenate([_dot(t[:, :half], seg), _dot(t[:, half:], seg)], axis=1)

    for d, (z_ref, bonus_ref) in enumerate(((zf_ref, bonus_f_ref), (zb_ref, bonus_b_ref))):
        r = z_ref[0, :, 0:RWKV_WIDTH]
        k = z_ref[0, :, RWKV_WIDTH:2 * RWKV_WIDTH]
        v = z_ref[0, :, 2 * RWKV_WIDTH:3 * RWKV_WIDTH]
        lora = z_ref[0, :, 3 * RWKV_WIDTH:]
        kk = k * kk_ref[...]
        kk = kk * lax.rsqrt(jnp.maximum(seg_sum(kk * kk), 1e-24))
        w_pre = w0_ref[d:d + 1, :] + _mm(jnp.tanh(lora), w2_ref[d])
        lw_s[d] = -DECAY_SCALE * _sigmoid(w_pre)
        alpha = _sigmoid(a0_ref[d:d + 1, :] + _mm(lora, a2_ref[d]))
        kd = k * (1.0 + (alpha - 1.0) * ka_ref[...])
        r_s[d] = r
        v_s[d] = v
        kk_s[d] = kk
        kd_s[d] = kd
        b_s[d] = kk * alpha
        bonus_ref[0] = (seg_sum(r * kd * rk_ref[...]) * v).astype(bonus_ref.dtype)
        if d == 0:
            g_ref[0] = _mm(_sigmoid(lora), g2_ref[...]).astype(g_ref.dtype)

    ri = lax.broadcasted_iota(jnp.int32, (L, L), 0)
    ci = lax.broadcasted_iota(jnp.int32, (L, L), 1)
    eye = (ci == ri).astype(F32)
    ri2 = lax.broadcasted_iota(jnp.int32, (L, 2 * L), 0)
    ci2 = lax.broadcasted_iota(jnp.int32, (L, 2 * L), 1)
    eye_hi = (ci2 == ri2 + L).astype(F32)
    ci2 = jnp.where(ci2 >= L, ci2 - L, ci2)
    m_strict = (ci2 < ri2, ci2 > ri2)
    m_incl = (ci2 <= ri2, ci2 >= ri2)
    out_refs = (of_ref, ob_ref)

    def chunk_units(c):
        rows_d = (pl.ds(pl.multiple_of(c * L, L), L), pl.ds(pl.multiple_of((n_chunks - 1 - c) * L, L), L))
        units = []
        for d in range(2):
            rows = rows_d[d]
            lw = lw_s[d, rows, :]
            bc = b_s[d, rows, :]
            kdc = kd_s[d, rows, :]
            vc = v_s[d, rows, :]
            cum = _mm_exact_lhs(tri_ref[d], lw)
            tot = cum[L - 1:L, :] if d == 0 else cum[0:1, :]
            e_in = jnp.exp(-cum)
            e_rem = jnp.exp(tot - cum)
            e_tot = jnp.exp(tot)
            At = (-kk_s[d, rows, :] * jnp.exp(cum - lw)).astype(BF16)
            Rt = (r_s[d, rows, :] * jnp.exp(cum)).astype(BF16)
            Bt = (bc * e_in).astype(BF16)
            Kt = (kdc * e_in).astype(BF16)
            Bh = (bc * e_rem).astype(BF16)
            Kh = (kdc * e_rem).astype(BF16)
            vc = vc.astype(BF16)
            for h in range(RWKV_HEADS):
                sl = slice(h * HEAD_DIM, (h + 1) * HEAD_DIM)
                units.append((At[:, sl], Rt[:, sl], Bt[:, sl], Kt[:, sl], Bh[:, sl], Kh[:, sl],
                              vc[:, sl], e_tot[:, sl], m_strict[d], m_incl[d]))
        return rows_d, units

    n_units = 2 * RWKV_HEADS

    def group_body(j, carry):
        rows, units = [], []
        for g in range(CHUNK_GROUP):
            rows_d, chunk = chunk_units(j * CHUNK_GROUP + g)
            rows.append(rows_d)
            units.extend(chunk)
        Q, GH = _chunk_units(units, eye, eye_hi)
        M = [state_s[u] for u in range(n_units)]
        for g in range(CHUNK_GROUP):
            Qg, GHg = Q[g * n_units:(g + 1) * n_units], GH[g * n_units:(g + 1) * n_units]
            M_split = [_split2(m) for m in M]
            G_split = [_split2(gh[:, :N]) for gh in GHg]
            prod = [_dot(jnp.concatenate([q[:, :N].astype(BF16), g_hi, g_lo], axis=0), m_hi)
                    for q, (g_hi, g_lo), (m_hi, _) in zip(Qg, G_split, M_split)]
            out = [p[:L] + q[:, N:] for p, q in zip(prod, Qg)]
            M = [p[L:L + N] + (p[L + N:] + _dot(g_hi, m_lo)) + gh[:, N:]
                 for p, (g_hi, _), (_, m_lo), gh in zip(prod, G_split, M_split, GHg)]
            for d in range(2):
                for h in range(RWKV_HEADS):
                    out_refs[d][0, rows[g][d], h * N:(h + 1) * N] = out[d * RWKV_HEADS + h]
        for u in range(n_units):
            state_s[u] = M[u]
        return carry

    lax.fori_loop(0, n_chunks // CHUNK_GROUP, group_body, 0)


def _rwkv(z_rwkv, w0, w2_p, a0, a2_p, g2_p, k_k, k_a, r_k, seg, tri, tm):
    B, S, ZC = z_rwkv.shape
    C = RWKV_WIDTH
    n_tiles = S // tm
    fwd = lambda b, i: (b, i, 0)
    bwd = lambda b, i: (b, n_tiles - 1 - i, 0)
    out_f = pl.BlockSpec((1, tm, C), fwd)
    out_b = pl.BlockSpec((1, tm, C), bwd)
    out_sds = lambda dtype: jax.ShapeDtypeStruct((B, S, C), dtype)
    return pl.pallas_call(
        _rwkv_kernel,
        grid=(B, n_tiles),
        in_specs=[pl.BlockSpec((1, tm, ZC), fwd),
                  pl.BlockSpec((1, tm, ZC), bwd),
                  _const_spec(w0.shape), _const_spec(w2_p.shape),
                  _const_spec(a0.shape), _const_spec(a2_p.shape), _const_spec(g2_p.shape),
                  _const_spec(k_k.shape), _const_spec(k_a.shape), _const_spec(r_k.shape),
                  _const_spec(seg.shape), _const_spec(tri.shape)],
        out_specs=[out_f, out_b, out_f, out_b, out_f],
        out_shape=[out_sds(F32), out_sds(F32), out_sds(BF16), out_sds(BF16), out_sds(BF16)],
        scratch_shapes=[pltpu.VMEM((2, tm, C), F32)] * 6
                       + [pltpu.VMEM((2 * RWKV_HEADS, HEAD_DIM, HEAD_DIM), F32)],
        compiler_params=_params("parallel", "arbitrary"),
        name="rwkv",
    )(z_rwkv, z_rwkv, w0, w2_p, a0, a2_p, g2_p, k_k, k_a, r_k, seg, tri)


def _rope_tiles(pos_row, freq_col, place_cos, place_sin, base):
    ang = freq_col * pos_row.astype(F32)

    def place(t, p):
        return sum(lax.dot_general(part, p, (((0,), (0,)), ((), ())), preferred_element_type=F32)
                   for part in _split3(t))

    return place(jnp.cos(ang), place_cos) + base, place(jnp.sin(ang), place_sin)


def _mla_kernel(zq_ref, zkv_ref, pos_ref, freq_ref, pcos_ref, psin_ref, base_ref, qn_ref, wq_ref, wqs_ref, kvn_ref, wkv_ref,
                y_ref, k_s, v_s, cos_s, sin_s):
    i = pl.program_id(1)
    tq = zq_ref.shape[1]
    scale = (QK_NOPE_DIM + QK_ROPE_DIM) ** -0.5
    c_kv_lo, c_kv_hi = Q_LORA_RANK, Q_LORA_RANK + KV_LORA_RANK

    @pl.when(i == 0)
    def _():
        zkv = zkv_ref[0]
        kvn = _rms(zkv[:, c_kv_lo:c_kv_hi], kvn_ref[...]).astype(BF16)
        kvu = jnp.dot(kvn, wkv_ref[...], preferred_element_type=F32)
        cos_t, sin_t = _rope_tiles(pos_ref[0], freq_ref[...], pcos_ref[...], psin_ref[...], base_ref[...])
        cos_s[...] = cos_t
        sin_s[...] = sin_t
        k_rot = zkv[:, c_kv_hi:c_kv_hi + LANES] * cos_t + zkv[:, c_kv_hi + LANES:] * sin_t
        lane = lax.broadcasted_iota(jnp.int32, (1, LANES), 1)
        ones_col = (lane == V_HEAD_DIM).astype(F32)
        for h in range(MLA_HEADS):
            k_s[h] = (kvu[:, h * LANES:(h + 1) * LANES] + k_rot).astype(BF16)
            v_s[h] = (kvu[:, (MLA_HEADS + h) * LANES:(MLA_HEADS + h + 1) * LANES] + ones_col).astype(BF16)

    zq = zq_ref[0]
    qn = _rms(zq[:, :Q_LORA_RANK], qn_ref[...]).astype(BF16)
    q_a = jnp.dot(qn, wq_ref[...], preferred_element_type=F32)
    q_b = jnp.dot(qn, wqs_ref[...], preferred_element_type=F32)
    rows = pl.ds(pl.multiple_of(i * tq, tq), tq)
    cos_q = cos_s[rows, :] * (scale * LOG2_E)
    sin_q = sin_s[rows, :] * (scale * LOG2_E)

    def scores(h):
        sl = slice(h * LANES, (h + 1) * LANES)
        qh = (q_a[:, sl] * cos_q + q_b[:, sl] * sin_q).astype(BF16)
        return lax.dot_general(qh, k_s[h], (((1,), (1,)), ((), ())), preferred_element_type=F32)

    s = scores(0)
    for h in range(MLA_HEADS):
        s_next = scores(h + 1) if h + 1 < MLA_HEADS else None
        p = jnp.exp2((s - jnp.max(s, axis=-1, keepdims=True)).astype(BF16))
        o = jnp.dot(p, v_s[h], preferred_element_type=F32)
        y_ref[0, :, h * V_HEAD_DIM:(h + 1) * V_HEAD_DIM] = (
            o[:, :V_HEAD_DIM] / o[:, V_HEAD_DIM:V_HEAD_DIM + 1]).astype(y_ref.dtype)
        s = s_next


def _mla_attn(z_mla, pos_row, freq_col, place_cos, place_sin, base, q_norm, wq_p, wq_sw, kv_norm, wkv_p, tq):
    B, S, ZC = z_mla.shape
    return pl.pallas_call(
        _mla_kernel,
        grid=(B, S // tq),
        in_specs=[pl.BlockSpec((1, tq, ZC), lambda b, i: (b, i, 0)),
                  pl.BlockSpec((1, S, ZC), lambda b, i: (b, 0, 0)),
                  pl.BlockSpec((1, 1, S), lambda b, i: (b, 0, 0)),
                  _const_spec(freq_col.shape), _const_spec(place_cos.shape), _const_spec(place_sin.shape),
                  _const_spec(base.shape), _const_spec(q_norm.shape),
                  _const_spec(wq_p.shape), _const_spec(wq_sw.shape), _const_spec(kv_norm.shape),
                  _const_spec(wkv_p.shape)],
        out_specs=pl.BlockSpec((1, tq, MLA_WIDTH), lambda b, i: (b, i, 0)),
        out_shape=jax.ShapeDtypeStruct((B, S, MLA_WIDTH), BF16),
        scratch_shapes=[pltpu.VMEM((MLA_HEADS, S, LANES), BF16), pltpu.VMEM((MLA_HEADS, S, LANES), BF16),
                        pltpu.VMEM((S, LANES), F32), pltpu.VMEM((S, LANES), F32)],
        compiler_params=_params("parallel", "arbitrary"),
        name="mla_attn",
    )(z_mla, z_mla, pos_row, freq_col, place_cos, place_sin, base, q_norm, wq_p, wq_sw, kv_norm, wkv_p)


def _mix_out_rows(x, o_f, o_b, bonus, gate, lnw, lnb, segm, y_mla, w_r, w_m, g):
    o = o_f + o_b
    o_hi, o_lo = _split2(o)
    oc = o - (_dot(o_hi, segm) + _dot(o_lo, segm))
    var = _dot((oc * oc).astype(BF16), segm)
    o = oc * lax.rsqrt(var + LN_X_EPS) * lnw + lnb
    y_rwkv = (o + bonus) * gate
    y = _dot(y_rwkv.astype(BF16), w_r) + _dot(y_mla, w_m)
    return x + _rms(y, g)


def _mem_kv_kernel(mem_ref, g_ref, w_ref, k_ref, v_ref):
    m = _rms(mem_ref[0], g_ref[...]).astype(BF16)
    kv = jnp.dot(m, w_ref[...], preferred_element_type=F32)
    k_ref[0] = kv[:, :D_MODEL].astype(BF16)
    v_ref[0] = kv[:, D_MODEL:].astype(BF16)


def _mem_kv(mem, g, wkv):
    B, T, D = mem.shape
    return pl.pallas_call(
        _mem_kv_kernel,
        grid=(B,),
        in_specs=[pl.BlockSpec((1, T, D), lambda b: (b, 0, 0)), _const_spec(g.shape), _const_spec(wkv.shape)],
        out_specs=[pl.BlockSpec((1, T, D), lambda b: (b, 0, 0)), pl.BlockSpec((1, T, D), lambda b: (b, 0, 0))],
        out_shape=[jax.ShapeDtypeStruct((B, T, D), BF16), jax.ShapeDtypeStruct((B, T, D), BF16)],
        compiler_params=_params("parallel"),
        name="mem_kv",
    )(mem, g, wkv)


def _mem_attn_rows(x, k_ref, v_ref, g_pre, wq, wo, g_post, att_s):
    h = _rms(x, g_pre).astype(BF16)
    q = (_dot(h, wq) * (MEM_HEAD_DIM ** -0.5 * LOG2_E)).astype(BF16)

    def scores(hd):
        sl = slice(hd * MEM_HEAD_DIM, (hd + 1) * MEM_HEAD_DIM)
        return lax.dot_general(q[:, sl], k_ref[0, :, sl], (((1,), (1,)), ((), ())), preferred_element_type=F32)

    s = scores(0)
    for hd in range(MEM_HEADS):
        sl = slice(hd * MEM_HEAD_DIM, (hd + 1) * MEM_HEAD_DIM)
        s_next = scores(hd + 1) if hd + 1 < MEM_HEADS else None
        p = jnp.exp2(s - jnp.max(s, axis=-1, keepdims=True))
        denom = jnp.sum(p, axis=-1, keepdims=True)
        o = _dot(p.astype(BF16), v_ref[0, :, sl])
        att_s[:, sl] = (o / denom).astype(BF16)
        s = s_next
    return x + _rms(_dot(att_s[...], wo), g_post)


def _mix_mem_kernel(x_ref, of_ref, ob_ref, bf_ref, bb_ref, gate_ref, lnw_ref, lnb_ref, segm_ref, ym_ref,
                    wr_ref, wm_ref, gmix_ref, k_ref, v_ref, gpre_ref, wq_ref, wo_ref, gpost_ref, o_ref, att_s):
    bonus = bf_ref[0].astype(F32) + bb_ref[0].astype(F32)
    x = _mix_out_rows(x_ref[0], of_ref[0], ob_ref[0], bonus, gate_ref[0].astype(F32), lnw_ref[...], lnb_ref[...],
                      segm_ref[...], ym_ref[0], wr_ref[...], wm_ref[...], gmix_ref[...])
    o_ref[0] = _mem_attn_rows(x, k_ref, v_ref, gpre_ref[...], wq_ref[...], wo_ref[...], gpost_ref[...], att_s)


def _mix_mem(x, o_f, o_b, bonus_f, bonus_b, gate, lnw, lnb, segm, y_mla, w_r, w_m, g_mix,
             k, v, g_pre, wq, wo, g_post, tm):
    B, S, D = x.shape
    T = k.shape[1]
    C = RWKV_WIDTH
    row_spec = lambda cols: pl.BlockSpec((1, tm, cols), lambda b, i: (b, i, 0))
    kv_spec = pl.BlockSpec((1, T, D), lambda b, i: (b, 0, 0))
    consts = (lnw, lnb, segm)
    return pl.pallas_call(
        _mix_mem_kernel,
        grid=(B, S // tm),
        in_specs=[row_spec(D), row_spec(C), row_spec(C), row_spec(C), row_spec(C), row_spec(C)]
                 + [_const_spec(t.shape) for t in consts] + [row_spec(MLA_WIDTH)]
                 + [_const_spec(t.shape) for t in (w_r, w_m, g_mix)] + [kv_spec, kv_spec]
                 + [_const_spec(t.shape) for t in (g_pre, wq, wo, g_post)],
        out_specs=row_spec(D),
        out_shape=jax.ShapeDtypeStruct((B, S, D), F32),
        scratch_shapes=[pltpu.VMEM((tm, D), BF16)],
        compiler_params=_params("parallel", "parallel"),
        name="mix_mem",
    )(x, o_f, o_b, bonus_f, bonus_b, gate, lnw, lnb, segm, y_mla, w_r, w_m, g_mix, k, v, g_pre, wq, wo, g_post)


def _mlp_kernel(x_ref, gpre_ref, w1_ref, w2_ref, gpost_ref, o_ref):
    x = x_ref[0]
    h = _rms(x, gpre_ref[...]).astype(BF16)
    u = jnp.maximum(jnp.dot(h, w1_ref[...], preferred_element_type=F32), 0.0)
    y = jnp.dot((u * u).astype(BF16), w2_ref[...], preferred_element_type=F32)
    o_ref[0] = x + _rms(y, gpost_ref[...])


def _mlp(x, g_pre, w1, w2, g_post, tm):
    B, S, D = x.shape
    row_spec = pl.BlockSpec((1, tm, D), lambda b, i: (b, i, 0))
    return pl.pallas_call(
        _mlp_kernel,
        grid=(B, S // tm),
        in_specs=[row_spec, _const_spec(g_pre.shape), _const_spec(w1.shape), _const_spec(w2.shape),
                  _const_spec(g_post.shape)],
        out_specs=row_spec,
        out_shape=jax.ShapeDtypeStruct((B, S, D), F32),
        compiler_params=_params("parallel", "parallel"),
        name="mlp",
    )(x, g_pre, w1, w2, g_post)


def _pad_cols(w, n):
    return jnp.pad(w, ((0, 0), (0, n - w.shape[1])))


def _rope_tile(w):
    return jnp.pad(w, ((0, 0), (QK_NOPE_DIM, LANES - QK_NOPE_DIM - QK_ROPE_DIM)))


def _swap_halves(w):
    half = QK_ROPE_DIM // 2
    return jnp.concatenate([w[:, half:], w[:, :half]], axis=1)


def _lora_rows(w, lo, rows):
    return jnp.pad(w, ((lo, LORA_TILE - lo - rows), (0, 0)))


def kernel(x, mem, positions, norm_mix_pre, w_in, conv_rwkv, rwkv_w0, rwkv_w2, rwkv_a0, rwkv_a2, rwkv_g2, rwkv_k_k, rwkv_k_a, rwkv_r_k, rwkv_lnx_w, rwkv_lnx_b, mla_q_norm, mla_w_uq, mla_kv_norm, mla_w_ukv, w_out, norm_mix_post, norm_mem_pre, norm_memtok, mem_wq, mem_wkv, mem_wo, norm_mem_post, norm_mlp_pre, mlp_w1, mlp_w2, norm_mlp_post):
    depth = w_in.shape[0]
    C = RWKV_WIDTH
    head_of = jnp.arange(C) // HEAD_DIM
    seg = (head_of[:, None] == head_of[None, :]).astype(BF16)
    seg_mean = (seg.astype(F32) / HEAD_DIM).astype(BF16)
    inv_freq = ROPE_THETA ** (-jnp.arange(0, QK_ROPE_DIM, 2, dtype=F32) / QK_ROPE_DIM)
    half = QK_ROPE_DIM // 2
    lane = jnp.arange(LANES)[None, :]
    f_idx = jnp.arange(half)[:, None]
    first, second = lane == QK_NOPE_DIM + f_idx, lane == QK_NOPE_DIM + half + f_idx
    place_cos = (first | second).astype(BF16)
    place_sin = (second.astype(F32) - first.astype(F32)).astype(BF16)
    rope_base = (lane < QK_NOPE_DIM).astype(F32)
    freq_col = inv_freq[:, None]
    pos_row = positions[:, None, :]
    step = jnp.arange(CHUNK)
    tri = jnp.stack([step[None, :] <= step[:, None], step[None, :] >= step[:, None]]).astype(F32)
    row = lambda t: t.reshape(1, -1)

    for l in range(depth):
        w = w_in[l]
        mla0 = RWKV_COLS
        w_rope = w[:, mla0 + Q_LORA_RANK + KV_LORA_RANK:]
        w_r = _pad_cols(w[:, :RWKV_COLS], RWKV_TILE_COLS).astype(BF16)
        w_m = jnp.concatenate([w[:, mla0:mla0 + Q_LORA_RANK + KV_LORA_RANK], _rope_tile(w_rope),
                               _rope_tile(_swap_halves(w_rope))], axis=1).astype(BF16)
        conv_p = _pad_cols(conv_rwkv[l], RWKV_TILE_COLS)
        z_rwkv, z_mla = _in_proj(x, row(norm_mix_pre[l]), w_r, w_m, conv_p, tm=512)

        w2_p = jnp.stack([_lora_rows(rwkv_w2[l, d], d * DECAY_LORA, DECAY_LORA) for d in range(2)])
        a_lo = 2 * DECAY_LORA
        a2_p = jnp.stack([_lora_rows(rwkv_a2[l, d], a_lo + d * ICLR_LORA, ICLR_LORA) for d in range(2)])
        g2_p = _lora_rows(rwkv_g2[l], a_lo + 2 * ICLR_LORA, GATE_LORA)
        o_f, o_b, bonus_f, bonus_b, gate = _rwkv(z_rwkv, rwkv_w0[l], w2_p, rwkv_a0[l], a2_p, g2_p,
                                                 row(rwkv_k_k[l]), row(rwkv_k_a[l]), row(rwkv_r_k[l]),
                                                 seg[:C // 2, :C // 2], tri, tm=512)

        qk = QK_NOPE_DIM + QK_ROPE_DIM
        w_uq = mla_w_uq[l].reshape(Q_LORA_RANK, MLA_HEADS, qk)
        wq_p = jnp.pad(w_uq, ((0, 0), (0, 0), (0, LANES - qk))).reshape(Q_LORA_RANK, MLA_HEADS * LANES)
        uq_rope = w_uq[:, :, QK_NOPE_DIM:]
        uq_sw = jnp.concatenate([uq_rope[:, :, half:], uq_rope[:, :, :half]], axis=2)
        wq_sw = jnp.pad(uq_sw, ((0, 0), (0, 0), (QK_NOPE_DIM, LANES - qk))).reshape(Q_LORA_RANK, MLA_HEADS * LANES)
        w_ukv = mla_w_ukv[l].reshape(KV_LORA_RANK, MLA_HEADS, QK_NOPE_DIM + V_HEAD_DIM)
        wk_p = jnp.pad(w_ukv[:, :, :QK_NOPE_DIM], ((0, 0), (0, 0), (0, LANES - QK_NOPE_DIM)))
        wv_p = jnp.pad(w_ukv[:, :, QK_NOPE_DIM:], ((0, 0), (0, 0), (0, LANES - V_HEAD_DIM)))
        wkv_p = jnp.concatenate([wk_p.reshape(KV_LORA_RANK, MLA_HEADS * LANES),
                                 wv_p.reshape(KV_LORA_RANK, MLA_HEADS * LANES)], axis=1)
        y_mla = _mla_attn(z_mla, pos_row, freq_col, place_cos, place_sin, rope_base, row(mla_q_norm[l]), wq_p.astype(BF16), wq_sw.astype(BF16),
                          row(mla_kv_norm[l]), wkv_p.astype(BF16), tq=512)

        wo = w_out[l].astype(BF16)
        k_mem, v_mem = _mem_kv(mem, row(norm_memtok[l]), mem_wkv[l].astype(BF16))
        x = _mix_mem(x, o_f, o_b, bonus_f, bonus_b, gate, row(rwkv_lnx_w[l]), row(rwkv_lnx_b[l]), seg_mean,
                     y_mla, wo[:C], wo[C:], row(norm_mix_post[l]), k_mem, v_mem, row(norm_mem_pre[l]),
                     mem_wq[l].astype(BF16), mem_wo[l].astype(BF16), row(norm_mem_post[l]), tm=512)

        x = _mlp(x, row(norm_mlp_pre[l]), mlp_w1[l].astype(BF16), mlp_w2[l].astype(BF16),
                 row(norm_mlp_post[l]), tm=512)
    return x
```

```python
import math

import jax
import jax.numpy as jnp
from jax import lax
from jax.experimental import pallas as pl
from jax.experimental.pallas import tpu as pltpu

F32 = jnp.float32
BF16 = jnp.bfloat16

D_MODEL = 1024
NORM_EPS = 1e-6

RWKV_HEADS = 8
HEAD_DIM = 64
RWKV_WIDTH = RWKV_HEADS * HEAD_DIM
DECAY_LORA = 32
ICLR_LORA = 32
GATE_LORA = 96
LORA_COLS = 2 * DECAY_LORA + 2 * ICLR_LORA + GATE_LORA
LORA_TILE = 256
RWKV_COLS = 3 * RWKV_WIDTH + LORA_COLS
RWKV_TILE_COLS = 3 * RWKV_WIDTH + LORA_TILE
LN_X_EPS = 64e-5
CHUNK = 64
CHUNK_GROUP = 2
DECAY_SCALE = math.exp(-0.5)

MLA_HEADS = 8
QK_NOPE_DIM = 64
QK_ROPE_DIM = 32
V_HEAD_DIM = 64
MLA_WIDTH = MLA_HEADS * V_HEAD_DIM
Q_LORA_RANK = 256
KV_LORA_RANK = 128
ROPE_THETA = 10000.0
LOG2_E = math.log2(math.e)
LANES = 128
MLA_TILE_COLS = Q_LORA_RANK + KV_LORA_RANK + 2 * LANES

MEM_HEADS = 4
MEM_HEAD_DIM = D_MODEL // MEM_HEADS
D_FF = 4 * D_MODEL

VMEM_LIMIT = 56 * 1024 * 1024


def _mm(a, b):
    return jnp.dot(a.astype(BF16), b.astype(BF16), preferred_element_type=F32)


def _split2(t):
    hi = t.astype(BF16)
    lo = (t - hi.astype(F32)).astype(BF16)
    return hi, lo


def _split3(t):
    hi = t.astype(BF16)
    rest = t - hi.astype(F32)
    mid = rest.astype(BF16)
    lo = (rest - mid.astype(F32)).astype(BF16)
    return hi, mid, lo


def _dot(a, b):
    return jnp.dot(a, b, preferred_element_type=F32)


def _mm_x3(a, b):
    a_hi, a_lo = _split2(a)
    b_hi, b_lo = _split2(b)
    return _dot(a_hi, b_hi) + (_dot(a_hi, b_lo) + _dot(a_lo, b_hi))


def _mm_exact_lhs(a, b):
    a = a.astype(BF16)
    hi, lo = _split2(b)
    return _dot(a, hi) + _dot(a, lo)


def _mm_nt(a, b):
    return lax.dot_general(a.astype(BF16), b.astype(BF16), (((1,), (1,)), ((), ())),
                           preferred_element_type=F32)


def _mm_tn(a, b):
    return lax.dot_general(a.astype(BF16), b.astype(BF16), (((0,), (0,)), ((), ())),
                           preferred_element_type=F32)


def _rms(x, g, eps=NORM_EPS):
    return x * lax.rsqrt(jnp.mean(x * x, axis=-1, keepdims=True) + eps) * g


def _sigmoid(x):
    return 0.5 * jnp.tanh(0.5 * x) + 0.5


def _params(*sem):
    return pltpu.CompilerParams(dimension_semantics=sem, vmem_limit_bytes=VMEM_LIMIT)


def _const_spec(shape):
    nd = len(shape)
    return pl.BlockSpec(shape, lambda *_: (0,) * nd)


def _in_proj_kernel(x_ref, xp_ref, xn_ref, g_ref, wr_ref, wm_ref, conv_ref, zr_ref, zm_ref):
    i = pl.program_id(1)
    last = pl.num_programs(1) - 1
    tm = x_ref.shape[1]
    g = g_ref[...]
    h = _rms(x_ref[0], g).astype(BF16)
    zm_ref[0] = jnp.dot(h, wm_ref[...], preferred_element_type=F32)
    z = jnp.dot(h, wr_ref[...], preferred_element_type=F32)
    halo = _rms(jnp.concatenate([xp_ref[0], xn_ref[0]], axis=0), g).astype(BF16)
    z_halo = jnp.dot(halo, wr_ref[...], preferred_element_type=F32)
    prev_row = jnp.where(i == 0, 0.0, z_halo[7:8, :])
    next_row = jnp.where(i == last, 0.0, z_halo[8:9, :])
    row = lax.broadcasted_iota(jnp.int32, (tm, 1), 0)
    z_dn = jnp.where(row == 0, prev_row, pltpu.roll(z, 1, 0))
    z_up = jnp.where(row == tm - 1, next_row, pltpu.roll(z, tm - 1, 0))
    zr_ref[0] = conv_ref[0:1, :] * z_dn + conv_ref[1:2, :] * z + conv_ref[2:3, :] * z_up


def _in_proj(x, g, w_r, w_m, conv_p, tm):
    B, S, D = x.shape
    halo = tm // 8
    n_halo = S // 8
    return pl.pallas_call(
        _in_proj_kernel,
        grid=(B, S // tm),
        in_specs=[pl.BlockSpec((1, tm, D), lambda b, i: (b, i, 0)),
                  pl.BlockSpec((1, 8, D), lambda b, i: (b, jnp.maximum(i * halo - 1, 0), 0)),
                  pl.BlockSpec((1, 8, D), lambda b, i: (b, jnp.minimum((i + 1) * halo, n_halo - 1), 0)),
                  _const_spec((1, D)),
                  _const_spec(w_r.shape),
                  _const_spec(w_m.shape),
                  _const_spec(conv_p.shape)],
        out_specs=[pl.BlockSpec((1, tm, RWKV_TILE_COLS), lambda b, i: (b, i, 0)),
                   pl.BlockSpec((1, tm, MLA_TILE_COLS), lambda b, i: (b, i, 0))],
        out_shape=[jax.ShapeDtypeStruct((B, S, RWKV_TILE_COLS), F32),
                   jax.ShapeDtypeStruct((B, S, MLA_TILE_COLS), F32)],
        compiler_params=_params("parallel", "parallel"),
        name="in_proj",
    )(x, x, x, g, w_r, w_m, conv_p)


def _chunk_units(units, eye, eye_hi):
    L, N = units[0][0].shape
    zeros = jnp.zeros((L, N), BF16)
    AA = [_mm_nt(jnp.concatenate([At, Rt], axis=0), jnp.concatenate([Bt, Kt], axis=0))
          for (At, Rt, Bt, Kt, *_) in units]
    A_a = [jnp.where(u[8], aa[:L, :], 0.0) for u, aa in zip(units, AA)]
    A_r = [jnp.where(u[9], aa[L:, :], 0.0) for u, aa in zip(units, AA)]
    AkV = [_mm(a.astype(BF16)[:, L:], u[6]) for u, a in zip(units, A_a)]
    low = lax.broadcasted_iota(jnp.int32, (L, 2 * L), 1) < L
    S = [jnp.where(low, a, eye_hi) for a in A_a]
    span = 1
    while span < L:
        R = [_mm(s[:, :L], s) for s in S]
        S = [jnp.where(low, r, r + s) for r, s in zip(R, S)]
        span *= 2
    W = [_mm(s.astype(BF16)[:, L:], jnp.concatenate([u[0], akv.astype(BF16)], axis=1))
         for s, u, akv in zip(S, units, AkV)]
    Z = [jnp.concatenate([w.astype(BF16), jnp.concatenate([zeros, u[6]], axis=1)], axis=0)
         for u, w in zip(units, W)]
    lhs = [jnp.concatenate([a_r.astype(BF16), jnp.concatenate([u[4], u[5]], axis=0).T], axis=0)
           for u, a_r in zip(units, A_r)]
    QGH = [_mm(l, z) for l, z in zip(lhs, Z)]
    Q = [jnp.concatenate([u[1], zeros], axis=1) + t[:L] for u, t in zip(units, QGH)]
    GH = [jnp.concatenate([eye * u[7], zeros.astype(F32)], axis=1) + t[L:] for u, t in zip(units, QGH)]
    return Q, GH


def _rwkv_kernel(zf_ref, zb_ref,
                 w0_ref, w2_ref, a0_ref, a2_ref, g2_ref, kk_ref, ka_ref, rk_ref, seg_ref, tri_ref,
                 of_ref, ob_ref, bonus_f_ref, bonus_b_ref, g_ref,
                 r_s, v_s, kk_s, b_s, kd_s, lw_s, state_s):
    i = pl.program_id(1)
    n_tiles = pl.num_programs(1)
    tm = zf_ref.shape[1]
    L, N = CHUNK, HEAD_DIM
    n_chunks = tm // L

    @pl.when(i == 0)
    def _():
        state_s[...] = jnp.zeros_like(state_s)

    seg = seg_ref[...]

    def seg_sum(t):
        t = t.astype(BF16)
        half = seg.shape[0]
        return jnp.concatenate([_dot(t[:, :half], seg), _dot(t[:, half:], seg)], axis=1)

    for d, (z_ref, bonus_ref) in enumerate(((zf_ref, bonus_f_ref), (zb_ref, bonus_b_ref))):
        r = z_ref[0, :, 0:RWKV_WIDTH]
        k = z_ref[0, :, RWKV_WIDTH:2 * RWKV_WIDTH]
        v = z_ref[0, :, 2 * RWKV_WIDTH:3 * RWKV_WIDTH]
        lora = z_ref[0, :, 3 * RWKV_WIDTH:]
        kk = k * kk_ref[...]
        kk = kk * lax.rsqrt(jnp.maximum(seg_sum(kk * kk), 1e-24))
        w_pre = w0_ref[d:d + 1, :] + _mm(jnp.tanh(lora), w2_ref[d])
        lw_s[d] = -DECAY_SCALE * _sigmoid(w_pre)
        alpha = _sigmoid(a0_ref[d:d + 1, :] + _mm(lora, a2_ref[d]))
        kd = k * (1.0 + (alpha - 1.0) * ka_ref[...])
        r_s[d] = r
        v_s[d] = v
        kk_s[d] = kk
        kd_s[d] = kd
        b_s[d] = kk * alpha
        bonus_ref[0] = (seg_sum(r * kd * rk_ref[...]) * v).astype(bonus_ref.dtype)
        if d == 0:
            g_ref[0] = _mm(_sigmoid(lora), g2_ref[...]).astype(g_ref.dtype)

    ri = lax.broadcasted_iota(jnp.int32, (L, L), 0)
    ci = lax.broadcasted_iota(jnp.int32, (L, L), 1)
    eye = (ci == ri).astype(F32)
    ri2 = lax.broadcasted_iota(jnp.int32, (L, 2 * L), 0)
    ci2 = lax.broadcasted_iota(jnp.int32, (L, 2 * L), 1)
    eye_hi = (ci2 == ri2 + L).astype(F32)
    ci2 = jnp.where(ci2 >= L, ci2 - L, ci2)
    m_strict = (ci2 < ri2, ci2 > ri2)
    m_incl = (ci2 <= ri2, ci2 >= ri2)
    out_refs = (of_ref, ob_ref)

    def chunk_units(c):
        rows_d = (pl.ds(pl.multiple_of(c * L, L), L), pl.ds(pl.multiple_of((n_chunks - 1 - c) * L, L), L))
        units = []
        for d in range(2):
            rows = rows_d[d]
            lw = lw_s[d, rows, :]
            bc = b_s[d, rows, :]
            kdc = kd_s[d, rows, :]
            vc = v_s[d, rows, :]
            cum = _mm_exact_lhs(tri_ref[d], lw)
            tot = cum[L - 1:L, :] if d == 0 else cum[0:1, :]
            e_in = jnp.exp(-cum)
            e_rem = jnp.exp(tot - cum)
            e_tot = jnp.exp(tot)
            At = (-kk_s[d, rows, :] * jnp.exp(cum - lw)).astype(BF16)
            Rt = (r_s[d, rows, :] * jnp.exp(cum)).astype(BF16)
            Bt = (bc * e_in).astype(BF16)
            Kt = (kdc * e_in).astype(BF16)
            Bh = (bc * e_rem).astype(BF16)
            Kh = (kdc * e_rem).astype(BF16)
            vc = vc.astype(BF16)
            for h in range(RWKV_HEADS):
                sl = slice(h * HEAD_DIM, (h + 1) * HEAD_DIM)
                units.append((At[:, sl], Rt[:, sl], Bt[:, sl], Kt[:, sl], Bh[:, sl], Kh[:, sl],
                              vc[:, sl], e_tot[:, sl], m_strict[d], m_incl[d]))
        return rows_d, units

    n_units = 2 * RWKV_HEADS

    def group_body(j, carry):
        rows, units = [], []
        for g in range(CHUNK_GROUP):
            rows_d, chunk = chunk_units(j * CHUNK_GROUP + g)
            rows.append(rows_d)
            units.extend(chunk)
        Q, GH = _chunk_units(units, eye, eye_hi)
        M = [state_s[u] for u in range(n_units)]
        for g in range(CHUNK_GROUP):
            Qg, GHg = Q[g * n_units:(g + 1) * n_units], GH[g * n_units:(g + 1) * n_units]
            M_split = [_split2(m) for m in M]
            G_split = [_split2(gh[:, :N]) for gh in GHg]
            prod = [_dot(jnp.concatenate([q[:, :N].astype(BF16), g_hi, g_lo], axis=0), m_hi)
                    for q, (g_hi, g_lo), (m_hi, _) in zip(Qg, G_split, M_split)]
            out = [p[:L] + q[:, N:] for p, q in zip(prod, Qg)]
            M = [p[L:L + N] + (p[L + N:] + _dot(g_hi, m_lo)) + gh[:, N:]
                 for p, (g_hi, _), (_, m_lo), gh in zip(prod, G_split, M_split, GHg)]
            for d in range(2):
                for h in range(RWKV_HEADS):
                    out_refs[d][0, rows[g][d], h * N:(h + 1) * N] = out[d * RWKV_HEADS + h]
        for u in range(n_units):
            state_s[u] = M[u]
        return carry

    lax.fori_loop(0, n_chunks // CHUNK_GROUP, group_body, 0)


def _rwkv(z_rwkv, w0, w2_p, a0, a2_p, g2_p, k_k, k_a, r_k, seg, tri, tm):
    B, S, ZC = z_rwkv.shape
    C = RWKV_WIDTH
    n_tiles = S // tm
    fwd = lambda b, i: (b, i, 0)
    bwd = lambda b, i: (b, n_tiles - 1 - i, 0)
    out_f = pl.BlockSpec((1, tm, C), fwd)
    out_b = pl.BlockSpec((1, tm, C), bwd)
    out_sds = lambda dtype: jax.ShapeDtypeStruct((B, S, C), dtype)
    return pl.pallas_call(
        _rwkv_kernel,
        grid=(B, n_tiles),
        in_specs=[pl.BlockSpec((1, tm, ZC), fwd),
                  pl.BlockSpec((1, tm, ZC), bwd),
                  _const_spec(w0.shape), _const_spec(w2_p.shape),
                  _const_spec(a0.shape), _const_spec(a2_p.shape), _const_spec(g2_p.shape),
                  _const_spec(k_k.shape), _const_spec(k_a.shape), _const_spec(r_k.shape),
                  _const_spec(seg.shape), _const_spec(tri.shape)],
        out_specs=[out_f, out_b, out_f, out_b, out_f],
        out_shape=[out_sds(F32), out_sds(F32), out_sds(BF16), out_sds(BF16), out_sds(BF16)],
        scratch_shapes=[pltpu.VMEM((2, tm, C), F32)] * 6
                       + [pltpu.VMEM((2 * RWKV_HEADS, HEAD_DIM, HEAD_DIM), F32)],
        compiler_params=_params("parallel", "arbitrary"),
        name="rwkv",
    )(z_rwkv, z_rwkv, w0, w2_p, a0, a2_p, g2_p, k_k, k_a, r_k, seg, tri)


def _rope_tiles(pos_row, freq_col, place_cos, place_sin, base):
    ang = freq_col * pos_row.astype(F32)

    def place(t, p):
        return sum(lax.dot_general(part, p, (((0,), (0,)), ((), ())), preferred_element_type=F32)
                   for part in _split3(t))

    return place(jnp.cos(ang), place_cos) + base, place(jnp.sin(ang), place_sin)


def _mla_kernel(zq_ref, zkv_ref, pos_ref, freq_ref, pcos_ref, psin_ref, base_ref, qn_ref, wq_ref, kvn_ref, wkv_ref,
                y_ref, k_s, v_s, rot_s):
    i = pl.program_id(1)
    tq = zq_ref.shape[1]
    scale = (QK_NOPE_DIM + QK_ROPE_DIM) ** -0.5
    c_kv_lo, c_kv_hi = Q_LORA_RANK, Q_LORA_RANK + KV_LORA_RANK

    @pl.when(i == 0)
    def _():
        zkv = zkv_ref[0]
        kvn = _rms(zkv[:, c_kv_lo:c_kv_hi], kvn_ref[...]).astype(BF16)
        kvu = jnp.dot(kvn, wkv_ref[...], preferred_element_type=F32)
        cos_t, sin_t = _rope_tiles(pos_ref[0], freq_ref[...], pcos_ref[...], psin_ref[...], base_ref[...])
        rot_s[...] = cos_t + pltpu.roll(sin_t, QK_ROPE_DIM, 1)
        k_rot = zkv[:, c_kv_hi:c_kv_hi + LANES] * cos_t + zkv[:, c_kv_hi + LANES:] * sin_t
        k_rot = k_rot + pltpu.roll(k_rot, QK_ROPE_DIM, 1)
        lane = lax.broadcasted_iota(jnp.int32, (1, LANES), 1)
        ones_col = (lane == V_HEAD_DIM).astype(F32)
        for h in range(MLA_HEADS):
            k_s[h] = (kvu[:, h * LANES:(h + 1) * LANES] + k_rot).astype(BF16)
            v_s[h] = (kvu[:, (MLA_HEADS + h) * LANES:(MLA_HEADS + h + 1) * LANES] + ones_col).astype(BF16)

    zq = zq_ref[0]
    qn = _rms(zq[:, :Q_LORA_RANK], qn_ref[...]).astype(BF16)
    q = jnp.dot(qn, wq_ref[...], preferred_element_type=F32)
    rows = pl.ds(pl.multiple_of(i * tq, tq), tq)
    rot_q = rot_s[rows, :] * (scale * LOG2_E)

    def scores(h):
        sl = slice(h * LANES, (h + 1) * LANES)
        qh = (q[:, sl] * rot_q).astype(BF16)
        return lax.dot_general(qh, k_s[h], (((1,), (1,)), ((), ())), preferred_element_type=F32)

    s = scores(0)
    for h in range(MLA_HEADS):
        s_next = scores(h + 1) if h + 1 < MLA_HEADS else None
        p = jnp.exp2((s - jnp.max(s, axis=-1, keepdims=True)).astype(BF16))
        o = jnp.dot(p, v_s[h], preferred_element_type=F32)
        y_ref[0, :, h * V_HEAD_DIM:(h + 1) * V_HEAD_DIM] = (
            o[:, :V_HEAD_DIM] / o[:, V_HEAD_DIM:V_HEAD_DIM + 1]).astype(y_ref.dtype)
        s = s_next


def _mla_attn(z_mla, pos_row, freq_col, place_cos, place_sin, base, q_norm, wq_p, kv_norm, wkv_p, tq):
    B, S, ZC = z_mla.shape
    return pl.pallas_call(
        _mla_kernel,
        grid=(B, S // tq),
        in_specs=[pl.BlockSpec((1, tq, ZC), lambda b, i: (b, i, 0)),
                  pl.BlockSpec((1, S, ZC), lambda b, i: (b, 0, 0)),
                  pl.BlockSpec((1, 1, S), lambda b, i: (b, 0, 0)),
                  _const_spec(freq_col.shape), _const_spec(place_cos.shape), _const_spec(place_sin.shape),
                  _const_spec(base.shape), _const_spec(q_norm.shape),
                  _const_spec(wq_p.shape), _const_spec(kv_norm.shape),
                  _const_spec(wkv_p.shape)],
        out_specs=pl.BlockSpec((1, tq, MLA_WIDTH), lambda b, i: (b, i, 0)),
        out_shape=jax.ShapeDtypeStruct((B, S, MLA_WIDTH), BF16),
        scratch_shapes=[pltpu.VMEM((MLA_HEADS, S, LANES), BF16), pltpu.VMEM((MLA_HEADS, S, LANES), BF16),
                        pltpu.VMEM((S, LANES), F32)],
        compiler_params=_params("parallel", "arbitrary"),
        name="mla_attn",
    )(z_mla, z_mla, pos_row, freq_col, place_cos, place_sin, base, q_norm, wq_p, kv_norm, wkv_p)


def _mix_out_rows(x, o_f, o_b, bonus, gate, lnw, lnb, segm, y_mla, w_r, w_m, g):
    o = o_f + o_b
    o_hi, o_lo = _split2(o)
    oc = o - (_dot(o_hi, segm) + _dot(o_lo, segm))
    var = _dot((oc * oc).astype(BF16), segm)
    o = oc * lax.rsqrt(var + LN_X_EPS) * lnw + lnb
    y_rwkv = (o + bonus) * gate
    y = _dot(y_rwkv.astype(BF16), w_r) + _dot(y_mla, w_m)
    return x + _rms(y, g)


def _mem_attn_rows(x, k_ref, v_ref, g_pre, wq, wo, g_post, att_s):
    h = _rms(x, g_pre).astype(BF16)
    q = (_dot(h, wq) * (MEM_HEAD_DIM ** -0.5 * LOG2_E)).astype(BF16)

    def scores(hd):
        sl = slice(hd * MEM_HEAD_DIM, (hd + 1) * MEM_HEAD_DIM)
        return lax.dot_general(q[:, sl], k_ref[:, sl], (((1,), (1,)), ((), ())), preferred_element_type=F32)

    s = scores(0)
    for hd in range(MEM_HEADS):
        sl = slice(hd * MEM_HEAD_DIM, (hd + 1) * MEM_HEAD_DIM)
        s_next = scores(hd + 1) if hd + 1 < MEM_HEADS else None
        p = jnp.exp2(s - jnp.max(s, axis=-1, keepdims=True))
        denom = jnp.sum(p, axis=-1, keepdims=True)
        o = _dot(p.astype(BF16), v_ref[:, sl])
        att_s[:, sl] = (o / denom).astype(BF16)
        s = s_next
    return x + _rms(_dot(att_s[...], wo), g_post)


def _mix_mem_kernel(x_ref, of_ref, ob_ref, bf_ref, bb_ref, gate_ref, lnw_ref, lnb_ref, segm_ref, ym_ref,
                    wr_ref, wm_ref, gmix_ref, mem_ref, gmem_ref, wkv_ref, gpre_ref, wq_ref, wo_ref, gpost_ref,
                    o_ref, k_s, v_s, att_s):
    @pl.when(pl.program_id(1) == 0)
    def _():
        m = _rms(mem_ref[0], gmem_ref[...]).astype(BF16)
        kv = _dot(m, wkv_ref[...])
        k_s[...] = kv[:, :D_MODEL].astype(BF16)
        v_s[...] = kv[:, D_MODEL:].astype(BF16)

    bonus = bf_ref[0].astype(F32) + bb_ref[0].astype(F32)
    x = _mix_out_rows(x_ref[0], of_ref[0], ob_ref[0], bonus, gate_ref[0].astype(F32), lnw_ref[...], lnb_ref[...],
                      segm_ref[...], ym_ref[0], wr_ref[...], wm_ref[...], gmix_ref[...])
    o_ref[0] = _mem_attn_rows(x, k_s, v_s, gpre_ref[...], wq_ref[...], wo_ref[...], gpost_ref[...], att_s)


def _mix_mem(x, o_f, o_b, bonus_f, bonus_b, gate, lnw, lnb, segm, y_mla, w_r, w_m, g_mix,
             mem, g_mem, wkv, g_pre, wq, wo, g_post, tm):
    B, S, D = x.shape
    T = mem.shape[1]
    C = RWKV_WIDTH
    row_spec = lambda cols: pl.BlockSpec((1, tm, cols), lambda b, i: (b, i, 0))
    return pl.pallas_call(
        _mix_mem_kernel,
        grid=(B, S // tm),
        in_specs=[row_spec(D), row_spec(C), row_spec(C), row_spec(C), row_spec(C), row_spec(C)]
                 + [_const_spec(t.shape) for t in (lnw, lnb, segm)] + [row_spec(MLA_WIDTH)]
                 + [_const_spec(t.shape) for t in (w_r, w_m, g_mix)]
                 + [pl.BlockSpec((1, T, D), lambda b, i: (b, 0, 0))]
                 + [_const_spec(t.shape) for t in (g_mem, wkv, g_pre, wq, wo, g_post)],
        out_specs=row_spec(D),
        out_shape=jax.ShapeDtypeStruct((B, S, D), F32),
        scratch_shapes=[pltpu.VMEM((T, D), BF16), pltpu.VMEM((T, D), BF16), pltpu.VMEM((tm, D), BF16)],
        compiler_params=_params("parallel", "arbitrary"),
        name="mix_mem",
    )(x, o_f, o_b, bonus_f, bonus_b, gate, lnw, lnb, segm, y_mla, w_r, w_m, g_mix,
      mem, g_mem, wkv, g_pre, wq, wo, g_post)


def _mlp_kernel(x_ref, gpre_ref, w1_ref, w2_ref, gpost_ref, o_ref):
    x = x_ref[0]
    h = _rms(x, gpre_ref[...]).astype(BF16)
    u = jnp.maximum(jnp.dot(h, w1_ref[...], preferred_element_type=F32), 0.0)
    y = jnp.dot((u * u).astype(BF16), w2_ref[...], preferred_element_type=F32)
    o_ref[0] = x + _rms(y, gpost_ref[...])


def _mlp(x, g_pre, w1, w2, g_post, tm):
    B, S, D = x.shape
    row_spec = pl.BlockSpec((1, tm, D), lambda b, i: (b, i, 0))
    return pl.pallas_call(
        _mlp_kernel,
        grid=(B, S // tm),
        in_specs=[row_spec, _const_spec(g_pre.shape), _const_spec(w1.shape), _const_spec(w2.shape),
                  _const_spec(g_post.shape)],
        out_specs=row_spec,
        out_shape=jax.ShapeDtypeStruct((B, S, D), F32),
        compiler_params=_params("parallel", "parallel"),
        name="mlp",
    )(x, g_pre, w1, w2, g_post)


def _pad_cols(w, n):
    return jnp.pad(w, ((0, 0), (0, n - w.shape[1])))


def _rope_tile(w):
    return jnp.pad(w, ((0, 0), (QK_NOPE_DIM, LANES - QK_NOPE_DIM - QK_ROPE_DIM)))


def _swap_halves(w):
    half = QK_ROPE_DIM // 2
    return jnp.concatenate([w[:, half:], w[:, :half]], axis=1)


def _lora_rows(w, lo, rows):
    return jnp.pad(w, ((lo, LORA_TILE - lo - rows), (0, 0)))


def kernel(x, mem, positions, norm_mix_pre, w_in, conv_rwkv, rwkv_w0, rwkv_w2, rwkv_a0, rwkv_a2, rwkv_g2, rwkv_k_k, rwkv_k_a, rwkv_r_k, rwkv_lnx_w, rwkv_lnx_b, mla_q_norm, mla_w_uq, mla_kv_norm, mla_w_ukv, w_out, norm_mix_post, norm_mem_pre, norm_memtok, mem_wq, mem_wkv, mem_wo, norm_mem_post, norm_mlp_pre, mlp_w1, mlp_w2, norm_mlp_post):
    depth = w_in.shape[0]
    C = RWKV_WIDTH
    head_of = jnp.arange(C) // HEAD_DIM
    seg = (head_of[:, None] == head_of[None, :]).astype(BF16)
    seg_mean = (seg.astype(F32) / HEAD_DIM).astype(BF16)
    inv_freq = ROPE_THETA ** (-jnp.arange(0, QK_ROPE_DIM, 2, dtype=F32) / QK_ROPE_DIM)
    half = QK_ROPE_DIM // 2
    lane = jnp.arange(LANES)[None, :]
    f_idx = jnp.arange(half)[:, None]
    first, second = lane == QK_NOPE_DIM + f_idx, lane == QK_NOPE_DIM + half + f_idx
    place_cos = (first | second).astype(BF16)
    place_sin = (second.astype(F32) - first.astype(F32)).astype(BF16)
    rope_base = (lane < QK_NOPE_DIM).astype(F32)
    freq_col = inv_freq[:, None]
    pos_row = positions[:, None, :]
    step = jnp.arange(CHUNK)
    tri = jnp.stack([step[None, :] <= step[:, None], step[None, :] >= step[:, None]]).astype(F32)
    row = lambda t: t.reshape(1, -1)

    for l in range(depth):
        w = w_in[l]
        mla0 = RWKV_COLS
        w_rope = w[:, mla0 + Q_LORA_RANK + KV_LORA_RANK:]
        w_r = _pad_cols(w[:, :RWKV_COLS], RWKV_TILE_COLS).astype(BF16)
        w_m = jnp.concatenate([w[:, mla0:mla0 + Q_LORA_RANK + KV_LORA_RANK], _rope_tile(w_rope),
                               _rope_tile(_swap_halves(w_rope))], axis=1).astype(BF16)
        conv_p = _pad_cols(conv_rwkv[l], RWKV_TILE_COLS)
        z_rwkv, z_mla = _in_proj(x, row(norm_mix_pre[l]), w_r, w_m, conv_p, tm=512)

        w2_p = jnp.stack([_lora_rows(rwkv_w2[l, d], d * DECAY_LORA, DECAY_LORA) for d in range(2)])
        a_lo = 2 * DECAY_LORA
        a2_p = jnp.stack([_lora_rows(rwkv_a2[l, d], a_lo + d * ICLR_LORA, ICLR_LORA) for d in range(2)])
        g2_p = _lora_rows(rwkv_g2[l], a_lo + 2 * ICLR_LORA, GATE_LORA)
        o_f, o_b, bonus_f, bonus_b, gate = _rwkv(z_rwkv, rwkv_w0[l], w2_p, rwkv_a0[l], a2_p, g2_p,
                                                 row(rwkv_k_k[l]), row(rwkv_k_a[l]), row(rwkv_r_k[l]),
                                                 seg[:C // 2, :C // 2], tri, tm=512)

        qk = QK_NOPE_DIM + QK_ROPE_DIM
        w_uq = mla_w_uq[l].reshape(Q_LORA_RANK, MLA_HEADS, qk)
        uq_rope = w_uq[:, :, QK_NOPE_DIM:]
        uq_sw = jnp.concatenate([uq_rope[:, :, half:], uq_rope[:, :, :half]], axis=2)
        wq_p = jnp.concatenate([w_uq, uq_sw], axis=2).reshape(Q_LORA_RANK, MLA_HEADS * LANES)
        w_ukv = mla_w_ukv[l].reshape(KV_LORA_RANK, MLA_HEADS, QK_NOPE_DIM + V_HEAD_DIM)
        wk_p = jnp.pad(w_ukv[:, :, :QK_NOPE_DIM], ((0, 0), (0, 0), (0, LANES - QK_NOPE_DIM)))
        wv_p = jnp.pad(w_ukv[:, :, QK_NOPE_DIM:], ((0, 0), (0, 0), (0, LANES - V_HEAD_DIM)))
        wkv_p = jnp.concatenate([wk_p.reshape(KV_LORA_RANK, MLA_HEADS * LANES),
                                 wv_p.reshape(KV_LORA_RANK, MLA_HEADS * LANES)], axis=1)
        y_mla = _mla_attn(z_mla, pos_row, freq_col, place_cos, place_sin, rope_base, row(mla_q_norm[l]),
                          wq_p.astype(BF16), row(mla_kv_norm[l]), wkv_p.astype(BF16), tq=512)

        wo = w_out[l].astype(BF16)
        x = _mix_mem(x, o_f, o_b, bonus_f, bonus_b, gate, row(rwkv_lnx_w[l]), row(rwkv_lnx_b[l]), seg_mean,
                     y_mla, wo[:C], wo[C:], row(norm_mix_post[l]), mem, row(norm_memtok[l]),
                     mem_wkv[l].astype(BF16), row(norm_mem_pre[l]),
                     mem_wq[l].astype(BF16), mem_wo[l].astype(BF16), row(norm_mem_post[l]), tm=512)

        x = _mlp(x, row(norm_mlp_pre[l]), mlp_w1[l].astype(BF16), mlp_w2[l].astype(BF16),
                 row(norm_mlp_post[l]), tm=512)
    return x
```

```python
import math

import jax
import jax.numpy as jnp
from jax import lax
from jax.experimental import pallas as pl
from jax.experimental.pallas import tpu as pltpu

F32 = jnp.float32
BF16 = jnp.bfloat16

D_MODEL = 1024
NORM_EPS = 1e-6

RWKV_HEADS = 8
HEAD_DIM = 64
RWKV_WIDTH = RWKV_HEADS * HEAD_DIM
DECAY_LORA = 32
ICLR_LORA = 32
GATE_LORA = 96
LORA_COLS = 2 * DECAY_LORA + 2 * ICLR_LORA + GATE_LORA
LORA_TILE = 256
RWKV_COLS = 3 * RWKV_WIDTH + LORA_COLS
RWKV_TILE_COLS = 3 * RWKV_WIDTH + LORA_TILE
LN_X_EPS = 64e-5
CHUNK = 64
CHUNK_GROUP = 2
DECAY_SCALE = math.exp(-0.5)

MLA_HEADS = 8
QK_NOPE_DIM = 64
QK_ROPE_DIM = 32
V_HEAD_DIM = 64
MLA_WIDTH = MLA_HEADS * V_HEAD_DIM
Q_LORA_RANK = 256
KV_LORA_RANK = 128
ROPE_THETA = 10000.0
LOG2_E = math.log2(math.e)
LANES = 128
MLA_TILE_COLS = Q_LORA_RANK + KV_LORA_RANK + 2 * LANES

MEM_HEADS = 4
MEM_HEAD_DIM = D_MODEL // MEM_HEADS
D_FF = 4 * D_MODEL

VMEM_LIMIT = 56 * 1024 * 1024


def _mm(a, b):
    return jnp.dot(a.astype(BF16), b.astype(BF16), preferred_element_type=F32)


def _split2(t):
    hi = t.astype(BF16)
    lo = (t - hi.astype(F32)).astype(BF16)
    return hi, lo


def _split3(t):
    hi = t.astype(BF16)
    rest = t - hi.astype(F32)
    mid = rest.astype(BF16)
    lo = (rest - mid.astype(F32)).astype(BF16)
    return hi, mid, lo


def _dot(a, b):
    return jnp.dot(a, b, preferred_element_type=F32)


def _mm_exact_lhs(a, b):
    a = a.astype(BF16)
    hi, lo = _split2(b)
    return _dot(a, hi) + _dot(a, lo)


def _mm_nt(a, b):
    return lax.dot_general(a.astype(BF16), b.astype(BF16), (((1,), (1,)), ((), ())),
                           preferred_element_type=F32)


def _mm_tn(a, b):
    return lax.dot_general(a.astype(BF16), b.astype(BF16), (((0,), (0,)), ((), ())),
                           preferred_element_type=F32)


def _rms(x, g, eps=NORM_EPS):
    return x * lax.rsqrt(jnp.mean(x * x, axis=-1, keepdims=True) + eps) * g


def _sigmoid(x):
    return 0.5 * jnp.tanh(0.5 * x) + 0.5


def _params(*sem):
    return pltpu.CompilerParams(dimension_semantics=sem, vmem_limit_bytes=VMEM_LIMIT)


def _const_spec(shape):
    nd = len(shape)
    return pl.BlockSpec(shape, lambda *_: (0,) * nd)


def _in_proj_kernel(x_ref, xp_ref, xn_ref, g_ref, wr_ref, wm_ref, conv_ref, zr_ref, zm_ref):
    i = pl.program_id(1)
    last = pl.num_programs(1) - 1
    tm = x_ref.shape[1]
    g = g_ref[...]
    h = _rms(x_ref[0], g).astype(BF16)
    zm_ref[0] = jnp.dot(h, wm_ref[...], preferred_element_type=F32)
    z = jnp.dot(h, wr_ref[...], preferred_element_type=F32)
    halo = _rms(jnp.concatenate([xp_ref[0], xn_ref[0]], axis=0), g).astype(BF16)
    z_halo = jnp.dot(halo, wr_ref[...], preferred_element_type=F32)
    prev_row = jnp.where(i == 0, 0.0, z_halo[7:8, :])
    next_row = jnp.where(i == last, 0.0, z_halo[8:9, :])
    row = lax.broadcasted_iota(jnp.int32, (tm, 1), 0)
    z_dn = jnp.where(row == 0, prev_row, pltpu.roll(z, 1, 0))
    z_up = jnp.where(row == tm - 1, next_row, pltpu.roll(z, tm - 1, 0))
    zr_ref[0] = conv_ref[0:1, :] * z_dn + conv_ref[1:2, :] * z + conv_ref[2:3, :] * z_up


def _in_proj(x, g, w_r, w_m, conv_p, tm):
    B, S, D = x.shape
    halo = tm // 8
    n_halo = S // 8
    return pl.pallas_call(
        _in_proj_kernel,
        grid=(B, S // tm),
        in_specs=[pl.BlockSpec((1, tm, D), lambda b, i: (b, i, 0)),
                  pl.BlockSpec((1, 8, D), lambda b, i: (b, jnp.maximum(i * halo - 1, 0), 0)),
                  pl.BlockSpec((1, 8, D), lambda b, i: (b, jnp.minimum((i + 1) * halo, n_halo - 1), 0)),
                  _const_spec((1, D)),
                  _const_spec(w_r.shape),
                  _const_spec(w_m.shape),
                  _const_spec(conv_p.shape)],
        out_specs=[pl.BlockSpec((1, tm, RWKV_TILE_COLS), lambda b, i: (b, i, 0)),
                   pl.BlockSpec((1, tm, MLA_TILE_COLS), lambda b, i: (b, i, 0))],
        out_shape=[jax.ShapeDtypeStruct((B, S, RWKV_TILE_COLS), F32),
                   jax.ShapeDtypeStruct((B, S, MLA_TILE_COLS), F32)],
        compiler_params=_params("parallel", "parallel"),
        name="in_proj",
    )(x, x, x, g, w_r, w_m, conv_p)


def _chunk_units(units, eye, eye_hi):
    L, N = units[0][0].shape
    zeros = jnp.zeros((L, N), BF16)
    AA = [_mm_nt(jnp.concatenate([At, Rt], axis=0), jnp.concatenate([Bt, Kt], axis=0))
          for (At, Rt, Bt, Kt, *_) in units]
    A_a = [jnp.where(u[8], aa[:L, :], 0.0) for u, aa in zip(units, AA)]
    A_r = [jnp.where(u[9], aa[L:, :], 0.0) for u, aa in zip(units, AA)]
    AkV = [_mm(a.astype(BF16)[:, L:], u[6]) for u, a in zip(units, A_a)]
    low = lax.broadcasted_iota(jnp.int32, (L, 2 * L), 1) < L
    S = [jnp.where(low, a, eye_hi) for a in A_a]
    span = 1
    while span < L:
        R = [_mm(s[:, :L], s) for s in S]
        S = [jnp.where(low, r, r + s) for r, s in zip(R, S)]
        span *= 2
    W = [_mm(s.astype(BF16)[:, L:], jnp.concatenate([u[0], akv.astype(BF16)], axis=1))
         for s, u, akv in zip(S, units, AkV)]
    Z = [jnp.concatenate([w.astype(BF16), jnp.concatenate([zeros, u[6]], axis=1)], axis=0)
         for u, w in zip(units, W)]
    lhs = [jnp.concatenate([a_r.astype(BF16), jnp.concatenate([u[4], u[5]], axis=0).T], axis=0)
           for u, a_r in zip(units, A_r)]
    QGH = [_mm(l, z) for l, z in zip(lhs, Z)]
    Q = [jnp.concatenate([u[1], zeros], axis=1) + t[:L] for u, t in zip(units, QGH)]
    GH = [jnp.concatenate([eye * u[7], zeros.astype(F32)], axis=1) + t[L:] for u, t in zip(units, QGH)]
    return Q, GH


def _rwkv_kernel(zf_ref, zb_ref,
                 w0_ref, w2_ref, a0_ref, a2_ref, g2_ref, kk_ref, ka_ref, rk_ref, seg_ref, tri_ref,
                 of_ref, ob_ref, bonus_f_ref, bonus_b_ref, g_ref,
                 r_s, v_s, kk_s, b_s, kd_s, lw_s, state_s):
    i = pl.program_id(1)
    n_tiles = pl.num_programs(1)
    tm = zf_ref.shape[1]
    L, N = CHUNK, HEAD_DIM
    n_chunks = tm // L

    @pl.when(i == 0)
    def _():
        state_s[...] = jnp.zeros_like(state_s)

    seg = seg_ref[...]

    def seg_sum(t):
        t = t.astype(BF16)
        half = seg.shape[0]
        return jnp.concatenate([_dot(t[:, :half], seg), _dot(t[:, half:], seg)], axis=1)

    for d, (z_ref, bonus_ref) in enumerate(((zf_ref, bonus_f_ref), (zb_ref, bonus_b_ref))):
        r = z_ref[0, :, 0:RWKV_WIDTH]
        k = z_ref[0, :, RWKV_WIDTH:2 * RWKV_WIDTH]
        v = z_ref[0, :, 2 * RWKV_WIDTH:3 * RWKV_WIDTH]
        lora = z_ref[0, :, 3 * RWKV_WIDTH:]
        kk = k * kk_ref[...]
        kk = kk * lax.rsqrt(jnp.maximum(seg_sum(kk * kk), 1e-24))
        w_pre = w0_ref[d:d + 1, :] + _mm(jnp.tanh(lora), w2_ref[d])
        lw_s[d] = -DECAY_SCALE * _sigmoid(w_pre)
        alpha = _sigmoid(a0_ref[d:d + 1, :] + _mm(lora, a2_ref[d]))
        kd = k * (1.0 + (alpha - 1.0) * ka_ref[...])
        r_s[d] = r
        v_s[d] = v
        kk_s[d] = kk
        kd_s[d] = kd
        b_s[d] = kk * alpha
        bonus_ref[0] = (seg_sum(r * kd * rk_ref[...]) * v).astype(bonus_ref.dtype)
        if d == 0:
            g_ref[0] = _mm(_sigmoid(lora), g2_ref[...]).astype(g_ref.dtype)

    ri = lax.broadcasted_iota(jnp.int32, (L, L), 0)
    ci = lax.broadcasted_iota(jnp.int32, (L, L), 1)
    eye = (ci == ri).astype(F32)
    ri2 = lax.broadcasted_iota(jnp.int32, (L, 2 * L), 0)
    ci2 = lax.broadcasted_iota(jnp.int32, (L, 2 * L), 1)
    eye_hi = (ci2 == ri2 + L).astype(F32)
    ci2 = jnp.where(ci2 >= L, ci2 - L, ci2)
    m_strict = (ci2 < ri2, ci2 > ri2)
    m_incl = (ci2 <= ri2, ci2 >= ri2)
    out_refs = (of_ref, ob_ref)

    def chunk_units(c):
        rows_d = (pl.ds(pl.multiple_of(c * L, L), L), pl.ds(pl.multiple_of((n_chunks - 1 - c) * L, L), L))
        units = []
        for d in range(2):
            rows = rows_d[d]
            lw = lw_s[d, rows, :]
            bc = b_s[d, rows, :]
            kdc = kd_s[d, rows, :]
            vc = v_s[d, rows, :]
            cum = _mm_exact_lhs(tri_ref[d], lw)
            tot = cum[L - 1:L, :] if d == 0 else cum[0:1, :]
            e_in = jnp.exp(-cum)
            e_rem = jnp.exp(tot - cum)
            e_tot = jnp.exp(tot)
            At = (-kk_s[d, rows, :] * jnp.exp(cum - lw)).astype(BF16)
            Rt = (r_s[d, rows, :] * jnp.exp(cum)).astype(BF16)
            Bt = (bc * e_in).astype(BF16)
            Kt = (kdc * e_in).astype(BF16)
            Bh = (bc * e_rem).astype(BF16)
            Kh = (kdc * e_rem).astype(BF16)
            vc = vc.astype(BF16)
            for h in range(RWKV_HEADS):
                sl = slice(h * HEAD_DIM, (h + 1) * HEAD_DIM)
                units.append((At[:, sl], Rt[:, sl], Bt[:, sl], Kt[:, sl], Bh[:, sl], Kh[:, sl],
                              vc[:, sl], e_tot[:, sl], m_strict[d], m_incl[d]))
        return rows_d, units

    n_units = 2 * RWKV_HEADS

    def group_body(j, carry):
        rows, units = [], []
        for g in range(CHUNK_GROUP):
            rows_d, chunk = chunk_units(j * CHUNK_GROUP + g)
            rows.append(rows_d)
            units.extend(chunk)
        Q, GH = _chunk_units(units, eye, eye_hi)
        M = [state_s[u] for u in range(n_units)]
        for g in range(CHUNK_GROUP):
            sl = slice(g * n_units, (g + 1) * n_units)
            prod = [_mm(jnp.concatenate([q[:, :N], gh[:, :N]], axis=0), m) for q, gh, m in zip(Q[sl], GH[sl], M)]
            out = [p[:L] + q[:, N:] for p, q in zip(prod, Q[sl])]
            M = [p[L:] + gh[:, N:] for p, gh in zip(prod, GH[sl])]
            for d in range(2):
                for h in range(RWKV_HEADS):
                    out_refs[d][0, rows[g][d], h * N:(h + 1) * N] = out[d * RWKV_HEADS + h]
        for u in range(n_units):
            state_s[u] = M[u]
        return carry

    lax.fori_loop(0, n_chunks // CHUNK_GROUP, group_body, 0)


def _rwkv(z_rwkv, w0, w2_p, a0, a2_p, g2_p, k_k, k_a, r_k, seg, tri, tm):
    B, S, ZC = z_rwkv.shape
    C = RWKV_WIDTH
    n_tiles = S // tm
    fwd = lambda b, i: (b, i, 0)
    bwd = lambda b, i: (b, n_tiles - 1 - i, 0)
    out_f = pl.BlockSpec((1, tm, C), fwd)
    out_b = pl.BlockSpec((1, tm, C), bwd)
    out_sds = lambda dtype: jax.ShapeDtypeStruct((B, S, C), dtype)
    return pl.pallas_call(
        _rwkv_kernel,
        grid=(B, n_tiles),
        in_specs=[pl.BlockSpec((1, tm, ZC), fwd),
                  pl.BlockSpec((1, tm, ZC), bwd),
                  _const_spec(w0.shape), _const_spec(w2_p.shape),
                  _const_spec(a0.shape), _const_spec(a2_p.shape), _const_spec(g2_p.shape),
                  _const_spec(k_k.shape), _const_spec(k_a.shape), _const_spec(r_k.shape),
                  _const_spec(seg.shape), _const_spec(tri.shape)],
        out_specs=[out_f, out_b, out_f, out_b, out_f],
        out_shape=[out_sds(F32), out_sds(F32), out_sds(BF16), out_sds(BF16), out_sds(BF16)],
        scratch_shapes=[pltpu.VMEM((2, tm, C), F32)] * 6
                       + [pltpu.VMEM((2 * RWKV_HEADS, HEAD_DIM, HEAD_DIM), F32)],
        compiler_params=_params("parallel", "arbitrary"),
        name="rwkv",
    )(z_rwkv, z_rwkv, w0, w2_p, a0, a2_p, g2_p, k_k, k_a, r_k, seg, tri)


def _rope_tiles(pos_row, freq_col, place_cos, place_sin, base):
    ang = freq_col * pos_row.astype(F32)

    def place(t, p):
        return sum(lax.dot_general(part, p, (((0,), (0,)), ((), ())), preferred_element_type=F32)
                   for part in _split3(t))

    return place(jnp.cos(ang), place_cos) + base, place(jnp.sin(ang), place_sin)


def _mla_kernel(zq_ref, zkv_ref, pos_ref, freq_ref, pcos_ref, psin_ref, base_ref, qn_ref, wq_ref, kvn_ref, wkv_ref,
                y_ref, k_s, v_s, rot_s):
    i = pl.program_id(1)
    tq = zq_ref.shape[1]
    scale = (QK_NOPE_DIM + QK_ROPE_DIM) ** -0.5
    c_kv_lo, c_kv_hi = Q_LORA_RANK, Q_LORA_RANK + KV_LORA_RANK

    @pl.when(i == 0)
    def _():
        zkv = zkv_ref[0]
        kvn = _rms(zkv[:, c_kv_lo:c_kv_hi], kvn_ref[...]).astype(BF16)
        kvu = jnp.dot(kvn, wkv_ref[...], preferred_element_type=F32)
        cos_t, sin_t = _rope_tiles(pos_ref[0], freq_ref[...], pcos_ref[...], psin_ref[...], base_ref[...])
        rot_s[...] = cos_t + pltpu.roll(sin_t, QK_ROPE_DIM, 1)
        k_rot = zkv[:, c_kv_hi:c_kv_hi + LANES] * cos_t + zkv[:, c_kv_hi + LANES:] * sin_t
        k_rot = k_rot + pltpu.roll(k_rot, QK_ROPE_DIM, 1)
        lane = lax.broadcasted_iota(jnp.int32, (1, LANES), 1)
        ones_col = (lane == V_HEAD_DIM).astype(F32)
        for h in range(MLA_HEADS):
            k_s[h] = (kvu[:, h * LANES:(h + 1) * LANES] + k_rot).astype(BF16)
            v_s[h] = (kvu[:, (MLA_HEADS + h) * LANES:(MLA_HEADS + h + 1) * LANES] + ones_col).astype(BF16)

    zq = zq_ref[0]
    qn = _rms(zq[:, :Q_LORA_RANK], qn_ref[...]).astype(BF16)
    q = jnp.dot(qn, wq_ref[...], preferred_element_type=F32)
    rows = pl.ds(pl.multiple_of(i * tq, tq), tq)
    rot_q = rot_s[rows, :] * (scale * LOG2_E)

    def scores(h):
        sl = slice(h * LANES, (h + 1) * LANES)
        qh = (q[:, sl] * rot_q).astype(BF16)
        return lax.dot_general(qh, k_s[h], (((1,), (1,)), ((), ())), preferred_element_type=F32)

    s = scores(0)
    for h in range(MLA_HEADS):
        s_next = scores(h + 1) if h + 1 < MLA_HEADS else None
        p = jnp.exp2((s - jnp.max(s, axis=-1, keepdims=True)).astype(BF16))
        o = jnp.dot(p, v_s[h], preferred_element_type=F32)
        y_ref[0, :, h * V_HEAD_DIM:(h + 1) * V_HEAD_DIM] = (
            o[:, :V_HEAD_DIM] / o[:, V_HEAD_DIM:V_HEAD_DIM + 1]).astype(y_ref.dtype)
        s = s_next


def _mla_attn(z_mla, pos_row, freq_col, place_cos, place_sin, base, q_norm, wq_p, kv_norm, wkv_p, tq):
    B, S, ZC = z_mla.shape
    return pl.pallas_call(
        _mla_kernel,
        grid=(B, S // tq),
        in_specs=[pl.BlockSpec((1, tq, ZC), lambda b, i: (b, i, 0)),
                  pl.BlockSpec((1, S, ZC), lambda b, i: (b, 0, 0)),
                  pl.BlockSpec((1, 1, S), lambda b, i: (b, 0, 0)),
                  _const_spec(freq_col.shape), _const_spec(place_cos.shape), _const_spec(place_sin.shape),
                  _const_spec(base.shape), _const_spec(q_norm.shape),
                  _const_spec(wq_p.shape), _const_spec(kv_norm.shape),
                  _const_spec(wkv_p.shape)],
        out_specs=pl.BlockSpec((1, tq, MLA_WIDTH), lambda b, i: (b, i, 0)),
        out_shape=jax.ShapeDtypeStruct((B, S, MLA_WIDTH), BF16),
        scratch_shapes=[pltpu.VMEM((MLA_HEADS, S, LANES), BF16), pltpu.VMEM((MLA_HEADS, S, LANES), BF16),
                        pltpu.VMEM((S, LANES), F32)],
        compiler_params=_params("parallel", "arbitrary"),
        name="mla_attn",
    )(z_mla, z_mla, pos_row, freq_col, place_cos, place_sin, base, q_norm, wq_p, kv_norm, wkv_p)


def _mix_out_rows(x, o_f, o_b, bonus, gate, lnw, lnb, segm, y_mla, w_r, w_m, g):
    o = o_f + o_b
    o_hi, o_lo = _split2(o)
    oc = o - (_dot(o_hi, segm) + _dot(o_lo, segm))
    var = _dot((oc * oc).astype(BF16), segm)
    o = oc * lax.rsqrt(var + LN_X_EPS) * lnw + lnb
    y_rwkv = (o + bonus) * gate
    y = _dot(y_rwkv.astype(BF16), w_r) + _dot(y_mla, w_m)
    return x + _rms(y, g)


def _mem_attn_rows(x, k_ref, v_ref, g_pre, wq, wo, g_post, att_s):
    h = _rms(x, g_pre).astype(BF16)
    q = (_dot(h, wq) * (MEM_HEAD_DIM ** -0.5 * LOG2_E)).astype(BF16)

    def scores(hd):
        sl = slice(hd * MEM_HEAD_DIM, (hd + 1) * MEM_HEAD_DIM)
        return lax.dot_general(q[:, sl], k_ref[:, sl], (((1,), (1,)), ((), ())), preferred_element_type=F32)

    s = scores(0)
    for hd in range(MEM_HEADS):
        sl = slice(hd * MEM_HEAD_DIM, (hd + 1) * MEM_HEAD_DIM)
        s_next = scores(hd + 1) if hd + 1 < MEM_HEADS else None
        p = jnp.exp2(s - jnp.max(s, axis=-1, keepdims=True))
        denom = jnp.sum(p, axis=-1, keepdims=True)
        o = _dot(p.astype(BF16), v_ref[:, sl])
        att_s[:, sl] = (o / denom).astype(BF16)
        s = s_next
    return x + _rms(_dot(att_s[...], wo), g_post)


def _mix_mem_kernel(x_ref, of_ref, ob_ref, bf_ref, bb_ref, gate_ref, lnw_ref, lnb_ref, segm_ref, ym_ref,
                    wr_ref, wm_ref, gmix_ref, mem_ref, gmem_ref, wkv_ref, gpre_ref, wq_ref, wo_ref, gpost_ref,
                    o_ref, k_s, v_s, att_s):
    @pl.when(pl.program_id(1) == 0)
    def _():
        m = _rms(mem_ref[0], gmem_ref[...]).astype(BF16)
        kv = _dot(m, wkv_ref[...])
        k_s[...] = kv[:, :D_MODEL].astype(BF16)
        v_s[...] = kv[:, D_MODEL:].astype(BF16)

    bonus = bf_ref[0].astype(F32) + bb_ref[0].astype(F32)
    x = _mix_out_rows(x_ref[0], of_ref[0], ob_ref[0], bonus, gate_ref[0].astype(F32), lnw_ref[...], lnb_ref[...],
                      segm_ref[...], ym_ref[0], wr_ref[...], wm_ref[...], gmix_ref[...])
    o_ref[0] = _mem_attn_rows(x, k_s, v_s, gpre_ref[...], wq_ref[...], wo_ref[...], gpost_ref[...], att_s)


def _mix_mem(x, o_f, o_b, bonus_f, bonus_b, gate, lnw, lnb, segm, y_mla, w_r, w_m, g_mix,
             mem, g_mem, wkv, g_pre, wq, wo, g_post, tm):
    B, S, D = x.shape
    T = mem.shape[1]
    C = RWKV_WIDTH
    row_spec = lambda cols: pl.BlockSpec((1, tm, cols), lambda b, i: (b, i, 0))
    return pl.pallas_call(
        _mix_mem_kernel,
        grid=(B, S // tm),
        in_specs=[row_spec(D), row_spec(C), row_spec(C), row_spec(C), row_spec(C), row_spec(C)]
                 + [_const_spec(t.shape) for t in (lnw, lnb, segm)] + [row_spec(MLA_WIDTH)]
                 + [_const_spec(t.shape) for t in (w_r, w_m, g_mix)]
                 + [pl.BlockSpec((1, T, D), lambda b, i: (b, 0, 0))]
                 + [_const_spec(t.shape) for t in (g_mem, wkv, g_pre, wq, wo, g_post)],
        out_specs=row_spec(D),
        out_shape=jax.ShapeDtypeStruct((B, S, D), F32),
        scratch_shapes=[pltpu.VMEM((T, D), BF16), pltpu.VMEM((T, D), BF16), pltpu.VMEM((tm, D), BF16)],
        compiler_params=_params("parallel", "arbitrary"),
        name="mix_mem",
    )(x, o_f, o_b, bonus_f, bonus_b, gate, lnw, lnb, segm, y_mla, w_r, w_m, g_mix,
      mem, g_mem, wkv, g_pre, wq, wo, g_post)


def _mlp_kernel(x_ref, gpre_ref, w1_ref, w2_ref, gpost_ref, o_ref):
    x = x_ref[0]
    h = _rms(x, gpre_ref[...]).astype(BF16)
    u = jnp.maximum(jnp.dot(h, w1_ref[...], preferred_element_type=F32), 0.0)
    y = jnp.dot((u * u).astype(BF16), w2_ref[...], preferred_element_type=F32)
    o_ref[0] = x + _rms(y, gpost_ref[...])


def _mlp(x, g_pre, w1, w2, g_post, tm):
    B, S, D = x.shape
    row_spec = pl.BlockSpec((1, tm, D), lambda b, i: (b, i, 0))
    return pl.pallas_call(
        _mlp_kernel,
        grid=(B, S // tm),
        in_specs=[row_spec, _const_spec(g_pre.shape), _const_spec(w1.shape), _const_spec(w2.shape),
                  _const_spec(g_post.shape)],
        out_specs=row_spec,
        out_shape=jax.ShapeDtypeStruct((B, S, D), F32),
        compiler_params=_params("parallel", "parallel"),
        name="mlp",
    )(x, g_pre, w1, w2, g_post)


def _pad_cols(w, n):
    return jnp.pad(w, ((0, 0), (0, n - w.shape[1])))


def _rope_tile(w):
    return jnp.pad(w, ((0, 0), (QK_NOPE_DIM, LANES - QK_NOPE_DIM - QK_ROPE_DIM)))


def _swap_halves(w):
    half = QK_ROPE_DIM // 2
    return jnp.concatenate([w[:, half:], w[:, :half]], axis=1)


def _lora_rows(w, lo, rows):
    return jnp.pad(w, ((lo, LORA_TILE - lo - rows), (0, 0)))


def kernel(x, mem, positions, norm_mix_pre, w_in, conv_rwkv, rwkv_w0, rwkv_w2, rwkv_a0, rwkv_a2, rwkv_g2, rwkv_k_k, rwkv_k_a, rwkv_r_k, rwkv_lnx_w, rwkv_lnx_b, mla_q_norm, mla_w_uq, mla_kv_norm, mla_w_ukv, w_out, norm_mix_post, norm_mem_pre, norm_memtok, mem_wq, mem_wkv, mem_wo, norm_mem_post, norm_mlp_pre, mlp_w1, mlp_w2, norm_mlp_post):
    depth = w_in.shape[0]
    C = RWKV_WIDTH
    head_of = jnp.arange(C) // HEAD_DIM
    seg = (head_of[:, None] == head_of[None, :]).astype(BF16)
    seg_mean = (seg.astype(F32) / HEAD_DIM).astype(BF16)
    inv_freq = ROPE_THETA ** (-jnp.arange(0, QK_ROPE_DIM, 2, dtype=F32) / QK_ROPE_DIM)
    half = QK_ROPE_DIM // 2
    lane = jnp.arange(LANES)[None, :]
    f_idx = jnp.arange(half)[:, None]
    first, second = lane == QK_NOPE_DIM + f_idx, lane == QK_NOPE_DIM + half + f_idx
    place_cos = (first | second).astype(BF16)
    place_sin = (second.astype(F32) - first.astype(F32)).astype(BF16)
    rope_base = (lane < QK_NOPE_DIM).astype(F32)
    freq_col = inv_freq[:, None]
    pos_row = positions[:, None, :]
    step = jnp.arange(CHUNK)
    tri = jnp.stack([step[None, :] <= step[:, None], step[None, :] >= step[:, None]]).astype(F32)
    row = lambda t: t.reshape(1, -1)

    for l in range(depth):
        w = w_in[l]
        mla0 = RWKV_COLS
        w_rope = w[:, mla0 + Q_LORA_RANK + KV_LORA_RANK:]
        w_r = _pad_cols(w[:, :RWKV_COLS], RWKV_TILE_COLS).astype(BF16)
        w_m = jnp.concatenate([w[:, mla0:mla0 + Q_LORA_RANK + KV_LORA_RANK], _rope_tile(w_rope),
                               _rope_tile(_swap_halves(w_rope))], axis=1).astype(BF16)
        conv_p = _pad_cols(conv_rwkv[l], RWKV_TILE_COLS)
        z_rwkv, z_mla = _in_proj(x, row(norm_mix_pre[l]), w_r, w_m, conv_p, tm=512)

        w2_p = jnp.stack([_lora_rows(rwkv_w2[l, d], d * DECAY_LORA, DECAY_LORA) for d in range(2)])
        a_lo = 2 * DECAY_LORA
        a2_p = jnp.stack([_lora_rows(rwkv_a2[l, d], a_lo + d * ICLR_LORA, ICLR_LORA) for d in range(2)])
        g2_p = _lora_rows(rwkv_g2[l], a_lo + 2 * ICLR_LORA, GATE_LORA)
        o_f, o_b, bonus_f, bonus_b, gate = _rwkv(z_rwkv, rwkv_w0[l], w2_p, rwkv_a0[l], a2_p, g2_p,
                                                 row(rwkv_k_k[l]), row(rwkv_k_a[l]), row(rwkv_r_k[l]),
                                                 seg[:C // 2, :C // 2], tri, tm=512)

        qk = QK_NOPE_DIM + QK_ROPE_DIM
        w_uq = mla_w_uq[l].reshape(Q_LORA_RANK, MLA_HEADS, qk)
        uq_rope = w_uq[:, :, QK_NOPE_DIM:]
        uq_sw = jnp.concatenate([uq_rope[:, :, half:], uq_rope[:, :, :half]], axis=2)
        wq_p = jnp.concatenate([w_uq, uq_sw], axis=2).reshape(Q_LORA_RANK, MLA_HEADS * LANES)
        w_ukv = mla_w_ukv[l].reshape(KV_LORA_RANK, MLA_HEADS, QK_NOPE_DIM + V_HEAD_DIM)
        wk_p = jnp.pad(w_ukv[:, :, :QK_NOPE_DIM], ((0, 0), (0, 0), (0, LANES - QK_NOPE_DIM)))
        wv_p = jnp.pad(w_ukv[:, :, QK_NOPE_DIM:], ((0, 0), (0, 0), (0, LANES - V_HEAD_DIM)))
        wkv_p = jnp.concatenate([wk_p.reshape(KV_LORA_RANK, MLA_HEADS * LANES),
                                 wv_p.reshape(KV_LORA_RANK, MLA_HEADS * LANES)], axis=1)
        y_mla = _mla_attn(z_mla, pos_row, freq_col, place_cos, place_sin, rope_base, row(mla_q_norm[l]),
                          wq_p.astype(BF16), row(mla_kv_norm[l]), wkv_p.astype(BF16), tq=512)

        wo = w_out[l].astype(BF16)
        x = _mix_mem(x, o_f, o_b, bonus_f, bonus_b, gate, row(rwkv_lnx_w[l]), row(rwkv_lnx_b[l]), seg_mean,
                     y_mla, wo[:C], wo[C:], row(norm_mix_post[l]), mem, row(norm_memtok[l]),
                     mem_wkv[l].astype(BF16), row(norm_mem_pre[l]),
                     mem_wq[l].astype(BF16), mem_wo[l].astype(BF16), row(norm_mem_post[l]), tm=512)

        x = _mlp(x, row(norm_mlp_pre[l]), mlp_w1[l].astype(BF16), mlp_w2[l].astype(BF16),
                 row(norm_mlp_post[l]), tm=512)
    return x
```

```python
import math

import jax
import jax.numpy as jnp
from jax import lax
from jax.experimental import pallas as pl
from jax.experimental.pallas import tpu as pltpu

F32 = jnp.float32
BF16 = jnp.bfloat16

D_MODEL = 1024
NORM_EPS = 1e-6

RWKV_HEADS = 8
HEAD_DIM = 64
RWKV_WIDTH = RWKV_HEADS * HEAD_DIM
DECAY_LORA = 32
ICLR_LORA = 32
GATE_LORA = 96
LORA_COLS = 2 * DECAY_LORA + 2 * ICLR_LORA + GATE_LORA
LORA_TILE = 256
RWKV_COLS = 3 * RWKV_WIDTH + LORA_COLS
RWKV_TILE_COLS = 3 * RWKV_WIDTH + LORA_TILE
LN_X_EPS = 64e-5
CHUNK = 64
CHUNK_GROUP = 2
DECAY_SCALE = math.exp(-0.5)

MLA_HEADS = 8
QK_NOPE_DIM = 64
QK_ROPE_DIM = 32
V_HEAD_DIM = 64
MLA_WIDTH = MLA_HEADS * V_HEAD_DIM
Q_LORA_RANK = 256
KV_LORA_RANK = 128
ROPE_THETA = 10000.0
LOG2_E = math.log2(math.e)
LANES = 128
MLA_TILE_COLS = Q_LORA_RANK + KV_LORA_RANK + LANES

MEM_HEADS = 4
MEM_HEAD_DIM = D_MODEL // MEM_HEADS
D_FF = 4 * D_MODEL

VMEM_LIMIT = 56 * 1024 * 1024


def _mm(a, b):
    return jnp.dot(a.astype(BF16), b.astype(BF16), preferred_element_type=F32)


def _split2(t):
    hi = t.astype(BF16)
    lo = (t - hi.astype(F32)).astype(BF16)
    return hi, lo


def _split3(t):
    hi = t.astype(BF16)
    rest = t - hi.astype(F32)
    mid = rest.astype(BF16)
    lo = (rest - mid.astype(F32)).astype(BF16)
    return hi, mid, lo


def _dot(a, b):
    return jnp.dot(a, b, preferred_element_type=F32)


def _seg_dot(t, seg):
    half = seg.shape[0]
    return jnp.concatenate([_dot(t[:, :half], seg), _dot(t[:, half:], seg)], axis=1)


def _mm_exact_lhs(a, b):
    a = a.astype(BF16)
    hi, lo = _split2(b)
    return _dot(a, hi) + _dot(a, lo)


def _mm_nt(a, b):
    return lax.dot_general(a.astype(BF16), b.astype(BF16), (((1,), (1,)), ((), ())),
                           preferred_element_type=F32)


def _mm_tn(a, b):
    return lax.dot_general(a.astype(BF16), b.astype(BF16), (((0,), (0,)), ((), ())),
                           preferred_element_type=F32)


def _rms(x, g, eps=NORM_EPS):
    return x * lax.rsqrt(jnp.mean(x * x, axis=-1, keepdims=True) + eps) * g


def _sigmoid(x):
    return 0.5 * jnp.tanh(0.5 * x) + 0.5


def _params(*sem):
    return pltpu.CompilerParams(dimension_semantics=sem, vmem_limit_bytes=VMEM_LIMIT)


def _const_spec(shape):
    nd = len(shape)
    return pl.BlockSpec(shape, lambda *_: (0,) * nd)


def _in_proj_kernel(x_ref, xp_ref, xn_ref, g_ref, wr_ref, wm_ref, conv_ref, zr_ref, zm_ref):
    i = pl.program_id(1)
    last = pl.num_programs(1) - 1
    tm = x_ref.shape[1]
    g = g_ref[...]
    h = _rms(x_ref[0], g).astype(BF16)
    zm_ref[0] = jnp.dot(h, wm_ref[...], preferred_element_type=F32)
    z = jnp.dot(h, wr_ref[...], preferred_element_type=F32)
    halo = _rms(jnp.concatenate([xp_ref[0], xn_ref[0]], axis=0), g).astype(BF16)
    z_halo = jnp.dot(halo, wr_ref[...], preferred_element_type=F32)
    prev_row = jnp.where(i == 0, 0.0, z_halo[7:8, :])
    next_row = jnp.where(i == last, 0.0, z_halo[8:9, :])
    row = lax.broadcasted_iota(jnp.int32, (tm, 1), 0)
    z_dn = jnp.where(row == 0, prev_row, pltpu.roll(z, 1, 0))
    z_up = jnp.where(row == tm - 1, next_row, pltpu.roll(z, tm - 1, 0))
    zr_ref[0] = conv_ref[0:1, :] * z_dn + conv_ref[1:2, :] * z + conv_ref[2:3, :] * z_up


def _in_proj(x, g, w_r, w_m, conv_p, tm):
    B, S, D = x.shape
    halo = tm // 8
    n_halo = S // 8
    return pl.pallas_call(
        _in_proj_kernel,
        grid=(B, S // tm),
        in_specs=[pl.BlockSpec((1, tm, D), lambda b, i: (b, i, 0)),
                  pl.BlockSpec((1, 8, D), lambda b, i: (b, jnp.maximum(i * halo - 1, 0), 0)),
                  pl.BlockSpec((1, 8, D), lambda b, i: (b, jnp.minimum((i + 1) * halo, n_halo - 1), 0)),
                  _const_spec((1, D)),
                  _const_spec(w_r.shape),
                  _const_spec(w_m.shape),
                  _const_spec(conv_p.shape)],
        out_specs=[pl.BlockSpec((1, tm, RWKV_TILE_COLS), lambda b, i: (b, i, 0)),
                   pl.BlockSpec((1, tm, MLA_TILE_COLS), lambda b, i: (b, i, 0))],
        out_shape=[jax.ShapeDtypeStruct((B, S, RWKV_TILE_COLS), F32),
                   jax.ShapeDtypeStruct((B, S, MLA_TILE_COLS), F32)],
        compiler_params=_params("parallel", "parallel"),
        name="in_proj",
    )(x, x, x, g, w_r, w_m, conv_p)


def _chunk_units(units, eye, eye_hi):
    L, N = units[0][0].shape
    zeros = jnp.zeros((L, N), BF16)
    AA = [_mm_nt(jnp.concatenate([At, Rt], axis=0), jnp.concatenate([Bt, Kt], axis=0))
          for (At, Rt, Bt, Kt, *_) in units]
    A_a = [jnp.where(u[8], aa[:L, :], 0.0) for u, aa in zip(units, AA)]
    A_r = [jnp.where(u[9], aa[L:, :], 0.0) for u, aa in zip(units, AA)]
    AkV = [_mm(a.astype(BF16)[:, L:], u[6]) for u, a in zip(units, A_a)]
    low = lax.broadcasted_iota(jnp.int32, (L, 2 * L), 1) < L
    S = [jnp.where(low, a, eye_hi) for a in A_a]
    span = 1
    while span < L:
        R = [_mm(s[:, :L], s) for s in S]
        S = [jnp.where(low, r, r + s) for r, s in zip(R, S)]
        span *= 2
    W = [_mm(s.astype(BF16)[:, L:], jnp.concatenate([u[0], akv.astype(BF16)], axis=1))
         for s, u, akv in zip(S, units, AkV)]
    Z = [jnp.concatenate([w.astype(BF16), jnp.concatenate([zeros, u[6]], axis=1)], axis=0)
         for u, w in zip(units, W)]
    lhs = [jnp.concatenate([a_r.astype(BF16), jnp.concatenate([u[4], u[5]], axis=0).T], axis=0)
           for u, a_r in zip(units, A_r)]
    QGH = [_mm(l, z) for l, z in zip(lhs, Z)]
    Q = [jnp.concatenate([u[1], zeros], axis=1) + t[:L] for u, t in zip(units, QGH)]
    GH = [jnp.concatenate([eye * u[7], zeros.astype(F32)], axis=1) + t[L:] for u, t in zip(units, QGH)]
    return Q, GH


def _rwkv_kernel(zf_ref, zb_ref,
                 w0_ref, w2_ref, a0_ref, a2_ref, g2_ref, kk_ref, ka_ref, rk_ref, seg_ref, tri_ref,
                 of_ref, ob_ref, bonus_f_ref, bonus_b_ref, g_ref,
                 r_s, v_s, kk_s, b_s, kd_s, lw_s, state_s):
    i = pl.program_id(1)
    n_tiles = pl.num_programs(1)
    tm = zf_ref.shape[1]
    L, N = CHUNK, HEAD_DIM
    n_chunks = tm // L

    @pl.when(i == 0)
    def _():
        state_s[...] = jnp.zeros_like(state_s)

    seg = seg_ref[...]

    def seg_sum(t):
        return _seg_dot(t.astype(BF16), seg)

    for d, (z_ref, bonus_ref) in enumerate(((zf_ref, bonus_f_ref), (zb_ref, bonus_b_ref))):
        r = z_ref[0, :, 0:RWKV_WIDTH]
        k = z_ref[0, :, RWKV_WIDTH:2 * RWKV_WIDTH]
        v = z_ref[0, :, 2 * RWKV_WIDTH:3 * RWKV_WIDTH]
        lora = z_ref[0, :, 3 * RWKV_WIDTH:]
        kk = k * kk_ref[...]
        kk = kk * lax.rsqrt(jnp.maximum(seg_sum(kk * kk), 1e-24))
        w_pre = w0_ref[d:d + 1, :] + _mm(jnp.tanh(lora), w2_ref[d])
        lw_s[d] = -DECAY_SCALE * _sigmoid(w_pre)
        alpha = _sigmoid(a0_ref[d:d + 1, :] + _mm(lora, a2_ref[d]))
        kd = k * (1.0 + (alpha - 1.0) * ka_ref[...])
        r_s[d] = r
        v_s[d] = v
        kk_s[d] = kk
        kd_s[d] = kd
        b_s[d] = kk * alpha
        bonus_ref[0] = (seg_sum(r * kd * rk_ref[...]) * v).astype(bonus_ref.dtype)
        if d == 0:
            g_ref[0] = _mm(_sigmoid(lora), g2_ref[...]).astype(g_ref.dtype)

    ri = lax.broadcasted_iota(jnp.int32, (L, L), 0)
    ci = lax.broadcasted_iota(jnp.int32, (L, L), 1)
    eye = (ci == ri).astype(F32)
    ri2 = lax.broadcasted_iota(jnp.int32, (L, 2 * L), 0)
    ci2 = lax.broadcasted_iota(jnp.int32, (L, 2 * L), 1)
    eye_hi = (ci2 == ri2 + L).astype(F32)
    ci2 = jnp.where(ci2 >= L, ci2 - L, ci2)
    m_strict = (ci2 < ri2, ci2 > ri2)
    m_incl = (ci2 <= ri2, ci2 >= ri2)
    out_refs = (of_ref, ob_ref)

    def chunk_units(c):
        rows_d = (pl.ds(pl.multiple_of(c * L, L), L), pl.ds(pl.multiple_of((n_chunks - 1 - c) * L, L), L))
        units = []
        for d in range(2):
            rows = rows_d[d]
            lw = lw_s[d, rows, :]
            bc = b_s[d, rows, :]
            kdc = kd_s[d, rows, :]
            vc = v_s[d, rows, :]
            cum = _mm_exact_lhs(tri_ref[d], lw)
            tot = cum[L - 1:L, :] if d == 0 else cum[0:1, :]
            e_in = jnp.exp(-cum)
            e_rem = jnp.exp(tot - cum)
            e_tot = jnp.exp(tot)
            At = (-kk_s[d, rows, :] * jnp.exp(cum - lw)).astype(BF16)
            Rt = (r_s[d, rows, :] * jnp.exp(cum)).astype(BF16)
            Bt = (bc * e_in).astype(BF16)
            Kt = (kdc * e_in).astype(BF16)
            Bh = (bc * e_rem).astype(BF16)
            Kh = (kdc * e_rem).astype(BF16)
            vc = vc.astype(BF16)
            for h in range(RWKV_HEADS):
                sl = slice(h * HEAD_DIM, (h + 1) * HEAD_DIM)
                units.append((At[:, sl], Rt[:, sl], Bt[:, sl], Kt[:, sl], Bh[:, sl], Kh[:, sl],
                              vc[:, sl], e_tot[:, sl], m_strict[d], m_incl[d]))
        return rows_d, units

    n_units = 2 * RWKV_HEADS

    def group_body(j, carry):
        rows, units = [], []
        for g in range(CHUNK_GROUP):
            rows_d, chunk = chunk_units(j * CHUNK_GROUP + g)
            rows.append(rows_d)
            units.extend(chunk)
        Q, GH = _chunk_units(units, eye, eye_hi)
        M = [state_s[u] for u in range(n_units)]
        for g in range(CHUNK_GROUP):
            sl = slice(g * n_units, (g + 1) * n_units)
            prod = [_mm(jnp.concatenate([q[:, :N], gh[:, :N]], axis=0), m) for q, gh, m in zip(Q[sl], GH[sl], M)]
            out = [p[:L] + q[:, N:] for p, q in zip(prod, Q[sl])]
            M = [p[L:] + gh[:, N:] for p, gh in zip(prod, GH[sl])]
            for d in range(2):
                for h in range(RWKV_HEADS):
                    out_refs[d][0, rows[g][d], h * N:(h + 1) * N] = out[d * RWKV_HEADS + h]
        for u in range(n_units):
            state_s[u] = M[u]
        return carry

    lax.fori_loop(0, n_chunks // CHUNK_GROUP, group_body, 0)


def _rwkv(z_rwkv, w0, w2_p, a0, a2_p, g2_p, k_k, k_a, r_k, seg, tri, tm):
    B, S, ZC = z_rwkv.shape
    C = RWKV_WIDTH
    n_tiles = S // tm
    fwd = lambda b, i: (b, i, 0)
    bwd = lambda b, i: (b, n_tiles - 1 - i, 0)
    out_f = pl.BlockSpec((1, tm, C), fwd)
    out_b = pl.BlockSpec((1, tm, C), bwd)
    out_sds = lambda dtype: jax.ShapeDtypeStruct((B, S, C), dtype)
    return pl.pallas_call(
        _rwkv_kernel,
        grid=(B, n_tiles),
        in_specs=[pl.BlockSpec((1, tm, ZC), fwd),
                  pl.BlockSpec((1, tm, ZC), bwd),
                  _const_spec(w0.shape), _const_spec(w2_p.shape),
                  _const_spec(a0.shape), _const_spec(a2_p.shape), _const_spec(g2_p.shape),
                  _const_spec(k_k.shape), _const_spec(k_a.shape), _const_spec(r_k.shape),
                  _const_spec(seg.shape), _const_spec(tri.shape)],
        out_specs=[out_f, out_b, out_f, out_b, out_f],
        out_shape=[out_sds(F32), out_sds(F32), out_sds(BF16), out_sds(BF16), out_sds(BF16)],
        scratch_shapes=[pltpu.VMEM((2, tm, C), F32)] * 6
                       + [pltpu.VMEM((2 * RWKV_HEADS, HEAD_DIM, HEAD_DIM), F32)],
        compiler_params=_params("parallel", "arbitrary"),
        name="rwkv",
    )(z_rwkv, z_rwkv, w0, w2_p, a0, a2_p, g2_p, k_k, k_a, r_k, seg, tri)


def _rope_tiles(pos_row, freq_col, place_cos, place_sin, base):
    ang = freq_col * pos_row.astype(F32)

    def place(t, p):
        return sum(lax.dot_general(part, p, (((0,), (0,)), ((), ())), preferred_element_type=F32)
                   for part in _split3(t))

    return place(jnp.cos(ang), place_cos) + base, place(jnp.sin(ang), place_sin)


def _mla_kernel(zq_ref, zkv_ref, pos_ref, freq_ref, pcos_ref, psin_ref, base_ref, qn_ref, wq_ref, kvn_ref, wkv_ref,
                y_ref, k_s, v_s, rot_s):
    i = pl.program_id(1)
    tq = zq_ref.shape[1]
    scale = (QK_NOPE_DIM + QK_ROPE_DIM) ** -0.5
    c_kv_lo, c_kv_hi = Q_LORA_RANK, Q_LORA_RANK + KV_LORA_RANK

    @pl.when(i == 0)
    def _():
        zkv = zkv_ref[0]
        kvn = _rms(zkv[:, c_kv_lo:c_kv_hi], kvn_ref[...]).astype(BF16)
        kvu = jnp.dot(kvn, wkv_ref[...], preferred_element_type=F32)
        cos_t, sin_t = _rope_tiles(pos_ref[0], freq_ref[...], pcos_ref[...], psin_ref[...], base_ref[...])
        rot_s[...] = cos_t + pltpu.roll(sin_t, QK_ROPE_DIM, 1)
        k_rope = zkv[:, c_kv_hi:]
        k_rot = k_rope * cos_t + pltpu.roll(k_rope, LANES - QK_ROPE_DIM, 1) * sin_t
        k_rot = k_rot + pltpu.roll(k_rot, QK_ROPE_DIM, 1)
        lane = lax.broadcasted_iota(jnp.int32, (1, LANES), 1)
        ones_col = (lane == V_HEAD_DIM).astype(F32)
        for h in range(MLA_HEADS):
            k_s[h] = (kvu[:, h * LANES:(h + 1) * LANES] + k_rot).astype(BF16)
            v_s[h] = (kvu[:, (MLA_HEADS + h) * LANES:(MLA_HEADS + h + 1) * LANES] + ones_col).astype(BF16)

    zq = zq_ref[0]
    qn = _rms(zq[:, :Q_LORA_RANK], qn_ref[...]).astype(BF16)
    q = jnp.dot(qn, wq_ref[...], preferred_element_type=F32)
    rows = pl.ds(pl.multiple_of(i * tq, tq), tq)
    rot_q = rot_s[rows, :] * (scale * LOG2_E)

    def scores(h):
        sl = slice(h * LANES, (h + 1) * LANES)
        qh = (q[:, sl] * rot_q).astype(BF16)
        return lax.dot_general(qh, k_s[h], (((1,), (1,)), ((), ())), preferred_element_type=F32)

    s = scores(0)
    for h in range(MLA_HEADS):
        s_next = scores(h + 1) if h + 1 < MLA_HEADS else None
        p = jnp.exp2((s - jnp.max(s, axis=-1, keepdims=True)).astype(BF16))
        o = jnp.dot(p, v_s[h], preferred_element_type=F32)
        y_ref[0, :, h * V_HEAD_DIM:(h + 1) * V_HEAD_DIM] = (
            o[:, :V_HEAD_DIM] / o[:, V_HEAD_DIM:V_HEAD_DIM + 1]).astype(y_ref.dtype)
        s = s_next


def _mla_attn(z_mla, pos_row, freq_col, place_cos, place_sin, base, q_norm, wq_p, kv_norm, wkv_p, tq):
    B, S, ZC = z_mla.shape
    return pl.pallas_call(
        _mla_kernel,
        grid=(B, S // tq),
        in_specs=[pl.BlockSpec((1, tq, ZC), lambda b, i: (b, i, 0)),
                  pl.BlockSpec((1, S, ZC), lambda b, i: (b, 0, 0)),
                  pl.BlockSpec((1, 1, S), lambda b, i: (b, 0, 0)),
                  _const_spec(freq_col.shape), _const_spec(place_cos.shape), _const_spec(place_sin.shape),
                  _const_spec(base.shape), _const_spec(q_norm.shape),
                  _const_spec(wq_p.shape), _const_spec(kv_norm.shape),
                  _const_spec(wkv_p.shape)],
        out_specs=pl.BlockSpec((1, tq, MLA_WIDTH), lambda b, i: (b, i, 0)),
        out_shape=jax.ShapeDtypeStruct((B, S, MLA_WIDTH), BF16),
        scratch_shapes=[pltpu.VMEM((MLA_HEADS, S, LANES), BF16), pltpu.VMEM((MLA_HEADS, S, LANES), BF16),
                        pltpu.VMEM((S, LANES), F32)],
        compiler_params=_params("parallel", "arbitrary"),
        name="mla_attn",
    )(z_mla, z_mla, pos_row, freq_col, place_cos, place_sin, base, q_norm, wq_p, kv_norm, wkv_p)


def _mix_out_rows(x, o_f, o_b, bonus, gate, lnw, lnb, segm, y_mla, w_r, w_m, g):
    o = o_f + o_b
    o_hi, o_lo = _split2(o)
    oc = o - (_seg_dot(o_hi, segm) + _seg_dot(o_lo, segm))
    var = _seg_dot((oc * oc).astype(BF16), segm)
    o = oc * lax.rsqrt(var + LN_X_EPS) * lnw + lnb
    y_rwkv = (o + bonus) * gate
    y = _dot(y_rwkv.astype(BF16), w_r) + _dot(y_mla, w_m)
    return x + _rms(y, g)


def _mem_attn_rows(x, k_ref, v_ref, g_pre, wq, wo, g_post, att_s):
    h = _rms(x, g_pre).astype(BF16)
    q = (_dot(h, wq) * (MEM_HEAD_DIM ** -0.5 * LOG2_E)).astype(BF16)

    def scores(hd):
        sl = slice(hd * MEM_HEAD_DIM, (hd + 1) * MEM_HEAD_DIM)
        return lax.dot_general(q[:, sl], k_ref[:, sl], (((1,), (1,)), ((), ())), preferred_element_type=F32)

    s = scores(0)
    for hd in range(MEM_HEADS):
        sl = slice(hd * MEM_HEAD_DIM, (hd + 1) * MEM_HEAD_DIM)
        s_next = scores(hd + 1) if hd + 1 < MEM_HEADS else None
        p = jnp.exp2(s - jnp.max(s, axis=-1, keepdims=True))
        denom = jnp.sum(p, axis=-1, keepdims=True)
        o = _dot(p.astype(BF16), v_ref[:, sl])
        att_s[:, sl] = (o / denom).astype(BF16)
        s = s_next
    return x + _rms(_dot(att_s[...], wo), g_post)


def _mix_mem_kernel(x_ref, of_ref, ob_ref, bf_ref, bb_ref, gate_ref, lnw_ref, lnb_ref, segm_ref, ym_ref,
                    wr_ref, wm_ref, gmix_ref, mem_ref, gmem_ref, wkv_ref, gpre_ref, wq_ref, wo_ref, gpost_ref,
                    o_ref, k_s, v_s, att_s):
    @pl.when(pl.program_id(1) == 0)
    def _():
        m = _rms(mem_ref[0], gmem_ref[...]).astype(BF16)
        kv = _dot(m, wkv_ref[...])
        k_s[...] = kv[:, :D_MODEL].astype(BF16)
        v_s[...] = kv[:, D_MODEL:].astype(BF16)

    bonus = bf_ref[0].astype(F32) + bb_ref[0].astype(F32)
    x = _mix_out_rows(x_ref[0], of_ref[0], ob_ref[0], bonus, gate_ref[0].astype(F32), lnw_ref[...], lnb_ref[...],
                      segm_ref[...], ym_ref[0], wr_ref[...], wm_ref[...], gmix_ref[...])
    o_ref[0] = _mem_attn_rows(x, k_s, v_s, gpre_ref[...], wq_ref[...], wo_ref[...], gpost_ref[...], att_s)


def _mix_mem(x, o_f, o_b, bonus_f, bonus_b, gate, lnw, lnb, segm, y_mla, w_r, w_m, g_mix,
             mem, g_mem, wkv, g_pre, wq, wo, g_post, tm):
    B, S, D = x.shape
    T = mem.shape[1]
    C = RWKV_WIDTH
    row_spec = lambda cols: pl.BlockSpec((1, tm, cols), lambda b, i: (b, i, 0))
    return pl.pallas_call(
        _mix_mem_kernel,
        grid=(B, S // tm),
        in_specs=[row_spec(D), row_spec(C), row_spec(C), row_spec(C), row_spec(C), row_spec(C)]
                 + [_const_spec(t.shape) for t in (lnw, lnb, segm)] + [row_spec(MLA_WIDTH)]
                 + [_const_spec(t.shape) for t in (w_r, w_m, g_mix)]
                 + [pl.BlockSpec((1, T, D), lambda b, i: (b, 0, 0))]
                 + [_const_spec(t.shape) for t in (g_mem, wkv, g_pre, wq, wo, g_post)],
        out_specs=row_spec(D),
        out_shape=jax.ShapeDtypeStruct((B, S, D), F32),
        scratch_shapes=[pltpu.VMEM((T, D), BF16), pltpu.VMEM((T, D), BF16), pltpu.VMEM((tm, D), BF16)],
        compiler_params=_params("parallel", "arbitrary"),
        name="mix_mem",
    )(x, o_f, o_b, bonus_f, bonus_b, gate, lnw, lnb, segm, y_mla, w_r, w_m, g_mix,
      mem, g_mem, wkv, g_pre, wq, wo, g_post)


def _mlp_kernel(x_ref, gpre_ref, w1_ref, w2_ref, gpost_ref, o_ref):
    x = x_ref[0]
    h = _rms(x, gpre_ref[...]).astype(BF16)
    u = jnp.maximum(jnp.dot(h, w1_ref[...], preferred_element_type=F32), 0.0)
    y = jnp.dot((u * u).astype(BF16), w2_ref[...], preferred_element_type=F32)
    o_ref[0] = x + _rms(y, gpost_ref[...])


def _mlp(x, g_pre, w1, w2, g_post, tm):
    B, S, D = x.shape
    row_spec = pl.BlockSpec((1, tm, D), lambda b, i: (b, i, 0))
    return pl.pallas_call(
        _mlp_kernel,
        grid=(B, S // tm),
        in_specs=[row_spec, _const_spec(g_pre.shape), _const_spec(w1.shape), _const_spec(w2.shape),
                  _const_spec(g_post.shape)],
        out_specs=row_spec,
        out_shape=jax.ShapeDtypeStruct((B, S, D), F32),
        compiler_params=_params("parallel", "parallel"),
        name="mlp",
    )(x, g_pre, w1, w2, g_post)


def _pad_cols(w, n):
    return jnp.pad(w, ((0, 0), (0, n - w.shape[1])))


def _rope_tile(w):
    half = QK_ROPE_DIM // 2
    return jnp.concatenate([jnp.zeros((w.shape[0], QK_NOPE_DIM), w.dtype), w, w[:, half:], w[:, :half]], axis=1)


def _lora_rows(w, lo, rows):
    return jnp.pad(w, ((lo, LORA_TILE - lo - rows), (0, 0)))


def kernel(x, mem, positions, norm_mix_pre, w_in, conv_rwkv, rwkv_w0, rwkv_w2, rwkv_a0, rwkv_a2, rwkv_g2, rwkv_k_k, rwkv_k_a, rwkv_r_k, rwkv_lnx_w, rwkv_lnx_b, mla_q_norm, mla_w_uq, mla_kv_norm, mla_w_ukv, w_out, norm_mix_post, norm_mem_pre, norm_memtok, mem_wq, mem_wkv, mem_wo, norm_mem_post, norm_mlp_pre, mlp_w1, mlp_w2, norm_mlp_post):
    depth = w_in.shape[0]
    C = RWKV_WIDTH
    head_of = jnp.arange(C) // HEAD_DIM
    seg = (head_of[:, None] == head_of[None, :]).astype(BF16)
    seg = seg[:C // 2, :C // 2]
    seg_mean = (seg.astype(F32) / HEAD_DIM).astype(BF16)
    inv_freq = ROPE_THETA ** (-jnp.arange(0, QK_ROPE_DIM, 2, dtype=F32) / QK_ROPE_DIM)
    half = QK_ROPE_DIM // 2
    lane = jnp.arange(LANES)[None, :]
    f_idx = jnp.arange(half)[:, None]
    first, second = lane == QK_NOPE_DIM + f_idx, lane == QK_NOPE_DIM + half + f_idx
    place_cos = (first | second).astype(BF16)
    place_sin = (second.astype(F32) - first.astype(F32)).astype(BF16)
    rope_base = (lane < QK_NOPE_DIM).astype(F32)
    freq_col = inv_freq[:, None]
    pos_row = positions[:, None, :]
    step = jnp.arange(CHUNK)
    tri = jnp.stack([step[None, :] <= step[:, None], step[None, :] >= step[:, None]]).astype(F32)
    row = lambda t: t.reshape(1, -1)

    for l in range(depth):
        w = w_in[l]
        mla0 = RWKV_COLS
        w_rope = w[:, mla0 + Q_LORA_RANK + KV_LORA_RANK:]
        w_r = _pad_cols(w[:, :RWKV_COLS], RWKV_TILE_COLS).astype(BF16)
        w_m = jnp.concatenate([w[:, mla0:mla0 + Q_LORA_RANK + KV_LORA_RANK], _rope_tile(w_rope)],
                              axis=1).astype(BF16)
        conv_p = _pad_cols(conv_rwkv[l], RWKV_TILE_COLS)
        z_rwkv, z_mla = _in_proj(x, row(norm_mix_pre[l]), w_r, w_m, conv_p, tm=512)

        w2_p = jnp.stack([_lora_rows(rwkv_w2[l, d], d * DECAY_LORA, DECAY_LORA) for d in range(2)])
        a_lo = 2 * DECAY_LORA
        a2_p = jnp.stack([_lora_rows(rwkv_a2[l, d], a_lo + d * ICLR_LORA, ICLR_LORA) for d in range(2)])
        g2_p = _lora_rows(rwkv_g2[l], a_lo + 2 * ICLR_LORA, GATE_LORA)
        o_f, o_b, bonus_f, bonus_b, gate = _rwkv(z_rwkv, rwkv_w0[l], w2_p, rwkv_a0[l], a2_p, g2_p,
                                                 row(rwkv_k_k[l]), row(rwkv_k_a[l]), row(rwkv_r_k[l]),
                                                 seg, tri, tm=512)

        qk = QK_NOPE_DIM + QK_ROPE_DIM
        w_uq = mla_w_uq[l].reshape(Q_LORA_RANK, MLA_HEADS, qk)
        uq_rope = w_uq[:, :, QK_NOPE_DIM:]
        uq_sw = jnp.concatenate([uq_rope[:, :, half:], uq_rope[:, :, :half]], axis=2)
        wq_p = jnp.concatenate([w_uq, uq_sw], axis=2).reshape(Q_LORA_RANK, MLA_HEADS * LANES)
        w_ukv = mla_w_ukv[l].reshape(KV_LORA_RANK, MLA_HEADS, QK_NOPE_DIM + V_HEAD_DIM)
        wk_p = jnp.pad(w_ukv[:, :, :QK_NOPE_DIM], ((0, 0), (0, 0), (0, LANES - QK_NOPE_DIM)))
        wv_p = jnp.pad(w_ukv[:, :, QK_NOPE_DIM:], ((0, 0), (0, 0), (0, LANES - V_HEAD_DIM)))
        wkv_p = jnp.concatenate([wk_p.reshape(KV_LORA_RANK, MLA_HEADS * LANES),
                                 wv_p.reshape(KV_LORA_RANK, MLA_HEADS * LANES)], axis=1)
        y_mla = _mla_attn(z_mla, pos_row, freq_col, place_cos, place_sin, rope_base, row(mla_q_norm[l]),
                          wq_p.astype(BF16), row(mla_kv_norm[l]), wkv_p.astype(BF16), tq=512)

        wo = w_out[l].astype(BF16)
        x = _mix_mem(x, o_f, o_b, bonus_f, bonus_b, gate, row(rwkv_lnx_w[l]), row(rwkv_lnx_b[l]), seg_mean,
                     y_mla, wo[:C], wo[C:], row(norm_mix_post[l]), mem, row(norm_memtok[l]),
                     mem_wkv[l].astype(BF16), row(norm_mem_pre[l]),
                     mem_wq[l].astype(BF16), mem_wo[l].astype(BF16), row(norm_mem_post[l]), tm=512)

        x = _mlp(x, row(norm_mlp_pre[l]), mlp_w1[l].astype(BF16), mlp_w2[l].astype(BF16),
                 row(norm_mlp_post[l]), tm=512)
    return x
```

```python
import math

import jax
import jax.numpy as jnp
from jax import lax
from jax.experimental import pallas as pl
from jax.experimental.pallas import tpu as pltpu

F32 = jnp.float32
BF16 = jnp.bfloat16

D_MODEL = 1024
NORM_EPS = 1e-6

RWKV_HEADS = 8
HEAD_DIM = 64
RWKV_WIDTH = RWKV_HEADS * HEAD_DIM
DECAY_LORA = 32
ICLR_LORA = 32
GATE_LORA = 96
LORA_COLS = 2 * DECAY_LORA + 2 * ICLR_LORA + GATE_LORA
LORA_TILE = 256
RWKV_COLS = 3 * RWKV_WIDTH + LORA_COLS
RWKV_TILE_COLS = 3 * RWKV_WIDTH + LORA_TILE
LN_X_EPS = 64e-5
CHUNK = 64
CHUNK_GROUP = 2
DECAY_SCALE = math.exp(-0.5)

MLA_HEADS = 8
QK_NOPE_DIM = 64
QK_ROPE_DIM = 32
V_HEAD_DIM = 64
MLA_WIDTH = MLA_HEADS * V_HEAD_DIM
Q_LORA_RANK = 256
KV_LORA_RANK = 128
ROPE_THETA = 10000.0
LOG2_E = math.log2(math.e)
LANES = 128
MLA_TILE_COLS = Q_LORA_RANK + KV_LORA_RANK + LANES

MEM_HEADS = 4
MEM_HEAD_DIM = D_MODEL // MEM_HEADS

VMEM_LIMIT = 56 * 1024 * 1024


def _mm(a, b):
    return jnp.dot(a.astype(BF16), b.astype(BF16), preferred_element_type=F32)


def _split2(t):
    hi = t.astype(BF16)
    lo = (t - hi.astype(F32)).astype(BF16)
    return hi, lo


def _split3(t):
    hi = t.astype(BF16)
    rest = t - hi.astype(F32)
    mid = rest.astype(BF16)
    lo = (rest - mid.astype(F32)).astype(BF16)
    return hi, mid, lo


def _dot(a, b):
    return jnp.dot(a, b, preferred_element_type=F32)


def _seg_dot(t, seg):
    half = seg.shape[0]
    return jnp.concatenate([_dot(t[:, :half], seg), _dot(t[:, half:], seg)], axis=1)


def _mm_exact_lhs(a, b):
    a = a.astype(BF16)
    hi, lo = _split2(b)
    return _dot(a, hi) + _dot(a, lo)


def _mm_nt(a, b):
    return lax.dot_general(a.astype(BF16), b.astype(BF16), (((1,), (1,)), ((), ())),
                           preferred_element_type=F32)


def _rms(x, g, eps=NORM_EPS):
    return x * lax.rsqrt(jnp.mean(x * x, axis=-1, keepdims=True) + eps) * g


def _sigmoid(x):
    return 0.5 * jnp.tanh(0.5 * x) + 0.5


def _params(*sem):
    return pltpu.CompilerParams(dimension_semantics=sem, vmem_limit_bytes=VMEM_LIMIT)


def _const_spec(shape):
    nd = len(shape)
    return pl.BlockSpec(shape, lambda *_: (0,) * nd)


def _in_proj_kernel(x_ref, xp_ref, xn_ref, g_ref, wr_ref, wm_ref, conv_ref, zr_ref, zm_ref):
    i = pl.program_id(1)
    last = pl.num_programs(1) - 1
    tm = x_ref.shape[1]
    g = g_ref[...]
    h = _rms(x_ref[0], g).astype(BF16)
    zm_ref[0] = jnp.dot(h, wm_ref[...], preferred_element_type=F32)
    z = jnp.dot(h, wr_ref[...], preferred_element_type=F32)
    halo = _rms(jnp.concatenate([xp_ref[0], xn_ref[0]], axis=0), g).astype(BF16)
    z_halo = jnp.dot(halo, wr_ref[...], preferred_element_type=F32)
    prev_row = jnp.where(i == 0, 0.0, z_halo[7:8, :])
    next_row = jnp.where(i == last, 0.0, z_halo[8:9, :])
    row = lax.broadcasted_iota(jnp.int32, (tm, 1), 0)
    z_dn = jnp.where(row == 0, prev_row, pltpu.roll(z, 1, 0))
    z_up = jnp.where(row == tm - 1, next_row, pltpu.roll(z, tm - 1, 0))
    zr_ref[0] = conv_ref[0:1, :] * z_dn + conv_ref[1:2, :] * z + conv_ref[2:3, :] * z_up


def _in_proj(x, g, w_r, w_m, conv_p, tm):
    B, S, D = x.shape
    halo = tm // 8
    n_halo = S // 8
    return pl.pallas_call(
        _in_proj_kernel,
        grid=(B, S // tm),
        in_specs=[pl.BlockSpec((1, tm, D), lambda b, i: (b, i, 0)),
                  pl.BlockSpec((1, 8, D), lambda b, i: (b, jnp.maximum(i * halo - 1, 0), 0)),
                  pl.BlockSpec((1, 8, D), lambda b, i: (b, jnp.minimum((i + 1) * halo, n_halo - 1), 0)),
                  _const_spec((1, D)),
                  _const_spec(w_r.shape),
                  _const_spec(w_m.shape),
                  _const_spec(conv_p.shape)],
        out_specs=[pl.BlockSpec((1, tm, RWKV_TILE_COLS), lambda b, i: (b, i, 0)),
                   pl.BlockSpec((1, tm, MLA_TILE_COLS), lambda b, i: (b, i, 0))],
        out_shape=[jax.ShapeDtypeStruct((B, S, RWKV_TILE_COLS), F32),
                   jax.ShapeDtypeStruct((B, S, MLA_TILE_COLS), F32)],
        compiler_params=_params("parallel", "parallel"),
        name="in_proj",
    )(x, x, x, g, w_r, w_m, conv_p)


def _chunk_units(units, eye, eye_hi):
    L, N = units[0][0].shape
    zeros = jnp.zeros((L, N), BF16)
    AA = [_mm_nt(jnp.concatenate([At, Rt], axis=0), jnp.concatenate([Bt, Kt], axis=0))
          for (At, Rt, Bt, Kt, *_) in units]
    A_a = [jnp.where(u[8], aa[:L, :], 0.0) for u, aa in zip(units, AA)]
    A_r = [jnp.where(u[9], aa[L:, :], 0.0) for u, aa in zip(units, AA)]
    AkV = [_mm(a.astype(BF16)[:, L:], u[6]) for u, a in zip(units, A_a)]
    low = lax.broadcasted_iota(jnp.int32, (L, 2 * L), 1) < L
    S = [jnp.where(low, a, eye_hi) for a in A_a]
    span = 1
    while span < L:
        R = [_mm(s[:, :L], s) for s in S]
        S = [jnp.where(low, r, r + s) for r, s in zip(R, S)]
        span *= 2
    W = [_mm(s.astype(BF16)[:, L:], jnp.concatenate([u[0], akv.astype(BF16)], axis=1))
         for s, u, akv in zip(S, units, AkV)]
    Z = [jnp.concatenate([w.astype(BF16), jnp.concatenate([zeros, u[6]], axis=1)], axis=0)
         for u, w in zip(units, W)]
    lhs = [jnp.concatenate([a_r.astype(BF16), jnp.concatenate([u[4], u[5]], axis=0).T], axis=0)
           for u, a_r in zip(units, A_r)]
    QGH = [_mm(l, z) for l, z in zip(lhs, Z)]
    Q = [jnp.concatenate([u[1], zeros], axis=1) + t[:L] for u, t in zip(units, QGH)]
    GH = [jnp.concatenate([eye * u[7], zeros.astype(F32)], axis=1) + t[L:] for u, t in zip(units, QGH)]
    return Q, GH


def _rwkv_kernel(zf_ref, zb_ref,
                 w0_ref, w2_ref, a0_ref, a2_ref, g2_ref, kk_ref, ka_ref, rk_ref, seg_ref, tri_ref,
                 of_ref, ob_ref, bonus_f_ref, bonus_b_ref, g_ref,
                 kk_s, b_s, kd_s, lw_s, state_s):
    i = pl.program_id(1)
    tm = zf_ref.shape[1]
    L, N = CHUNK, HEAD_DIM
    n_chunks = tm // L

    @pl.when(i == 0)
    def _():
        state_s[...] = jnp.zeros_like(state_s)

    seg = seg_ref[...]

    def seg_sum(t):
        return _seg_dot(t.astype(BF16), seg)

    for d, (z_ref, bonus_ref) in enumerate(((zf_ref, bonus_f_ref), (zb_ref, bonus_b_ref))):
        r = z_ref[0, :, 0:RWKV_WIDTH]
        k = z_ref[0, :, RWKV_WIDTH:2 * RWKV_WIDTH]
        v = z_ref[0, :, 2 * RWKV_WIDTH:3 * RWKV_WIDTH]
        lora = z_ref[0, :, 3 * RWKV_WIDTH:]
        kk = k * kk_ref[...]
        kk = kk * lax.rsqrt(jnp.maximum(seg_sum(kk * kk), 1e-24))
        w_pre = w0_ref[d:d + 1, :] + _mm(jnp.tanh(lora), w2_ref[d])
        lw_s[d] = -DECAY_SCALE * _sigmoid(w_pre)
        alpha = _sigmoid(a0_ref[d:d + 1, :] + _mm(lora, a2_ref[d]))
        kd = k * (1.0 + (alpha - 1.0) * ka_ref[...])
        kk_s[d] = kk
        kd_s[d] = kd
        b_s[d] = kk * alpha
        bonus_ref[0] = (seg_sum(r * kd * rk_ref[...]) * v).astype(bonus_ref.dtype)
        if d == 0:
            g_ref[0] = _mm(_sigmoid(lora), g2_ref[...]).astype(g_ref.dtype)

    ri = lax.broadcasted_iota(jnp.int32, (L, L), 0)
    ci = lax.broadcasted_iota(jnp.int32, (L, L), 1)
    eye = (ci == ri).astype(F32)
    ri2 = lax.broadcasted_iota(jnp.int32, (L, 2 * L), 0)
    ci2 = lax.broadcasted_iota(jnp.int32, (L, 2 * L), 1)
    eye_hi = (ci2 == ri2 + L).astype(F32)
    ci2 = jnp.where(ci2 >= L, ci2 - L, ci2)
    m_strict = (ci2 < ri2, ci2 > ri2)
    m_incl = (ci2 <= ri2, ci2 >= ri2)
    z_refs = (zf_ref, zb_ref)
    out_refs = (of_ref, ob_ref)

    def chunk_units(c):
        rows_d = (pl.ds(pl.multiple_of(c * L, L), L), pl.ds(pl.multiple_of((n_chunks - 1 - c) * L, L), L))
        units = []
        for d in range(2):
            rows = rows_d[d]
            lw = lw_s[d, rows, :]
            bc = b_s[d, rows, :]
            kdc = kd_s[d, rows, :]
            vc = z_refs[d][0, rows, 2 * RWKV_WIDTH:3 * RWKV_WIDTH]
            cum = _mm_exact_lhs(tri_ref[d], lw)
            tot = cum[L - 1:L, :] if d == 0 else cum[0:1, :]
            e_in = jnp.exp(-cum)
            e_rem = jnp.exp(tot - cum)
            e_tot = jnp.exp(tot)
            At = (-kk_s[d, rows, :] * jnp.exp(cum - lw)).astype(BF16)
            Rt = (z_refs[d][0, rows, 0:RWKV_WIDTH] * jnp.exp(cum)).astype(BF16)
            Bt = (bc * e_in).astype(BF16)
            Kt = (kdc * e_in).astype(BF16)
            Bh = (bc * e_rem).astype(BF16)
            Kh = (kdc * e_rem).astype(BF16)
            vc = vc.astype(BF16)
            for h in range(RWKV_HEADS):
                sl = slice(h * HEAD_DIM, (h + 1) * HEAD_DIM)
                units.append((At[:, sl], Rt[:, sl], Bt[:, sl], Kt[:, sl], Bh[:, sl], Kh[:, sl],
                              vc[:, sl], e_tot[:, sl], m_strict[d], m_incl[d]))
        return rows_d, units

    n_units = 2 * RWKV_HEADS

    def group_body(j, carry):
        rows, units = [], []
        for g in range(CHUNK_GROUP):
            rows_d, chunk = chunk_units(j * CHUNK_GROUP + g)
            rows.append(rows_d)
            units.extend(chunk)
        Q, GH = _chunk_units(units, eye, eye_hi)
        M = [state_s[u] for u in range(n_units)]
        for g in range(CHUNK_GROUP):
            sl = slice(g * n_units, (g + 1) * n_units)
            prod = [_mm(jnp.concatenate([q[:, :N], gh[:, :N]], axis=0), m) for q, gh, m in zip(Q[sl], GH[sl], M)]
            out = [p[:L] + q[:, N:] for p, q in zip(prod, Q[sl])]
            M = [p[L:] + gh[:, N:] for p, gh in zip(prod, GH[sl])]
            for d in range(2):
                for h in range(RWKV_HEADS):
                    out_refs[d][0, rows[g][d], h * N:(h + 1) * N] = out[d * RWKV_HEADS + h]
        for u in range(n_units):
            state_s[u] = M[u]
        return carry

    lax.fori_loop(0, n_chunks // CHUNK_GROUP, group_body, 0)


def _rwkv(z_rwkv, w0, w2_p, a0, a2_p, g2_p, k_k, k_a, r_k, seg, tri, tm):
    B, S, ZC = z_rwkv.shape
    C = RWKV_WIDTH
    n_tiles = S // tm
    fwd = lambda b, i: (b, i, 0)
    bwd = lambda b, i: (b, n_tiles - 1 - i, 0)
    out_f = pl.BlockSpec((1, tm, C), fwd)
    out_b = pl.BlockSpec((1, tm, C), bwd)
    out_sds = lambda dtype: jax.ShapeDtypeStruct((B, S, C), dtype)
    return pl.pallas_call(
        _rwkv_kernel,
        grid=(B, n_tiles),
        in_specs=[pl.BlockSpec((1, tm, ZC), fwd),
                  pl.BlockSpec((1, tm, ZC), bwd),
                  _const_spec(w0.shape), _const_spec(w2_p.shape),
                  _const_spec(a0.shape), _const_spec(a2_p.shape), _const_spec(g2_p.shape),
                  _const_spec(k_k.shape), _const_spec(k_a.shape), _const_spec(r_k.shape),
                  _const_spec(seg.shape), _const_spec(tri.shape)],
        out_specs=[out_f, out_b, out_f, out_b, out_f],
        out_shape=[out_sds(F32), out_sds(F32), out_sds(BF16), out_sds(BF16), out_sds(BF16)],
        scratch_shapes=[pltpu.VMEM((2, tm, C), F32)] * 4
                       + [pltpu.VMEM((2 * RWKV_HEADS, HEAD_DIM, HEAD_DIM), F32)],
        compiler_params=_params("parallel", "arbitrary"),
        name="rwkv",
    )(z_rwkv, z_rwkv, w0, w2_p, a0, a2_p, g2_p, k_k, k_a, r_k, seg, tri)


def _rope_tiles(pos_row, freq_col, place_cos, place_sin, base):
    ang = freq_col * pos_row.astype(F32)

    def place(t, p):
        return sum(lax.dot_general(part, p, (((0,), (0,)), ((), ())), preferred_element_type=F32)
                   for part in _split3(t))

    return place(jnp.cos(ang), place_cos) + base, place(jnp.sin(ang), place_sin)


def _mla_kernel(zq_ref, zkv_ref, pos_ref, freq_ref, pcos_ref, psin_ref, base_ref, qn_ref, wq_ref, kvn_ref, wkv_ref,
                y_ref, k_s, v_s, rot_s):
    i = pl.program_id(1)
    tq = zq_ref.shape[1]
    scale = (QK_NOPE_DIM + QK_ROPE_DIM) ** -0.5
    c_kv_lo, c_kv_hi = Q_LORA_RANK, Q_LORA_RANK + KV_LORA_RANK

    @pl.when(i == 0)
    def _():
        zkv = zkv_ref[0]
        kvn = _rms(zkv[:, c_kv_lo:c_kv_hi], kvn_ref[...]).astype(BF16)
        kvu = jnp.dot(kvn, wkv_ref[...], preferred_element_type=F32)
        cos_t, sin_t = _rope_tiles(pos_ref[0], freq_ref[...], pcos_ref[...], psin_ref[...], base_ref[...])
        rot_s[...] = cos_t + pltpu.roll(sin_t, QK_ROPE_DIM, 1)
        k_rope = zkv[:, c_kv_hi:]
        k_rot = k_rope * cos_t + pltpu.roll(k_rope, LANES - QK_ROPE_DIM, 1) * sin_t
        k_rot = k_rot + pltpu.roll(k_rot, QK_ROPE_DIM, 1)
        lane = lax.broadcasted_iota(jnp.int32, (1, LANES), 1)
        ones_col = (lane == V_HEAD_DIM).astype(F32)
        for h in range(MLA_HEADS):
            k_s[h] = (kvu[:, h * LANES:(h + 1) * LANES] + k_rot).astype(BF16)
            v_s[h] = (kvu[:, (MLA_HEADS + h) * LANES:(MLA_HEADS + h + 1) * LANES] + ones_col).astype(BF16)

    zq = zq_ref[0]
    qn = _rms(zq[:, :Q_LORA_RANK], qn_ref[...]).astype(BF16)
    q = jnp.dot(qn, wq_ref[...], preferred_element_type=F32)
    rows = pl.ds(pl.multiple_of(i * tq, tq), tq)
    rot_q = rot_s[rows, :] * (scale * LOG2_E)

    def scores(h):
        sl = slice(h * LANES, (h + 1) * LANES)
        qh = (q[:, sl] * rot_q).astype(BF16)
        return lax.dot_general(qh, k_s[h], (((1,), (1,)), ((), ())), preferred_element_type=F32)

    s = scores(0)
    for h in range(MLA_HEADS):
        s_next = scores(h + 1) if h + 1 < MLA_HEADS else None
        p = jnp.exp2((s - jnp.max(s, axis=-1, keepdims=True)).astype(BF16))
        o = jnp.dot(p, v_s[h], preferred_element_type=F32)
        y_ref[0, :, h * V_HEAD_DIM:(h + 1) * V_HEAD_DIM] = (
            o[:, :V_HEAD_DIM] / o[:, V_HEAD_DIM:V_HEAD_DIM + 1]).astype(y_ref.dtype)
        s = s_next


def _mla_attn(z_mla, pos_row, freq_col, place_cos, place_sin, base, q_norm, wq_p, kv_norm, wkv_p, tq):
    B, S, ZC = z_mla.shape
    return pl.pallas_call(
        _mla_kernel,
        grid=(B, S // tq),
        in_specs=[pl.BlockSpec((1, tq, ZC), lambda b, i: (b, i, 0)),
                  pl.BlockSpec((1, S, ZC), lambda b, i: (b, 0, 0)),
                  pl.BlockSpec((1, 1, S), lambda b, i: (b, 0, 0)),
                  _const_spec(freq_col.shape), _const_spec(place_cos.shape), _const_spec(place_sin.shape),
                  _const_spec(base.shape), _const_spec(q_norm.shape),
                  _const_spec(wq_p.shape), _const_spec(kv_norm.shape),
                  _const_spec(wkv_p.shape)],
        out_specs=pl.BlockSpec((1, tq, MLA_WIDTH), lambda b, i: (b, i, 0)),
        out_shape=jax.ShapeDtypeStruct((B, S, MLA_WIDTH), BF16),
        scratch_shapes=[pltpu.VMEM((MLA_HEADS, S, LANES), BF16), pltpu.VMEM((MLA_HEADS, S, LANES), BF16),
                        pltpu.VMEM((S, LANES), F32)],
        compiler_params=_params("parallel", "arbitrary"),
        name="mla_attn",
    )(z_mla, z_mla, pos_row, freq_col, place_cos, place_sin, base, q_norm, wq_p, kv_norm, wkv_p)


def _mix_out_rows(x, o_f, o_b, bonus, gate, lnw, lnb, segm, y_mla, w_r, w_m, g):
    o = o_f + o_b
    o_hi, o_lo = _split2(o)
    oc = o - (_seg_dot(o_hi, segm) + _seg_dot(o_lo, segm))
    var = _seg_dot((oc * oc).astype(BF16), segm)
    o = oc * lax.rsqrt(var + LN_X_EPS) * lnw + lnb
    y_rwkv = (o + bonus) * gate
    y = _dot(y_rwkv.astype(BF16), w_r) + _dot(y_mla, w_m)
    return x + _rms(y, g)


def _mem_attn_rows(x, k_ref, v_ref, g_pre, wq, wo, g_post, att_s):
    h = _rms(x, g_pre).astype(BF16)
    q = (_dot(h, wq) * (MEM_HEAD_DIM ** -0.5 * LOG2_E)).astype(BF16)

    def scores(hd):
        sl = slice(hd * MEM_HEAD_DIM, (hd + 1) * MEM_HEAD_DIM)
        return lax.dot_general(q[:, sl], k_ref[:, sl], (((1,), (1,)), ((), ())), preferred_element_type=F32)

    s = scores(0)
    for hd in range(MEM_HEADS):
        sl = slice(hd * MEM_HEAD_DIM, (hd + 1) * MEM_HEAD_DIM)
        s_next = scores(hd + 1) if hd + 1 < MEM_HEADS else None
        p = jnp.exp2(s - jnp.max(s, axis=-1, keepdims=True))
        denom = jnp.sum(p, axis=-1, keepdims=True)
        o = _dot(p.astype(BF16), v_ref[:, sl])
        att_s[:, sl] = (o / denom).astype(BF16)
        s = s_next
    return x + _rms(_dot(att_s[...], wo), g_post)


def _mix_mem_kernel(x_ref, of_ref, ob_ref, bf_ref, bb_ref, gate_ref, lnw_ref, lnb_ref, segm_ref, ym_ref,
                    wr_ref, wm_ref, gmix_ref, mem_ref, gmem_ref, wkv_ref, gpre_ref, wq_ref, wo_ref, gpost_ref,
                    o_ref, k_s, v_s, att_s):
    @pl.when(pl.program_id(1) == 0)
    def _():
        m = _rms(mem_ref[0], gmem_ref[...]).astype(BF16)
        kv = _dot(m, wkv_ref[...])
        k_s[...] = kv[:, :D_MODEL].astype(BF16)
        v_s[...] = kv[:, D_MODEL:].astype(BF16)

    bonus = bf_ref[0].astype(F32) + bb_ref[0].astype(F32)
    x = _mix_out_rows(x_ref[0], of_ref[0], ob_ref[0], bonus, gate_ref[0].astype(F32), lnw_ref[...], lnb_ref[...],
                      segm_ref[...], ym_ref[0], wr_ref[...], wm_ref[...], gmix_ref[...])
    o_ref[0] = _mem_attn_rows(x, k_s, v_s, gpre_ref[...], wq_ref[...], wo_ref[...], gpost_ref[...], att_s)


def _mix_mem(x, o_f, o_b, bonus_f, bonus_b, gate, lnw, lnb, segm, y_mla, w_r, w_m, g_mix,
             mem, g_mem, wkv, g_pre, wq, wo, g_post, tm):
    B, S, D = x.shape
    T = mem.shape[1]
    C = RWKV_WIDTH
    row_spec = lambda cols: pl.BlockSpec((1, tm, cols), lambda b, i: (b, i, 0))
    return pl.pallas_call(
        _mix_mem_kernel,
        grid=(B, S // tm),
        in_specs=[row_spec(D), row_spec(C), row_spec(C), row_spec(C), row_spec(C), row_spec(C)]
                 + [_const_spec(t.shape) for t in (lnw, lnb, segm)] + [row_spec(MLA_WIDTH)]
                 + [_const_spec(t.shape) for t in (w_r, w_m, g_mix)]
                 + [pl.BlockSpec((1, T, D), lambda b, i: (b, 0, 0))]
                 + [_const_spec(t.shape) for t in (g_mem, wkv, g_pre, wq, wo, g_post)],
        out_specs=row_spec(D),
        out_shape=jax.ShapeDtypeStruct((B, S, D), F32),
        scratch_shapes=[pltpu.VMEM((T, D), BF16), pltpu.VMEM((T, D), BF16), pltpu.VMEM((tm, D), BF16)],
        compiler_params=_params("parallel", "arbitrary"),
        name="mix_mem",
    )(x, o_f, o_b, bonus_f, bonus_b, gate, lnw, lnb, segm, y_mla, w_r, w_m, g_mix,
      mem, g_mem, wkv, g_pre, wq, wo, g_post)


def _mlp_kernel(x_ref, gpre_ref, w1_ref, w2_ref, gpost_ref, o_ref):
    x = x_ref[0]
    h = _rms(x, gpre_ref[...]).astype(BF16)
    u = jnp.maximum(jnp.dot(h, w1_ref[...], preferred_element_type=F32), 0.0)
    y = jnp.dot((u * u).astype(BF16), w2_ref[...], preferred_element_type=F32)
    o_ref[0] = x + _rms(y, gpost_ref[...])


def _mlp(x, g_pre, w1, w2, g_post, tm):
    B, S, D = x.shape
    row_spec = pl.BlockSpec((1, tm, D), lambda b, i: (b, i, 0))
    return pl.pallas_call(
        _mlp_kernel,
        grid=(B, S // tm),
        in_specs=[row_spec, _const_spec(g_pre.shape), _const_spec(w1.shape), _const_spec(w2.shape),
                  _const_spec(g_post.shape)],
        out_specs=row_spec,
        out_shape=jax.ShapeDtypeStruct((B, S, D), F32),
        compiler_params=_params("parallel", "parallel"),
        name="mlp",
    )(x, g_pre, w1, w2, g_post)


def _pad_cols(w, n):
    return jnp.pad(w, ((0, 0), (0, n - w.shape[1])))


def _rope_tile(w):
    half = QK_ROPE_DIM // 2
    return jnp.concatenate([jnp.zeros((w.shape[0], QK_NOPE_DIM), w.dtype), w, w[:, half:], w[:, :half]], axis=1)


def _lora_rows(w, lo, rows):
    return jnp.pad(w, ((lo, LORA_TILE - lo - rows), (0, 0)))


def kernel(x, mem, positions, norm_mix_pre, w_in, conv_rwkv, rwkv_w0, rwkv_w2, rwkv_a0, rwkv_a2, rwkv_g2, rwkv_k_k, rwkv_k_a, rwkv_r_k, rwkv_lnx_w, rwkv_lnx_b, mla_q_norm, mla_w_uq, mla_kv_norm, mla_w_ukv, w_out, norm_mix_post, norm_mem_pre, norm_memtok, mem_wq, mem_wkv, mem_wo, norm_mem_post, norm_mlp_pre, mlp_w1, mlp_w2, norm_mlp_post):
    depth = w_in.shape[0]
    C = RWKV_WIDTH
    head_of = jnp.arange(C) // HEAD_DIM
    seg = (head_of[:, None] == head_of[None, :]).astype(BF16)
    seg = seg[:C // 2, :C // 2]
    seg_mean = (seg.astype(F32) / HEAD_DIM).astype(BF16)
    inv_freq = ROPE_THETA ** (-jnp.arange(0, QK_ROPE_DIM, 2, dtype=F32) / QK_ROPE_DIM)
    half = QK_ROPE_DIM // 2
    lane = jnp.arange(LANES)[None, :]
    f_idx = jnp.arange(half)[:, None]
    first, second = lane == QK_NOPE_DIM + f_idx, lane == QK_NOPE_DIM + half + f_idx
    place_cos = (first | second).astype(BF16)
    place_sin = (second.astype(F32) - first.astype(F32)).astype(BF16)
    rope_base = (lane < QK_NOPE_DIM).astype(F32)
    freq_col = inv_freq[:, None]
    pos_row = positions[:, None, :]
    step = jnp.arange(CHUNK)
    tri = jnp.stack([step[None, :] <= step[:, None], step[None, :] >= step[:, None]]).astype(F32)
    row = lambda t: t.reshape(1, -1)

    for l in range(depth):
        w = w_in[l]
        mla0 = RWKV_COLS
        w_rope = w[:, mla0 + Q_LORA_RANK + KV_LORA_RANK:]
        w_r = _pad_cols(w[:, :RWKV_COLS], RWKV_TILE_COLS).astype(BF16)
        w_m = jnp.concatenate([w[:, mla0:mla0 + Q_LORA_RANK + KV_LORA_RANK], _rope_tile(w_rope)],
                              axis=1).astype(BF16)
        conv_p = _pad_cols(conv_rwkv[l], RWKV_TILE_COLS)
        z_rwkv, z_mla = _in_proj(x, row(norm_mix_pre[l]), w_r, w_m, conv_p, tm=512)

        w2_p = jnp.stack([_lora_rows(rwkv_w2[l, d], d * DECAY_LORA, DECAY_LORA) for d in range(2)])
        a_lo = 2 * DECAY_LORA
        a2_p = jnp.stack([_lora_rows(rwkv_a2[l, d], a_lo + d * ICLR_LORA, ICLR_LORA) for d in range(2)])
        g2_p = _lora_rows(rwkv_g2[l], a_lo + 2 * ICLR_LORA, GATE_LORA)
        o_f, o_b, bonus_f, bonus_b, gate = _rwkv(z_rwkv, rwkv_w0[l], w2_p, rwkv_a0[l], a2_p, g2_p,
                                                 row(rwkv_k_k[l]), row(rwkv_k_a[l]), row(rwkv_r_k[l]),
                                                 seg, tri, tm=512)

        qk = QK_NOPE_DIM + QK_ROPE_DIM
        w_uq = mla_w_uq[l].reshape(Q_LORA_RANK, MLA_HEADS, qk)
        uq_rope = w_uq[:, :, QK_NOPE_DIM:]
        uq_sw = jnp.concatenate([uq_rope[:, :, half:], uq_rope[:, :, :half]], axis=2)
        wq_p = jnp.concatenate([w_uq, uq_sw], axis=2).reshape(Q_LORA_RANK, MLA_HEADS * LANES)
        w_ukv = mla_w_ukv[l].reshape(KV_LORA_RANK, MLA_HEADS, QK_NOPE_DIM + V_HEAD_DIM)
        wk_p = jnp.pad(w_ukv[:, :, :QK_NOPE_DIM], ((0, 0), (0, 0), (0, LANES - QK_NOPE_DIM)))
        wv_p = jnp.pad(w_ukv[:, :, QK_NOPE_DIM:], ((0, 0), (0, 0), (0, LANES - V_HEAD_DIM)))
        wkv_p = jnp.concatenate([wk_p.reshape(KV_LORA_RANK, MLA_HEADS * LANES),
                                 wv_p.reshape(KV_LORA_RANK, MLA_HEADS * LANES)], axis=1)
        y_mla = _mla_attn(z_mla, pos_row, freq_col, place_cos, place_sin, rope_base, row(mla_q_norm[l]),
                          wq_p.astype(BF16), row(mla_kv_norm[l]), wkv_p.astype(BF16), tq=512)

        wo = w_out[l].astype(BF16)
        x = _mix_mem(x, o_f, o_b, bonus_f, bonus_b, gate, row(rwkv_lnx_w[l]), row(rwkv_lnx_b[l]), seg_mean,
                     y_mla, wo[:C], wo[C:], row(norm_mix_post[l]), mem, row(norm_memtok[l]),
                     mem_wkv[l].astype(BF16), row(norm_mem_pre[l]),
                     mem_wq[l].astype(BF16), mem_wo[l].astype(BF16), row(norm_mem_post[l]), tm=512)

        x = _mlp(x, row(norm_mlp_pre[l]), mlp_w1[l].astype(BF16), mlp_w2[l].astype(BF16),
                 row(norm_mlp_post[l]), tm=512)
    return x
```

```python
import math

import jax
import jax.numpy as jnp
from jax import lax
from jax.experimental import pallas as pl
from jax.experimental.pallas import tpu as pltpu

F32 = jnp.float32
BF16 = jnp.bfloat16

D_MODEL = 1024
NORM_EPS = 1e-6

RWKV_HEADS = 8
HEAD_DIM = 64
RWKV_WIDTH = RWKV_HEADS * HEAD_DIM
DECAY_LORA = 32
ICLR_LORA = 32
GATE_LORA = 96
LORA_COLS = 2 * DECAY_LORA + 2 * ICLR_LORA + GATE_LORA
LORA_TILE = 256
RWKV_COLS = 3 * RWKV_WIDTH + LORA_COLS
RWKV_TILE_COLS = 3 * RWKV_WIDTH + LORA_TILE
LN_X_EPS = 64e-5
CHUNK = 64
CHUNK_GROUP = 2
DECAY_SCALE = math.exp(-0.5)

MLA_HEADS = 8
QK_NOPE_DIM = 64
QK_ROPE_DIM = 32
V_HEAD_DIM = 64
MLA_WIDTH = MLA_HEADS * V_HEAD_DIM
Q_LORA_RANK = 256
KV_LORA_RANK = 128
ROPE_THETA = 10000.0
LOG2_E = math.log2(math.e)
LANES = 128
MLA_TILE_COLS = Q_LORA_RANK + KV_LORA_RANK + LANES

MEM_HEADS = 4
MEM_HEAD_DIM = D_MODEL // MEM_HEADS

V7X_VMEM_BYTES = 64 * 1024 * 1024
VMEM_LIMIT = V7X_VMEM_BYTES - 8 * 1024 * 1024


def _mm(a, b):
    return jnp.dot(a.astype(BF16), b.astype(BF16), preferred_element_type=F32)


def _split2(t):
    hi = t.astype(BF16)
    lo = (t - hi.astype(F32)).astype(BF16)
    return hi, lo


def _split3(t):
    hi = t.astype(BF16)
    rest = t - hi.astype(F32)
    mid = rest.astype(BF16)
    lo = (rest - mid.astype(F32)).astype(BF16)
    return hi, mid, lo


def _dot(a, b):
    return jnp.dot(a, b, preferred_element_type=F32)


def _seg_dot(t, seg):
    half = seg.shape[0]
    return jnp.concatenate([_dot(t[:, :half], seg), _dot(t[:, half:], seg)], axis=1)


def _mm_exact_lhs(a, b):
    a = a.astype(BF16)
    hi, lo = _split2(b)
    return _dot(a, hi) + _dot(a, lo)


def _mm_nt(a, b):
    return lax.dot_general(a.astype(BF16), b.astype(BF16), (((1,), (1,)), ((), ())),
                           preferred_element_type=F32)


def _rms(x, g, eps=NORM_EPS):
    return x * lax.rsqrt(jnp.mean(x * x, axis=-1, keepdims=True) + eps) * g


def _sigmoid(x):
    return 0.5 * jnp.tanh(0.5 * x) + 0.5


def _params(*sem):
    return pltpu.CompilerParams(dimension_semantics=sem, vmem_limit_bytes=VMEM_LIMIT)


def _const_spec(shape):
    nd = len(shape)
    return pl.BlockSpec(shape, lambda *_: (0,) * nd)


def _in_proj_kernel(x_ref, xp_ref, xn_ref, g_ref, wr_ref, wm_ref, conv_ref, zr_ref, zm_ref):
    i = pl.program_id(1)
    last = pl.num_programs(1) - 1
    tm = x_ref.shape[1]
    g = g_ref[...]
    h = _rms(x_ref[0], g).astype(BF16)
    zm_ref[0] = jnp.dot(h, wm_ref[...], preferred_element_type=F32)
    z = jnp.dot(h, wr_ref[...], preferred_element_type=F32)
    halo = _rms(jnp.concatenate([xp_ref[0], xn_ref[0]], axis=0), g).astype(BF16)
    z_halo = jnp.dot(halo, wr_ref[...], preferred_element_type=F32)
    prev_row = jnp.where(i == 0, 0.0, z_halo[7:8, :])
    next_row = jnp.where(i == last, 0.0, z_halo[8:9, :])
    row = lax.broadcasted_iota(jnp.int32, (tm, 1), 0)
    z_dn = jnp.where(row == 0, prev_row, pltpu.roll(z, 1, 0))
    z_up = jnp.where(row == tm - 1, next_row, pltpu.roll(z, tm - 1, 0))
    zr_ref[0] = conv_ref[0:1, :] * z_dn + conv_ref[1:2, :] * z + conv_ref[2:3, :] * z_up


def _in_proj(x, g, w_r, w_m, conv_p, tm):
    B, S, D = x.shape
    halo = tm // 8
    n_halo = S // 8
    return pl.pallas_call(
        _in_proj_kernel,
        grid=(B, S // tm),
        in_specs=[pl.BlockSpec((1, tm, D), lambda b, i: (b, i, 0)),
                  pl.BlockSpec((1, 8, D), lambda b, i: (b, jnp.maximum(i * halo - 1, 0), 0)),
                  pl.BlockSpec((1, 8, D), lambda b, i: (b, jnp.minimum((i + 1) * halo, n_halo - 1), 0)),
                  _const_spec((1, D)),
                  _const_spec(w_r.shape),
                  _const_spec(w_m.shape),
                  _const_spec(conv_p.shape)],
        out_specs=[pl.BlockSpec((1, tm, RWKV_TILE_COLS), lambda b, i: (b, i, 0)),
                   pl.BlockSpec((1, tm, MLA_TILE_COLS), lambda b, i: (b, i, 0))],
        out_shape=[jax.ShapeDtypeStruct((B, S, RWKV_TILE_COLS), F32),
                   jax.ShapeDtypeStruct((B, S, MLA_TILE_COLS), F32)],
        compiler_params=_params("parallel", "parallel"),
        name="in_proj",
    )(x, x, x, g, w_r, w_m, conv_p)


def _chunk_units(units, eye, eye_hi):
    L, N = units[0][0].shape
    zeros = jnp.zeros((L, N), BF16)
    AA = [_mm_nt(jnp.concatenate([At, Rt], axis=0), jnp.concatenate([Bt, Kt], axis=0))
          for (At, Rt, Bt, Kt, *_) in units]
    A_a = [jnp.where(u[8], aa[:L, :], 0.0) for u, aa in zip(units, AA)]
    A_r = [jnp.where(u[9], aa[L:, :], 0.0) for u, aa in zip(units, AA)]
    AkV = [_mm(a.astype(BF16)[:, L:], u[6]) for u, a in zip(units, A_a)]
    low = lax.broadcasted_iota(jnp.int32, (L, 2 * L), 1) < L
    S = [jnp.where(low, a, eye_hi) for a in A_a]
    span = 1
    while span < L:
        R = [_mm(s[:, :L], s) for s in S]
        S = [jnp.where(low, r, r + s) for r, s in zip(R, S)]
        span *= 2
    W = [_mm(s.astype(BF16)[:, L:], jnp.concatenate([u[0], akv.astype(BF16)], axis=1))
         for s, u, akv in zip(S, units, AkV)]
    Z = [jnp.concatenate([w.astype(BF16), jnp.concatenate([zeros, u[6]], axis=1)], axis=0)
         for u, w in zip(units, W)]
    lhs = [jnp.concatenate([a_r.astype(BF16), jnp.concatenate([u[4], u[5]], axis=0).T], axis=0)
           for u, a_r in zip(units, A_r)]
    QGH = [_mm(l, z) for l, z in zip(lhs, Z)]
    Q = [jnp.concatenate([u[1], zeros], axis=1) + t[:L] for u, t in zip(units, QGH)]
    GH = [jnp.concatenate([eye * u[7], zeros.astype(F32)], axis=1) + t[L:] for u, t in zip(units, QGH)]
    return Q, GH


def _rwkv_kernel(zf_ref, zb_ref,
                 w0_ref, w2_ref, a0_ref, a2_ref, g2_ref, kk_ref, ka_ref, rk_ref, seg_ref, tri_ref,
                 of_ref, ob_ref, bonus_f_ref, bonus_b_ref, g_ref,
                 kk_s, b_s, kd_s, lw_s, state_s):
    i = pl.program_id(1)
    tm = zf_ref.shape[1]
    L, N = CHUNK, HEAD_DIM
    n_chunks = tm // L

    @pl.when(i == 0)
    def _():
        state_s[...] = jnp.zeros_like(state_s)

    seg = seg_ref[...]

    def seg_sum(t):
        return _seg_dot(t.astype(BF16), seg)

    for d, (z_ref, bonus_ref) in enumerate(((zf_ref, bonus_f_ref), (zb_ref, bonus_b_ref))):
        r = z_ref[0, :, 0:RWKV_WIDTH]
        k = z_ref[0, :, RWKV_WIDTH:2 * RWKV_WIDTH]
        v = z_ref[0, :, 2 * RWKV_WIDTH:3 * RWKV_WIDTH]
        lora = z_ref[0, :, 3 * RWKV_WIDTH:]
        kk = k * kk_ref[...]
        kk = kk * lax.rsqrt(jnp.maximum(seg_sum(kk * kk), 1e-24))
        w_pre = w0_ref[d:d + 1, :] + _mm(jnp.tanh(lora), w2_ref[d])
        lw_s[d] = -DECAY_SCALE * _sigmoid(w_pre)
        alpha = _sigmoid(a0_ref[d:d + 1, :] + _mm(lora, a2_ref[d]))
        kd = k * (1.0 + (alpha - 1.0) * ka_ref[...])
        kk_s[d] = kk
        kd_s[d] = kd
        b_s[d] = kk * alpha
        bonus_ref[0] = (seg_sum(r * kd * rk_ref[...]) * v).astype(bonus_ref.dtype)
        if d == 0:
            g_ref[0] = _mm(_sigmoid(lora), g2_ref[...]).astype(g_ref.dtype)

    ri = lax.broadcasted_iota(jnp.int32, (L, L), 0)
    ci = lax.broadcasted_iota(jnp.int32, (L, L), 1)
    eye = (ci == ri).astype(F32)
    ri2 = lax.broadcasted_iota(jnp.int32, (L, 2 * L), 0)
    ci2 = lax.broadcasted_iota(jnp.int32, (L, 2 * L), 1)
    eye_hi = (ci2 == ri2 + L).astype(F32)
    ci2 = jnp.where(ci2 >= L, ci2 - L, ci2)
    m_strict = (ci2 < ri2, ci2 > ri2)
    m_incl = (ci2 <= ri2, ci2 >= ri2)
    z_refs = (zf_ref, zb_ref)
    out_refs = (of_ref, ob_ref)

    def chunk_units(c):
        rows_d = (pl.ds(pl.multiple_of(c * L, L), L), pl.ds(pl.multiple_of((n_chunks - 1 - c) * L, L), L))
        units = []
        for d in range(2):
            rows = rows_d[d]
            lw = lw_s[d, rows, :]
            bc = b_s[d, rows, :]
            kdc = kd_s[d, rows, :]
            vc = z_refs[d][0, rows, 2 * RWKV_WIDTH:3 * RWKV_WIDTH]
            cum = _mm_exact_lhs(tri_ref[d], lw)
            tot = cum[L - 1:L, :] if d == 0 else cum[0:1, :]
            e_in = jnp.exp(-cum)
            e_rem = jnp.exp(tot - cum)
            e_tot = jnp.exp(tot)
            At = (-kk_s[d, rows, :] * jnp.exp(cum - lw)).astype(BF16)
            Rt = (z_refs[d][0, rows, 0:RWKV_WIDTH] * jnp.exp(cum)).astype(BF16)
            Bt = (bc * e_in).astype(BF16)
            Kt = (kdc * e_in).astype(BF16)
            Bh = (bc * e_rem).astype(BF16)
            Kh = (kdc * e_rem).astype(BF16)
            vc = vc.astype(BF16)
            for h in range(RWKV_HEADS):
                sl = slice(h * HEAD_DIM, (h + 1) * HEAD_DIM)
                units.append((At[:, sl], Rt[:, sl], Bt[:, sl], Kt[:, sl], Bh[:, sl], Kh[:, sl],
                              vc[:, sl], e_tot[:, sl], m_strict[d], m_incl[d]))
        return rows_d, units

    n_units = 2 * RWKV_HEADS

    def group_body(j, carry):
        rows, units = [], []
        for g in range(CHUNK_GROUP):
            rows_d, chunk = chunk_units(j * CHUNK_GROUP + g)
            rows.append(rows_d)
            units.extend(chunk)
        Q, GH = _chunk_units(units, eye, eye_hi)
        M = [state_s[u] for u in range(n_units)]
        for g in range(CHUNK_GROUP):
            sl = slice(g * n_units, (g + 1) * n_units)
            prod = [_mm(jnp.concatenate([q[:, :N], gh[:, :N]], axis=0), m) for q, gh, m in zip(Q[sl], GH[sl], M)]
            out = [p[:L] + q[:, N:] for p, q in zip(prod, Q[sl])]
            M = [p[L:] + gh[:, N:] for p, gh in zip(prod, GH[sl])]
            for d in range(2):
                for h in range(RWKV_HEADS):
                    out_refs[d][0, rows[g][d], h * N:(h + 1) * N] = out[d * RWKV_HEADS + h]
        for u in range(n_units):
            state_s[u] = M[u]
        return carry

    lax.fori_loop(0, n_chunks // CHUNK_GROUP, group_body, 0)


def _rwkv(z_rwkv, w0, w2_p, a0, a2_p, g2_p, k_k, k_a, r_k, seg, tri, tm):
    B, S, ZC = z_rwkv.shape
    C = RWKV_WIDTH
    n_tiles = S // tm
    fwd = lambda b, i: (b, i, 0)
    bwd = lambda b, i: (b, n_tiles - 1 - i, 0)
    out_f = pl.BlockSpec((1, tm, C), fwd)
    out_b = pl.BlockSpec((1, tm, C), bwd)
    out_sds = lambda dtype: jax.ShapeDtypeStruct((B, S, C), dtype)
    return pl.pallas_call(
        _rwkv_kernel,
        grid=(B, n_tiles),
        in_specs=[pl.BlockSpec((1, tm, ZC), fwd),
                  pl.BlockSpec((1, tm, ZC), bwd),
                  _const_spec(w0.shape), _const_spec(w2_p.shape),
                  _const_spec(a0.shape), _const_spec(a2_p.shape), _const_spec(g2_p.shape),
                  _const_spec(k_k.shape), _const_spec(k_a.shape), _const_spec(r_k.shape),
                  _const_spec(seg.shape), _const_spec(tri.shape)],
        out_specs=[out_f, out_b, out_f, out_b, out_f],
        out_shape=[out_sds(F32), out_sds(F32), out_sds(BF16), out_sds(BF16), out_sds(BF16)],
        scratch_shapes=[pltpu.VMEM((2, tm, C), F32)] * 4
                       + [pltpu.VMEM((2 * RWKV_HEADS, HEAD_DIM, HEAD_DIM), F32)],
        compiler_params=_params("parallel", "arbitrary"),
        name="rwkv",
    )(z_rwkv, z_rwkv, w0, w2_p, a0, a2_p, g2_p, k_k, k_a, r_k, seg, tri)


def _rope_tiles(pos_row, freq_col, place_cos, place_sin, base):
    ang = freq_col * pos_row.astype(F32)

    def place(t, p):
        return sum(lax.dot_general(part, p, (((0,), (0,)), ((), ())), preferred_element_type=F32)
                   for part in _split3(t))

    return place(jnp.cos(ang), place_cos) + base, place(jnp.sin(ang), place_sin)


def _mla_kernel(zq_ref, zkv_ref, pos_ref, freq_ref, pcos_ref, psin_ref, base_ref, qn_ref, wq_ref, kvn_ref, wkv_ref,
                y_ref, k_s, v_s, rot_s):
    i = pl.program_id(1)
    tq = zq_ref.shape[1]
    scale = (QK_NOPE_DIM + QK_ROPE_DIM) ** -0.5
    c_kv_lo, c_kv_hi = Q_LORA_RANK, Q_LORA_RANK + KV_LORA_RANK

    @pl.when(i == 0)
    def _():
        zkv = zkv_ref[0]
        kvn = _rms(zkv[:, c_kv_lo:c_kv_hi], kvn_ref[...]).astype(BF16)
        kvu = jnp.dot(kvn, wkv_ref[...], preferred_element_type=F32)
        cos_t, sin_t = _rope_tiles(pos_ref[0], freq_ref[...], pcos_ref[...], psin_ref[...], base_ref[...])
        rot_s[...] = cos_t + pltpu.roll(sin_t, QK_ROPE_DIM, 1)
        k_rope = zkv[:, c_kv_hi:]
        k_rot = k_rope * cos_t + pltpu.roll(k_rope, LANES - QK_ROPE_DIM, 1) * sin_t
        k_rot = k_rot + pltpu.roll(k_rot, QK_ROPE_DIM, 1)
        lane = lax.broadcasted_iota(jnp.int32, (1, LANES), 1)
        ones_col = (lane == V_HEAD_DIM).astype(F32)
        for h in range(MLA_HEADS):
            k_s[h] = (kvu[:, h * LANES:(h + 1) * LANES] + k_rot).astype(BF16)
            v_s[h] = (kvu[:, (MLA_HEADS + h) * LANES:(MLA_HEADS + h + 1) * LANES] + ones_col).astype(BF16)

    zq = zq_ref[0]
    qn = _rms(zq[:, :Q_LORA_RANK], qn_ref[...]).astype(BF16)
    q = jnp.dot(qn, wq_ref[...], preferred_element_type=F32)
    rows = pl.ds(pl.multiple_of(i * tq, tq), tq)
    rot_q = rot_s[rows, :] * (scale * LOG2_E)

    def scores(h):
        sl = slice(h * LANES, (h + 1) * LANES)
        qh = (q[:, sl] * rot_q).astype(BF16)
        return lax.dot_general(qh, k_s[h], (((1,), (1,)), ((), ())), preferred_element_type=F32)

    s = scores(0)
    for h in range(MLA_HEADS):
        s_next = scores(h + 1) if h + 1 < MLA_HEADS else None
        p = jnp.exp2((s - jnp.max(s, axis=-1, keepdims=True)).astype(BF16))
        o = jnp.dot(p, v_s[h], preferred_element_type=F32)
        y_ref[0, :, h * V_HEAD_DIM:(h + 1) * V_HEAD_DIM] = (
            o[:, :V_HEAD_DIM] / o[:, V_HEAD_DIM:V_HEAD_DIM + 1]).astype(y_ref.dtype)
        s = s_next


def _mla_attn(z_mla, pos_row, freq_col, place_cos, place_sin, base, q_norm, wq_p, kv_norm, wkv_p, tq):
    B, S, ZC = z_mla.shape
    return pl.pallas_call(
        _mla_kernel,
        grid=(B, S // tq),
        in_specs=[pl.BlockSpec((1, tq, ZC), lambda b, i: (b, i, 0)),
                  pl.BlockSpec((1, S, ZC), lambda b, i: (b, 0, 0)),
                  pl.BlockSpec((1, 1, S), lambda b, i: (b, 0, 0)),
                  _const_spec(freq_col.shape), _const_spec(place_cos.shape), _const_spec(place_sin.shape),
                  _const_spec(base.shape), _const_spec(q_norm.shape),
                  _const_spec(wq_p.shape), _const_spec(kv_norm.shape),
                  _const_spec(wkv_p.shape)],
        out_specs=pl.BlockSpec((1, tq, MLA_WIDTH), lambda b, i: (b, i, 0)),
        out_shape=jax.ShapeDtypeStruct((B, S, MLA_WIDTH), BF16),
        scratch_shapes=[pltpu.VMEM((MLA_HEADS, S, LANES), BF16), pltpu.VMEM((MLA_HEADS, S, LANES), BF16),
                        pltpu.VMEM((S, LANES), F32)],
        compiler_params=_params("parallel", "arbitrary"),
        name="mla_attn",
    )(z_mla, z_mla, pos_row, freq_col, place_cos, place_sin, base, q_norm, wq_p, kv_norm, wkv_p)


def _mix_out_rows(x, o_f, o_b, bonus, gate, lnw, lnb, segm, y_mla, w_r, w_m, g):
    o = o_f + o_b
    o_hi, o_lo = _split2(o)
    oc = o - (_seg_dot(o_hi, segm) + _seg_dot(o_lo, segm))
    var = _seg_dot((oc * oc).astype(BF16), segm)
    o = oc * lax.rsqrt(var + LN_X_EPS) * lnw + lnb
    y_rwkv = (o + bonus) * gate
    y = _dot(y_rwkv.astype(BF16), w_r) + _dot(y_mla, w_m)
    return x + _rms(y, g)


def _mem_attn_rows(x, k_ref, v_ref, g_pre, wq, wo, g_post, att_s):
    h = _rms(x, g_pre).astype(BF16)
    q = (_dot(h, wq) * (MEM_HEAD_DIM ** -0.5 * LOG2_E)).astype(BF16)

    def scores(hd):
        sl = slice(hd * MEM_HEAD_DIM, (hd + 1) * MEM_HEAD_DIM)
        return lax.dot_general(q[:, sl], k_ref[:, sl], (((1,), (1,)), ((), ())), preferred_element_type=F32)

    s = scores(0)
    for hd in range(MEM_HEADS):
        sl = slice(hd * MEM_HEAD_DIM, (hd + 1) * MEM_HEAD_DIM)
        s_next = scores(hd + 1) if hd + 1 < MEM_HEADS else None
        p = jnp.exp2(s - jnp.max(s, axis=-1, keepdims=True))
        denom = jnp.sum(p, axis=-1, keepdims=True)
        o = _dot(p.astype(BF16), v_ref[:, sl])
        att_s[:, sl] = (o / denom).astype(BF16)
        s = s_next
    return x + _rms(_dot(att_s[...], wo), g_post)


def _mix_mem_kernel(x_ref, of_ref, ob_ref, bf_ref, bb_ref, gate_ref, lnw_ref, lnb_ref, segm_ref, ym_ref,
                    wr_ref, wm_ref, gmix_ref, mem_ref, gmem_ref, wkv_ref, gpre_ref, wq_ref, wo_ref, gpost_ref,
                    o_ref, k_s, v_s, att_s):
    @pl.when(pl.program_id(1) == 0)
    def _():
        m = _rms(mem_ref[0], gmem_ref[...]).astype(BF16)
        kv = _dot(m, wkv_ref[...])
        k_s[...] = kv[:, :D_MODEL].astype(BF16)
        v_s[...] = kv[:, D_MODEL:].astype(BF16)

    bonus = bf_ref[0].astype(F32) + bb_ref[0].astype(F32)
    x = _mix_out_rows(x_ref[0], of_ref[0], ob_ref[0], bonus, gate_ref[0].astype(F32), lnw_ref[...], lnb_ref[...],
                      segm_ref[...], ym_ref[0], wr_ref[...], wm_ref[...], gmix_ref[...])
    o_ref[0] = _mem_attn_rows(x, k_s, v_s, gpre_ref[...], wq_ref[...], wo_ref[...], gpost_ref[...], att_s)


def _mix_mem(x, o_f, o_b, bonus_f, bonus_b, gate, lnw, lnb, segm, y_mla, w_r, w_m, g_mix,
             mem, g_mem, wkv, g_pre, wq, wo, g_post, tm):
    B, S, D = x.shape
    T = mem.shape[1]
    C = RWKV_WIDTH
    row_spec = lambda cols: pl.BlockSpec((1, tm, cols), lambda b, i: (b, i, 0))
    return pl.pallas_call(
        _mix_mem_kernel,
        grid=(B, S // tm),
        in_specs=[row_spec(D), row_spec(C), row_spec(C), row_spec(C), row_spec(C), row_spec(C)]
                 + [_const_spec(t.shape) for t in (lnw, lnb, segm)] + [row_spec(MLA_WIDTH)]
                 + [_const_spec(t.shape) for t in (w_r, w_m, g_mix)]
                 + [pl.BlockSpec((1, T, D), lambda b, i: (b, 0, 0))]
                 + [_const_spec(t.shape) for t in (g_mem, wkv, g_pre, wq, wo, g_post)],
        out_specs=row_spec(D),
        out_shape=jax.ShapeDtypeStruct((B, S, D), F32),
        scratch_shapes=[pltpu.VMEM((T, D), BF16), pltpu.VMEM((T, D), BF16), pltpu.VMEM((tm, D), BF16)],
        compiler_params=_params("parallel", "arbitrary"),
        name="mix_mem",
    )(x, o_f, o_b, bonus_f, bonus_b, gate, lnw, lnb, segm, y_mla, w_r, w_m, g_mix,
      mem, g_mem, wkv, g_pre, wq, wo, g_post)


def _mlp_kernel(x_ref, gpre_ref, w1_ref, w2_ref, gpost_ref, o_ref):
    x = x_ref[0]
    h = _rms(x, gpre_ref[...]).astype(BF16)
    u = jnp.maximum(jnp.dot(h, w1_ref[...], preferred_element_type=F32), 0.0)
    y = jnp.dot((u * u).astype(BF16), w2_ref[...], preferred_element_type=F32)
    o_ref[0] = x + _rms(y, gpost_ref[...])


def _mlp(x, g_pre, w1, w2, g_post, tm):
    B, S, D = x.shape
    row_spec = pl.BlockSpec((1, tm, D), lambda b, i: (b, i, 0))
    return pl.pallas_call(
        _mlp_kernel,
        grid=(B, S // tm),
        in_specs=[row_spec, _const_spec(g_pre.shape), _const_spec(w1.shape), _const_spec(w2.shape),
                  _const_spec(g_post.shape)],
        out_specs=row_spec,
        out_shape=jax.ShapeDtypeStruct((B, S, D), F32),
        compiler_params=_params("parallel", "parallel"),
        name="mlp",
    )(x, g_pre, w1, w2, g_post)


def _pad_cols(w, n):
    return jnp.pad(w, ((0, 0), (0, n - w.shape[1])))


def _rope_tile(w):
    half = QK_ROPE_DIM // 2
    return jnp.concatenate([jnp.zeros((w.shape[0], QK_NOPE_DIM), w.dtype), w, w[:, half:], w[:, :half]], axis=1)


def _lora_rows(w, lo, rows):
    return jnp.pad(w, ((lo, LORA_TILE - lo - rows), (0, 0)))


def kernel(x, mem, positions, norm_mix_pre, w_in, conv_rwkv, rwkv_w0, rwkv_w2, rwkv_a0, rwkv_a2, rwkv_g2, rwkv_k_k, rwkv_k_a, rwkv_r_k, rwkv_lnx_w, rwkv_lnx_b, mla_q_norm, mla_w_uq, mla_kv_norm, mla_w_ukv, w_out, norm_mix_post, norm_mem_pre, norm_memtok, mem_wq, mem_wkv, mem_wo, norm_mem_post, norm_mlp_pre, mlp_w1, mlp_w2, norm_mlp_post):
    depth = w_in.shape[0]
    C = RWKV_WIDTH
    head_of = jnp.arange(C) // HEAD_DIM
    seg = (head_of[:, None] == head_of[None, :]).astype(BF16)
    seg = seg[:C // 2, :C // 2]
    seg_mean = (seg.astype(F32) / HEAD_DIM).astype(BF16)
    inv_freq = ROPE_THETA ** (-jnp.arange(0, QK_ROPE_DIM, 2, dtype=F32) / QK_ROPE_DIM)
    half = QK_ROPE_DIM // 2
    lane = jnp.arange(LANES)[None, :]
    f_idx = jnp.arange(half)[:, None]
    first, second = lane == QK_NOPE_DIM + f_idx, lane == QK_NOPE_DIM + half + f_idx
    place_cos = (first | second).astype(BF16)
    place_sin = (second.astype(F32) - first.astype(F32)).astype(BF16)
    rope_base = (lane < QK_NOPE_DIM).astype(F32)
    freq_col = inv_freq[:, None]
    pos_row = positions[:, None, :]
    step = jnp.arange(CHUNK)
    tri = jnp.stack([step[None, :] <= step[:, None], step[None, :] >= step[:, None]]).astype(F32)
    row = lambda t: t.reshape(1, -1)

    for l in range(depth):
        w = w_in[l]
        mla0 = RWKV_COLS
        w_rope = w[:, mla0 + Q_LORA_RANK + KV_LORA_RANK:]
        w_r = _pad_cols(w[:, :RWKV_COLS], RWKV_TILE_COLS).astype(BF16)
        w_m = jnp.concatenate([w[:, mla0:mla0 + Q_LORA_RANK + KV_LORA_RANK], _rope_tile(w_rope)],
                              axis=1).astype(BF16)
        conv_p = _pad_cols(conv_rwkv[l], RWKV_TILE_COLS)
        z_rwkv, z_mla = _in_proj(x, row(norm_mix_pre[l]), w_r, w_m, conv_p, tm=512)

        w2_p = jnp.stack([_lora_rows(rwkv_w2[l, d], d * DECAY_LORA, DECAY_LORA) for d in range(2)])
        a_lo = 2 * DECAY_LORA
        a2_p = jnp.stack([_lora_rows(rwkv_a2[l, d], a_lo + d * ICLR_LORA, ICLR_LORA) for d in range(2)])
        g2_p = _lora_rows(rwkv_g2[l], a_lo + 2 * ICLR_LORA, GATE_LORA)
        o_f, o_b, bonus_f, bonus_b, gate = _rwkv(z_rwkv, rwkv_w0[l], w2_p, rwkv_a0[l], a2_p, g2_p,
                                                 row(rwkv_k_k[l]), row(rwkv_k_a[l]), row(rwkv_r_k[l]),
                                                 seg, tri, tm=512)

        qk = QK_NOPE_DIM + QK_ROPE_DIM
        w_uq = mla_w_uq[l].reshape(Q_LORA_RANK, MLA_HEADS, qk)
        uq_rope = w_uq[:, :, QK_NOPE_DIM:]
        uq_sw = jnp.concatenate([uq_rope[:, :, half:], uq_rope[:, :, :half]], axis=2)
        wq_p = jnp.concatenate([w_uq, uq_sw], axis=2).reshape(Q_LORA_RANK, MLA_HEADS * LANES)
        w_ukv = mla_w_ukv[l].reshape(KV_LORA_RANK, MLA_HEADS, QK_NOPE_DIM + V_HEAD_DIM)
        wk_p = jnp.pad(w_ukv[:, :, :QK_NOPE_DIM], ((0, 0), (0, 0), (0, LANES - QK_NOPE_DIM)))
        wv_p = jnp.pad(w_ukv[:, :, QK_NOPE_DIM:], ((0, 0), (0, 0), (0, LANES - V_HEAD_DIM)))
        wkv_p = jnp.concatenate([wk_p.reshape(KV_LORA_RANK, MLA_HEADS * LANES),
                                 wv_p.reshape(KV_LORA_RANK, MLA_HEADS * LANES)], axis=1)
        y_mla = _mla_attn(z_mla, pos_row, freq_col, place_cos, place_sin, rope_base, row(mla_q_norm[l]),
                          wq_p.astype(BF16), row(mla_kv_norm[l]), wkv_p.astype(BF16), tq=1024)

        wo = w_out[l].astype(BF16)
        x = _mix_mem(x, o_f, o_b, bonus_f, bonus_b, gate, row(rwkv_lnx_w[l]), row(rwkv_lnx_b[l]), seg_mean,
                     y_mla, wo[:C], wo[C:], row(norm_mix_post[l]), mem, row(norm_memtok[l]),
                     mem_wkv[l].astype(BF16), row(norm_mem_pre[l]),
                     mem_wq[l].astype(BF16), mem_wo[l].astype(BF16), row(norm_mem_post[l]), tm=512)

        x = _mlp(x, row(norm_mlp_pre[l]), mlp_w1[l].astype(BF16), mlp_w2[l].astype(BF16),
                 row(norm_mlp_post[l]), tm=512)
    return x
```

```python
import math

import jax
import jax.numpy as jnp
from jax import lax
from jax.experimental import pallas as pl
from jax.experimental.pallas import tpu as pltpu

F32 = jnp.float32
BF16 = jnp.bfloat16

D_MODEL = 1024
NORM_EPS = 1e-6

RWKV_HEADS = 8
HEAD_DIM = 64
RWKV_WIDTH = RWKV_HEADS * HEAD_DIM
DECAY_LORA = 32
ICLR_LORA = 32
GATE_LORA = 96
LORA_COLS = 2 * DECAY_LORA + 2 * ICLR_LORA + GATE_LORA
LORA_TILE = 256
RWKV_COLS = 3 * RWKV_WIDTH + LORA_COLS
RWKV_TILE_COLS = 3 * RWKV_WIDTH + LORA_TILE
LN_X_EPS = 64e-5
CHUNK = 64
CHUNK_GROUP = 2
DECAY_SCALE = math.exp(-0.5)

MLA_HEADS = 8
QK_NOPE_DIM = 64
QK_ROPE_DIM = 32
V_HEAD_DIM = 64
MLA_WIDTH = MLA_HEADS * V_HEAD_DIM
Q_LORA_RANK = 256
KV_LORA_RANK = 128
ROPE_THETA = 10000.0
LOG2_E = math.log2(math.e)
LANES = 128
MLA_TILE_COLS = Q_LORA_RANK + KV_LORA_RANK + LANES

MEM_HEADS = 4
MEM_HEAD_DIM = D_MODEL // MEM_HEADS

V7X_VMEM_BYTES = 64 * 1024 * 1024
VMEM_LIMIT = V7X_VMEM_BYTES - 8 * 1024 * 1024


def _mm(a, b):
    return jnp.dot(a.astype(BF16), b.astype(BF16), preferred_element_type=F32)


def _split2(t):
    hi = t.astype(BF16)
    lo = (t - hi.astype(F32)).astype(BF16)
    return hi, lo


def _split3(t):
    hi = t.astype(BF16)
    rest = t - hi.astype(F32)
    mid = rest.astype(BF16)
    lo = (rest - mid.astype(F32)).astype(BF16)
    return hi, mid, lo


def _dot(a, b):
    return jnp.dot(a, b, preferred_element_type=F32)


def _seg_dot(t, seg):
    half = seg.shape[0]
    return jnp.concatenate([_dot(t[:, :half], seg), _dot(t[:, half:], seg)], axis=1)


def _mm_exact_lhs(a, b):
    a = a.astype(BF16)
    hi, lo = _split2(b)
    return _dot(a, hi) + _dot(a, lo)


def _mm_nt(a, b):
    return lax.dot_general(a.astype(BF16), b.astype(BF16), (((1,), (1,)), ((), ())),
                           preferred_element_type=F32)


def _rms(x, g, eps=NORM_EPS):
    return x * lax.rsqrt(jnp.mean(x * x, axis=-1, keepdims=True) + eps) * g


def _sigmoid(x):
    return 0.5 * jnp.tanh(0.5 * x) + 0.5


def _params(*sem):
    return pltpu.CompilerParams(dimension_semantics=sem, vmem_limit_bytes=VMEM_LIMIT)


def _const_spec(shape):
    nd = len(shape)
    return pl.BlockSpec(shape, lambda *_: (0,) * nd)


def _in_proj_kernel(x_ref, xp_ref, xn_ref, g_ref, wr_ref, wm_ref, conv_ref, zr_ref, zm_ref):
    i = pl.program_id(1)
    last = pl.num_programs(1) - 1
    tm = x_ref.shape[1]
    g = g_ref[...]
    h = _rms(x_ref[0], g).astype(BF16)
    zm_ref[0] = jnp.dot(h, wm_ref[...], preferred_element_type=F32)
    z = jnp.dot(h, wr_ref[...], preferred_element_type=F32)
    halo = _rms(jnp.concatenate([xp_ref[0], xn_ref[0]], axis=0), g).astype(BF16)
    z_halo = jnp.dot(halo, wr_ref[...], preferred_element_type=F32)
    prev_row = jnp.where(i == 0, 0.0, z_halo[7:8, :])
    next_row = jnp.where(i == last, 0.0, z_halo[8:9, :])
    row = lax.broadcasted_iota(jnp.int32, (tm, 1), 0)
    z_dn = jnp.where(row == 0, prev_row, pltpu.roll(z, 1, 0))
    z_up = jnp.where(row == tm - 1, next_row, pltpu.roll(z, tm - 1, 0))
    zr_ref[0] = conv_ref[0:1, :] * z_dn + conv_ref[1:2, :] * z + conv_ref[2:3, :] * z_up


def _in_proj(x, g, w_r, w_m, conv_p, tm):
    B, S, D = x.shape
    halo = tm // 8
    n_halo = S // 8
    return pl.pallas_call(
        _in_proj_kernel,
        grid=(B, S // tm),
        in_specs=[pl.BlockSpec((1, tm, D), lambda b, i: (b, i, 0)),
                  pl.BlockSpec((1, 8, D), lambda b, i: (b, jnp.maximum(i * halo - 1, 0), 0)),
                  pl.BlockSpec((1, 8, D), lambda b, i: (b, jnp.minimum((i + 1) * halo, n_halo - 1), 0)),
                  _const_spec((1, D)),
                  _const_spec(w_r.shape),
                  _const_spec(w_m.shape),
                  _const_spec(conv_p.shape)],
        out_specs=[pl.BlockSpec((1, tm, RWKV_TILE_COLS), lambda b, i: (b, i, 0)),
                   pl.BlockSpec((1, tm, MLA_TILE_COLS), lambda b, i: (b, i, 0))],
        out_shape=[jax.ShapeDtypeStruct((B, S, RWKV_TILE_COLS), F32),
                   jax.ShapeDtypeStruct((B, S, MLA_TILE_COLS), F32)],
        compiler_params=_params("parallel", "parallel"),
        name="in_proj",
    )(x, x, x, g, w_r, w_m, conv_p)


def _chunk_units(units, eye, eye_hi):
    L, N = units[0][0].shape
    zeros = jnp.zeros((L, N), BF16)
    AA = [_mm_nt(jnp.concatenate([At, Rt], axis=0), jnp.concatenate([Bt, Kt], axis=0))
          for (At, Rt, Bt, Kt, *_) in units]
    A_a = [jnp.where(u[8], aa[:L, :], 0.0) for u, aa in zip(units, AA)]
    A_r = [jnp.where(u[9], aa[L:, :], 0.0) for u, aa in zip(units, AA)]
    AkV = [_mm(a.astype(BF16)[:, L:], u[6]) for u, a in zip(units, A_a)]
    low = lax.broadcasted_iota(jnp.int32, (L, 2 * L), 1) < L
    S = [jnp.where(low, a, eye_hi) for a in A_a]
    span = 1
    while span < L:
        R = [_mm(s[:, :L], s) for s in S]
        S = [jnp.where(low, r, r + s) for r, s in zip(R, S)]
        span *= 2
    W = [_mm(s.astype(BF16)[:, L:], jnp.concatenate([u[0], akv.astype(BF16)], axis=1))
         for s, u, akv in zip(S, units, AkV)]
    Z = [jnp.concatenate([w.astype(BF16), jnp.concatenate([zeros, u[6]], axis=1)], axis=0)
         for u, w in zip(units, W)]
    lhs = [jnp.concatenate([a_r.astype(BF16), jnp.concatenate([u[4], u[5]], axis=0).T], axis=0)
           for u, a_r in zip(units, A_r)]
    QGH = [_mm(l, z) for l, z in zip(lhs, Z)]
    Q = [jnp.concatenate([u[1], zeros], axis=1) + t[:L] for u, t in zip(units, QGH)]
    GH = [jnp.concatenate([eye * u[7], zeros.astype(F32)], axis=1) + t[L:] for u, t in zip(units, QGH)]
    return Q, GH


def _rwkv_kernel(zf_ref, zb_ref,
                 w0_ref, w2_ref, a0_ref, a2_ref, g2_ref, kk_ref, ka_ref, rk_ref, seg_ref, tri_ref,
                 of_ref, ob_ref, bonus_f_ref, bonus_b_ref, g_ref,
                 kk_s, b_s, kd_s, lw_s, state_s):
    i = pl.program_id(1)
    tm = zf_ref.shape[1]
    L, N = CHUNK, HEAD_DIM
    n_chunks = tm // L

    @pl.when(i == 0)
    def _():
        state_s[...] = jnp.zeros_like(state_s)

    seg = seg_ref[...]

    def seg_sum(t):
        return _seg_dot(t.astype(BF16), seg)

    for d, (z_ref, bonus_ref) in enumerate(((zf_ref, bonus_f_ref), (zb_ref, bonus_b_ref))):
        r = z_ref[0, :, 0:RWKV_WIDTH]
        k = z_ref[0, :, RWKV_WIDTH:2 * RWKV_WIDTH]
        v = z_ref[0, :, 2 * RWKV_WIDTH:3 * RWKV_WIDTH]
        lora = z_ref[0, :, 3 * RWKV_WIDTH:]
        kk = k * kk_ref[...]
        kk = kk * lax.rsqrt(jnp.maximum(seg_sum(kk * kk), 1e-24))
        w_pre = w0_ref[d:d + 1, :] + _mm(jnp.tanh(lora), w2_ref[d])
        lw_s[d] = -DECAY_SCALE * _sigmoid(w_pre)
        alpha = _sigmoid(a0_ref[d:d + 1, :] + _mm(lora, a2_ref[d]))
        kd = k * (1.0 + (alpha - 1.0) * ka_ref[...])
        kk_s[d] = kk
        kd_s[d] = kd
        b_s[d] = kk * alpha
        bonus_ref[0] = (seg_sum(r * kd * rk_ref[...]) * v).astype(bonus_ref.dtype)
        if d == 0:
            g_ref[0] = _mm(_sigmoid(lora), g2_ref[...]).astype(g_ref.dtype)

    ri = lax.broadcasted_iota(jnp.int32, (L, L), 0)
    ci = lax.broadcasted_iota(jnp.int32, (L, L), 1)
    eye = (ci == ri).astype(F32)
    ri2 = lax.broadcasted_iota(jnp.int32, (L, 2 * L), 0)
    ci2 = lax.broadcasted_iota(jnp.int32, (L, 2 * L), 1)
    eye_hi = (ci2 == ri2 + L).astype(F32)
    ci2 = jnp.where(ci2 >= L, ci2 - L, ci2)
    m_strict = (ci2 < ri2, ci2 > ri2)
    m_incl = (ci2 <= ri2, ci2 >= ri2)
    z_refs = (zf_ref, zb_ref)
    out_refs = (of_ref, ob_ref)

    def chunk_units(c):
        rows_d = (pl.ds(pl.multiple_of(c * L, L), L), pl.ds(pl.multiple_of((n_chunks - 1 - c) * L, L), L))
        units = []
        for d in range(2):
            rows = rows_d[d]
            lw = lw_s[d, rows, :]
            bc = b_s[d, rows, :]
            kdc = kd_s[d, rows, :]
            vc = z_refs[d][0, rows, 2 * RWKV_WIDTH:3 * RWKV_WIDTH]
            cum = _mm_exact_lhs(tri_ref[d], lw)
            tot = cum[L - 1:L, :] if d == 0 else cum[0:1, :]
            e_in = jnp.exp(-cum)
            e_rem = jnp.exp(tot - cum)
            e_tot = jnp.exp(tot)
            At = (-kk_s[d, rows, :] * jnp.exp(cum - lw)).astype(BF16)
            Rt = (z_refs[d][0, rows, 0:RWKV_WIDTH] * jnp.exp(cum)).astype(BF16)
            Bt = (bc * e_in).astype(BF16)
            Kt = (kdc * e_in).astype(BF16)
            Bh = (bc * e_rem).astype(BF16)
            Kh = (kdc * e_rem).astype(BF16)
            vc = vc.astype(BF16)
            for h in range(RWKV_HEADS):
                sl = slice(h * HEAD_DIM, (h + 1) * HEAD_DIM)
                units.append((At[:, sl], Rt[:, sl], Bt[:, sl], Kt[:, sl], Bh[:, sl], Kh[:, sl],
                              vc[:, sl], e_tot[:, sl], m_strict[d], m_incl[d]))
        return rows_d, units

    n_units = 2 * RWKV_HEADS

    def group_body(j, carry):
        rows, units = [], []
        for g in range(CHUNK_GROUP):
            rows_d, chunk = chunk_units(j * CHUNK_GROUP + g)
            rows.append(rows_d)
            units.extend(chunk)
        Q, GH = _chunk_units(units, eye, eye_hi)
        M = [state_s[u] for u in range(n_units)]
        for g in range(CHUNK_GROUP):
            sl = slice(g * n_units, (g + 1) * n_units)
            prod = [_mm(jnp.concatenate([q[:, :N], gh[:, :N]], axis=0), m) for q, gh, m in zip(Q[sl], GH[sl], M)]
            out = [p[:L] + q[:, N:] for p, q in zip(prod, Q[sl])]
            M = [p[L:] + gh[:, N:] for p, gh in zip(prod, GH[sl])]
            for d in range(2):
                for h in range(RWKV_HEADS):
                    out_refs[d][0, rows[g][d], h * N:(h + 1) * N] = out[d * RWKV_HEADS + h]
        for u in range(n_units):
            state_s[u] = M[u]
        return carry

    lax.fori_loop(0, n_chunks // CHUNK_GROUP, group_body, 0)


def _rwkv(z_rwkv, w0, w2_p, a0, a2_p, g2_p, k_k, k_a, r_k, seg, tri, tm):
    B, S, ZC = z_rwkv.shape
    C = RWKV_WIDTH
    n_tiles = S // tm
    fwd = lambda b, i: (b, i, 0)
    bwd = lambda b, i: (b, n_tiles - 1 - i, 0)
    out_f = pl.BlockSpec((1, tm, C), fwd)
    out_b = pl.BlockSpec((1, tm, C), bwd)
    out_sds = lambda dtype: jax.ShapeDtypeStruct((B, S, C), dtype)
    return pl.pallas_call(
        _rwkv_kernel,
        grid=(B, n_tiles),
        in_specs=[pl.BlockSpec((1, tm, ZC), fwd),
                  pl.BlockSpec((1, tm, ZC), bwd),
                  _const_spec(w0.shape), _const_spec(w2_p.shape),
                  _const_spec(a0.shape), _const_spec(a2_p.shape), _const_spec(g2_p.shape),
                  _const_spec(k_k.shape), _const_spec(k_a.shape), _const_spec(r_k.shape),
                  _const_spec(seg.shape), _const_spec(tri.shape)],
        out_specs=[out_f, out_b, out_f, out_b, out_f],
        out_shape=[out_sds(F32), out_sds(F32), out_sds(BF16), out_sds(BF16), out_sds(BF16)],
        scratch_shapes=[pltpu.VMEM((2, tm, C), F32)] * 4
                       + [pltpu.VMEM((2 * RWKV_HEADS, HEAD_DIM, HEAD_DIM), F32)],
        compiler_params=_params("parallel", "arbitrary"),
        name="rwkv",
    )(z_rwkv, z_rwkv, w0, w2_p, a0, a2_p, g2_p, k_k, k_a, r_k, seg, tri)


def _rope_tiles(pos_row, freq_col, place_cos, place_sin, base):
    ang = freq_col * pos_row.astype(F32)

    def place(t, p):
        return sum(lax.dot_general(part, p, (((0,), (0,)), ((), ())), preferred_element_type=F32)
                   for part in _split3(t))

    return place(jnp.cos(ang), place_cos) + base, place(jnp.sin(ang), place_sin)


def _mla_kernel(zq_ref, zkv_ref, pos_ref, freq_ref, pcos_ref, psin_ref, base_ref, qn_ref, wq_ref, kvn_ref, wkv_ref,
                y_ref, k_s, v_s, rot_s):
    i = pl.program_id(1)
    tq = zq_ref.shape[1]
    scale = (QK_NOPE_DIM + QK_ROPE_DIM) ** -0.5
    c_kv_lo, c_kv_hi = Q_LORA_RANK, Q_LORA_RANK + KV_LORA_RANK

    @pl.when(i == 0)
    def _():
        zkv = zkv_ref[0]
        kvn = _rms(zkv[:, c_kv_lo:c_kv_hi], kvn_ref[...]).astype(BF16)
        kvu = jnp.dot(kvn, wkv_ref[...], preferred_element_type=F32)
        cos_t, sin_t = _rope_tiles(pos_ref[0], freq_ref[...], pcos_ref[...], psin_ref[...], base_ref[...])
        rot_s[...] = cos_t + pltpu.roll(sin_t, QK_ROPE_DIM, 1)
        k_rope = zkv[:, c_kv_hi:]
        k_rot = k_rope * cos_t + pltpu.roll(k_rope, LANES - QK_ROPE_DIM, 1) * sin_t
        k_rot = k_rot + pltpu.roll(k_rot, QK_ROPE_DIM, 1)
        lane = lax.broadcasted_iota(jnp.int32, (1, LANES), 1)
        ones_col = (lane == V_HEAD_DIM).astype(F32)
        for h in range(MLA_HEADS):
            k_s[h] = (kvu[:, h * LANES:(h + 1) * LANES] + k_rot).astype(BF16)
            v_s[h] = (kvu[:, (MLA_HEADS + h) * LANES:(MLA_HEADS + h + 1) * LANES] + ones_col).astype(BF16)

    zq = zq_ref[0]
    qn = _rms(zq[:, :Q_LORA_RANK], qn_ref[...]).astype(BF16)
    q = jnp.dot(qn, wq_ref[...], preferred_element_type=F32)
    rows = pl.ds(pl.multiple_of(i * tq, tq), tq)
    rot_q = rot_s[rows, :] * (scale * LOG2_E)

    def scores(h):
        sl = slice(h * LANES, (h + 1) * LANES)
        qh = (q[:, sl] * rot_q).astype(BF16)
        return lax.dot_general(qh, k_s[h], (((1,), (1,)), ((), ())), preferred_element_type=F32)

    s = scores(0)
    for h in range(MLA_HEADS):
        s_next = scores(h + 1) if h + 1 < MLA_HEADS else None
        p = jnp.exp2((s - jnp.max(s, axis=-1, keepdims=True)).astype(BF16))
        o = jnp.dot(p, v_s[h], preferred_element_type=F32)
        y_ref[0, :, h * V_HEAD_DIM:(h + 1) * V_HEAD_DIM] = (
            o[:, :V_HEAD_DIM] / o[:, V_HEAD_DIM:V_HEAD_DIM + 1]).astype(y_ref.dtype)
        s = s_next


def _mla_attn(z_mla, pos_row, freq_col, place_cos, place_sin, base, q_norm, wq_p, kv_norm, wkv_p, tq):
    B, S, ZC = z_mla.shape
    return pl.pallas_call(
        _mla_kernel,
        grid=(B, S // tq),
        in_specs=[pl.BlockSpec((1, tq, ZC), lambda b, i: (b, i, 0)),
                  pl.BlockSpec((1, S, ZC), lambda b, i: (b, 0, 0)),
                  pl.BlockSpec((1, 1, S), lambda b, i: (b, 0, 0)),
                  _const_spec(freq_col.shape), _const_spec(place_cos.shape), _const_spec(place_sin.shape),
                  _const_spec(base.shape), _const_spec(q_norm.shape),
                  _const_spec(wq_p.shape), _const_spec(kv_norm.shape),
                  _const_spec(wkv_p.shape)],
        out_specs=pl.BlockSpec((1, tq, MLA_WIDTH), lambda b, i: (b, i, 0)),
        out_shape=jax.ShapeDtypeStruct((B, S, MLA_WIDTH), BF16),
        scratch_shapes=[pltpu.VMEM((MLA_HEADS, S, LANES), BF16), pltpu.VMEM((MLA_HEADS, S, LANES), BF16),
                        pltpu.VMEM((S, LANES), F32)],
        compiler_params=_params("parallel", "arbitrary"),
        name="mla_attn",
    )(z_mla, z_mla, pos_row, freq_col, place_cos, place_sin, base, q_norm, wq_p, kv_norm, wkv_p)


def _mix_out_rows(x, o_f, o_b, bonus, gate, lnw, lnb, segm, y_mla, w_r, w_m, g):
    o = o_f + o_b
    o_hi, o_lo = _split2(o)
    oc = o - (_seg_dot(o_hi, segm) + _seg_dot(o_lo, segm))
    var = _seg_dot((oc * oc).astype(BF16), segm)
    o = oc * lax.rsqrt(var + LN_X_EPS) * lnw + lnb
    y_rwkv = (o + bonus) * gate
    y = _dot(y_rwkv.astype(BF16), w_r) + _dot(y_mla, w_m)
    return x + _rms(y, g)


def _mem_attn_rows(x, k_ref, v_ref, g_pre, wq, wo, g_post, att_s):
    h = _rms(x, g_pre).astype(BF16)
    q = (_dot(h, wq) * (MEM_HEAD_DIM ** -0.5 * LOG2_E)).astype(BF16)

    def scores(hd):
        sl = slice(hd * MEM_HEAD_DIM, (hd + 1) * MEM_HEAD_DIM)
        return lax.dot_general(q[:, sl], k_ref[:, sl], (((1,), (1,)), ((), ())), preferred_element_type=F32)

    s = scores(0)
    for hd in range(MEM_HEADS):
        sl = slice(hd * MEM_HEAD_DIM, (hd + 1) * MEM_HEAD_DIM)
        s_next = scores(hd + 1) if hd + 1 < MEM_HEADS else None
        p = jnp.exp2(s - jnp.max(s, axis=-1, keepdims=True))
        denom = jnp.sum(p, axis=-1, keepdims=True)
        o = _dot(p.astype(BF16), v_ref[:, sl])
        att_s[:, sl] = (o / denom).astype(BF16)
        s = s_next
    return x + _rms(_dot(att_s[...], wo), g_post)


def _mix_mem_kernel(x_ref, of_ref, ob_ref, bf_ref, bb_ref, gate_ref, lnw_ref, lnb_ref, segm_ref, ym_ref,
                    wr_ref, wm_ref, gmix_ref, mem_ref, gmem_ref, wkv_ref, gpre_ref, wq_ref, wo_ref, gpost_ref,
                    o_ref, k_s, v_s, att_s):
    @pl.when(pl.program_id(1) == 0)
    def _():
        m = _rms(mem_ref[0], gmem_ref[...]).astype(BF16)
        kv = _dot(m, wkv_ref[...])
        k_s[...] = kv[:, :D_MODEL].astype(BF16)
        v_s[...] = kv[:, D_MODEL:].astype(BF16)

    bonus = bf_ref[0].astype(F32) + bb_ref[0].astype(F32)
    x = _mix_out_rows(x_ref[0], of_ref[0], ob_ref[0], bonus, gate_ref[0].astype(F32), lnw_ref[...], lnb_ref[...],
                      segm_ref[...], ym_ref[0], wr_ref[...], wm_ref[...], gmix_ref[...])
    o_ref[0] = _mem_attn_rows(x, k_s, v_s, gpre_ref[...], wq_ref[...], wo_ref[...], gpost_ref[...], att_s)


def _mix_mem(x, o_f, o_b, bonus_f, bonus_b, gate, lnw, lnb, segm, y_mla, w_r, w_m, g_mix,
             mem, g_mem, wkv, g_pre, wq, wo, g_post, tm):
    B, S, D = x.shape
    T = mem.shape[1]
    C = RWKV_WIDTH
    row_spec = lambda cols: pl.BlockSpec((1, tm, cols), lambda b, i: (b, i, 0))
    return pl.pallas_call(
        _mix_mem_kernel,
        grid=(B, S // tm),
        in_specs=[row_spec(D), row_spec(C), row_spec(C), row_spec(C), row_spec(C), row_spec(C)]
                 + [_const_spec(t.shape) for t in (lnw, lnb, segm)] + [row_spec(MLA_WIDTH)]
                 + [_const_spec(t.shape) for t in (w_r, w_m, g_mix)]
                 + [pl.BlockSpec((1, T, D), lambda b, i: (b, 0, 0))]
                 + [_const_spec(t.shape) for t in (g_mem, wkv, g_pre, wq, wo, g_post)],
        out_specs=row_spec(D),
        out_shape=jax.ShapeDtypeStruct((B, S, D), F32),
        scratch_shapes=[pltpu.VMEM((T, D), BF16), pltpu.VMEM((T, D), BF16), pltpu.VMEM((tm, D), BF16)],
        compiler_params=_params("parallel", "arbitrary"),
        name="mix_mem",
    )(x, o_f, o_b, bonus_f, bonus_b, gate, lnw, lnb, segm, y_mla, w_r, w_m, g_mix,
      mem, g_mem, wkv, g_pre, wq, wo, g_post)


def _mlp_kernel(x_ref, gpre_ref, w1_ref, w2_ref, gpost_ref, o_ref):
    x = x_ref[0]
    h = _rms(x, gpre_ref[...]).astype(BF16)
    u = jnp.maximum(jnp.dot(h, w1_ref[...], preferred_element_type=F32), 0.0)
    y = jnp.dot((u * u).astype(BF16), w2_ref[...], preferred_element_type=F32)
    o_ref[0] = x + _rms(y, gpost_ref[...])


def _mlp(x, g_pre, w1, w2, g_post, tm):
    B, S, D = x.shape
    row_spec = pl.BlockSpec((1, tm, D), lambda b, i: (b, i, 0))
    return pl.pallas_call(
        _mlp_kernel,
        grid=(B, S // tm),
        in_specs=[row_spec, _const_spec(g_pre.shape), _const_spec(w1.shape), _const_spec(w2.shape),
                  _const_spec(g_post.shape)],
        out_specs=row_spec,
        out_shape=jax.ShapeDtypeStruct((B, S, D), F32),
        compiler_params=_params("parallel", "parallel"),
        name="mlp",
    )(x, g_pre, w1, w2, g_post)


def _pad_cols(w, n):
    return jnp.pad(w, ((0, 0), (0, n - w.shape[1])))


def _rope_tile(w):
    half = QK_ROPE_DIM // 2
    return jnp.concatenate([jnp.zeros((w.shape[0], QK_NOPE_DIM), w.dtype), w, w[:, half:], w[:, :half]], axis=1)


def _lora_rows(w, lo, rows):
    return jnp.pad(w, ((lo, LORA_TILE - lo - rows), (0, 0)))


def kernel(x, mem, positions, norm_mix_pre, w_in, conv_rwkv, rwkv_w0, rwkv_w2, rwkv_a0, rwkv_a2, rwkv_g2, rwkv_k_k, rwkv_k_a, rwkv_r_k, rwkv_lnx_w, rwkv_lnx_b, mla_q_norm, mla_w_uq, mla_kv_norm, mla_w_ukv, w_out, norm_mix_post, norm_mem_pre, norm_memtok, mem_wq, mem_wkv, mem_wo, norm_mem_post, norm_mlp_pre, mlp_w1, mlp_w2, norm_mlp_post):
    depth = w_in.shape[0]
    C = RWKV_WIDTH
    head_of = jnp.arange(C) // HEAD_DIM
    seg = (head_of[:, None] == head_of[None, :]).astype(BF16)
    seg = seg[:C // 2, :C // 2]
    seg_mean = (seg.astype(F32) / HEAD_DIM).astype(BF16)
    inv_freq = ROPE_THETA ** (-jnp.arange(0, QK_ROPE_DIM, 2, dtype=F32) / QK_ROPE_DIM)
    half = QK_ROPE_DIM // 2
    lane = jnp.arange(LANES)[None, :]
    f_idx = jnp.arange(half)[:, None]
    first, second = lane == QK_NOPE_DIM + f_idx, lane == QK_NOPE_DIM + half + f_idx
    place_cos = (first | second).astype(BF16)
    place_sin = (second.astype(F32) - first.astype(F32)).astype(BF16)
    rope_base = (lane < QK_NOPE_DIM).astype(F32)
    freq_col = inv_freq[:, None]
    pos_row = positions[:, None, :]
    step = jnp.arange(CHUNK)
    tri = jnp.stack([step[None, :] <= step[:, None], step[None, :] >= step[:, None]]).astype(F32)
    row = lambda t: t.reshape(1, -1)

    for l in range(depth):
        w = w_in[l]
        mla0 = RWKV_COLS
        w_rope = w[:, mla0 + Q_LORA_RANK + KV_LORA_RANK:]
        w_r = _pad_cols(w[:, :RWKV_COLS], RWKV_TILE_COLS).astype(BF16)
        w_m = jnp.concatenate([w[:, mla0:mla0 + Q_LORA_RANK + KV_LORA_RANK], _rope_tile(w_rope)],
                              axis=1).astype(BF16)
        conv_p = _pad_cols(conv_rwkv[l], RWKV_TILE_COLS)
        z_rwkv, z_mla = _in_proj(x, row(norm_mix_pre[l]), w_r, w_m, conv_p, tm=512)

        w2_p = jnp.stack([_lora_rows(rwkv_w2[l, d], d * DECAY_LORA, DECAY_LORA) for d in range(2)])
        a_lo = 2 * DECAY_LORA
        a2_p = jnp.stack([_lora_rows(rwkv_a2[l, d], a_lo + d * ICLR_LORA, ICLR_LORA) for d in range(2)])
        g2_p = _lora_rows(rwkv_g2[l], a_lo + 2 * ICLR_LORA, GATE_LORA)
        o_f, o_b, bonus_f, bonus_b, gate = _rwkv(z_rwkv, rwkv_w0[l], w2_p, rwkv_a0[l], a2_p, g2_p,
                                                 row(rwkv_k_k[l]), row(rwkv_k_a[l]), row(rwkv_r_k[l]),
                                                 seg, tri, tm=512)

        qk = QK_NOPE_DIM + QK_ROPE_DIM
        w_uq = mla_w_uq[l].reshape(Q_LORA_RANK, MLA_HEADS, qk)
        uq_rope = w_uq[:, :, QK_NOPE_DIM:]
        uq_sw = jnp.concatenate([uq_rope[:, :, half:], uq_rope[:, :, :half]], axis=2)
        wq_p = jnp.concatenate([w_uq, uq_sw], axis=2).reshape(Q_LORA_RANK, MLA_HEADS * LANES)
        w_ukv = mla_w_ukv[l].reshape(KV_LORA_RANK, MLA_HEADS, QK_NOPE_DIM + V_HEAD_DIM)
        wk_p = jnp.pad(w_ukv[:, :, :QK_NOPE_DIM], ((0, 0), (0, 0), (0, LANES - QK_NOPE_DIM)))
        wv_p = jnp.pad(w_ukv[:, :, QK_NOPE_DIM:], ((0, 0), (0, 0), (0, LANES - V_HEAD_DIM)))
        wkv_p = jnp.concatenate([wk_p.reshape(KV_LORA_RANK, MLA_HEADS * LANES),
                                 wv_p.reshape(KV_LORA_RANK, MLA_HEADS * LANES)], axis=1)
        y_mla = _mla_attn(z_mla, pos_row, freq_col, place_cos, place_sin, rope_base, row(mla_q_norm[l]),
                          wq_p.astype(BF16), row(mla_kv_norm[l]), wkv_p.astype(BF16), tq=512)

        wo = w_out[l].astype(BF16)
        x = _mix_mem(x, o_f, o_b, bonus_f, bonus_b, gate, row(rwkv_lnx_w[l]), row(rwkv_lnx_b[l]), seg_mean,
                     y_mla, wo[:C], wo[C:], row(norm_mix_post[l]), mem, row(norm_memtok[l]),
                     mem_wkv[l].astype(BF16), row(norm_mem_pre[l]),
                     mem_wq[l].astype(BF16), mem_wo[l].astype(BF16), row(norm_mem_post[l]), tm=512)

        x = _mlp(x, row(norm_mlp_pre[l]), mlp_w1[l].astype(BF16), mlp_w2[l].astype(BF16),
                 row(norm_mlp_post[l]), tm=512)
    return x
```

```python
import math

import jax
import jax.numpy as jnp
from jax import lax
from jax.experimental import pallas as pl
from jax.experimental.pallas import tpu as pltpu

F32 = jnp.float32
BF16 = jnp.bfloat16

D_MODEL = 1024
NORM_EPS = 1e-6

RWKV_HEADS = 8
HEAD_DIM = 64
RWKV_WIDTH = RWKV_HEADS * HEAD_DIM
DECAY_LORA = 32
ICLR_LORA = 32
GATE_LORA = 96
LORA_COLS = 2 * DECAY_LORA + 2 * ICLR_LORA + GATE_LORA
LORA_TILE = 256
RWKV_COLS = 3 * RWKV_WIDTH + LORA_COLS
RWKV_TILE_COLS = 3 * RWKV_WIDTH + LORA_TILE
LN_X_EPS = 64e-5
CHUNK = 64
CHUNK_GROUP = 2
DECAY_SCALE = math.exp(-0.5)

MLA_HEADS = 8
QK_NOPE_DIM = 64
QK_ROPE_DIM = 32
V_HEAD_DIM = 64
MLA_WIDTH = MLA_HEADS * V_HEAD_DIM
Q_LORA_RANK = 256
KV_LORA_RANK = 128
ROPE_THETA = 10000.0
LOG2_E = math.log2(math.e)
LANES = 128
MLA_TILE_COLS = Q_LORA_RANK + KV_LORA_RANK + LANES

MEM_HEADS = 4
MEM_HEAD_DIM = D_MODEL // MEM_HEADS

V7X_VMEM_BYTES = 64 * 1024 * 1024
VMEM_LIMIT = V7X_VMEM_BYTES - 8 * 1024 * 1024


def _mm(a, b):
    return jnp.dot(a.astype(BF16), b.astype(BF16), preferred_element_type=F32)


def _split2(t):
    hi = t.astype(BF16)
    lo = (t - hi.astype(F32)).astype(BF16)
    return hi, lo


def _split3(t):
    hi = t.astype(BF16)
    rest = t - hi.astype(F32)
    mid = rest.astype(BF16)
    lo = (rest - mid.astype(F32)).astype(BF16)
    return hi, mid, lo


def _dot(a, b):
    return jnp.dot(a, b, preferred_element_type=F32)


def _seg_dot(t, seg):
    half = seg.shape[0]
    return jnp.concatenate([_dot(t[:, :half], seg), _dot(t[:, half:], seg)], axis=1)


def _mm_exact_lhs(a, b):
    a = a.astype(BF16)
    hi, lo = _split2(b)
    return _dot(a, hi) + _dot(a, lo)


def _mm_nt(a, b):
    return lax.dot_general(a.astype(BF16), b.astype(BF16), (((1,), (1,)), ((), ())),
                           preferred_element_type=F32)


def _rms(x, g, eps=NORM_EPS):
    return x * lax.rsqrt(jnp.mean(x * x, axis=-1, keepdims=True) + eps) * g


def _sigmoid(x):
    return 0.5 * jnp.tanh(0.5 * x) + 0.5


def _params(*sem):
    return pltpu.CompilerParams(dimension_semantics=sem, vmem_limit_bytes=VMEM_LIMIT)


def _const_spec(shape):
    nd = len(shape)
    return pl.BlockSpec(shape, lambda *_: (0,) * nd)


def _in_proj_kernel(x_ref, xp_ref, xn_ref, g_ref, wr_ref, wm_ref, conv_ref, zr_ref, zm_ref):
    i = pl.program_id(1)
    last = pl.num_programs(1) - 1
    tm = x_ref.shape[1]
    g = g_ref[...]
    h = _rms(x_ref[0], g).astype(BF16)
    zm_ref[0] = jnp.dot(h, wm_ref[...], preferred_element_type=F32)
    z = jnp.dot(h, wr_ref[...], preferred_element_type=F32)
    halo = _rms(jnp.concatenate([xp_ref[0], xn_ref[0]], axis=0), g).astype(BF16)
    z_halo = jnp.dot(halo, wr_ref[...], preferred_element_type=F32)
    prev_row = jnp.where(i == 0, 0.0, z_halo[7:8, :])
    next_row = jnp.where(i == last, 0.0, z_halo[8:9, :])
    row = lax.broadcasted_iota(jnp.int32, (tm, 1), 0)
    z_dn = jnp.where(row == 0, prev_row, pltpu.roll(z, 1, 0))
    z_up = jnp.where(row == tm - 1, next_row, pltpu.roll(z, tm - 1, 0))
    zr_ref[0] = (conv_ref[0:1, :] * z_dn + conv_ref[1:2, :] * z + conv_ref[2:3, :] * z_up).astype(zr_ref.dtype)


def _in_proj(x, g, w_r, w_m, conv_p, tm):
    B, S, D = x.shape
    halo = tm // 8
    n_halo = S // 8
    return pl.pallas_call(
        _in_proj_kernel,
        grid=(B, S // tm),
        in_specs=[pl.BlockSpec((1, tm, D), lambda b, i: (b, i, 0)),
                  pl.BlockSpec((1, 8, D), lambda b, i: (b, jnp.maximum(i * halo - 1, 0), 0)),
                  pl.BlockSpec((1, 8, D), lambda b, i: (b, jnp.minimum((i + 1) * halo, n_halo - 1), 0)),
                  _const_spec((1, D)),
                  _const_spec(w_r.shape),
                  _const_spec(w_m.shape),
                  _const_spec(conv_p.shape)],
        out_specs=[pl.BlockSpec((1, tm, RWKV_TILE_COLS), lambda b, i: (b, i, 0)),
                   pl.BlockSpec((1, tm, MLA_TILE_COLS), lambda b, i: (b, i, 0))],
        out_shape=[jax.ShapeDtypeStruct((B, S, RWKV_TILE_COLS), BF16),
                   jax.ShapeDtypeStruct((B, S, MLA_TILE_COLS), F32)],
        compiler_params=_params("parallel", "parallel"),
        name="in_proj",
    )(x, x, x, g, w_r, w_m, conv_p)


def _chunk_units(units, eye, eye_hi):
    L, N = units[0][0].shape
    zeros = jnp.zeros((L, N), BF16)
    AA = [_mm_nt(jnp.concatenate([At, Rt], axis=0), jnp.concatenate([Bt, Kt], axis=0))
          for (At, Rt, Bt, Kt, *_) in units]
    A_a = [jnp.where(u[8], aa[:L, :], 0.0) for u, aa in zip(units, AA)]
    A_r = [jnp.where(u[9], aa[L:, :], 0.0) for u, aa in zip(units, AA)]
    AkV = [_mm(a.astype(BF16)[:, L:], u[6]) for u, a in zip(units, A_a)]
    low = lax.broadcasted_iota(jnp.int32, (L, 2 * L), 1) < L
    S = [jnp.where(low, a, eye_hi) for a in A_a]
    span = 1
    while span < L:
        R = [_mm(s[:, :L], s) for s in S]
        S = [jnp.where(low, r, r + s) for r, s in zip(R, S)]
        span *= 2
    W = [_mm(s.astype(BF16)[:, L:], jnp.concatenate([u[0], akv.astype(BF16)], axis=1))
         for s, u, akv in zip(S, units, AkV)]
    Z = [jnp.concatenate([w.astype(BF16), jnp.concatenate([zeros, u[6]], axis=1)], axis=0)
         for u, w in zip(units, W)]
    lhs = [jnp.concatenate([a_r.astype(BF16), jnp.concatenate([u[4], u[5]], axis=0).T], axis=0)
           for u, a_r in zip(units, A_r)]
    QGH = [_mm(l, z) for l, z in zip(lhs, Z)]
    Q = [jnp.concatenate([u[1], zeros], axis=1) + t[:L] for u, t in zip(units, QGH)]
    GH = [jnp.concatenate([eye * u[7], zeros.astype(F32)], axis=1) + t[L:] for u, t in zip(units, QGH)]
    return Q, GH


def _rwkv_kernel(zf_ref, zb_ref,
                 w0_ref, w2_ref, a0_ref, a2_ref, g2_ref, kk_ref, ka_ref, rk_ref, seg_ref, tri_ref,
                 of_ref, ob_ref, bonus_f_ref, bonus_b_ref, g_ref,
                 kk_s, b_s, kd_s, lw_s, state_s):
    i = pl.program_id(1)
    tm = zf_ref.shape[1]
    L, N = CHUNK, HEAD_DIM
    n_chunks = tm // L

    @pl.when(i == 0)
    def _():
        state_s[...] = jnp.zeros_like(state_s)

    seg = seg_ref[...]

    def seg_sum(t):
        return _seg_dot(t.astype(BF16), seg)

    for d, (z_ref, bonus_ref) in enumerate(((zf_ref, bonus_f_ref), (zb_ref, bonus_b_ref))):
        r = z_ref[0, :, 0:RWKV_WIDTH].astype(F32)
        k = z_ref[0, :, RWKV_WIDTH:2 * RWKV_WIDTH].astype(F32)
        v = z_ref[0, :, 2 * RWKV_WIDTH:3 * RWKV_WIDTH].astype(F32)
        lora = z_ref[0, :, 3 * RWKV_WIDTH:].astype(F32)
        kk = k * kk_ref[...]
        kk = kk * lax.rsqrt(jnp.maximum(seg_sum(kk * kk), 1e-24))
        w_pre = w0_ref[d:d + 1, :] + _mm(jnp.tanh(lora), w2_ref[d])
        lw_s[d] = -DECAY_SCALE * _sigmoid(w_pre)
        alpha = _sigmoid(a0_ref[d:d + 1, :] + _mm(lora, a2_ref[d]))
        kd = k * (1.0 + (alpha - 1.0) * ka_ref[...])
        kk_s[d] = kk
        kd_s[d] = kd
        b_s[d] = kk * alpha
        bonus_ref[0] = (seg_sum(r * kd * rk_ref[...]) * v).astype(bonus_ref.dtype)
        if d == 0:
            g_ref[0] = _mm(_sigmoid(lora), g2_ref[...]).astype(g_ref.dtype)

    ri = lax.broadcasted_iota(jnp.int32, (L, L), 0)
    ci = lax.broadcasted_iota(jnp.int32, (L, L), 1)
    eye = (ci == ri).astype(F32)
    ri2 = lax.broadcasted_iota(jnp.int32, (L, 2 * L), 0)
    ci2 = lax.broadcasted_iota(jnp.int32, (L, 2 * L), 1)
    eye_hi = (ci2 == ri2 + L).astype(F32)
    ci2 = jnp.where(ci2 >= L, ci2 - L, ci2)
    m_strict = (ci2 < ri2, ci2 > ri2)
    m_incl = (ci2 <= ri2, ci2 >= ri2)
    z_refs = (zf_ref, zb_ref)
    out_refs = (of_ref, ob_ref)

    def chunk_units(c):
        rows_d = (pl.ds(pl.multiple_of(c * L, L), L), pl.ds(pl.multiple_of((n_chunks - 1 - c) * L, L), L))
        units = []
        for d in range(2):
            rows = rows_d[d]
            lw = lw_s[d, rows, :]
            bc = b_s[d, rows, :]
            kdc = kd_s[d, rows, :]
            vc = z_refs[d][0, rows, 2 * RWKV_WIDTH:3 * RWKV_WIDTH]
            cum = _mm_exact_lhs(tri_ref[d], lw)
            tot = cum[L - 1:L, :] if d == 0 else cum[0:1, :]
            e_in = jnp.exp(-cum)
            e_rem = jnp.exp(tot - cum)
            e_tot = jnp.exp(tot)
            At = (-kk_s[d, rows, :] * jnp.exp(cum - lw)).astype(BF16)
            Rt = (z_refs[d][0, rows, 0:RWKV_WIDTH].astype(F32) * jnp.exp(cum)).astype(BF16)
            Bt = (bc * e_in).astype(BF16)
            Kt = (kdc * e_in).astype(BF16)
            Bh = (bc * e_rem).astype(BF16)
            Kh = (kdc * e_rem).astype(BF16)
            vc = vc.astype(BF16)
            for h in range(RWKV_HEADS):
                sl = slice(h * HEAD_DIM, (h + 1) * HEAD_DIM)
                units.append((At[:, sl], Rt[:, sl], Bt[:, sl], Kt[:, sl], Bh[:, sl], Kh[:, sl],
                              vc[:, sl], e_tot[:, sl], m_strict[d], m_incl[d]))
        return rows_d, units

    n_units = 2 * RWKV_HEADS

    def group_body(j, carry):
        rows, units = [], []
        for g in range(CHUNK_GROUP):
            rows_d, chunk = chunk_units(j * CHUNK_GROUP + g)
            rows.append(rows_d)
            units.extend(chunk)
        Q, GH = _chunk_units(units, eye, eye_hi)
        M = [state_s[u] for u in range(n_units)]
        for g in range(CHUNK_GROUP):
            sl = slice(g * n_units, (g + 1) * n_units)
            prod = [_mm(jnp.concatenate([q[:, :N], gh[:, :N]], axis=0), m) for q, gh, m in zip(Q[sl], GH[sl], M)]
            out = [p[:L] + q[:, N:] for p, q in zip(prod, Q[sl])]
            M = [p[L:] + gh[:, N:] for p, gh in zip(prod, GH[sl])]
            for d in range(2):
                for h in range(RWKV_HEADS):
                    out_refs[d][0, rows[g][d], h * N:(h + 1) * N] = out[d * RWKV_HEADS + h]
        for u in range(n_units):
            state_s[u] = M[u]
        return carry

    lax.fori_loop(0, n_chunks // CHUNK_GROUP, group_body, 0)


def _rwkv(z_rwkv, w0, w2_p, a0, a2_p, g2_p, k_k, k_a, r_k, seg, tri, tm):
    B, S, ZC = z_rwkv.shape
    C = RWKV_WIDTH
    n_tiles = S // tm
    fwd = lambda b, i: (b, i, 0)
    bwd = lambda b, i: (b, n_tiles - 1 - i, 0)
    out_f = pl.BlockSpec((1, tm, C), fwd)
    out_b = pl.BlockSpec((1, tm, C), bwd)
    out_sds = lambda dtype: jax.ShapeDtypeStruct((B, S, C), dtype)
    return pl.pallas_call(
        _rwkv_kernel,
        grid=(B, n_tiles),
        in_specs=[pl.BlockSpec((1, tm, ZC), fwd),
                  pl.BlockSpec((1, tm, ZC), bwd),
                  _const_spec(w0.shape), _const_spec(w2_p.shape),
                  _const_spec(a0.shape), _const_spec(a2_p.shape), _const_spec(g2_p.shape),
                  _const_spec(k_k.shape), _const_spec(k_a.shape), _const_spec(r_k.shape),
                  _const_spec(seg.shape), _const_spec(tri.shape)],
        out_specs=[out_f, out_b, out_f, out_b, out_f],
        out_shape=[out_sds(F32), out_sds(F32), out_sds(BF16), out_sds(BF16), out_sds(BF16)],
        scratch_shapes=[pltpu.VMEM((2, tm, C), F32)] * 4
                       + [pltpu.VMEM((2 * RWKV_HEADS, HEAD_DIM, HEAD_DIM), F32)],
        compiler_params=_params("parallel", "arbitrary"),
        name="rwkv",
    )(z_rwkv, z_rwkv, w0, w2_p, a0, a2_p, g2_p, k_k, k_a, r_k, seg, tri)


def _rope_tiles(pos_row, freq_col, place_cos, place_sin, base):
    ang = freq_col * pos_row.astype(F32)

    def place(t, p):
        return sum(lax.dot_general(part, p, (((0,), (0,)), ((), ())), preferred_element_type=F32)
                   for part in _split3(t))

    return place(jnp.cos(ang), place_cos) + base, place(jnp.sin(ang), place_sin)


def _mla_kernel(zq_ref, zkv_ref, pos_ref, freq_ref, pcos_ref, psin_ref, base_ref, qn_ref, wq_ref, kvn_ref, wkv_ref,
                y_ref, k_s, v_s, rot_s):
    i = pl.program_id(1)
    tq = zq_ref.shape[1]
    scale = (QK_NOPE_DIM + QK_ROPE_DIM) ** -0.5
    c_kv_lo, c_kv_hi = Q_LORA_RANK, Q_LORA_RANK + KV_LORA_RANK

    @pl.when(i == 0)
    def _():
        zkv = zkv_ref[0]
        kvn = _rms(zkv[:, c_kv_lo:c_kv_hi], kvn_ref[...]).astype(BF16)
        kvu = jnp.dot(kvn, wkv_ref[...], preferred_element_type=F32)
        cos_t, sin_t = _rope_tiles(pos_ref[0], freq_ref[...], pcos_ref[...], psin_ref[...], base_ref[...])
        rot_s[...] = cos_t + pltpu.roll(sin_t, QK_ROPE_DIM, 1)
        k_rope = zkv[:, c_kv_hi:]
        k_rot = k_rope * cos_t + pltpu.roll(k_rope, LANES - QK_ROPE_DIM, 1) * sin_t
        k_rot = k_rot + pltpu.roll(k_rot, QK_ROPE_DIM, 1)
        lane = lax.broadcasted_iota(jnp.int32, (1, LANES), 1)
        ones_col = (lane == V_HEAD_DIM).astype(F32)
        for h in range(MLA_HEADS):
            k_s[h] = (kvu[:, h * LANES:(h + 1) * LANES] + k_rot).astype(BF16)
            v_s[h] = (kvu[:, (MLA_HEADS + h) * LANES:(MLA_HEADS + h + 1) * LANES] + ones_col).astype(BF16)

    zq = zq_ref[0]
    qn = _rms(zq[:, :Q_LORA_RANK], qn_ref[...]).astype(BF16)
    q = jnp.dot(qn, wq_ref[...], preferred_element_type=F32)
    rows = pl.ds(pl.multiple_of(i * tq, tq), tq)
    rot_q = rot_s[rows, :] * (scale * LOG2_E)

    def scores(h):
        sl = slice(h * LANES, (h + 1) * LANES)
        qh = (q[:, sl] * rot_q).astype(BF16)
        return lax.dot_general(qh, k_s[h], (((1,), (1,)), ((), ())), preferred_element_type=F32)

    s = scores(0)
    for h in range(MLA_HEADS):
        s_next = scores(h + 1) if h + 1 < MLA_HEADS else None
        p = jnp.exp2((s - jnp.max(s, axis=-1, keepdims=True)).astype(BF16))
        o = jnp.dot(p, v_s[h], preferred_element_type=F32)
        y_ref[0, :, h * V_HEAD_DIM:(h + 1) * V_HEAD_DIM] = (
            o[:, :V_HEAD_DIM] / o[:, V_HEAD_DIM:V_HEAD_DIM + 1]).astype(y_ref.dtype)
        s = s_next


def _mla_attn(z_mla, pos_row, freq_col, place_cos, place_sin, base, q_norm, wq_p, kv_norm, wkv_p, tq):
    B, S, ZC = z_mla.shape
    return pl.pallas_call(
        _mla_kernel,
        grid=(B, S // tq),
        in_specs=[pl.BlockSpec((1, tq, ZC), lambda b, i: (b, i, 0)),
                  pl.BlockSpec((1, S, ZC), lambda b, i: (b, 0, 0)),
                  pl.BlockSpec((1, 1, S), lambda b, i: (b, 0, 0)),
                  _const_spec(freq_col.shape), _const_spec(place_cos.shape), _const_spec(place_sin.shape),
                  _const_spec(base.shape), _const_spec(q_norm.shape),
                  _const_spec(wq_p.shape), _const_spec(kv_norm.shape),
                  _const_spec(wkv_p.shape)],
        out_specs=pl.BlockSpec((1, tq, MLA_WIDTH), lambda b, i: (b, i, 0)),
        out_shape=jax.ShapeDtypeStruct((B, S, MLA_WIDTH), BF16),
        scratch_shapes=[pltpu.VMEM((MLA_HEADS, S, LANES), BF16), pltpu.VMEM((MLA_HEADS, S, LANES), BF16),
                        pltpu.VMEM((S, LANES), F32)],
        compiler_params=_params("parallel", "arbitrary"),
        name="mla_attn",
    )(z_mla, z_mla, pos_row, freq_col, place_cos, place_sin, base, q_norm, wq_p, kv_norm, wkv_p)


def _mix_out_rows(x, o_f, o_b, bonus, gate, lnw, lnb, segm, y_mla, w_r, w_m, g):
    o = o_f + o_b
    o_hi, o_lo = _split2(o)
    oc = o - (_seg_dot(o_hi, segm) + _seg_dot(o_lo, segm))
    var = _seg_dot((oc * oc).astype(BF16), segm)
    o = oc * lax.rsqrt(var + LN_X_EPS) * lnw + lnb
    y_rwkv = (o + bonus) * gate
    y = _dot(y_rwkv.astype(BF16), w_r) + _dot(y_mla, w_m)
    return x + _rms(y, g)


def _mem_attn_rows(x, k_ref, v_ref, g_pre, wq, wo, g_post, att_s):
    h = _rms(x, g_pre).astype(BF16)
    q = (_dot(h, wq) * (MEM_HEAD_DIM ** -0.5 * LOG2_E)).astype(BF16)

    def scores(hd):
        sl = slice(hd * MEM_HEAD_DIM, (hd + 1) * MEM_HEAD_DIM)
        return lax.dot_general(q[:, sl], k_ref[:, sl], (((1,), (1,)), ((), ())), preferred_element_type=F32)

    s = scores(0)
    for hd in range(MEM_HEADS):
        sl = slice(hd * MEM_HEAD_DIM, (hd + 1) * MEM_HEAD_DIM)
        s_next = scores(hd + 1) if hd + 1 < MEM_HEADS else None
        p = jnp.exp2(s - jnp.max(s, axis=-1, keepdims=True))
        denom = jnp.sum(p, axis=-1, keepdims=True)
        o = _dot(p.astype(BF16), v_ref[:, sl])
        att_s[:, sl] = (o / denom).astype(BF16)
        s = s_next
    return x + _rms(_dot(att_s[...], wo), g_post)


def _mix_mem_kernel(x_ref, of_ref, ob_ref, bf_ref, bb_ref, gate_ref, lnw_ref, lnb_ref, segm_ref, ym_ref,
                    wr_ref, wm_ref, gmix_ref, mem_ref, gmem_ref, wkv_ref, gpre_ref, wq_ref, wo_ref, gpost_ref,
                    o_ref, k_s, v_s, att_s):
    @pl.when(pl.program_id(1) == 0)
    def _():
        m = _rms(mem_ref[0], gmem_ref[...]).astype(BF16)
        kv = _dot(m, wkv_ref[...])
        k_s[...] = kv[:, :D_MODEL].astype(BF16)
        v_s[...] = kv[:, D_MODEL:].astype(BF16)

    bonus = bf_ref[0].astype(F32) + bb_ref[0].astype(F32)
    x = _mix_out_rows(x_ref[0], of_ref[0], ob_ref[0], bonus, gate_ref[0].astype(F32), lnw_ref[...], lnb_ref[...],
                      segm_ref[...], ym_ref[0], wr_ref[...], wm_ref[...], gmix_ref[...])
    o_ref[0] = _mem_attn_rows(x, k_s, v_s, gpre_ref[...], wq_ref[...], wo_ref[...], gpost_ref[...], att_s)


def _mix_mem(x, o_f, o_b, bonus_f, bonus_b, gate, lnw, lnb, segm, y_mla, w_r, w_m, g_mix,
             mem, g_mem, wkv, g_pre, wq, wo, g_post, tm):
    B, S, D = x.shape
    T = mem.shape[1]
    C = RWKV_WIDTH
    row_spec = lambda cols: pl.BlockSpec((1, tm, cols), lambda b, i: (b, i, 0))
    return pl.pallas_call(
        _mix_mem_kernel,
        grid=(B, S // tm),
        in_specs=[row_spec(D), row_spec(C), row_spec(C), row_spec(C), row_spec(C), row_spec(C)]
                 + [_const_spec(t.shape) for t in (lnw, lnb, segm)] + [row_spec(MLA_WIDTH)]
                 + [_const_spec(t.shape) for t in (w_r, w_m, g_mix)]
                 + [pl.BlockSpec((1, T, D), lambda b, i: (b, 0, 0))]
                 + [_const_spec(t.shape) for t in (g_mem, wkv, g_pre, wq, wo, g_post)],
        out_specs=row_spec(D),
        out_shape=jax.ShapeDtypeStruct((B, S, D), F32),
        scratch_shapes=[pltpu.VMEM((T, D), BF16), pltpu.VMEM((T, D), BF16), pltpu.VMEM((tm, D), BF16)],
        compiler_params=_params("parallel", "arbitrary"),
        name="mix_mem",
    )(x, o_f, o_b, bonus_f, bonus_b, gate, lnw, lnb, segm, y_mla, w_r, w_m, g_mix,
      mem, g_mem, wkv, g_pre, wq, wo, g_post)


def _mlp_kernel(x_ref, gpre_ref, w1_ref, w2_ref, gpost_ref, o_ref):
    x = x_ref[0]
    h = _rms(x, gpre_ref[...]).astype(BF16)
    u = jnp.maximum(jnp.dot(h, w1_ref[...], preferred_element_type=F32), 0.0)
    y = jnp.dot((u * u).astype(BF16), w2_ref[...], preferred_element_type=F32)
    o_ref[0] = x + _rms(y, gpost_ref[...])


def _mlp(x, g_pre, w1, w2, g_post, tm):
    B, S, D = x.shape
    row_spec = pl.BlockSpec((1, tm, D), lambda b, i: (b, i, 0))
    return pl.pallas_call(
        _mlp_kernel,
        grid=(B, S // tm),
        in_specs=[row_spec, _const_spec(g_pre.shape), _const_spec(w1.shape), _const_spec(w2.shape),
                  _const_spec(g_post.shape)],
        out_specs=row_spec,
        out_shape=jax.ShapeDtypeStruct((B, S, D), F32),
        compiler_params=_params("parallel", "parallel"),
        name="mlp",
    )(x, g_pre, w1, w2, g_post)


def _pad_cols(w, n):
    return jnp.pad(w, ((0, 0), (0, n - w.shape[1])))


def _rope_tile(w):
    half = QK_ROPE_DIM // 2
    return jnp.concatenate([jnp.zeros((w.shape[0], QK_NOPE_DIM), w.dtype), w, w[:, half:], w[:, :half]], axis=1)


def _lora_rows(w, lo, rows):
    return jnp.pad(w, ((lo, LORA_TILE - lo - rows), (0, 0)))


def kernel(x, mem, positions, norm_mix_pre, w_in, conv_rwkv, rwkv_w0, rwkv_w2, rwkv_a0, rwkv_a2, rwkv_g2, rwkv_k_k, rwkv_k_a, rwkv_r_k, rwkv_lnx_w, rwkv_lnx_b, mla_q_norm, mla_w_uq, mla_kv_norm, mla_w_ukv, w_out, norm_mix_post, norm_mem_pre, norm_memtok, mem_wq, mem_wkv, mem_wo, norm_mem_post, norm_mlp_pre, mlp_w1, mlp_w2, norm_mlp_post):
    depth = w_in.shape[0]
    C = RWKV_WIDTH
    head_of = jnp.arange(C) // HEAD_DIM
    seg = (head_of[:, None] == head_of[None, :]).astype(BF16)
    seg = seg[:C // 2, :C // 2]
    seg_mean = (seg.astype(F32) / HEAD_DIM).astype(BF16)
    inv_freq = ROPE_THETA ** (-jnp.arange(0, QK_ROPE_DIM, 2, dtype=F32) / QK_ROPE_DIM)
    half = QK_ROPE_DIM // 2
    lane = jnp.arange(LANES)[None, :]
    f_idx = jnp.arange(half)[:, None]
    first, second = lane == QK_NOPE_DIM + f_idx, lane == QK_NOPE_DIM + half + f_idx
    place_cos = (first | second).astype(BF16)
    place_sin = (second.astype(F32) - first.astype(F32)).astype(BF16)
    rope_base = (lane < QK_NOPE_DIM).astype(F32)
    freq_col = inv_freq[:, None]
    pos_row = positions[:, None, :]
    step = jnp.arange(CHUNK)
    tri = jnp.stack([step[None, :] <= step[:, None], step[None, :] >= step[:, None]]).astype(F32)
    row = lambda t: t.reshape(1, -1)

    for l in range(depth):
        w = w_in[l]
        mla0 = RWKV_COLS
        w_rope = w[:, mla0 + Q_LORA_RANK + KV_LORA_RANK:]
        w_r = _pad_cols(w[:, :RWKV_COLS], RWKV_TILE_COLS).astype(BF16)
        w_m = jnp.concatenate([w[:, mla0:mla0 + Q_LORA_RANK + KV_LORA_RANK], _rope_tile(w_rope)],
                              axis=1).astype(BF16)
        conv_p = _pad_cols(conv_rwkv[l], RWKV_TILE_COLS)
        z_rwkv, z_mla = _in_proj(x, row(norm_mix_pre[l]), w_r, w_m, conv_p, tm=512)

        w2_p = jnp.stack([_lora_rows(rwkv_w2[l, d], d * DECAY_LORA, DECAY_LORA) for d in range(2)])
        a_lo = 2 * DECAY_LORA
        a2_p = jnp.stack([_lora_rows(rwkv_a2[l, d], a_lo + d * ICLR_LORA, ICLR_LORA) for d in range(2)])
        g2_p = _lora_rows(rwkv_g2[l], a_lo + 2 * ICLR_LORA, GATE_LORA)
        o_f, o_b, bonus_f, bonus_b, gate = _rwkv(z_rwkv, rwkv_w0[l], w2_p, rwkv_a0[l], a2_p, g2_p,
                                                 row(rwkv_k_k[l]), row(rwkv_k_a[l]), row(rwkv_r_k[l]),
                                                 seg, tri, tm=512)

        qk = QK_NOPE_DIM + QK_ROPE_DIM
        w_uq = mla_w_uq[l].reshape(Q_LORA_RANK, MLA_HEADS, qk)
        uq_rope = w_uq[:, :, QK_NOPE_DIM:]
        uq_sw = jnp.concatenate([uq_rope[:, :, half:], uq_rope[:, :, :half]], axis=2)
        wq_p = jnp.concatenate([w_uq, uq_sw], axis=2).reshape(Q_LORA_RANK, MLA_HEADS * LANES)
        w_ukv = mla_w_ukv[l].reshape(KV_LORA_RANK, MLA_HEADS, QK_NOPE_DIM + V_HEAD_DIM)
        wk_p = jnp.pad(w_ukv[:, :, :QK_NOPE_DIM], ((0, 0), (0, 0), (0, LANES - QK_NOPE_DIM)))
        wv_p = jnp.pad(w_ukv[:, :, QK_NOPE_DIM:], ((0, 0), (0, 0), (0, LANES - V_HEAD_DIM)))
        wkv_p = jnp.concatenate([wk_p.reshape(KV_LORA_RANK, MLA_HEADS * LANES),
                                 wv_p.reshape(KV_LORA_RANK, MLA_HEADS * LANES)], axis=1)
        y_mla = _mla_attn(z_mla, pos_row, freq_col, place_cos, place_sin, rope_base, row(mla_q_norm[l]),
                          wq_p.astype(BF16), row(mla_kv_norm[l]), wkv_p.astype(BF16), tq=512)

        wo = w_out[l].astype(BF16)
        x = _mix_mem(x, o_f, o_b, bonus_f, bonus_b, gate, row(rwkv_lnx_w[l]), row(rwkv_lnx_b[l]), seg_mean,
                     y_mla, wo[:C], wo[C:], row(norm_mix_post[l]), mem, row(norm_memtok[l]),
                     mem_wkv[l].astype(BF16), row(norm_mem_pre[l]),
                     mem_wq[l].astype(BF16), mem_wo[l].astype(BF16), row(norm_mem_post[l]), tm=512)

        x = _mlp(x, row(norm_mlp_pre[l]), mlp_w1[l].astype(BF16), mlp_w2[l].astype(BF16),
                 row(norm_mlp_post[l]), tm=512)
    return x
```
